```python
import math
import jax, jax.numpy as jnp
from jax import lax
import numpy as np

D_MODEL = 2048
BATCH = 1
SEQ = 8192
DEPTH = 1
DEC_BATCH = 128
DEC_SEQ = 1
PAST_LEN = 8192
PAGE_SIZE = 128

HEAD_DIM = 64
MIX_WIDTH = D_MODEL
RWKV_WIDTH = MIX_WIDTH // 2
N_RWKV_HEADS = RWKV_WIDTH // HEAD_DIM
ATTN_WIDTH = MIX_WIDTH - RWKV_WIDTH
N_Q_HEADS = ATTN_WIDTH // HEAD_DIM
N_KV_HEADS = max(1, N_Q_HEADS // 8)
Q_PER_KV = N_Q_HEADS // N_KV_HEADS
KV_WIDTH = N_KV_HEADS * HEAD_DIM
WINDOW = 128
D_DECAY_LORA = max(32, int(round(1.8 * RWKV_WIDTH ** 0.5 / 32)) * 32)
D_AAA_LORA = max(32, int(round(1.8 * RWKV_WIDTH ** 0.5 / 32)) * 32)
D_GATE_LORA = max(32, int(round(0.6 * RWKV_WIDTH ** 0.8 / 32)) * 32)
N_SHIFTED = 6
IN_SPLITS = (RWKV_WIDTH, RWKV_WIDTH, RWKV_WIDTH, D_DECAY_LORA, D_AAA_LORA, D_GATE_LORA,
             ATTN_WIDTH, KV_WIDTH, KV_WIDTH)
IN_COLS = sum(IN_SPLITS)
N_EXPERTS = 32
TOP_K = 4
D_FF = D_MODEL
SWIGLU_ALPHA = 1.702
SWIGLU_LIMIT = 7.0
NORM_EPS = 1e-6
GN_EPS = 64e-5
L2_EPS = 1e-12

kernel_name = "hybrid_rwkv7_swa_sink_moe_decode_step"


def _rmsnorm(x, g):
    xf = x.astype(jnp.float32)
    y = xf * lax.rsqrt(jnp.mean(xf * xf, axis=-1, keepdims=True) + NORM_EPS)
    return (y * g.astype(jnp.float32)).astype(x.dtype)


def _alibi_slopes():
    i = jnp.arange(1, N_Q_HEADS + 1, dtype=jnp.float32)
    return jnp.exp2(-8.0 * i / N_Q_HEADS).reshape(N_KV_HEADS, Q_PER_KV)


def _in_proj(h, h_prev, w_in, mu):
    offs = [0]
    for n in IN_SPLITS:
        offs.append(offs[-1] + n)
    dx = h_prev - h
    outs = []
    for i in range(N_SHIFTED):
        xi = h + dx * mu[i]
        outs.append(xi @ w_in[:, offs[i]:offs[i + 1]])
    qkv = h @ w_in[:, offs[N_SHIFTED]:]
    q, k, v = jnp.split(qkv, [ATTN_WIDTH, ATTN_WIDTH + KV_WIDTH], axis=-1)
    return outs + [q, k, v]


def _rwkv7(r, k, v, wl, al, gl, wkv0, p):
    f32 = jnp.float32
    B, T, _ = r.shape
    hs = (B, T, N_RWKV_HEADS, HEAD_DIM)
    w = -jax.nn.softplus(-(p["w0"] + jnp.tanh(wl) @ p["w2"]).astype(f32)) - 0.5
    decay = jnp.exp(-jnp.exp(w)).reshape(hs)
    a = jax.nn.sigmoid((p["a0"] + al @ p["a2"]).astype(f32))
    g = jax.nn.sigmoid(gl) @ p["g2"]
    kk = (k * p["k_k"]).astype(f32).reshape(hs)
    kk = kk / jnp.maximum(jnp.sqrt(jnp.sum(kk * kk, axis=-1, keepdims=True)), L2_EPS)
    kf = (k.astype(f32) * (1.0 + (a - 1.0) * p["k_a"].astype(f32))).reshape(hs)
    a = a.reshape(hs)
    rf = r.astype(f32).reshape(hs)
    vf = v.astype(f32).reshape(hs)

    def step(S, inp):
        r_t, w_t, k_t, v_t, a_t, b_t = inp
        sa = jnp.einsum('bhvk,bhk->bhv', S, a_t)
        S = (S * w_t[:, :, None, :] + sa[..., None] * b_t[:, :, None, :]
             + v_t[..., None] * k_t[:, :, None, :])
        return S, jnp.einsum('bhvk,bhk->bhv', S, r_t)

    xs = tuple(jnp.moveaxis(t, 1, 0) for t in (rf, decay, kf, vf, -kk, kk * a))
    S_T, y = lax.scan(step, wkv0.astype(f32), xs)
    y = jnp.moveaxis(y, 0, 1)
    mean = jnp.mean(y, axis=-1, keepdims=True)
    var = jnp.mean(jnp.square(y - mean), axis=-1, keepdims=True)
    y = ((y - mean) * lax.rsqrt(var + GN_EPS)).reshape(B, T, RWKV_WIDTH)
    y = y * p["ln_w"].astype(f32) + p["ln_b"].astype(f32)
    bonus = jnp.sum(rf * kf * p["r_k"].astype(f32), axis=-1, keepdims=True) * vf
    y = (y + bonus.reshape(B, T, RWKV_WIDTH)) * g.astype(f32)
    return y.astype(r.dtype), S_T


def _sink_softmax(s, sinks):
    sk = jnp.broadcast_to(sinks, s.shape[:-1] + (1,)).astype(jnp.float32)
    return jax.nn.softmax(jnp.concatenate([s, sk], axis=-1), axis=-1)[..., :-1]


def _swa_prompt(q, k, v, sinks, slopes):
    B, T, Hkv, G, N = q.shape
    nb = T // WINDOW
    qb = q.reshape(B, nb, WINDOW, Hkv, G, N)
    kb = k.reshape(B, nb, WINDOW, Hkv, N)
    vb = v.reshape(B, nb, WINDOW, Hkv, N)
    pad = jnp.zeros_like(kb[:, :1])
    kc = jnp.concatenate([jnp.concatenate([pad, kb[:, :-1]], axis=1), kb], axis=2)
    vc = jnp.concatenate([jnp.concatenate([pad, vb[:, :-1]], axis=1), vb], axis=2)
    s = jnp.einsum('bnqhgd,bnkhd->bhgnqk', qb, kc,
                   preferred_element_type=jnp.float32) / math.sqrt(N)
    qi = jnp.arange(WINDOW)[:, None]
    kj = jnp.arange(2 * WINDOW)[None, :]
    dist = qi + WINDOW - kj
    blk = jnp.arange(nb)[:, None, None]
    valid = ((dist >= 0) & (dist <= WINDOW))[None] & ((blk > 0) | (kj[None] >= WINDOW))
    s = jnp.where(valid, s - slopes[:, :, None, None, None] * dist.astype(jnp.float32), -jnp.inf)
    pr = _sink_softmax(s, sinks.reshape(1, Hkv, G, 1, 1, 1))
    out = jnp.einsum('bhgnqk,bnkhd->bnqhgd', pr.astype(v.dtype), vc)
    return out.reshape(B, T, Hkv * G * N)


def _swa_sample(q, k, v, k_buf, v_buf, sinks, slopes):
    B, S, Hkv, G, N = q.shape
    Wb = k_buf.shape[1]
    kc = jnp.concatenate([k_buf.astype(k.dtype), k], axis=1)
    vc = jnp.concatenate([v_buf.astype(v.dtype), v], axis=1)
    dist = jnp.arange(S)[:, None] + Wb - jnp.arange(Wb + S)[None, :]
    valid = (dist >= 0) & (dist <= WINDOW)
    s = jnp.einsum('bqhgd,bkhd->bhgqk', q, kc,
                   preferred_element_type=jnp.float32) / math.sqrt(N)
    s = jnp.where(valid, s - slopes[:, :, None, None] * dist.astype(jnp.float32), -jnp.inf)
    pr = _sink_softmax(s, sinks.reshape(1, Hkv, G, 1, 1))
    out = jnp.einsum('bhgqk,bkhd->bqhgd', pr.astype(v.dtype), vc)
    return out.reshape(B, S, Hkv * G * N), kc[:, -Wb:], vc[:, -Wb:]


def _moe(x, rw, rb, w1, b1, w2, b2):
    f32 = jnp.float32
    logits = (x @ rw + rb).astype(f32)
    top_v, top_i = lax.top_k(logits, TOP_K)
    top_w = jax.nn.softmax(top_v, axis=-1)
    gates = jnp.sum(jax.nn.one_hot(top_i, N_EXPERTS, dtype=f32) * top_w[..., None], axis=1)
    out = jnp.zeros((x.shape[0], x.shape[1]), f32)
    for e in range(N_EXPERTS):
        hh = x @ w1[e] + b1[e]
        x_glu = jnp.minimum(hh[:, 0::2], SWIGLU_LIMIT)
        x_lin = jnp.clip(hh[:, 1::2], -SWIGLU_LIMIT, SWIGLU_LIMIT)
        act = x_glu * jax.nn.sigmoid(SWIGLU_ALPHA * x_glu) * (x_lin + 1.0)
        out = out + gates[:, e:e + 1] * (act @ w2[e] + b2[e]).astype(f32)
    return out.astype(x.dtype)


def _layer(x, c, shift_in, wkv_in, k_buf, v_buf, p, slopes):
    B, T, D = x.shape
    mod = (jax.nn.silu(c) @ p["w_ada"] + p["b_ada"])[:, None, :]
    sh1, sc1, gt1, sh2, sc2, gt2 = jnp.split(mod, 6, axis=-1)
    h = _rmsnorm(x, p["g_pre_mix"]) * (1.0 + sc1) + sh1
    h_prev = jnp.concatenate([shift_in[:, None, :].astype(h.dtype), h[:, :-1]], axis=1)
    r, k, v, wl, al, gl, q, ka, va = _in_proj(h, h_prev, p["w_in"], p["mu"])
    y_rwkv, wkv_out = _rwkv7(r, k, v, wl, al, gl, wkv_in, p)
    q = q.reshape(B, T, N_KV_HEADS, Q_PER_KV, HEAD_DIM)
    ka = ka.reshape(B, T, N_KV_HEADS, HEAD_DIM)
    va = va.reshape(B, T, N_KV_HEADS, HEAD_DIM)
    if k_buf is None:
        y_attn = _swa_prompt(q, ka, va, p["sinks"], slopes)
        n_keep = min(WINDOW, T)
        k_new, v_new = ka[:, T - n_keep:], va[:, T - n_keep:]
    else:
        y_attn, k_new, v_new = _swa_sample(q, ka, va, k_buf, v_buf, p["sinks"], slopes)
    mix = jnp.concatenate([y_rwkv, y_attn.astype(y_rwkv.dtype)], axis=-1) @ p["w_out"]
    x = x + gt1 * _rmsnorm(mix, p["g_post_mix"])
    h2 = _rmsnorm(x, p["g_pre_ffn"]) * (1.0 + sc2) + sh2
    f = _moe(h2.reshape(B * T, D), p["router_w"], p["router_b"], p["moe_w1"], p["moe_b1"],
             p["moe_w2"], p["moe_b2"]).reshape(B, T, D)
    x = x + gt2 * _rmsnorm(f, p["g_post_ffn"])
    return x, h[:, -1], wkv_out, k_new, v_new


def setup_inputs(seed: int = 0) -> dict:
    key = jax.random.key(seed)
    ks = iter(jax.random.split(key, 48))
    f32 = jnp.float32
    L, D, RW, E, F = DEPTH, D_MODEL, RWKV_WIDTH, N_EXPERTS, D_FF
    w_buf = min(WINDOW, PAST_LEN)

    def nrm(shape, scale):
        return jax.random.normal(next(ks), shape, f32) * scale

    def gain(shape):
        return 1.0 + 0.02 * jax.random.normal(next(ks), shape, f32)

    return {
        "x_prompt": nrm((BATCH, SEQ, D), 1.0),
        "x_sample": nrm((DEC_BATCH, DEC_SEQ, D), 1.0),
        "cache_k": nrm((L, DEC_BATCH, w_buf, N_KV_HEADS, HEAD_DIM), 1.0),
        "cache_v": nrm((L, DEC_BATCH, w_buf, N_KV_HEADS, HEAD_DIM), 1.0),
        "state_wkv": nrm((L, DEC_BATCH, N_RWKV_HEADS, HEAD_DIM, HEAD_DIM), 0.3),
        "state_shift": nrm((L, DEC_BATCH, D), 1.0),
        "c_prompt": nrm((BATCH, D), 1.0),
        "c_sample": nrm((DEC_BATCH, D), 1.0),
        "w_ada": nrm((L, D, 6 * D), 0.5 * D ** -0.5),
        "b_ada": nrm((L, 6 * D), 0.02),
        "g_pre_mix": gain((L, D)),
        "g_post_mix": gain((L, D)),
        "g_pre_ffn": gain((L, D)),
        "g_post_ffn": gain((L, D)),
        "mu_shift": jax.random.uniform(next(ks), (L, N_SHIFTED, D), f32, 0.0, 1.0),
        "w_in": nrm((L, D, IN_COLS), D ** -0.5),
        "rwkv_w0": jax.random.uniform(next(ks), (L, RW), f32, -5.0, 1.0),
        "rwkv_w2": nrm((L, D_DECAY_LORA, RW), 0.1 * D_DECAY_LORA ** -0.5),
        "rwkv_a0": nrm((L, RW), 0.1),
        "rwkv_a2": nrm((L, D_AAA_LORA, RW), 0.1 * D_AAA_LORA ** -0.5),
        "rwkv_g2": nrm((L, D_GATE_LORA, RW), D_GATE_LORA ** -0.5),
        "rwkv_k_k": 0.85 + nrm((L, RW), 0.02),
        "rwkv_k_a": gain((L, RW)),
        "rwkv_r_k": nrm((L, N_RWKV_HEADS, HEAD_DIM), 0.1),
        "rwkv_ln_w": gain((L, RW)),
        "rwkv_ln_b": nrm((L, RW), 0.02),
        "attn_sinks": nrm((L, N_Q_HEADS), 0.5),
        "w_out": nrm((L, MIX_WIDTH, D), MIX_WIDTH ** -0.5),
        "router_w": nrm((L, D, E), D ** -0.5),
        "router_b": nrm((L, E), 0.01),
        "moe_w1": nrm((L, E, D, 2 * F), D ** -0.5),
        "moe_b1": nrm((L, E, 2 * F), 0.01),
        "moe_w2": nrm((L, E, F, D), F ** -0.5),
        "moe_b2": nrm((L, E, D), 0.01),
    }


def reference(x_prompt, x_sample, cache_k, cache_v, state_wkv, state_shift, c_prompt, c_sample,
              w_ada, b_ada, g_pre_mix, g_post_mix, g_pre_ffn, g_post_ffn, mu_shift, w_in,
              rwkv_w0, rwkv_w2, rwkv_a0, rwkv_a2, rwkv_g2, rwkv_k_k, rwkv_k_a, rwkv_r_k,
              rwkv_ln_w, rwkv_ln_b, attn_sinks, w_out, router_w, router_b,
              moe_w1, moe_b1, moe_w2, moe_b2):
    slopes = _alibi_slopes()
    yp, ys = x_prompt, x_sample
    kp, vp, wp, sp, kd, vd, wd, sd = [], [], [], [], [], [], [], []
    for l in range(DEPTH):
        p = {
            "w_ada": w_ada[l], "b_ada": b_ada[l],
            "g_pre_mix": g_pre_mix[l], "g_post_mix": g_post_mix[l],
            "g_pre_ffn": g_pre_ffn[l], "g_post_ffn": g_post_ffn[l],
            "mu": mu_shift[l], "w_in": w_in[l],
            "w0": rwkv_w0[l], "w2": rwkv_w2[l], "a0": rwkv_a0[l], "a2": rwkv_a2[l],
            "g2": rwkv_g2[l], "k_k": rwkv_k_k[l], "k_a": rwkv_k_a[l], "r_k": rwkv_r_k[l],
            "ln_w": rwkv_ln_w[l], "ln_b": rwkv_ln_b[l], "sinks": attn_sinks[l],
            "w_out": w_out[l], "router_w": router_w[l], "router_b": router_b[l],
            "moe_w1": moe_w1[l], "moe_b1": moe_b1[l], "moe_w2": moe_w2[l], "moe_b2": moe_b2[l],
        }
        shift0 = jnp.zeros((yp.shape[0], yp.shape[2]), yp.dtype)
        wkv0 = jnp.zeros((yp.shape[0], N_RWKV_HEADS, HEAD_DIM, HEAD_DIM), jnp.float32)
        yp, s_p, w_p, k_p, v_p = _layer(yp, c_prompt, shift0, wkv0, None, None, p, slopes)
        ys, s_d, w_d, k_d, v_d = _layer(ys, c_sample, state_shift[l], state_wkv[l],
                                        cache_k[l], cache_v[l], p, slopes)
        kp.append(k_p); vp.append(v_p); wp.append(w_p); sp.append(s_p)
        kd.append(k_d); vd.append(v_d); wd.append(w_d); sd.append(s_d)
    return (yp, ys, jnp.stack(kp), jnp.stack(vp), jnp.stack(wp), jnp.stack(sp),
            jnp.stack(kd), jnp.stack(vd), jnp.stack(wd), jnp.stack(sd))
```

```python
import functools
import math

import jax
import jax.numpy as jnp
from jax import lax
from jax.experimental import pallas as pl
from jax.experimental.pallas import tpu as pltpu

F32 = jnp.float32
BF16 = jnp.bfloat16

D_MODEL = 2048
HEAD_DIM = 64
RWKV_WIDTH = 1024
N_RWKV_HEADS = 16
ATTN_WIDTH = 1024
N_Q_HEADS = 16
N_KV_HEADS = 2
Q_PER_KV = 8
KV_WIDTH = 128
WINDOW = 128
N_SHIFTED = 6
N_EXPERTS = 32
TOP_K = 4
D_FF = 2048
SWIGLU_ALPHA = 1.702
SWIGLU_LIMIT = 7.0
NORM_EPS = 1e-6
GN_EPS = 64e-5
L2_EPS = 1e-12

LANES = 128
VMEM_LIMIT = 56 * 1024 * 1024

CHUNK = 64
RWKV_ROWS = 512
MOE_ROWS = 256
MOE_SUB = 4
MOE_FT = 256
GATHER_ROWS = 512
COMBINE_TOK = 128


def _dot(a, b):
    return jnp.dot(a, b, preferred_element_type=F32)


def _dot_nt(a, b):
    return lax.dot_general(a, b, (((1,), (1,)), ((), ())), preferred_element_type=F32)


def _split_bf16(x):
    hi = x.astype(BF16)
    lo = (x - hi.astype(F32)).astype(BF16)
    return hi, lo


def _dot3(a, b):
    ah, al = _split_bf16(a)
    bh, bl = _split_bf16(b)
    return _dot(ah, bh) + _dot(ah, bl) + _dot(al, bh)


def _seg_sum64(x):
    r = lax.broadcasted_iota(jnp.int32, (LANES, LANES), 0) // HEAD_DIM
    c = lax.broadcasted_iota(jnp.int32, (LANES, LANES), 1) // HEAD_DIM
    bd = jnp.where(r == c, 1.0, 0.0).astype(BF16)
    hi, lo = _split_bf16(x)
    outs = []
    for j in range(x.shape[1] // LANES):
        sl = slice(LANES * j, LANES * (j + 1))
        outs.append(_dot(hi[:, sl], bd) + _dot(lo[:, sl], bd))
    return jnp.concatenate(outs, axis=1)


def _rmsnorm(x, g):
    ms = jnp.mean(x * x, axis=-1, keepdims=True)
    return x * lax.rsqrt(ms + NORM_EPS) * g


def _resident(shape):
    nd = len(shape)
    return pl.BlockSpec(shape, lambda *_: (0,) * nd, pipeline_mode=pl.Buffered(1))


def _mod_kernel(c_ref, w_ref, b_ref, o_ref):
    c = c_ref[...]
    s = c * jax.nn.sigmoid(c)
    o_ref[...] = _dot3(s, w_ref[...]) + b_ref[...]


def _adaln_mod(c, w_ada, b_ada):
    rows, d = c.shape
    n = w_ada.shape[1]
    tn = 512
    return pl.pallas_call(
        _mod_kernel,
        grid=(n // tn,),
        in_specs=[pl.BlockSpec((rows, d), lambda j: (0, 0)),
                  pl.BlockSpec((d, tn), lambda j: (0, j)),
                  pl.BlockSpec((1, tn), lambda j: (0, j))],
        out_specs=pl.BlockSpec((rows, tn), lambda j: (0, j)),
        out_shape=jax.ShapeDtypeStruct((rows, n), F32),
        compiler_params=pltpu.CompilerParams(dimension_semantics=("arbitrary",), vmem_limit_bytes=VMEM_LIMIT),
        name="adaln_mod",
    )(c, w_ada, b_ada)


def _inproj_kernel(is_prompt, tm,
                   x_ref, prev_ref, sh_ref, sc_ref, gpm_ref, mu_ref,
                   wr_ref, wk_ref, wv_ref, wwl_ref, wal_ref, wgl_ref, wqkv_ref,
                   w0_ref, w2_ref, a0_ref, a2_ref, g2_ref, kk_ref, ka_ref, rk_ref,
                   r_o, lw_o, kf_o, v_o, na_o, b_o, g_o, bonus_o, q_o, kat_o, vat_o, h_o):
    i = pl.program_id(0)
    gpm = gpm_ref[...]
    sh = sh_ref[...]
    sc = sc_ref[...]
    if is_prompt:
        sh = sh[0:1]
        sc = sc[0:1]

    def modnorm(x):
        return _rmsnorm(x, gpm) * (1.0 + sc) + sh

    h = modnorm(x_ref[...])
    if is_prompt:
        hp = modnorm(prev_ref[...])[7:8, :]
        hp = jnp.where(i > 0, hp, 0.0)
        row = lax.broadcasted_iota(jnp.int32, h.shape, 0)
        hprev = jnp.where(row == 0, hp, pltpu.roll(h, 1, axis=0))
        h_o[...] = h[tm - 8:tm, :]
    else:
        hprev = prev_ref[...]
        h_o[...] = h
    dx = hprev - h
    mu = mu_ref[...]

    def branch(j, w_ref):
        xi = (h + dx * mu[j:j + 1, :]).astype(BF16)
        return _dot(xi, w_ref[...])

    r = branch(0, wr_ref)
    k = branch(1, wk_ref)
    v = branch(2, wv_ref)
    wl = branch(3, wwl_ref)
    al = branch(4, wal_ref)
    gl = branch(5, wgl_ref)
    qkv = _dot(h.astype(BF16), wqkv_ref[...])
    q_o[...] = qkv[:, :ATTN_WIDTH]
    kat_o[...] = qkv[:, ATTN_WIDTH:ATTN_WIDTH + KV_WIDTH]
    vat_o[...] = qkv[:, ATTN_WIDTH + KV_WIDTH:]

    z = w0_ref[...] + _dot(jnp.tanh(wl).astype(BF16), w2_ref[...])
    w_raw = -jnp.logaddexp(-z, 0.0) - 0.5
    lw_o[...] = -jnp.exp(w_raw)
    a = jax.nn.sigmoid(a0_ref[...] + _dot(al.astype(BF16), a2_ref[...]))
    g_o[...] = _dot(jax.nn.sigmoid(gl).astype(BF16), g2_ref[...])
    kk = k * kk_ref[...]
    kk = kk / jnp.maximum(jnp.sqrt(_seg_sum64(kk * kk)), L2_EPS)
    kf = k * (1.0 + (a - 1.0) * ka_ref[...])
    r_o[...] = r
    kf_o[...] = kf
    v_o[...] = v
    na_o[...] = -kk
    b_o[...] = kk * a
    bonus_o[...] = _seg_sum64(r * kf * rk_ref[...]) * v


def _in_proj(is_prompt, tm, x, prev, mod, gpm, mu8, wts, rw):
    m, d = x.shape
    grid = (m // tm,)
    row = lambda i: (i, 0)
    if is_prompt:
        prev_spec = pl.BlockSpec((8, d), lambda i: (jnp.maximum(i * (tm // 8) - 1, 0), 0))
        mod_rows = 8
        mod_map = lambda c: (lambda i: (0, c))
        h_shape, h_spec = (8, d), pl.BlockSpec((8, d), lambda i: (0, 0))
    else:
        prev_spec = pl.BlockSpec((tm, d), row)
        mod_rows = tm
        mod_map = lambda c: (lambda i: (i, c))
        h_shape, h_spec = (m, d), pl.BlockSpec((tm, d), row)
    in_specs = [pl.BlockSpec((tm, d), row), prev_spec,
                pl.BlockSpec((mod_rows, d), mod_map(0)), pl.BlockSpec((mod_rows, d), mod_map(1)),
                _resident((1, d)), _resident((8, d))]
    in_specs += [_resident(w.shape) for w in wts]
    in_specs += [_resident(p.shape) for p in rw]
    wide = jax.ShapeDtypeStruct((m, RWKV_WIDTH), F32)
    wide_spec = pl.BlockSpec((tm, RWKV_WIDTH), row)
    kv = jax.ShapeDtypeStruct((m, KV_WIDTH), F32)
    kv_spec = pl.BlockSpec((tm, KV_WIDTH), row)
    out_shape = [wide] * 9 + [kv, kv, jax.ShapeDtypeStruct(h_shape, F32)]
    out_specs = [wide_spec] * 9 + [kv_spec, kv_spec, h_spec]
    return pl.pallas_call(
        functools.partial(_inproj_kernel, is_prompt, tm),
        grid=grid, in_specs=in_specs, out_specs=out_specs, out_shape=out_shape,
        compiler_params=pltpu.CompilerParams(dimension_semantics=("arbitrary",), vmem_limit_bytes=VMEM_LIMIT),
        name="in_proj_prompt" if is_prompt else "in_proj_sample",
    )(x, prev, mod, mod, gpm, mu8, *wts, *rw)


def _rwkv_chunk_kernel(r_ref, lw_ref, k_ref, v_ref, a_ref, b_ref, y_ref, s_ref, st_ref):
    t = pl.program_id(1)
    C = CHUNK
    P = 2 * HEAD_DIM

    @pl.when(t == 0)
    def _():
        st_ref[...] = jnp.zeros_like(st_ref)

    ri = lax.broadcasted_iota(jnp.int32, (P, P), 0)
    ci = lax.broadcasted_iota(jnp.int32, (P, P), 1)
    bd = (ri // C) == (ci // C)
    tril_s = bd & ((ri % C) > (ci % C))
    tril_i = bd & ((ri % C) >= (ci % C))
    eye = jnp.where(ri == ci, 1.0, 0.0)
    lane0 = lax.broadcasted_iota(jnp.int32, (C, P), 1) < HEAD_DIM
    trow = lax.broadcasted_iota(jnp.int32, (C, P), 0)

    def stack(x):
        return jnp.concatenate([jnp.where(lane0, x, 0.0), jnp.where(lane0, 0.0, x)], axis=0)

    def dup(x):
        return jnp.concatenate([x, x], axis=0)

    def chunk(c, carry):
        sl = pl.ds(pl.multiple_of(c * C, C), C)
        lw = lw_ref[sl, :]
        cw = lw
        for s in (1, 2, 4, 8, 16, 32):
            cw = cw + jnp.where(trow >= s, pltpu.roll(cw, s, axis=0), 0.0)
        cw_last = cw[C - 1:C, :]
        e_neg = jnp.exp(-cw)
        e_end = jnp.exp(cw_last - cw)
        r = r_ref[sl, :]
        k = k_ref[sl, :]
        v = v_ref[sl, :]
        a = a_ref[sl, :]
        b = b_ref[sl, :]
        a2 = stack(a * jnp.exp(cw - lw))
        r2 = stack(r * jnp.exp(cw))
        v2 = stack(v).astype(BF16)
        lhs = jnp.concatenate([a2, r2], axis=0).astype(BF16)
        rhs = jnp.concatenate([dup(k * e_neg), dup(b * e_neg)], axis=0).astype(BF16)
        gram = _dot_nt(lhs, rhs)
        g_ak = jnp.where(tril_s, gram[0:P, 0:P], 0.0)
        l2 = jnp.where(tril_s, gram[0:P, P:2 * P], 0.0)
        p_rk = jnp.where(tril_i, gram[P:2 * P, 0:P], 0.0)
        p_rb = jnp.where(tril_i, gram[P:2 * P, P:2 * P], 0.0)
        inv = eye + l2
        lp = l2
        for _ in range(5):
            lpb = lp.astype(BF16)
            lp = _dot(lpb, lpb)
            inv = inv + _dot(lp.astype(BF16), inv.astype(BF16))
        st = st_ref[...]
        a_s = _dot_nt(lhs, st.astype(BF16))
        u2 = _dot(inv.astype(BF16), (a_s[0:P] + _dot(g_ak.astype(BF16), v2)).astype(BF16))
        uv = jnp.concatenate([u2.astype(BF16), v2], axis=0)
        y2 = a_s[P:2 * P] + _dot(jnp.concatenate([p_rb, p_rk], axis=1).astype(BF16), uv)
        y_ref[sl, :] = y2[0:C] + y2[C:2 * C]
        kbh = jnp.concatenate([dup(b * e_end), dup(k * e_end)], axis=0).astype(BF16)
        uvt = jnp.concatenate([u2, stack(v)], axis=0).T.astype(BF16)
        st_ref[...] = jnp.where(bd, st * jnp.exp(cw_last) + _dot(uvt, kbh), 0.0)
        return carry

    lax.fori_loop(0, RWKV_ROWS // C, chunk, 0)

    @pl.when(t == pl.num_programs(1) - 1)
    def _():
        s_ref[0] = st_ref[...]


def _rwkv_prompt(r, lw, kf, v, na, b):
    t = r.shape[0]
    n_pairs = RWKV_WIDTH // LANES
    spec = pl.BlockSpec((RWKV_ROWS, LANES), lambda p, i: (i, p))
    return pl.pallas_call(
        _rwkv_chunk_kernel,
        grid=(n_pairs, t // RWKV_ROWS),
        in_specs=[spec] * 6,
        out_specs=[spec, pl.BlockSpec((1, LANES, LANES), lambda p, i: (p, 0, 0))],
        out_shape=[jax.ShapeDtypeStruct((t, RWKV_WIDTH), F32),
                   jax.ShapeDtypeStruct((n_pairs, LANES, LANES), F32)],
        scratch_shapes=[pltpu.VMEM((LANES, LANES), F32)],
        compiler_params=pltpu.CompilerParams(dimension_semantics=("arbitrary", "arbitrary")),
        name="rwkv_chunked",
    )(r, lw, kf, v, na, b)


def _rwkv_step_kernel(bb, r_ref, lw_ref, k_ref, v_ref, a_ref, b_ref, s_ref, y_ref, so_ref):
    n = HEAD_DIM
    eye = jnp.where(lax.broadcasted_iota(jnp.int32, (n, n), 0) == lax.broadcasted_iota(jnp.int32, (n, n), 1), 1.0, 0.0)

    def body(bi, carry):
        rb = r_ref[bi]
        dec = jnp.exp(lw_ref[bi])
        kb = k_ref[bi]
        vb = v_ref[bi]
        ab = a_ref[bi]
        bb_ = b_ref[bi]
        ys = []
        for h in range(N_RWKV_HEADS):
            hs = slice(h, h + 1)
            s = s_ref[bi, h]
            sa = jnp.sum(s * ab[hs], axis=1, keepdims=True)
            vcol = jnp.sum(eye * vb[hs], axis=1, keepdims=True)
            s2 = s * dec[hs] + sa * bb_[hs] + vcol * kb[hs]
            so_ref[bi, h] = s2
            ycol = jnp.sum(s2 * rb[hs], axis=1, keepdims=True)
            ys.append(jnp.sum(eye * ycol, axis=0, keepdims=True))
        y_ref[bi] = jnp.concatenate(ys, axis=0)
        return carry

    lax.fori_loop(0, bb, body, 0)


def _rwkv_sample(r, lw, kf, v, na, b, state):
    nb = r.shape[0]
    bb = 8
    vec = lambda x: x.reshape(nb, N_RWKV_HEADS, HEAD_DIM)
    vspec = pl.BlockSpec((bb, N_RWKV_HEADS, HEAD_DIM), lambda i: (i, 0, 0))
    sspec = pl.BlockSpec((bb, N_RWKV_HEADS, HEAD_DIM, HEAD_DIM), lambda i: (i, 0, 0, 0))
    return pl.pallas_call(
        functools.partial(_rwkv_step_kernel, bb),
        grid=(nb // bb,),
        in_specs=[vspec] * 6 + [sspec],
        out_specs=[vspec, sspec],
        out_shape=[jax.ShapeDtypeStruct((nb, N_RWKV_HEADS, HEAD_DIM), F32),
                   jax.ShapeDtypeStruct(state.shape, F32)],
        compiler_params=pltpu.CompilerParams(dimension_semantics=("arbitrary",)),
        name="rwkv_step",
    )(vec(r), vec(lw), vec(kf), vec(v), vec(na), vec(b), state)


def _alibi_slope(head):
    return 2.0 ** (-8.0 * (head + 1) / N_Q_HEADS)


def _swa_prompt_kernel(q_ref, kc_ref, kp_ref, vc_ref, vp_ref, sink_ref, o_ref):
    n = pl.program_id(0)
    w = WINDOW
    kcat = jnp.concatenate([kp_ref[...], kc_ref[...]], axis=0)
    vcat = jnp.concatenate([vp_ref[...], vc_ref[...]], axis=0)
    lane_k = lax.broadcasted_iota(jnp.int32, kcat.shape, 1) < HEAD_DIM
    kswap = pltpu.roll(kcat, HEAD_DIM, axis=1)
    vswap = pltpu.roll(vcat, HEAD_DIM, axis=1)
    kdup = [jnp.where(lane_k, kcat, kswap).astype(BF16), jnp.where(lane_k, kswap, kcat).astype(BF16)]
    vdup = [jnp.where(lane_k, vcat, vswap).astype(BF16), jnp.where(lane_k, vswap, vcat).astype(BF16)]
    qi = lax.broadcasted_iota(jnp.int32, (w, 2 * w), 0)
    kj = lax.broadcasted_iota(jnp.int32, (w, 2 * w), 1)
    dist = qi + w - kj
    valid = (dist >= 0) & (dist <= w) & ((n > 0) | (kj >= w))
    distf = dist.astype(F32)
    lane_q = lax.broadcasted_iota(jnp.int32, (w, LANES), 1) < HEAD_DIM
    sinks = sink_ref[...]
    for j in range(N_Q_HEADS // 2):
        kvh = (2 * j) // Q_PER_KV
        qp = q_ref[:, LANES * j:LANES * (j + 1)] * (1.0 / math.sqrt(HEAD_DIM))
        q2 = jnp.concatenate([jnp.where(lane_q, qp, 0.0), jnp.where(lane_q, 0.0, qp)], axis=0).astype(BF16)
        s2 = _dot_nt(q2, kdup[kvh])
        halves = []
        for half in range(2):
            head = 2 * j + half
            sink = sinks[0:1, head:head + 1]
            s = jnp.where(valid, s2[w * half:w * (half + 1)] - _alibi_slope(head) * distf, -jnp.inf)
            m = jnp.maximum(jnp.max(s, axis=-1, keepdims=True), sink)
            p = jnp.exp(s - m)
            den = jnp.sum(p, axis=-1, keepdims=True) + jnp.exp(sink - m)
            halves.append(_dot(p.astype(BF16), vdup[kvh]) / den)
        o_ref[:, LANES * j:LANES * (j + 1)] = jnp.where(lane_q, halves[0], halves[1])


def _swa_prompt(q, ka, va, sinks):
    t = q.shape[0]
    w = WINDOW
    cur = lambda n: (n, 0)
    prv = lambda n: (jnp.maximum(n - 1, 0), 0)
    kvs = lambda f: pl.BlockSpec((w, KV_WIDTH), f)
    return pl.pallas_call(
        _swa_prompt_kernel,
        grid=(t // w,),
        in_specs=[pl.BlockSpec((w, ATTN_WIDTH), cur), kvs(cur), kvs(prv), kvs(cur), kvs(prv),
                  pl.BlockSpec((1, N_Q_HEADS), lambda n: (0, 0))],
        out_specs=pl.BlockSpec((w, ATTN_WIDTH), cur),
        out_shape=jax.ShapeDtypeStruct((t, ATTN_WIDTH), F32),
        compiler_params=pltpu.CompilerParams(dimension_semantics=("arbitrary",)),
        name="swa_prompt",
    )(q, ka, ka, va, va, sinks)


def _swa_sample_kernel(q_ref, kn_ref, vn_ref, ck_ref, cv_ref, sink_ref, o_ref, ko_ref, vo_ref):
    wb = ck_ref.shape[1]
    q = q_ref[...] * (1.0 / math.sqrt(HEAD_DIM))
    q2 = jnp.concatenate([q, q], axis=2)
    rowh = lax.broadcasted_iota(jnp.int32, q2.shape, 1) // Q_PER_KV
    laneh = lax.broadcasted_iota(jnp.int32, q2.shape, 2) // HEAD_DIM
    qb = jnp.where(rowh == laneh, q2, 0.0)
    kn = kn_ref[...]
    vn = vn_ref[...]
    ck = ck_ref[...]
    cv = cv_ref[...]
    s = jnp.einsum('bqd,bkd->bqk', qb.astype(BF16), ck.astype(BF16), preferred_element_type=F32)
    s_self = jnp.sum(qb * kn, axis=2, keepdims=True)
    head = lax.broadcasted_iota(jnp.int32, (1, N_Q_HEADS, 1), 1).astype(F32)
    slope = jnp.exp2(-8.0 * (head + 1.0) / N_Q_HEADS)
    dist = (wb - lax.broadcasted_iota(jnp.int32, (1, 1, wb), 2)).astype(F32)
    s = s - slope * dist
    sink = sink_ref[...][None]
    m = jnp.maximum(jnp.maximum(jnp.max(s, axis=2, keepdims=True), s_self), sink)
    p = jnp.exp(s - m)
    p_self = jnp.exp(s_self - m)
    den = jnp.sum(p, axis=2, keepdims=True) + p_self + jnp.exp(sink - m)
    o = jnp.einsum('bqk,bkd->bqd', p.astype(BF16), cv.astype(BF16), preferred_element_type=F32)
    o = (o + p_self * vn) / den
    sel = lax.broadcasted_iota(jnp.int32, (1, N_Q_HEADS, HEAD_DIM), 1) < Q_PER_KV
    o_ref[...] = jnp.where(sel, o[:, :, :HEAD_DIM], o[:, :, HEAD_DIM:])
    ko_ref[:, 0:wb - 1, :] = ck_ref[:, 1:wb, :]
    ko_ref[:, wb - 1:wb, :] = kn
    vo_ref[:, 0:wb - 1, :] = cv_ref[:, 1:wb, :]
    vo_ref[:, wb - 1:wb, :] = vn


def _swa_sample(q, ka, va, cache_k, cache_v, sinks_col):
    nb, wb = cache_k.shape[0], cache_k.shape[1]
    bb = 16
    b3 = lambda i: (i, 0, 0)
    nspec = pl.BlockSpec((bb, 1, KV_WIDTH), b3)
    cspec = pl.BlockSpec((bb, wb, KV_WIDTH), b3)
    qspec = pl.BlockSpec((bb, N_Q_HEADS, HEAD_DIM), b3)
    return pl.pallas_call(
        _swa_sample_kernel,
        grid=(nb // bb,),
        in_specs=[qspec, nspec, nspec, cspec, cspec, pl.BlockSpec((N_Q_HEADS, 1), lambda i: (0, 0))],
        out_specs=[qspec, cspec, cspec],
        out_shape=[jax.ShapeDtypeStruct((nb, N_Q_HEADS, HEAD_DIM), F32),
                   jax.ShapeDtypeStruct(cache_k.shape, F32), jax.ShapeDtypeStruct(cache_v.shape, F32)],
        compiler_params=pltpu.CompilerParams(dimension_semantics=("arbitrary",)),
        name="swa_sample",
    )(q.reshape(nb, N_Q_HEADS, HEAD_DIM), ka.reshape(nb, 1, KV_WIDTH), va.reshape(nb, 1, KV_WIDTH),
      cache_k, cache_v, sinks_col)


def _post_mix_kernel(is_prompt, yr_ref, g_ref, bonus_ref, ya_ref, x_ref, gt1_ref, sh2_ref, sc2_ref,
                     lnw_ref, lnb_ref, wout_ref, gpost_ref, gpre_ref, rw_ref, rb_ref,
                     x1_o, h2_o, ti_o, tw_o):
    gt1 = gt1_ref[...]
    sh2 = sh2_ref[...]
    sc2 = sc2_ref[...]
    if is_prompt:
        gt1, sh2, sc2 = gt1[0:1], sh2[0:1], sc2[0:1]
    y = yr_ref[...]
    mean = _seg_sum64(y) * (1.0 / HEAD_DIM)
    dlt = y - mean
    var = _seg_sum64(dlt * dlt) * (1.0 / HEAD_DIM)
    yn = dlt * lax.rsqrt(var + GN_EPS) * lnw_ref[...] + lnb_ref[...]
    yr = (yn + bonus_ref[...]) * g_ref[...]
    mix = _dot(jnp.concatenate([yr, ya_ref[...]], axis=1).astype(BF16), wout_ref[...])
    x1 = x_ref[...] + gt1 * _rmsnorm(mix, gpost_ref[...])
    x1_o[...] = x1
    h2 = _rmsnorm(x1, gpre_ref[...]) * (1.0 + sc2) + sh2
    for c in range(D_MODEL // LANES):
        h2_o[:, c, :] = h2[:, LANES * c:LANES * (c + 1)]
    logits = _dot3(h2, rw_ref[...]) + rb_ref[...]
    lane = lax.broadcasted_iota(jnp.int32, logits.shape, 1)
    vals, idxs = [], []
    for _ in range(TOP_K):
        m = jnp.max(logits, axis=1, keepdims=True)
        idx = jnp.min(jnp.where(logits == m, lane, N_EXPERTS), axis=1, keepdims=True)
        vals.append(m)
        idxs.append(idx)
        logits = jnp.where(lane == idx, -jnp.inf, logits)
    e = jnp.exp(jnp.concatenate(vals, axis=1) - vals[0])
    tw_o[...] = e / jnp.sum(e, axis=1, keepdims=True)
    ti_o[...] = jnp.concatenate(idxs, axis=1)


def _post_mix(is_prompt, tm, yr, g, bonus, ya, x, mod, lnw, lnb, wout, gpost, gpre, rw, rb):
    m, d = x.shape
    row = lambda i: (i, 0)
    if is_prompt:
        mod_rows = 8
        mod_map = lambda c: (lambda i: (0, c))
    else:
        mod_rows = tm
        mod_map = lambda c: (lambda i: (i, c))
    wide = pl.BlockSpec((tm, RWKV_WIDTH), row)
    in_specs = [wide, wide, wide, wide, pl.BlockSpec((tm, d), row),
                pl.BlockSpec((mod_rows, d), mod_map(2)), pl.BlockSpec((mod_rows, d), mod_map(3)),
                pl.BlockSpec((mod_rows, d), mod_map(4)),
                _resident(lnw.shape), _resident(lnb.shape), _resident(wout.shape), _resident(gpost.shape),
                _resident(gpre.shape), _resident(rw.shape), _resident(rb.shape)]
    out_shape = [jax.ShapeDtypeStruct((m, d), F32), jax.ShapeDtypeStruct((m, d // LANES, LANES), F32),
                 jax.ShapeDtypeStruct((m, TOP_K), jnp.int32), jax.ShapeDtypeStruct((m, TOP_K), F32)]
    out_specs = [pl.BlockSpec((tm, d), row), pl.BlockSpec((tm, d // LANES, LANES), lambda i: (i, 0, 0)),
                 pl.BlockSpec((tm, TOP_K), row), pl.BlockSpec((tm, TOP_K), row)]
    return pl.pallas_call(
        functools.partial(_post_mix_kernel, is_prompt),
        grid=(m // tm,), in_specs=in_specs, out_specs=out_specs, out_shape=out_shape,
        compiler_params=pltpu.CompilerParams(dimension_semantics=("arbitrary",), vmem_limit_bytes=VMEM_LIMIT),
        name="post_mix_prompt" if is_prompt else "post_mix_sample",
    )(yr, g, bonus, ya, x, mod, mod, mod, lnw, lnb, wout, gpost, gpre, rw, rb)


def _gather_kernel(idx_ref, src_hbm, dst_hbm, sem):
    base = pl.program_id(0) * GATHER_ROWS

    def copy(j):
        return pltpu.make_async_copy(src_hbm.at[idx_ref[0, 0, j]], dst_hbm.at[base + j], sem)

    def issue(j, c):
        copy(j).start()
        return c

    def wait(j, c):
        copy(j).wait()
        return c

    lax.fori_loop(0, GATHER_ROWS, issue, 0)
    lax.fori_loop(0, GATHER_ROWS, wait, 0)


def _gather_rows(src_tok, h2):
    n_rows = src_tok.shape[0]
    nblk = n_rows // GATHER_ROWS
    return pl.pallas_call(
        _gather_kernel,
        grid=(nblk,),
        in_specs=[pl.BlockSpec((1, 1, GATHER_ROWS), lambda i: (i, 0, 0), memory_space=pltpu.SMEM),
                  pl.BlockSpec(memory_space=pl.ANY)],
        out_specs=pl.BlockSpec(memory_space=pl.ANY),
        out_shape=jax.ShapeDtypeStruct((n_rows,) + h2.shape[1:], F32),
        scratch_shapes=[pltpu.SemaphoreType.DMA],
        compiler_params=pltpu.CompilerParams(dimension_semantics=("arbitrary",)),
        name="moe_gather",
    )(src_tok.reshape(nblk, 1, GATHER_ROWS), h2)


def _moe_kernel(ex_ref, rb_ref, ns_ref, xs_hbm, w1_ref, b1g_ref, b1l_ref, w2_ref, b2_ref, ys_hbm,
                xbuf, acc, w1p, w2b, sem_in, sem_out):
    s = pl.program_id(0)
    f = pl.program_id(1)
    ns = ns_ref[s]
    rb = rb_ref[s]
    rows = MOE_ROWS
    grp = 2 * LANES

    def x_copy(j):
        return pltpu.make_async_copy(xs_hbm.at[pl.ds((rb + j) * rows, rows)], xbuf.at[pl.ds(j * rows, rows)], sem_in)

    def y_copy(j):
        return pltpu.make_async_copy(acc.at[pl.ds(j * rows, rows)], ys_hbm.at[pl.ds((rb + j) * rows, rows)], sem_out)

    def for_tiles(fn):
        def body(j, c):
            fn(j)
            return c
        lax.fori_loop(0, ns, body, 0)

    @pl.when(ns > 0)
    def _():
        @pl.when(f == 0)
        def _():
            for_tiles(lambda j: x_copy(j).start())
            b2 = jnp.broadcast_to(b2_ref[0], (rows, D_MODEL))

            def init(j):
                acc[pl.ds(pl.multiple_of(j * rows, rows), rows), :] = b2
            for_tiles(init)
            for_tiles(lambda j: x_copy(j).wait())

        pr = lax.broadcasted_iota(jnp.int32, (grp, grp), 0)
        pc = lax.broadcasted_iota(jnp.int32, (grp, grp), 1)
        perm = jnp.where(pr == jnp.where(pc < LANES, 2 * pc, 2 * (pc - LANES) + 1), 1.0, 0.0).astype(BF16)
        n_grp = 2 * MOE_FT // grp
        for g in range(n_grp):
            gs = slice(grp * g, grp * (g + 1))
            w1p[:, gs] = _dot(w1_ref[0, :, gs].astype(BF16), perm).astype(BF16)
        w2b[...] = w2_ref[0].astype(BF16)
        b1g = b1g_ref[0]
        b1l = b1l_ref[0]

        def tile(j):
            sl = pl.ds(pl.multiple_of(j * rows, rows), rows)
            x = jnp.concatenate([xbuf[sl, c, :] for c in range(D_MODEL // LANES)], axis=1).astype(BF16)
            hh = _dot(x, w1p[...])
            glu = jnp.concatenate([hh[:, grp * g:grp * g + LANES] for g in range(n_grp)], axis=1) + b1g
            lin = jnp.concatenate([hh[:, grp * g + LANES:grp * (g + 1)] for g in range(n_grp)], axis=1) + b1l
            glu = jnp.minimum(glu, SWIGLU_LIMIT)
            lin = jnp.clip(lin, -SWIGLU_LIMIT, SWIGLU_LIMIT)
            act = glu * jax.nn.sigmoid(SWIGLU_ALPHA * glu) * (lin + 1.0)
            acc[sl, :] += _dot(act.astype(BF16), w2b[...])
        for_tiles(tile)

        @pl.when(f == pl.num_programs(1) - 1)
        def _():
            for_tiles(lambda j: y_copy(j).start())
            for_tiles(lambda j: y_copy(j).wait())

    @pl.when((ns < 0) & (f == 0))
    def _():
        acc[0:rows, :] = jnp.zeros((rows, D_MODEL), F32)

        def z_copy(j):
            return pltpu.make_async_copy(acc.at[pl.ds(0, rows)], ys_hbm.at[pl.ds((rb + j) * rows, rows)], sem_out)

        def body(j, c, op):
            op(z_copy(j))
            return c
        lax.fori_loop(0, -ns, functools.partial(body, op=lambda cp: cp.start()), 0)
        lax.fori_loop(0, -ns, functools.partial(body, op=lambda cp: cp.wait()), 0)


def _moe_experts(ex, rb, ns, xs, w1, b1g, b1l, w2, b2):
    n_rows = xs.shape[0]
    n_super = ex.shape[0]
    nf = D_FF // MOE_FT
    last = nf - 1

    def fcol(s, f, ns_):
        return jnp.where(ns_[s] > 0, f, last)

    grid_spec = pltpu.PrefetchScalarGridSpec(
        num_scalar_prefetch=3,
        grid=(n_super, nf),
        in_specs=[pl.BlockSpec(memory_space=pl.ANY),
                  pl.BlockSpec((1, D_MODEL, 2 * MOE_FT), lambda s, f, e_, r_, n_: (e_[s], 0, fcol(s, f, n_))),
                  pl.BlockSpec((1, 1, MOE_FT), lambda s, f, e_, r_, n_: (e_[s], 0, fcol(s, f, n_))),
                  pl.BlockSpec((1, 1, MOE_FT), lambda s, f, e_, r_, n_: (e_[s], 0, fcol(s, f, n_))),
                  pl.BlockSpec((1, MOE_FT, D_MODEL), lambda s, f, e_, r_, n_: (e_[s], fcol(s, f, n_), 0)),
                  pl.BlockSpec((1, 1, D_MODEL), lambda s, f, e_, r_, n_: (e_[s], 0, 0))],
        out_specs=pl.BlockSpec(memory_space=pl.ANY),
        scratch_shapes=[pltpu.VMEM((MOE_SUB * MOE_ROWS, D_MODEL // LANES, LANES), F32),
                        pltpu.VMEM((MOE_SUB * MOE_ROWS, D_MODEL), F32),
                        pltpu.VMEM((D_MODEL, 2 * MOE_FT), BF16),
                        pltpu.VMEM((MOE_FT, D_MODEL), BF16),
                        pltpu.SemaphoreType.DMA, pltpu.SemaphoreType.DMA])
    return pl.pallas_call(
        _moe_kernel,
        grid_spec=grid_spec,
        out_shape=jax.ShapeDtypeStruct((n_rows, D_MODEL), F32),
        compiler_params=pltpu.CompilerParams(dimension_semantics=("arbitrary", "arbitrary"),
                                             vmem_limit_bytes=VMEM_LIMIT),
        name="moe_experts",
    )(ex, rb, ns, xs, w1, b1g, b1l, w2, b2)


def _combine_kernel(is_prompt, idx_ref, ys_hbm, x1_ref, tw_ref, gt2_ref, gpost_ref, o_ref, buf, sem):
    tk = COMBINE_TOK
    n = TOP_K * tk

    def copy(j):
        return pltpu.make_async_copy(ys_hbm.at[pl.ds(idx_ref[0, 0, j], 1)], buf.at[pl.ds(j, 1)], sem)

    def issue(j, c):
        copy(j).start()
        return c

    def wait(j, c):
        copy(j).wait()
        return c

    lax.fori_loop(0, n, issue, 0)
    lax.fori_loop(0, n, wait, 0)
    tw = tw_ref[...]
    f = tw[:, 0:1] * buf[0:tk, :]
    for k in range(1, TOP_K):
        f = f + tw[:, k:k + 1] * buf[k * tk:(k + 1) * tk, :]
    gt2 = gt2_ref[...]
    if is_prompt:
        gt2 = gt2[0:1]
    o_ref[...] = x1_ref[...] + gt2 * _rmsnorm(f, gpost_ref[...])


def _combine(is_prompt, pos, ys, x1, tw, mod, gpost):
    m, d = x1.shape
    tk = COMBINE_TOK
    nblk = m // tk
    idx = pos.reshape(nblk, tk, TOP_K).transpose(0, 2, 1).reshape(nblk, 1, TOP_K * tk)
    row = lambda i: (i, 0)
    mod_spec = (pl.BlockSpec((8, d), lambda i: (0, 5)) if is_prompt else pl.BlockSpec((tk, d), lambda i: (i, 5)))
    return pl.pallas_call(
        functools.partial(_combine_kernel, is_prompt),
        grid=(nblk,),
        in_specs=[pl.BlockSpec((1, 1, TOP_K * tk), lambda i: (i, 0, 0), memory_space=pltpu.SMEM),
                  pl.BlockSpec(memory_space=pl.ANY),
                  pl.BlockSpec((tk, d), row), pl.BlockSpec((tk, TOP_K), row), mod_spec,
                  pl.BlockSpec((1, d), lambda i: (0, 0))],
        out_specs=pl.BlockSpec((tk, d), row),
        out_shape=jax.ShapeDtypeStruct((m, d), F32),
        scratch_shapes=[pltpu.VMEM((TOP_K * tk, d), F32), pltpu.SemaphoreType.DMA],
        compiler_params=pltpu.CompilerParams(dimension_semantics=("arbitrary",)),
        name="moe_combine_prompt" if is_prompt else "moe_combine_sample",
    )(idx, ys, x1, tw, mod, gpost)


def _routing_tables(top_i):
    n_assign = top_i.size
    e_flat = top_i.reshape(-1)
    onehot = (e_flat[:, None] == jnp.arange(N_EXPERTS, dtype=jnp.int32)[None, :]).astype(jnp.int32)
    counts = jnp.sum(onehot, axis=0)
    rank = jnp.sum((jnp.cumsum(onehot, axis=0) - onehot) * onehot, axis=1)
    tiles = (counts + MOE_ROWS - 1) // MOE_ROWS
    tile_start = jnp.cumsum(tiles) - tiles
    pos = tile_start[e_flat] * MOE_ROWS + rank
    n_rows = _padded_rows(n_assign)
    src_tok = jnp.zeros((n_rows,), jnp.int32).at[pos].set(jnp.arange(n_assign, dtype=jnp.int32) // TOP_K)
    supers = (tiles + MOE_SUB - 1) // MOE_SUB
    super_end = jnp.cumsum(supers)
    s_idx = jnp.arange(_max_supers(n_assign), dtype=jnp.int32)
    ex = jnp.minimum(jnp.searchsorted(super_end, s_idx, side='right'), N_EXPERTS - 1).astype(jnp.int32)
    j = s_idx - (super_end - supers)[ex]
    live = s_idx < super_end[-1]
    tail = jnp.sum(tiles) + MOE_SUB * (s_idx - super_end[-1])
    n_clear = jnp.clip(n_rows // MOE_ROWS - tail, 0, MOE_SUB)
    ns = jnp.where(live, jnp.clip(tiles[ex] - MOE_SUB * j, 0, MOE_SUB), -n_clear).astype(jnp.int32)
    rb = jnp.where(live, tile_start[ex] + MOE_SUB * j, tail).astype(jnp.int32)
    last_live = jnp.max(jnp.where(live, ex, 0))
    ex = jnp.where(live, ex, last_live).astype(jnp.int32)
    return pos.astype(jnp.int32), src_tok, ex, rb, ns


def _padded_rows(n_assign):
    rows = n_assign + N_EXPERTS * (MOE_ROWS - 1)
    return -(-rows // GATHER_ROWS) * GATHER_ROWS


def _max_supers(n_assign):
    return N_EXPERTS + -(-_padded_rows(n_assign) // (MOE_ROWS * MOE_SUB))


def kernel(x_prompt, x_sample, cache_k, cache_v, state_wkv, state_shift, c_prompt, c_sample, w_ada, b_ada, g_pre_mix, g_post_mix, g_pre_ffn, g_post_ffn, mu_shift, w_in, rwkv_w0, rwkv_w2, rwkv_a0, rwkv_a2, rwkv_g2, rwkv_k_k, rwkv_k_a, rwkv_r_k, rwkv_ln_w, rwkv_ln_b, attn_sinks, w_out, router_w, router_b, moe_w1, moe_b1, moe_w2, moe_b2):
    assert w_ada.shape[0] == 1, "single-layer step"
    d = D_MODEL
    t = x_prompt.shape[1]
    nb = x_sample.shape[0]
    xp = x_prompt.reshape(t, d)
    xs = x_sample.reshape(nb, d)

    c_all = jnp.concatenate([jnp.broadcast_to(c_prompt, (8, d)), c_sample], axis=0)
    mod = _adaln_mod(c_all, w_ada[0], b_ada[0].reshape(1, 6 * d))
    mod_p, mod_s = mod[:8], mod[8:]

    offs = [0, 1024, 2048, 3072, 3136, 3200, 3360]
    w_in0 = w_in[0]
    pad_to = lambda w, n: jnp.pad(w, ((0, 0), (0, n - w.shape[1])))
    wts = [w_in0[:, offs[0]:offs[1]], w_in0[:, offs[1]:offs[2]], w_in0[:, offs[2]:offs[3]],
           pad_to(w_in0[:, offs[3]:offs[4]], 128), pad_to(w_in0[:, offs[4]:offs[5]], 128),
           pad_to(w_in0[:, offs[5]:offs[6]], 256), w_in0[:, offs[6]:]]
    wts = [w.astype(BF16) for w in wts]
    pad_rows = lambda w, n: jnp.pad(w, ((0, n - w.shape[0]), (0, 0))).astype(BF16)
    row = lambda p: p.reshape(1, -1)
    rw = [row(rwkv_w0[0]), pad_rows(rwkv_w2[0], 128), row(rwkv_a0[0]), pad_rows(rwkv_a2[0], 128),
          pad_rows(rwkv_g2[0], 256), row(rwkv_k_k[0]), row(rwkv_k_a[0]), row(rwkv_r_k[0])]
    gpm = row(g_pre_mix[0])
    mu8 = jnp.pad(mu_shift[0], ((0, 8 - N_SHIFTED), (0, 0)))

    pr = _in_proj(True, 256, xp, xp, mod_p, gpm, mu8, wts, rw)
    sr = _in_proj(False, nb, xs, state_shift[0], mod_s, gpm, mu8, wts, rw)
    r_p, lw_p, kf_p, v_p, na_p, b_p, g_p, bonus_p, q_p, ka_p, va_p, hlast_p = pr
    r_s, lw_s, kf_s, v_s, na_s, b_s, g_s, bonus_s, q_s, ka_s, va_s, h_s = sr

    y_p, st_p = _rwkv_prompt(r_p, lw_p, kf_p, v_p, na_p, b_p)
    y_s, wkv_s = _rwkv_sample(r_s, lw_s, kf_s, v_s, na_s, b_s, state_wkv[0])
    y_s = y_s.reshape(nb, RWKV_WIDTH)
    sinks = attn_sinks[0]
    att_p = _swa_prompt(q_p, ka_p, va_p, sinks.reshape(1, N_Q_HEADS))
    wb = cache_k.shape[2]
    att_s, ck_new, cv_new = _swa_sample(q_s, ka_s, va_s, cache_k[0].reshape(nb, wb, KV_WIDTH),
                                        cache_v[0].reshape(nb, wb, KV_WIDTH), sinks.reshape(N_Q_HEADS, 1))
    att_s = att_s.reshape(nb, ATTN_WIDTH)

    post = [row(rwkv_ln_w[0]), row(rwkv_ln_b[0]), w_out[0].astype(BF16), row(g_post_mix[0]),
            row(g_pre_ffn[0]), router_w[0], row(router_b[0])]
    x1_p, h2_p, ti_p, tw_p = _post_mix(True, 256, y_p, g_p, bonus_p, att_p, xp, mod_p, *post)
    x1_s, h2_s, ti_s, tw_s = _post_mix(False, nb, y_s, g_s, bonus_s, att_s, xs, mod_s, *post)

    h2_all = jnp.concatenate([h2_p, h2_s], axis=0)
    pos, src_tok, ex, rb, ns = _routing_tables(jnp.concatenate([ti_p, ti_s], axis=0))
    x_sorted = _gather_rows(src_tok, h2_all).reshape(-1, d // LANES, LANES)
    b1 = moe_b1[0].reshape(N_EXPERTS, 1, D_FF, 2)
    ys_rows = _moe_experts(ex, rb, ns, x_sorted, moe_w1[0], b1[..., 0], b1[..., 1],
                           moe_w2[0], moe_b2[0].reshape(N_EXPERTS, 1, d))
    pos = pos.reshape(-1, TOP_K)
    gpf = row(g_post_ffn[0])
    out_p = _combine(True, pos[:t], ys_rows, x1_p, tw_p, mod_p, gpf)
    out_s = _combine(False, pos[t:], ys_rows, x1_s, tw_s, mod_s, gpf)

    n_keep = min(WINDOW, t)
    st_heads = jnp.stack([st_p[:, :HEAD_DIM, :HEAD_DIM], st_p[:, HEAD_DIM:, HEAD_DIM:]], axis=1)
    return (out_p.reshape(1, t, d),
            out_s.reshape(nb, 1, d),
            ka_p[t - n_keep:].reshape(1, 1, n_keep, N_KV_HEADS, HEAD_DIM),
            va_p[t - n_keep:].reshape(1, 1, n_keep, N_KV_HEADS, HEAD_DIM),
            st_heads.reshape(1, 1, N_RWKV_HEADS, HEAD_DIM, HEAD_DIM),
            hlast_p[7:8].reshape(1, 1, d),
            ck_new.reshape(1, nb, wb, N_KV_HEADS, HEAD_DIM),
            cv_new.reshape(1, nb, wb, N_KV_HEADS, HEAD_DIM),
            wkv_s.reshape(1, nb, N_RWKV_HEADS, HEAD_DIM, HEAD_DIM),
            h_s.reshape(1, nb, d))
```

```python
import functools
import math

import jax
import jax.numpy as jnp
from jax import lax
from jax.experimental import pallas as pl
from jax.experimental.pallas import tpu as pltpu

F32 = jnp.float32
BF16 = jnp.bfloat16

D_MODEL = 2048
HEAD_DIM = 64
RWKV_WIDTH = 1024
N_RWKV_HEADS = 16
ATTN_WIDTH = 1024
N_Q_HEADS = 16
N_KV_HEADS = 2
Q_PER_KV = 8
KV_WIDTH = 128
WINDOW = 128
N_SHIFTED = 6
N_EXPERTS = 32
TOP_K = 4
D_FF = 2048
SWIGLU_ALPHA = 1.702
SWIGLU_LIMIT = 7.0
NORM_EPS = 1e-6
GN_EPS = 64e-5
L2_EPS = 1e-12

LANES = 128
VMEM_LIMIT = 56 * 1024 * 1024

CHUNK = 64
RWKV_ROWS = 512
MOE_ROWS = 256
MOE_SUB = 4
MOE_FT = 256
COMBINE_TOK = 128


def _dot(a, b):
    return jnp.dot(a, b, preferred_element_type=F32)


def _dot_nt(a, b):
    return lax.dot_general(a, b, (((1,), (1,)), ((), ())), preferred_element_type=F32)


def _split_bf16(x):
    hi = x.astype(BF16)
    lo = (x - hi.astype(F32)).astype(BF16)
    return hi, lo


def _dot3(a, b):
    ah, al = _split_bf16(a)
    bh, bl = _split_bf16(b)
    return _dot(ah, bh) + _dot(ah, bl) + _dot(al, bh)


def _seg_sum64(x):
    r = lax.broadcasted_iota(jnp.int32, (LANES, LANES), 0) // HEAD_DIM
    c = lax.broadcasted_iota(jnp.int32, (LANES, LANES), 1) // HEAD_DIM
    bd = jnp.where(r == c, 1.0, 0.0).astype(BF16)
    hi, lo = _split_bf16(x)
    outs = []
    for j in range(x.shape[1] // LANES):
        sl = slice(LANES * j, LANES * (j + 1))
        outs.append(_dot(hi[:, sl], bd) + _dot(lo[:, sl], bd))
    return jnp.concatenate(outs, axis=1)


def _rmsnorm(x, g):
    ms = jnp.mean(x * x, axis=-1, keepdims=True)
    return x * lax.rsqrt(ms + NORM_EPS) * g


def _resident(shape):
    nd = len(shape)
    return pl.BlockSpec(shape, lambda *_: (0,) * nd, pipeline_mode=pl.Buffered(1))


def _mod_kernel(c_ref, w_ref, b_ref, o_ref):
    c = c_ref[...]
    s = c * jax.nn.sigmoid(c)
    o_ref[...] = _dot3(s, w_ref[...]) + b_ref[...]


def _adaln_mod(c, w_ada, b_ada):
    rows, d = c.shape
    n = w_ada.shape[1]
    tn = 512
    return pl.pallas_call(
        _mod_kernel,
        grid=(n // tn,),
        in_specs=[pl.BlockSpec((rows, d), lambda j: (0, 0)),
                  pl.BlockSpec((d, tn), lambda j: (0, j)),
                  pl.BlockSpec((1, tn), lambda j: (0, j))],
        out_specs=pl.BlockSpec((rows, tn), lambda j: (0, j)),
        out_shape=jax.ShapeDtypeStruct((rows, n), F32),
        compiler_params=pltpu.CompilerParams(dimension_semantics=("arbitrary",), vmem_limit_bytes=VMEM_LIMIT),
        name="adaln_mod",
    )(c, w_ada, b_ada)


def _inproj_kernel(is_prompt, tm,
                   x_ref, prev_ref, sh_ref, sc_ref, gpm_ref, mu_ref,
                   wr_ref, wk_ref, wv_ref, wwl_ref, wal_ref, wgl_ref, wqkv_ref,
                   w0_ref, w2_ref, a0_ref, a2_ref, g2_ref, kk_ref, ka_ref, rk_ref,
                   r_o, lw_o, kf_o, v_o, na_o, b_o, g_o, bonus_o, q_o, kat_o, vat_o, h_o):
    i = pl.program_id(0)
    gpm = gpm_ref[...]
    sh = sh_ref[...]
    sc = sc_ref[...]
    if is_prompt:
        sh = sh[0:1]
        sc = sc[0:1]

    def modnorm(x):
        return _rmsnorm(x, gpm) * (1.0 + sc) + sh

    h = modnorm(x_ref[...])
    if is_prompt:
        hp = modnorm(prev_ref[...])[7:8, :]
        hp = jnp.where(i > 0, hp, 0.0)
        row = lax.broadcasted_iota(jnp.int32, h.shape, 0)
        hprev = jnp.where(row == 0, hp, pltpu.roll(h, 1, axis=0))
        h_o[...] = h[tm - 8:tm, :]
    else:
        hprev = prev_ref[...]
        h_o[...] = h
    dx = hprev - h
    mu = mu_ref[...]

    def branch(j, w_ref):
        xi = (h + dx * mu[j:j + 1, :]).astype(BF16)
        return _dot(xi, w_ref[...])

    r = branch(0, wr_ref)
    k = branch(1, wk_ref)
    v = branch(2, wv_ref)
    wl = branch(3, wwl_ref)
    al = branch(4, wal_ref)
    gl = branch(5, wgl_ref)
    qkv = _dot(h.astype(BF16), wqkv_ref[...])
    q_o[...] = qkv[:, :ATTN_WIDTH]
    kat_o[...] = qkv[:, ATTN_WIDTH:ATTN_WIDTH + KV_WIDTH]
    vat_o[...] = qkv[:, ATTN_WIDTH + KV_WIDTH:]

    z = w0_ref[...] + _dot(jnp.tanh(wl).astype(BF16), w2_ref[...])
    w_raw = -jnp.logaddexp(-z, 0.0) - 0.5
    lw_o[...] = -jnp.exp(w_raw)
    a = jax.nn.sigmoid(a0_ref[...] + _dot(al.astype(BF16), a2_ref[...]))
    g_o[...] = _dot(jax.nn.sigmoid(gl).astype(BF16), g2_ref[...])
    kk = k * kk_ref[...]
    kk = kk / jnp.maximum(jnp.sqrt(_seg_sum64(kk * kk)), L2_EPS)
    kf = k * (1.0 + (a - 1.0) * ka_ref[...])
    r_o[...] = r
    kf_o[...] = kf
    v_o[...] = v
    na_o[...] = -kk
    b_o[...] = kk * a
    bonus_o[...] = _seg_sum64(r * kf * rk_ref[...]) * v


def _in_proj(is_prompt, tm, x, prev, mod, gpm, mu8, wts, rw):
    m, d = x.shape
    grid = (m // tm,)
    row = lambda i: (i, 0)
    if is_prompt:
        prev_spec = pl.BlockSpec((8, d), lambda i: (jnp.maximum(i * (tm // 8) - 1, 0), 0))
        mod_rows = 8
        mod_map = lambda c: (lambda i: (0, c))
        h_shape, h_spec = (8, d), pl.BlockSpec((8, d), lambda i: (0, 0))
    else:
        prev_spec = pl.BlockSpec((tm, d), row)
        mod_rows = tm
        mod_map = lambda c: (lambda i: (i, c))
        h_shape, h_spec = (m, d), pl.BlockSpec((tm, d), row)
    in_specs = [pl.BlockSpec((tm, d), row), prev_spec,
                pl.BlockSpec((mod_rows, d), mod_map(0)), pl.BlockSpec((mod_rows, d), mod_map(1)),
                _resident((1, d)), _resident((8, d))]
    in_specs += [_resident(w.shape) for w in wts]
    in_specs += [_resident(p.shape) for p in rw]
    wide = jax.ShapeDtypeStruct((m, RWKV_WIDTH), F32)
    wide_spec = pl.BlockSpec((tm, RWKV_WIDTH), row)
    kv = jax.ShapeDtypeStruct((m, KV_WIDTH), F32)
    kv_spec = pl.BlockSpec((tm, KV_WIDTH), row)
    out_shape = [wide] * 9 + [kv, kv, jax.ShapeDtypeStruct(h_shape, F32)]
    out_specs = [wide_spec] * 9 + [kv_spec, kv_spec, h_spec]
    return pl.pallas_call(
        functools.partial(_inproj_kernel, is_prompt, tm),
        grid=grid, in_specs=in_specs, out_specs=out_specs, out_shape=out_shape,
        compiler_params=pltpu.CompilerParams(dimension_semantics=("arbitrary",), vmem_limit_bytes=VMEM_LIMIT),
        name="in_proj_prompt" if is_prompt else "in_proj_sample",
    )(x, prev, mod, mod, gpm, mu8, *wts, *rw)


def _rwkv_chunk_kernel(r_ref, lw_ref, k_ref, v_ref, a_ref, b_ref, y_ref, s_ref, st_ref):
    t = pl.program_id(1)
    C = CHUNK
    P = 2 * HEAD_DIM

    @pl.when(t == 0)
    def _():
        st_ref[...] = jnp.zeros_like(st_ref)

    ri = lax.broadcasted_iota(jnp.int32, (P, P), 0)
    ci = lax.broadcasted_iota(jnp.int32, (P, P), 1)
    bd = (ri // C) == (ci // C)
    tril_s = bd & ((ri % C) > (ci % C))
    tril_i = bd & ((ri % C) >= (ci % C))
    eye = jnp.where(ri == ci, 1.0, 0.0)
    lane0 = lax.broadcasted_iota(jnp.int32, (C, P), 1) < HEAD_DIM
    trow = lax.broadcasted_iota(jnp.int32, (C, P), 0)

    def stack(x):
        return jnp.concatenate([jnp.where(lane0, x, 0.0), jnp.where(lane0, 0.0, x)], axis=0)

    def dup(x):
        return jnp.concatenate([x, x], axis=0)

    def chunk(c, carry):
        sl = pl.ds(pl.multiple_of(c * C, C), C)
        lw = lw_ref[sl, :]
        cw = lw
        for s in (1, 2, 4, 8, 16, 32):
            cw = cw + jnp.where(trow >= s, pltpu.roll(cw, s, axis=0), 0.0)
        cw_last = cw[C - 1:C, :]
        e_neg = jnp.exp(-cw)
        e_end = jnp.exp(cw_last - cw)
        r = r_ref[sl, :]
        k = k_ref[sl, :]
        v = v_ref[sl, :]
        a = a_ref[sl, :]
        b = b_ref[sl, :]
        a2 = stack(a * jnp.exp(cw - lw))
        r2 = stack(r * jnp.exp(cw))
        v2 = stack(v).astype(BF16)
        lhs = jnp.concatenate([a2, r2], axis=0).astype(BF16)
        rhs = jnp.concatenate([dup(k * e_neg), dup(b * e_neg)], axis=0).astype(BF16)
        gram = _dot_nt(lhs, rhs)
        g_ak = jnp.where(tril_s, gram[0:P, 0:P], 0.0)
        l2 = jnp.where(tril_s, gram[0:P, P:2 * P], 0.0)
        p_rk = jnp.where(tril_i, gram[P:2 * P, 0:P], 0.0)
        p_rb = jnp.where(tril_i, gram[P:2 * P, P:2 * P], 0.0)
        inv = eye + l2
        lp = l2
        for _ in range(5):
            lpb = lp.astype(BF16)
            lp = _dot(lpb, lpb)
            inv = inv + _dot(lp.astype(BF16), inv.astype(BF16))
        st = st_ref[...]
        a_s = _dot_nt(lhs, st.astype(BF16))
        u2 = _dot(inv.astype(BF16), (a_s[0:P] + _dot(g_ak.astype(BF16), v2)).astype(BF16))
        uv = jnp.concatenate([u2.astype(BF16), v2], axis=0)
        y2 = a_s[P:2 * P] + _dot(jnp.concatenate([p_rb, p_rk], axis=1).astype(BF16), uv)
        y_ref[sl, :] = y2[0:C] + y2[C:2 * C]
        kbh = jnp.concatenate([dup(b * e_end), dup(k * e_end)], axis=0).astype(BF16)
        uvt = jnp.concatenate([u2, stack(v)], axis=0).T.astype(BF16)
        st_ref[...] = jnp.where(bd, st * jnp.exp(cw_last) + _dot(uvt, kbh), 0.0)
        return carry

    lax.fori_loop(0, RWKV_ROWS // C, chunk, 0)

    @pl.when(t == pl.num_programs(1) - 1)
    def _():
        s_ref[0] = st_ref[...]


def _rwkv_prompt(r, lw, kf, v, na, b):
    t = r.shape[0]
    n_pairs = RWKV_WIDTH // LANES
    spec = pl.BlockSpec((RWKV_ROWS, LANES), lambda p, i: (i, p))
    return pl.pallas_call(
        _rwkv_chunk_kernel,
        grid=(n_pairs, t // RWKV_ROWS),
        in_specs=[spec] * 6,
        out_specs=[spec, pl.BlockSpec((1, LANES, LANES), lambda p, i: (p, 0, 0))],
        out_shape=[jax.ShapeDtypeStruct((t, RWKV_WIDTH), F32),
                   jax.ShapeDtypeStruct((n_pairs, LANES, LANES), F32)],
        scratch_shapes=[pltpu.VMEM((LANES, LANES), F32)],
        compiler_params=pltpu.CompilerParams(dimension_semantics=("arbitrary", "arbitrary")),
        name="rwkv_chunked",
    )(r, lw, kf, v, na, b)


def _rwkv_step_kernel(bb, r_ref, lw_ref, k_ref, v_ref, a_ref, b_ref, s_ref, y_ref, so_ref):
    n = HEAD_DIM
    eye = jnp.where(lax.broadcasted_iota(jnp.int32, (n, n), 0) == lax.broadcasted_iota(jnp.int32, (n, n), 1), 1.0, 0.0)

    def body(bi, carry):
        rb = r_ref[bi]
        dec = jnp.exp(lw_ref[bi])
        kb = k_ref[bi]
        vb = v_ref[bi]
        ab = a_ref[bi]
        bb_ = b_ref[bi]
        ys = []
        for h in range(N_RWKV_HEADS):
            hs = slice(h, h + 1)
            s = s_ref[bi, h]
            sa = jnp.sum(s * ab[hs], axis=1, keepdims=True)
            vcol = jnp.sum(eye * vb[hs], axis=1, keepdims=True)
            s2 = s * dec[hs] + sa * bb_[hs] + vcol * kb[hs]
            so_ref[bi, h] = s2
            ycol = jnp.sum(s2 * rb[hs], axis=1, keepdims=True)
            ys.append(jnp.sum(eye * ycol, axis=0, keepdims=True))
        y_ref[bi] = jnp.concatenate(ys, axis=0)
        return carry

    lax.fori_loop(0, bb, body, 0)


def _rwkv_sample(r, lw, kf, v, na, b, state):
    nb = r.shape[0]
    bb = 8
    vec = lambda x: x.reshape(nb, N_RWKV_HEADS, HEAD_DIM)
    vspec = pl.BlockSpec((bb, N_RWKV_HEADS, HEAD_DIM), lambda i: (i, 0, 0))
    sspec = pl.BlockSpec((bb, N_RWKV_HEADS, HEAD_DIM, HEAD_DIM), lambda i: (i, 0, 0, 0))
    return pl.pallas_call(
        functools.partial(_rwkv_step_kernel, bb),
        grid=(nb // bb,),
        in_specs=[vspec] * 6 + [sspec],
        out_specs=[vspec, sspec],
        out_shape=[jax.ShapeDtypeStruct((nb, N_RWKV_HEADS, HEAD_DIM), F32),
                   jax.ShapeDtypeStruct(state.shape, F32)],
        compiler_params=pltpu.CompilerParams(dimension_semantics=("arbitrary",)),
        name="rwkv_step",
    )(vec(r), vec(lw), vec(kf), vec(v), vec(na), vec(b), state)


def _alibi_slope(head):
    return 2.0 ** (-8.0 * (head + 1) / N_Q_HEADS)


def _swa_prompt_kernel(q_ref, kc_ref, kp_ref, vc_ref, vp_ref, sink_ref, o_ref):
    n = pl.program_id(0)
    w = WINDOW
    kcat = jnp.concatenate([kp_ref[...], kc_ref[...]], axis=0)
    vcat = jnp.concatenate([vp_ref[...], vc_ref[...]], axis=0)
    lane_k = lax.broadcasted_iota(jnp.int32, kcat.shape, 1) < HEAD_DIM
    kswap = pltpu.roll(kcat, HEAD_DIM, axis=1)
    vswap = pltpu.roll(vcat, HEAD_DIM, axis=1)
    kdup = [jnp.where(lane_k, kcat, kswap).astype(BF16), jnp.where(lane_k, kswap, kcat).astype(BF16)]
    vdup = [jnp.where(lane_k, vcat, vswap).astype(BF16), jnp.where(lane_k, vswap, vcat).astype(BF16)]
    qi = lax.broadcasted_iota(jnp.int32, (w, 2 * w), 0)
    kj = lax.broadcasted_iota(jnp.int32, (w, 2 * w), 1)
    dist = qi + w - kj
    valid = (dist >= 0) & (dist <= w) & ((n > 0) | (kj >= w))
    distf = dist.astype(F32)
    lane_q = lax.broadcasted_iota(jnp.int32, (w, LANES), 1) < HEAD_DIM
    sinks = sink_ref[...]
    for j in range(N_Q_HEADS // 2):
        kvh = (2 * j) // Q_PER_KV
        qp = q_ref[:, LANES * j:LANES * (j + 1)] * (1.0 / math.sqrt(HEAD_DIM))
        q2 = jnp.concatenate([jnp.where(lane_q, qp, 0.0), jnp.where(lane_q, 0.0, qp)], axis=0).astype(BF16)
        s2 = _dot_nt(q2, kdup[kvh])
        halves = []
        for half in range(2):
            head = 2 * j + half
            sink = sinks[0:1, head:head + 1]
            s = jnp.where(valid, s2[w * half:w * (half + 1)] - _alibi_slope(head) * distf, -jnp.inf)
            m = jnp.maximum(jnp.max(s, axis=-1, keepdims=True), sink)
            p = jnp.exp(s - m)
            den = jnp.sum(p, axis=-1, keepdims=True) + jnp.exp(sink - m)
            halves.append(_dot(p.astype(BF16), vdup[kvh]) / den)
        o_ref[:, LANES * j:LANES * (j + 1)] = jnp.where(lane_q, halves[0], halves[1])


def _swa_prompt(q, ka, va, sinks):
    t = q.shape[0]
    w = WINDOW
    cur = lambda n: (n, 0)
    prv = lambda n: (jnp.maximum(n - 1, 0), 0)
    kvs = lambda f: pl.BlockSpec((w, KV_WIDTH), f)
    return pl.pallas_call(
        _swa_prompt_kernel,
        grid=(t // w,),
        in_specs=[pl.BlockSpec((w, ATTN_WIDTH), cur), kvs(cur), kvs(prv), kvs(cur), kvs(prv),
                  pl.BlockSpec((1, N_Q_HEADS), lambda n: (0, 0))],
        out_specs=pl.BlockSpec((w, ATTN_WIDTH), cur),
        out_shape=jax.ShapeDtypeStruct((t, ATTN_WIDTH), F32),
        compiler_params=pltpu.CompilerParams(dimension_semantics=("arbitrary",)),
        name="swa_prompt",
    )(q, ka, ka, va, va, sinks)


def _swa_sample_kernel(q_ref, kn_ref, vn_ref, ck_ref, cv_ref, sink_ref, o_ref, ko_ref, vo_ref):
    wb = ck_ref.shape[1]
    q = q_ref[...] * (1.0 / math.sqrt(HEAD_DIM))
    q2 = jnp.concatenate([q, q], axis=2)
    rowh = lax.broadcasted_iota(jnp.int32, q2.shape, 1) // Q_PER_KV
    laneh = lax.broadcasted_iota(jnp.int32, q2.shape, 2) // HEAD_DIM
    qb = jnp.where(rowh == laneh, q2, 0.0)
    kn = kn_ref[...]
    vn = vn_ref[...]
    ck = ck_ref[...]
    cv = cv_ref[...]
    s = jnp.einsum('bqd,bkd->bqk', qb.astype(BF16), ck.astype(BF16), preferred_element_type=F32)
    s_self = jnp.sum(qb * kn, axis=2, keepdims=True)
    head = lax.broadcasted_iota(jnp.int32, (1, N_Q_HEADS, 1), 1).astype(F32)
    slope = jnp.exp2(-8.0 * (head + 1.0) / N_Q_HEADS)
    dist = (wb - lax.broadcasted_iota(jnp.int32, (1, 1, wb), 2)).astype(F32)
    s = s - slope * dist
    sink = sink_ref[...][None]
    m = jnp.maximum(jnp.maximum(jnp.max(s, axis=2, keepdims=True), s_self), sink)
    p = jnp.exp(s - m)
    p_self = jnp.exp(s_self - m)
    den = jnp.sum(p, axis=2, keepdims=True) + p_self + jnp.exp(sink - m)
    o = jnp.einsum('bqk,bkd->bqd', p.astype(BF16), cv.astype(BF16), preferred_element_type=F32)
    o = (o + p_self * vn) / den
    sel = lax.broadcasted_iota(jnp.int32, (1, N_Q_HEADS, HEAD_DIM), 1) < Q_PER_KV
    o_ref[...] = jnp.where(sel, o[:, :, :HEAD_DIM], o[:, :, HEAD_DIM:])
    ko_ref[:, 0:wb - 1, :] = ck_ref[:, 1:wb, :]
    ko_ref[:, wb - 1:wb, :] = kn
    vo_ref[:, 0:wb - 1, :] = cv_ref[:, 1:wb, :]
    vo_ref[:, wb - 1:wb, :] = vn


def _swa_sample(q, ka, va, cache_k, cache_v, sinks_col):
    nb, wb = cache_k.shape[0], cache_k.shape[1]
    bb = 16
    b3 = lambda i: (i, 0, 0)
    nspec = pl.BlockSpec((bb, 1, KV_WIDTH), b3)
    cspec = pl.BlockSpec((bb, wb, KV_WIDTH), b3)
    qspec = pl.BlockSpec((bb, N_Q_HEADS, HEAD_DIM), b3)
    return pl.pallas_call(
        _swa_sample_kernel,
        grid=(nb // bb,),
        in_specs=[qspec, nspec, nspec, cspec, cspec, pl.BlockSpec((N_Q_HEADS, 1), lambda i: (0, 0))],
        out_specs=[qspec, cspec, cspec],
        out_shape=[jax.ShapeDtypeStruct((nb, N_Q_HEADS, HEAD_DIM), F32),
                   jax.ShapeDtypeStruct(cache_k.shape, F32), jax.ShapeDtypeStruct(cache_v.shape, F32)],
        compiler_params=pltpu.CompilerParams(dimension_semantics=("arbitrary",)),
        name="swa_sample",
    )(q.reshape(nb, N_Q_HEADS, HEAD_DIM), ka.reshape(nb, 1, KV_WIDTH), va.reshape(nb, 1, KV_WIDTH),
      cache_k, cache_v, sinks_col)


def _post_mix_kernel(is_prompt, yr_ref, g_ref, bonus_ref, ya_ref, x_ref, gt1_ref, sh2_ref, sc2_ref,
                     lnw_ref, lnb_ref, wout_ref, gpost_ref, gpre_ref, rw_ref, rb_ref,
                     x1_o, h2_o, ti_o, tw_o):
    gt1 = gt1_ref[...]
    sh2 = sh2_ref[...]
    sc2 = sc2_ref[...]
    if is_prompt:
        gt1, sh2, sc2 = gt1[0:1], sh2[0:1], sc2[0:1]
    y = yr_ref[...]
    mean = _seg_sum64(y) * (1.0 / HEAD_DIM)
    dlt = y - mean
    var = _seg_sum64(dlt * dlt) * (1.0 / HEAD_DIM)
    yn = dlt * lax.rsqrt(var + GN_EPS) * lnw_ref[...] + lnb_ref[...]
    yr = (yn + bonus_ref[...]) * g_ref[...]
    mix = _dot(jnp.concatenate([yr, ya_ref[...]], axis=1).astype(BF16), wout_ref[...])
    x1 = x_ref[...] + gt1 * _rmsnorm(mix, gpost_ref[...])
    x1_o[...] = x1
    h2 = _rmsnorm(x1, gpre_ref[...]) * (1.0 + sc2) + sh2
    h2_o[...] = h2
    logits = _dot3(h2, rw_ref[...]) + rb_ref[...]
    lane = lax.broadcasted_iota(jnp.int32, logits.shape, 1)
    vals, idxs = [], []
    for _ in range(TOP_K):
        m = jnp.max(logits, axis=1, keepdims=True)
        idx = jnp.min(jnp.where(logits == m, lane, N_EXPERTS), axis=1, keepdims=True)
        vals.append(m)
        idxs.append(idx)
        logits = jnp.where(lane == idx, -jnp.inf, logits)
    e = jnp.exp(jnp.concatenate(vals, axis=1) - vals[0])
    tw_o[...] = e / jnp.sum(e, axis=1, keepdims=True)
    ti_o[...] = jnp.concatenate(idxs, axis=1)


def _post_mix(is_prompt, tm, yr, g, bonus, ya, x, mod, lnw, lnb, wout, gpost, gpre, rw, rb):
    m, d = x.shape
    row = lambda i: (i, 0)
    if is_prompt:
        mod_rows = 8
        mod_map = lambda c: (lambda i: (0, c))
    else:
        mod_rows = tm
        mod_map = lambda c: (lambda i: (i, c))
    wide = pl.BlockSpec((tm, RWKV_WIDTH), row)
    in_specs = [wide, wide, wide, wide, pl.BlockSpec((tm, d), row),
                pl.BlockSpec((mod_rows, d), mod_map(2)), pl.BlockSpec((mod_rows, d), mod_map(3)),
                pl.BlockSpec((mod_rows, d), mod_map(4)),
                _resident(lnw.shape), _resident(lnb.shape), _resident(wout.shape), _resident(gpost.shape),
                _resident(gpre.shape), _resident(rw.shape), _resident(rb.shape)]
    out_shape = [jax.ShapeDtypeStruct((m, d), F32), jax.ShapeDtypeStruct((m, d), F32),
                 jax.ShapeDtypeStruct((m, TOP_K), jnp.int32), jax.ShapeDtypeStruct((m, TOP_K), F32)]
    out_specs = [pl.BlockSpec((tm, d), row), pl.BlockSpec((tm, d), row),
                 pl.BlockSpec((tm, TOP_K), row), pl.BlockSpec((tm, TOP_K), row)]
    return pl.pallas_call(
        functools.partial(_post_mix_kernel, is_prompt),
        grid=(m // tm,), in_specs=in_specs, out_specs=out_specs, out_shape=out_shape,
        compiler_params=pltpu.CompilerParams(dimension_semantics=("arbitrary",), vmem_limit_bytes=VMEM_LIMIT),
        name="post_mix_prompt" if is_prompt else "post_mix_sample",
    )(yr, g, bonus, ya, x, mod, mod, mod, lnw, lnb, wout, gpost, gpre, rw, rb)


def _moe_kernel(ex_ref, rb_ref, ns_ref, tok_ref, h2_hbm, w1_ref, b1g_ref, b1l_ref, w2_ref, b2_ref, ys_hbm,
                xbuf, xb, acc, w1p, w2b, sem_in, sem_out):
    s = pl.program_id(0)
    f = pl.program_id(1)
    n_super = pl.num_programs(0)
    nf = pl.num_programs(1)
    ns = ns_ref[s]
    rb = rb_ref[s]
    rows = MOE_ROWS
    grp = 2 * LANES
    slot = s % 2

    def gather(sup, buf_slot, lo, hi, start):
        base = rb_ref[sup] * rows

        def body(j, c):
            cp = pltpu.make_async_copy(h2_hbm.at[pl.ds(tok_ref[base + j], 1)], xbuf.at[buf_slot, pl.ds(j, 1)],
                                       sem_in.at[buf_slot])
            if start:
                cp.start()
            else:
                cp.wait()
            return c
        lax.fori_loop(lo, hi, body, 0)

    def y_copy(j):
        return pltpu.make_async_copy(acc.at[pl.ds(j * rows, rows)], ys_hbm.at[pl.ds((rb + j) * rows, rows)], sem_out)

    def for_tiles(fn):
        def body(j, c):
            fn(j)
            return c
        lax.fori_loop(0, ns, body, 0)

    @pl.when(ns > 0)
    def _():
        @pl.when(f == 0)
        def _():
            @pl.when(s == 0)
            def _():
                gather(0, 0, 0, ns * rows, True)
            b2 = jnp.broadcast_to(b2_ref[0], (rows, D_MODEL))

            def init(j):
                acc[pl.ds(pl.multiple_of(j * rows, rows), rows), :] = b2
            for_tiles(init)
            gather(s, slot, 0, ns * rows, False)

            def cast(j):
                sl = pl.ds(pl.multiple_of(j * rows, rows), rows)
                xb[sl, :] = xbuf[slot, sl, :].astype(BF16)
            for_tiles(cast)

        nxt = jnp.minimum(s + 1, n_super - 1)
        share = jnp.where(s + 1 < n_super, jnp.maximum(ns_ref[nxt], 0), 0) * (rows // (D_FF // MOE_FT))
        gather(nxt, 1 - slot, f * share, (f + 1) * share, True)

        pr = lax.broadcasted_iota(jnp.int32, (grp, grp), 0)
        pc = lax.broadcasted_iota(jnp.int32, (grp, grp), 1)
        perm = jnp.where(pr == jnp.where(pc < LANES, 2 * pc, 2 * (pc - LANES) + 1), 1.0, 0.0).astype(BF16)
        n_grp = 2 * MOE_FT // grp
        for g in range(n_grp):
            gs = slice(grp * g, grp * (g + 1))
            w1p[:, gs] = _dot(w1_ref[0, :, gs].astype(BF16), perm).astype(BF16)
        w2b[...] = w2_ref[0].astype(BF16)
        b1g = b1g_ref[0]
        b1l = b1l_ref[0]

        def tile(j):
            sl = pl.ds(pl.multiple_of(j * rows, rows), rows)
            hh = _dot(xb[sl, :], w1p[...])
            glu = jnp.concatenate([hh[:, grp * g:grp * g + LANES] for g in range(n_grp)], axis=1) + b1g
            lin = jnp.concatenate([hh[:, grp * g + LANES:grp * (g + 1)] for g in range(n_grp)], axis=1) + b1l
            glu = jnp.minimum(glu, SWIGLU_LIMIT)
            lin = jnp.clip(lin, -SWIGLU_LIMIT, SWIGLU_LIMIT)
            act = glu * jax.nn.sigmoid(SWIGLU_ALPHA * glu) * (lin + 1.0)
            acc[sl, :] += _dot(act.astype(BF16), w2b[...])
        for_tiles(tile)

        @pl.when(f == pl.num_programs(1) - 1)
        def _():
            for_tiles(lambda j: y_copy(j).start())
            for_tiles(lambda j: y_copy(j).wait())

    @pl.when((ns < 0) & (f == 0))
    def _():
        acc[0:rows, :] = jnp.zeros((rows, D_MODEL), F32)

        def z_copy(j):
            return pltpu.make_async_copy(acc.at[pl.ds(0, rows)], ys_hbm.at[pl.ds((rb + j) * rows, rows)], sem_out)

        def body(j, c, op):
            op(z_copy(j))
            return c
        lax.fori_loop(0, -ns, functools.partial(body, op=lambda cp: cp.start()), 0)
        lax.fori_loop(0, -ns, functools.partial(body, op=lambda cp: cp.wait()), 0)


def _moe_experts(ex, rb, ns, src_tok, h2, w1, b1g, b1l, w2, b2):
    n_rows = src_tok.shape[0]
    n_super = ex.shape[0]
    nf = D_FF // MOE_FT
    last = nf - 1
    sub_rows = MOE_SUB * MOE_ROWS

    def fcol(s, f, ns_):
        return jnp.where(ns_[s] > 0, f, last)

    grid_spec = pltpu.PrefetchScalarGridSpec(
        num_scalar_prefetch=4,
        grid=(n_super, nf),
        in_specs=[pl.BlockSpec(memory_space=pl.ANY),
                  pl.BlockSpec((1, D_MODEL, 2 * MOE_FT), lambda s, f, e_, r_, n_, t_: (e_[s], 0, fcol(s, f, n_))),
                  pl.BlockSpec((1, 1, MOE_FT), lambda s, f, e_, r_, n_, t_: (e_[s], 0, fcol(s, f, n_))),
                  pl.BlockSpec((1, 1, MOE_FT), lambda s, f, e_, r_, n_, t_: (e_[s], 0, fcol(s, f, n_))),
                  pl.BlockSpec((1, MOE_FT, D_MODEL), lambda s, f, e_, r_, n_, t_: (e_[s], fcol(s, f, n_), 0)),
                  pl.BlockSpec((1, 1, D_MODEL), lambda s, f, e_, r_, n_, t_: (e_[s], 0, 0))],
        out_specs=pl.BlockSpec(memory_space=pl.ANY),
        scratch_shapes=[pltpu.VMEM((2, sub_rows, D_MODEL), F32),
                        pltpu.VMEM((sub_rows, D_MODEL), BF16),
                        pltpu.VMEM((sub_rows, D_MODEL), F32),
                        pltpu.VMEM((D_MODEL, 2 * MOE_FT), BF16),
                        pltpu.VMEM((MOE_FT, D_MODEL), BF16),
                        pltpu.SemaphoreType.DMA((2,)), pltpu.SemaphoreType.DMA])
    return pl.pallas_call(
        _moe_kernel,
        grid_spec=grid_spec,
        out_shape=jax.ShapeDtypeStruct((n_rows, D_MODEL), F32),
        compiler_params=pltpu.CompilerParams(dimension_semantics=("arbitrary", "arbitrary"),
                                             vmem_limit_bytes=VMEM_LIMIT),
        name="moe_experts",
    )(ex, rb, ns, src_tok, h2, w1, b1g, b1l, w2, b2)


def _combine_kernel(is_prompt, idx_ref, ys_hbm, x1_ref, tw_ref, gt2_ref, gpost_ref, o_ref, buf, sem):
    tk = COMBINE_TOK
    n = TOP_K * tk

    def copy(j):
        return pltpu.make_async_copy(ys_hbm.at[pl.ds(idx_ref[0, 0, j], 1)], buf.at[pl.ds(j, 1)], sem)

    def issue(j, c):
        copy(j).start()
        return c

    def wait(j, c):
        copy(j).wait()
        return c

    lax.fori_loop(0, n, issue, 0)
    lax.fori_loop(0, n, wait, 0)
    tw = tw_ref[...]
    f = tw[:, 0:1] * buf[0:tk, :]
    for k in range(1, TOP_K):
        f = f + tw[:, k:k + 1] * buf[k * tk:(k + 1) * tk, :]
    gt2 = gt2_ref[...]
    if is_prompt:
        gt2 = gt2[0:1]
    o_ref[...] = x1_ref[...] + gt2 * _rmsnorm(f, gpost_ref[...])


def _combine(is_prompt, pos, ys, x1, tw, mod, gpost):
    m, d = x1.shape
    tk = COMBINE_TOK
    nblk = m // tk
    idx = pos.reshape(nblk, tk, TOP_K).transpose(0, 2, 1).reshape(nblk, 1, TOP_K * tk)
    row = lambda i: (i, 0)
    mod_spec = (pl.BlockSpec((8, d), lambda i: (0, 5)) if is_prompt else pl.BlockSpec((tk, d), lambda i: (i, 5)))
    return pl.pallas_call(
        functools.partial(_combine_kernel, is_prompt),
        grid=(nblk,),
        in_specs=[pl.BlockSpec((1, 1, TOP_K * tk), lambda i: (i, 0, 0), memory_space=pltpu.SMEM),
                  pl.BlockSpec(memory_space=pl.ANY),
                  pl.BlockSpec((tk, d), row), pl.BlockSpec((tk, TOP_K), row), mod_spec,
                  pl.BlockSpec((1, d), lambda i: (0, 0))],
        out_specs=pl.BlockSpec((tk, d), row),
        out_shape=jax.ShapeDtypeStruct((m, d), F32),
        scratch_shapes=[pltpu.VMEM((TOP_K * tk, d), F32), pltpu.SemaphoreType.DMA],
        compiler_params=pltpu.CompilerParams(dimension_semantics=("arbitrary",)),
        name="moe_combine_prompt" if is_prompt else "moe_combine_sample",
    )(idx, ys, x1, tw, mod, gpost)


def _routing_tables(top_i):
    n_assign = top_i.size
    e_flat = top_i.reshape(-1)
    onehot = (e_flat[:, None] == jnp.arange(N_EXPERTS, dtype=jnp.int32)[None, :]).astype(jnp.int32)
    counts = jnp.sum(onehot, axis=0)
    rank = jnp.sum((jnp.cumsum(onehot, axis=0) - onehot) * onehot, axis=1)
    tiles = (counts + MOE_ROWS - 1) // MOE_ROWS
    tile_start = jnp.cumsum(tiles) - tiles
    pos = tile_start[e_flat] * MOE_ROWS + rank
    n_rows = _padded_rows(n_assign)
    src_tok = jnp.zeros((n_rows,), jnp.int32).at[pos].set(jnp.arange(n_assign, dtype=jnp.int32) // TOP_K)
    supers = (tiles + MOE_SUB - 1) // MOE_SUB
    super_end = jnp.cumsum(supers)
    s_idx = jnp.arange(_max_supers(n_assign), dtype=jnp.int32)
    ex = jnp.minimum(jnp.searchsorted(super_end, s_idx, side='right'), N_EXPERTS - 1).astype(jnp.int32)
    j = s_idx - (super_end - supers)[ex]
    live = s_idx < super_end[-1]
    tail = jnp.sum(tiles) + MOE_SUB * (s_idx - super_end[-1])
    n_clear = jnp.clip(n_rows // MOE_ROWS - tail, 0, MOE_SUB)
    ns = jnp.where(live, jnp.clip(tiles[ex] - MOE_SUB * j, 0, MOE_SUB), -n_clear).astype(jnp.int32)
    rb = jnp.where(live, tile_start[ex] + MOE_SUB * j, tail).astype(jnp.int32)
    last_live = jnp.max(jnp.where(live, ex, 0))
    ex = jnp.where(live, ex, last_live).astype(jnp.int32)
    return pos.astype(jnp.int32), src_tok, ex, rb, ns


def _padded_rows(n_assign):
    rows = n_assign + N_EXPERTS * (MOE_ROWS - 1)
    return -(-rows // MOE_ROWS) * MOE_ROWS


def _max_supers(n_assign):
    return N_EXPERTS + -(-_padded_rows(n_assign) // (MOE_ROWS * MOE_SUB))


def kernel(x_prompt, x_sample, cache_k, cache_v, state_wkv, state_shift, c_prompt, c_sample, w_ada, b_ada, g_pre_mix, g_post_mix, g_pre_ffn, g_post_ffn, mu_shift, w_in, rwkv_w0, rwkv_w2, rwkv_a0, rwkv_a2, rwkv_g2, rwkv_k_k, rwkv_k_a, rwkv_r_k, rwkv_ln_w, rwkv_ln_b, attn_sinks, w_out, router_w, router_b, moe_w1, moe_b1, moe_w2, moe_b2):
    assert w_ada.shape[0] == 1, "single-layer step"
    d = D_MODEL
    t = x_prompt.shape[1]
    nb = x_sample.shape[0]
    xp = x_prompt.reshape(t, d)
    xs = x_sample.reshape(nb, d)

    c_all = jnp.concatenate([jnp.broadcast_to(c_prompt, (8, d)), c_sample], axis=0)
    mod = _adaln_mod(c_all, w_ada[0], b_ada[0].reshape(1, 6 * d))
    mod_p, mod_s = mod[:8], mod[8:]

    offs = [0, 1024, 2048, 3072, 3136, 3200, 3360]
    w_in0 = w_in[0]
    pad_to = lambda w, n: jnp.pad(w, ((0, 0), (0, n - w.shape[1])))
    wts = [w_in0[:, offs[0]:offs[1]], w_in0[:, offs[1]:offs[2]], w_in0[:, offs[2]:offs[3]],
           pad_to(w_in0[:, offs[3]:offs[4]], 128), pad_to(w_in0[:, offs[4]:offs[5]], 128),
           pad_to(w_in0[:, offs[5]:offs[6]], 256), w_in0[:, offs[6]:]]
    wts = [w.astype(BF16) for w in wts]
    pad_rows = lambda w, n: jnp.pad(w, ((0, n - w.shape[0]), (0, 0))).astype(BF16)
    row = lambda p: p.reshape(1, -1)
    rw = [row(rwkv_w0[0]), pad_rows(rwkv_w2[0], 128), row(rwkv_a0[0]), pad_rows(rwkv_a2[0], 128),
          pad_rows(rwkv_g2[0], 256), row(rwkv_k_k[0]), row(rwkv_k_a[0]), row(rwkv_r_k[0])]
    gpm = row(g_pre_mix[0])
    mu8 = jnp.pad(mu_shift[0], ((0, 8 - N_SHIFTED), (0, 0)))

    pr = _in_proj(True, 256, xp, xp, mod_p, gpm, mu8, wts, rw)
    sr = _in_proj(False, nb, xs, state_shift[0], mod_s, gpm, mu8, wts, rw)
    r_p, lw_p, kf_p, v_p, na_p, b_p, g_p, bonus_p, q_p, ka_p, va_p, hlast_p = pr
    r_s, lw_s, kf_s, v_s, na_s, b_s, g_s, bonus_s, q_s, ka_s, va_s, h_s = sr

    y_p, st_p = _rwkv_prompt(r_p, lw_p, kf_p, v_p, na_p, b_p)
    y_s, wkv_s = _rwkv_sample(r_s, lw_s, kf_s, v_s, na_s, b_s, state_wkv[0])
    y_s = y_s.reshape(nb, RWKV_WIDTH)
    sinks = attn_sinks[0]
    att_p = _swa_prompt(q_p, ka_p, va_p, sinks.reshape(1, N_Q_HEADS))
    wb = cache_k.shape[2]
    att_s, ck_new, cv_new = _swa_sample(q_s, ka_s, va_s, cache_k[0].reshape(nb, wb, KV_WIDTH),
                                        cache_v[0].reshape(nb, wb, KV_WIDTH), sinks.reshape(N_Q_HEADS, 1))
    att_s = att_s.reshape(nb, ATTN_WIDTH)

    post = [row(rwkv_ln_w[0]), row(rwkv_ln_b[0]), w_out[0].astype(BF16), row(g_post_mix[0]),
            row(g_pre_ffn[0]), router_w[0], row(router_b[0])]
    x1_p, h2_p, ti_p, tw_p = _post_mix(True, 256, y_p, g_p, bonus_p, att_p, xp, mod_p, *post)
    x1_s, h2_s, ti_s, tw_s = _post_mix(False, nb, y_s, g_s, bonus_s, att_s, xs, mod_s, *post)

    h2_all = jnp.concatenate([h2_p, h2_s], axis=0)
    pos, src_tok, ex, rb, ns = _routing_tables(jnp.concatenate([ti_p, ti_s], axis=0))
    b1 = moe_b1[0].reshape(N_EXPERTS, 1, D_FF, 2)
    ys_rows = _moe_experts(ex, rb, ns, src_tok, h2_all, moe_w1[0], b1[..., 0], b1[..., 1],
                           moe_w2[0], moe_b2[0].reshape(N_EXPERTS, 1, d))
    pos = pos.reshape(-1, TOP_K)
    gpf = row(g_post_ffn[0])
    out_p = _combine(True, pos[:t], ys_rows, x1_p, tw_p, mod_p, gpf)
    out_s = _combine(False, pos[t:], ys_rows, x1_s, tw_s, mod_s, gpf)

    n_keep = min(WINDOW, t)
    st_heads = jnp.stack([st_p[:, :HEAD_DIM, :HEAD_DIM], st_p[:, HEAD_DIM:, HEAD_DIM:]], axis=1)
    return (out_p.reshape(1, t, d),
            out_s.reshape(nb, 1, d),
            ka_p[t - n_keep:].reshape(1, 1, n_keep, N_KV_HEADS, HEAD_DIM),
            va_p[t - n_keep:].reshape(1, 1, n_keep, N_KV_HEADS, HEAD_DIM),
            st_heads.reshape(1, 1, N_RWKV_HEADS, HEAD_DIM, HEAD_DIM),
            hlast_p[7:8].reshape(1, 1, d),
            ck_new.reshape(1, nb, wb, N_KV_HEADS, HEAD_DIM),
            cv_new.reshape(1, nb, wb, N_KV_HEADS, HEAD_DIM),
            wkv_s.reshape(1, nb, N_RWKV_HEADS, HEAD_DIM, HEAD_DIM),
            h_s.reshape(1, nb, d))
```

```python
import functools
import math

import jax
import jax.numpy as jnp
from jax import lax
from jax.experimental import pallas as pl
from jax.experimental.pallas import tpu as pltpu

F32 = jnp.float32
BF16 = jnp.bfloat16

D_MODEL = 2048
HEAD_DIM = 64
RWKV_WIDTH = 1024
N_RWKV_HEADS = 16
ATTN_WIDTH = 1024
N_Q_HEADS = 16
N_KV_HEADS = 2
Q_PER_KV = 8
KV_WIDTH = 128
WINDOW = 128
N_SHIFTED = 6
N_EXPERTS = 32
TOP_K = 4
D_FF = 2048
SWIGLU_ALPHA = 1.702
SWIGLU_LIMIT = 7.0
NORM_EPS = 1e-6
GN_EPS = 64e-5
L2_EPS = 1e-12

LANES = 128
VMEM_LIMIT = 56 * 1024 * 1024

CHUNK = 64
RWKV_ROWS = 256
RWKV_PAIRS = 8
MOE_ROWS = 256
MOE_SUB = 6
MOE_FT = 256
MOE_NF = D_FF // MOE_FT
COMBINE_TOK = 128


def _dot(a, b):
    return jnp.dot(a, b, preferred_element_type=F32)


def _dot_nt(a, b):
    return lax.dot_general(a, b, (((1,), (1,)), ((), ())), preferred_element_type=F32)


def _split_bf16(x):
    hi = x.astype(BF16)
    lo = (x - hi.astype(F32)).astype(BF16)
    return hi, lo


def _dot3(a, b):
    ah, al = _split_bf16(a)
    bh, bl = _split_bf16(b)
    return _dot(ah, bh) + _dot(ah, bl) + _dot(al, bh)


def _seg_sum64(x):
    r = lax.broadcasted_iota(jnp.int32, (LANES, LANES), 0) // HEAD_DIM
    c = lax.broadcasted_iota(jnp.int32, (LANES, LANES), 1) // HEAD_DIM
    bd = jnp.where(r == c, 1.0, 0.0).astype(BF16)
    hi, lo = _split_bf16(x)
    outs = []
    for j in range(x.shape[1] // LANES):
        sl = slice(LANES * j, LANES * (j + 1))
        outs.append(_dot(hi[:, sl], bd) + _dot(lo[:, sl], bd))
    return jnp.concatenate(outs, axis=1)


def _rmsnorm(x, g):
    ms = jnp.mean(x * x, axis=-1, keepdims=True)
    return x * lax.rsqrt(ms + NORM_EPS) * g


def _resident(shape):
    nd = len(shape)
    return pl.BlockSpec(shape, lambda *_: (0,) * nd, pipeline_mode=pl.Buffered(1))


def _mod_kernel(c_ref, w_ref, b_ref, o_ref):
    c = c_ref[...]
    s = c * jax.nn.sigmoid(c)
    o_ref[...] = _dot3(s, w_ref[...]) + b_ref[...]


def _adaln_mod(c, w_ada, b_ada):
    rows, d = c.shape
    n = w_ada.shape[1]
    tn = 512
    return pl.pallas_call(
        _mod_kernel,
        grid=(n // tn,),
        in_specs=[pl.BlockSpec((rows, d), lambda j: (0, 0)),
                  pl.BlockSpec((d, tn), lambda j: (0, j)),
                  pl.BlockSpec((1, tn), lambda j: (0, j))],
        out_specs=pl.BlockSpec((rows, tn), lambda j: (0, j)),
        out_shape=jax.ShapeDtypeStruct((rows, n), F32),
        compiler_params=pltpu.CompilerParams(dimension_semantics=("arbitrary",), vmem_limit_bytes=VMEM_LIMIT),
        name="adaln_mod",
    )(c, w_ada, b_ada)


def _inproj_kernel(is_prompt, tm,
                   x_ref, prev_ref, sh_ref, sc_ref, gpm_ref, mu_ref,
                   wr_ref, wk_ref, wv_ref, wwl_ref, wal_ref, wgl_ref, wqkv_ref,
                   w0_ref, w2_ref, a0_ref, a2_ref, g2_ref, kk_ref, ka_ref, rk_ref,
                   r_o, lw_o, kf_o, v_o, na_o, b_o, g_o, bonus_o, q_o, kat_o, vat_o, h_o):
    i = pl.program_id(0)
    gpm = gpm_ref[...]
    sh = sh_ref[...]
    sc = sc_ref[...]
    if is_prompt:
        sh = sh[0:1]
        sc = sc[0:1]

    def modnorm(x):
        return _rmsnorm(x, gpm) * (1.0 + sc) + sh

    h = modnorm(x_ref[...])
    if is_prompt:
        hp = modnorm(prev_ref[...])[7:8, :]
        hp = jnp.where(i > 0, hp, 0.0)
        row = lax.broadcasted_iota(jnp.int32, h.shape, 0)
        hprev = jnp.where(row == 0, hp, pltpu.roll(h, 1, axis=0))
        h_o[...] = h[tm - 8:tm, :]
    else:
        hprev = prev_ref[...]
        h_o[...] = h
    dx = hprev - h
    mu = mu_ref[...]

    def branch(j, w_ref):
        xi = (h + dx * mu[j:j + 1, :]).astype(BF16)
        return _dot(xi, w_ref[...])

    r = branch(0, wr_ref)
    k = branch(1, wk_ref)
    v = branch(2, wv_ref)
    wl = branch(3, wwl_ref)
    al = branch(4, wal_ref)
    gl = branch(5, wgl_ref)
    qkv = _dot(h.astype(BF16), wqkv_ref[...])
    q_o[...] = qkv[:, :ATTN_WIDTH]
    kat_o[...] = qkv[:, ATTN_WIDTH:ATTN_WIDTH + KV_WIDTH]
    vat_o[...] = qkv[:, ATTN_WIDTH + KV_WIDTH:]

    z = w0_ref[...] + _dot(jnp.tanh(wl).astype(BF16), w2_ref[...])
    w_raw = -jnp.logaddexp(-z, 0.0) - 0.5
    lw_o[...] = -jnp.exp(w_raw)
    a = jax.nn.sigmoid(a0_ref[...] + _dot(al.astype(BF16), a2_ref[...]))
    g_o[...] = _dot(jax.nn.sigmoid(gl).astype(BF16), g2_ref[...])
    kk = k * kk_ref[...]
    kk = kk / jnp.maximum(jnp.sqrt(_seg_sum64(kk * kk)), L2_EPS)
    kf = k * (1.0 + (a - 1.0) * ka_ref[...])
    r_o[...] = r
    kf_o[...] = kf
    v_o[...] = v
    na_o[...] = -kk
    b_o[...] = kk * a
    bonus_o[...] = _seg_sum64(r * kf * rk_ref[...]) * v


def _in_proj(is_prompt, tm, x, prev, mod, gpm, mu8, wts, rw):
    m, d = x.shape
    grid = (m // tm,)
    row = lambda i: (i, 0)
    if is_prompt:
        prev_spec = pl.BlockSpec((8, d), lambda i: (jnp.maximum(i * (tm // 8) - 1, 0), 0))
        mod_rows = 8
        mod_map = lambda c: (lambda i: (0, c))
        h_shape, h_spec = (8, d), pl.BlockSpec((8, d), lambda i: (0, 0))
    else:
        prev_spec = pl.BlockSpec((tm, d), row)
        mod_rows = tm
        mod_map = lambda c: (lambda i: (i, c))
        h_shape, h_spec = (m, d), pl.BlockSpec((tm, d), row)
    in_specs = [pl.BlockSpec((tm, d), row), prev_spec,
                pl.BlockSpec((mod_rows, d), mod_map(0)), pl.BlockSpec((mod_rows, d), mod_map(1)),
                _resident((1, d)), _resident((8, d))]
    in_specs += [_resident(w.shape) for w in wts]
    in_specs += [_resident(p.shape) for p in rw]
    wide = jax.ShapeDtypeStruct((m, RWKV_WIDTH), F32)
    wide_spec = pl.BlockSpec((tm, RWKV_WIDTH), row)
    kv = jax.ShapeDtypeStruct((m, KV_WIDTH), F32)
    kv_spec = pl.BlockSpec((tm, KV_WIDTH), row)
    out_shape = [wide] * 9 + [kv, kv, jax.ShapeDtypeStruct(h_shape, F32)]
    out_specs = [wide_spec] * 9 + [kv_spec, kv_spec, h_spec]
    return pl.pallas_call(
        functools.partial(_inproj_kernel, is_prompt, tm),
        grid=grid, in_specs=in_specs, out_specs=out_specs, out_shape=out_shape,
        compiler_params=pltpu.CompilerParams(dimension_semantics=("arbitrary",), vmem_limit_bytes=VMEM_LIMIT),
        name="in_proj_prompt" if is_prompt else "in_proj_sample",
    )(x, prev, mod, mod, gpm, mu8, *wts, *rw)


def _rwkv_chunk_kernel(r_ref, lw_ref, k_ref, v_ref, a_ref, b_ref, y_ref, s_ref, st_ref):
    t = pl.program_id(1)
    C = CHUNK
    P = 2 * HEAD_DIM

    @pl.when(t == 0)
    def _():
        st_ref[...] = jnp.zeros_like(st_ref)

    ri = lax.broadcasted_iota(jnp.int32, (P, P), 0)
    ci = lax.broadcasted_iota(jnp.int32, (P, P), 1)
    bd = (ri // C) == (ci // C)
    tril_s = bd & ((ri % C) > (ci % C))
    tril_i = bd & ((ri % C) >= (ci % C))
    eye = jnp.where(ri == ci, 1.0, 0.0)
    lane0 = lax.broadcasted_iota(jnp.int32, (C, P), 1) < HEAD_DIM
    trow = lax.broadcasted_iota(jnp.int32, (C, P), 0)

    def stack(x):
        return jnp.concatenate([jnp.where(lane0, x, 0.0), jnp.where(lane0, 0.0, x)], axis=0)

    def dup(x):
        return jnp.concatenate([x, x], axis=0)

    def prep(sl, pp):
        ln = slice(P * pp, P * (pp + 1))
        lw = lw_ref[sl, ln]
        cw = lw
        for s in (1, 2, 4, 8, 16, 32):
            cw = cw + jnp.where(trow >= s, pltpu.roll(cw, s, axis=0), 0.0)
        cw_last = cw[C - 1:C, :]
        e_neg = jnp.exp(-cw)
        e_end = jnp.exp(cw_last - cw)
        k = k_ref[sl, ln]
        v = v_ref[sl, ln]
        b = b_ref[sl, ln]
        a2 = stack(a_ref[sl, ln] * jnp.exp(cw - lw))
        r2 = stack(r_ref[sl, ln] * jnp.exp(cw))
        return dict(
            v2=stack(v),
            lhs=jnp.concatenate([a2, r2], axis=0).astype(BF16),
            rhs=jnp.concatenate([dup(k * e_neg), dup(b * e_neg)], axis=0).astype(BF16),
            kbh=jnp.concatenate([dup(b * e_end), dup(k * e_end)], axis=0).astype(BF16),
            decay=jnp.exp(cw_last))

    def chunk(c, carry):
        sl = pl.ds(pl.multiple_of(c * C, C), C)
        pairs = range(RWKV_PAIRS)
        st = [st_ref[pp] for pp in pairs]
        d = [prep(sl, pp) for pp in pairs]
        gram = [_dot_nt(d[pp]['lhs'], d[pp]['rhs']) for pp in pairs]
        l2 = [jnp.where(tril_s, gram[pp][0:P, P:2 * P], 0.0) for pp in pairs]
        a_s = [_dot_nt(d[pp]['lhs'], st[pp].astype(BF16)) for pp in pairs]
        v2b = [d[pp]['v2'].astype(BF16) for pp in pairs]
        rhs_u = [a_s[pp][0:P] + _dot(jnp.where(tril_s, gram[pp][0:P, 0:P], 0.0).astype(BF16), v2b[pp])
                 for pp in pairs]
        inv = [eye + l2[pp] for pp in pairs]
        lp = l2
        for _ in range(5):
            lpb = [lp[pp].astype(BF16) for pp in pairs]
            lp = [_dot(lpb[pp], lpb[pp]) for pp in pairs]
            inv = [inv[pp] + _dot(lp[pp].astype(BF16), inv[pp].astype(BF16)) for pp in pairs]
        u2 = [_dot(inv[pp].astype(BF16), rhs_u[pp].astype(BF16)) for pp in pairs]
        p_cat = [jnp.concatenate([jnp.where(tril_i, gram[pp][P:2 * P, P:2 * P], 0.0),
                                  jnp.where(tril_i, gram[pp][P:2 * P, 0:P], 0.0)], axis=1).astype(BF16)
                 for pp in pairs]
        y2 = [a_s[pp][P:2 * P] + _dot(p_cat[pp], jnp.concatenate([u2[pp].astype(BF16), v2b[pp]], axis=0))
              for pp in pairs]
        uvt = [jnp.concatenate([u2[pp], d[pp]['v2']], axis=0).T.astype(BF16) for pp in pairs]
        st_new = [jnp.where(bd, st[pp] * d[pp]['decay'] + _dot(uvt[pp], d[pp]['kbh']), 0.0) for pp in pairs]
        for pp in pairs:
            y_ref[sl, P * pp:P * (pp + 1)] = y2[pp][0:C] + y2[pp][C:2 * C]
            st_ref[pp] = st_new[pp]
        return carry

    lax.fori_loop(0, RWKV_ROWS // C, chunk, 0)

    @pl.when(t == pl.num_programs(1) - 1)
    def _():
        s_ref[...] = st_ref[...]


def _rwkv_prompt(r, lw, kf, v, na, b):
    t = r.shape[0]
    n_pairs = RWKV_WIDTH // LANES
    width = RWKV_PAIRS * LANES
    spec = pl.BlockSpec((RWKV_ROWS, width), lambda p, i: (i, p))
    return pl.pallas_call(
        _rwkv_chunk_kernel,
        grid=(n_pairs // RWKV_PAIRS, t // RWKV_ROWS),
        in_specs=[spec] * 6,
        out_specs=[spec, pl.BlockSpec((RWKV_PAIRS, LANES, LANES), lambda p, i: (p, 0, 0))],
        out_shape=[jax.ShapeDtypeStruct((t, RWKV_WIDTH), F32),
                   jax.ShapeDtypeStruct((n_pairs, LANES, LANES), F32)],
        scratch_shapes=[pltpu.VMEM((RWKV_PAIRS, LANES, LANES), F32)],
        compiler_params=pltpu.CompilerParams(dimension_semantics=("arbitrary", "arbitrary")),
        name="rwkv_chunked",
    )(r, lw, kf, v, na, b)


def _rwkv_step_kernel(bb, r_ref, lw_ref, k_ref, v_ref, a_ref, b_ref, s_ref, y_ref, so_ref):
    n = HEAD_DIM
    eye = jnp.where(lax.broadcasted_iota(jnp.int32, (n, n), 0) == lax.broadcasted_iota(jnp.int32, (n, n), 1), 1.0, 0.0)

    def body(bi, carry):
        rb = r_ref[bi]
        dec = jnp.exp(lw_ref[bi])
        kb = k_ref[bi]
        vb = v_ref[bi]
        ab = a_ref[bi]
        bb_ = b_ref[bi]
        ys = []
        for h in range(N_RWKV_HEADS):
            hs = slice(h, h + 1)
            s = s_ref[bi, h]
            sa = jnp.sum(s * ab[hs], axis=1, keepdims=True)
            vcol = jnp.sum(eye * vb[hs], axis=1, keepdims=True)
            s2 = s * dec[hs] + sa * bb_[hs] + vcol * kb[hs]
            so_ref[bi, h] = s2
            ycol = jnp.sum(s2 * rb[hs], axis=1, keepdims=True)
            ys.append(jnp.sum(eye * ycol, axis=0, keepdims=True))
        y_ref[bi] = jnp.concatenate(ys, axis=0)
        return carry

    lax.fori_loop(0, bb, body, 0)


def _rwkv_sample(r, lw, kf, v, na, b, state):
    nb = r.shape[0]
    bb = 8
    vec = lambda x: x.reshape(nb, N_RWKV_HEADS, HEAD_DIM)
    vspec = pl.BlockSpec((bb, N_RWKV_HEADS, HEAD_DIM), lambda i: (i, 0, 0))
    sspec = pl.BlockSpec((bb, N_RWKV_HEADS, HEAD_DIM, HEAD_DIM), lambda i: (i, 0, 0, 0))
    return pl.pallas_call(
        functools.partial(_rwkv_step_kernel, bb),
        grid=(nb // bb,),
        in_specs=[vspec] * 6 + [sspec],
        out_specs=[vspec, sspec],
        out_shape=[jax.ShapeDtypeStruct((nb, N_RWKV_HEADS, HEAD_DIM), F32),
                   jax.ShapeDtypeStruct(state.shape, F32)],
        compiler_params=pltpu.CompilerParams(dimension_semantics=("arbitrary",)),
        name="rwkv_step",
    )(vec(r), vec(lw), vec(kf), vec(v), vec(na), vec(b), state)


def _alibi_slope(head):
    return 2.0 ** (-8.0 * (head + 1) / N_Q_HEADS)


def _swa_prompt_kernel(q_ref, kc_ref, kp_ref, vc_ref, vp_ref, sink_ref, o_ref):
    n = pl.program_id(0)
    w = WINDOW
    kcat = jnp.concatenate([kp_ref[...], kc_ref[...]], axis=0)
    vcat = jnp.concatenate([vp_ref[...], vc_ref[...]], axis=0)
    lane_k = lax.broadcasted_iota(jnp.int32, kcat.shape, 1) < HEAD_DIM
    kswap = pltpu.roll(kcat, HEAD_DIM, axis=1)
    vswap = pltpu.roll(vcat, HEAD_DIM, axis=1)
    kdup = [jnp.where(lane_k, kcat, kswap).astype(BF16), jnp.where(lane_k, kswap, kcat).astype(BF16)]
    vdup = [jnp.where(lane_k, vcat, vswap).astype(BF16), jnp.where(lane_k, vswap, vcat).astype(BF16)]
    qi = lax.broadcasted_iota(jnp.int32, (w, 2 * w), 0)
    kj = lax.broadcasted_iota(jnp.int32, (w, 2 * w), 1)
    dist = qi + w - kj
    valid = (dist >= 0) & (dist <= w) & ((n > 0) | (kj >= w))
    distf = dist.astype(F32)
    lane_q = lax.broadcasted_iota(jnp.int32, (w, LANES), 1) < HEAD_DIM
    sinks = sink_ref[...]
    for j in range(N_Q_HEADS // 2):
        kvh = (2 * j) // Q_PER_KV
        qp = q_ref[:, LANES * j:LANES * (j + 1)] * (1.0 / math.sqrt(HEAD_DIM))
        q2 = jnp.concatenate([jnp.where(lane_q, qp, 0.0), jnp.where(lane_q, 0.0, qp)], axis=0).astype(BF16)
        s2 = _dot_nt(q2, kdup[kvh])
        halves = []
        for half in range(2):
            head = 2 * j + half
            sink = sinks[0:1, head:head + 1]
            s = jnp.where(valid, s2[w * half:w * (half + 1)] - _alibi_slope(head) * distf, -jnp.inf)
            m = jnp.maximum(jnp.max(s, axis=-1, keepdims=True), sink)
            p = jnp.exp(s - m)
            den = jnp.sum(p, axis=-1, keepdims=True) + jnp.exp(sink - m)
            halves.append(_dot(p.astype(BF16), vdup[kvh]) / den)
        o_ref[:, LANES * j:LANES * (j + 1)] = jnp.where(lane_q, halves[0], halves[1])


def _swa_prompt(q, ka, va, sinks):
    t = q.shape[0]
    w = WINDOW
    cur = lambda n: (n, 0)
    prv = lambda n: (jnp.maximum(n - 1, 0), 0)
    kvs = lambda f: pl.BlockSpec((w, KV_WIDTH), f)
    return pl.pallas_call(
        _swa_prompt_kernel,
        grid=(t // w,),
        in_specs=[pl.BlockSpec((w, ATTN_WIDTH), cur), kvs(cur), kvs(prv), kvs(cur), kvs(prv),
                  pl.BlockSpec((1, N_Q_HEADS), lambda n: (0, 0))],
        out_specs=pl.BlockSpec((w, ATTN_WIDTH), cur),
        out_shape=jax.ShapeDtypeStruct((t, ATTN_WIDTH), F32),
        compiler_params=pltpu.CompilerParams(dimension_semantics=("arbitrary",)),
        name="swa_prompt",
    )(q, ka, ka, va, va, sinks)


def _swa_sample_kernel(q_ref, kn_ref, vn_ref, ck_ref, cv_ref, sink_ref, o_ref, ko_ref, vo_ref):
    wb = ck_ref.shape[1]
    q = q_ref[...] * (1.0 / math.sqrt(HEAD_DIM))
    q2 = jnp.concatenate([q, q], axis=2)
    rowh = lax.broadcasted_iota(jnp.int32, q2.shape, 1) // Q_PER_KV
    laneh = lax.broadcasted_iota(jnp.int32, q2.shape, 2) // HEAD_DIM
    qb = jnp.where(rowh == laneh, q2, 0.0)
    kn = kn_ref[...]
    vn = vn_ref[...]
    ck = ck_ref[...]
    cv = cv_ref[...]
    s = jnp.einsum('bqd,bkd->bqk', qb.astype(BF16), ck.astype(BF16), preferred_element_type=F32)
    s_self = jnp.sum(qb * kn, axis=2, keepdims=True)
    head = lax.broadcasted_iota(jnp.int32, (1, N_Q_HEADS, 1), 1).astype(F32)
    slope = jnp.exp2(-8.0 * (head + 1.0) / N_Q_HEADS)
    dist = (wb - lax.broadcasted_iota(jnp.int32, (1, 1, wb), 2)).astype(F32)
    s = s - slope * dist
    sink = sink_ref[...][None]
    m = jnp.maximum(jnp.maximum(jnp.max(s, axis=2, keepdims=True), s_self), sink)
    p = jnp.exp(s - m)
    p_self = jnp.exp(s_self - m)
    den = jnp.sum(p, axis=2, keepdims=True) + p_self + jnp.exp(sink - m)
    o = jnp.einsum('bqk,bkd->bqd', p.astype(BF16), cv.astype(BF16), preferred_element_type=F32)
    o = (o + p_self * vn) / den
    sel = lax.broadcasted_iota(jnp.int32, (1, N_Q_HEADS, HEAD_DIM), 1) < Q_PER_KV
    o_ref[...] = jnp.where(sel, o[:, :, :HEAD_DIM], o[:, :, HEAD_DIM:])
    ko_ref[:, 0:wb - 1, :] = ck_ref[:, 1:wb, :]
    ko_ref[:, wb - 1:wb, :] = kn
    vo_ref[:, 0:wb - 1, :] = cv_ref[:, 1:wb, :]
    vo_ref[:, wb - 1:wb, :] = vn


def _swa_sample(q, ka, va, cache_k, cache_v, sinks_col):
    nb, wb = cache_k.shape[0], cache_k.shape[1]
    bb = 16
    b3 = lambda i: (i, 0, 0)
    nspec = pl.BlockSpec((bb, 1, KV_WIDTH), b3)
    cspec = pl.BlockSpec((bb, wb, KV_WIDTH), b3)
    qspec = pl.BlockSpec((bb, N_Q_HEADS, HEAD_DIM), b3)
    return pl.pallas_call(
        _swa_sample_kernel,
        grid=(nb // bb,),
        in_specs=[qspec, nspec, nspec, cspec, cspec, pl.BlockSpec((N_Q_HEADS, 1), lambda i: (0, 0))],
        out_specs=[qspec, cspec, cspec],
        out_shape=[jax.ShapeDtypeStruct((nb, N_Q_HEADS, HEAD_DIM), F32),
                   jax.ShapeDtypeStruct(cache_k.shape, F32), jax.ShapeDtypeStruct(cache_v.shape, F32)],
        compiler_params=pltpu.CompilerParams(dimension_semantics=("arbitrary",)),
        name="swa_sample",
    )(q.reshape(nb, N_Q_HEADS, HEAD_DIM), ka.reshape(nb, 1, KV_WIDTH), va.reshape(nb, 1, KV_WIDTH),
      cache_k, cache_v, sinks_col)


def _post_mix_kernel(is_prompt, yr_ref, g_ref, bonus_ref, ya_ref, x_ref, gt1_ref, sh2_ref, sc2_ref,
                     lnw_ref, lnb_ref, wout_ref, gpost_ref, gpre_ref, rw_ref, rb_ref,
                     x1_o, h2_o, ti_o, tw_o):
    gt1 = gt1_ref[...]
    sh2 = sh2_ref[...]
    sc2 = sc2_ref[...]
    if is_prompt:
        gt1, sh2, sc2 = gt1[0:1], sh2[0:1], sc2[0:1]
    y = yr_ref[...]
    mean = _seg_sum64(y) * (1.0 / HEAD_DIM)
    dlt = y - mean
    var = _seg_sum64(dlt * dlt) * (1.0 / HEAD_DIM)
    yn = dlt * lax.rsqrt(var + GN_EPS) * lnw_ref[...] + lnb_ref[...]
    yr = (yn + bonus_ref[...]) * g_ref[...]
    mix = _dot(jnp.concatenate([yr, ya_ref[...]], axis=1).astype(BF16), wout_ref[...])
    x1 = x_ref[...] + gt1 * _rmsnorm(mix, gpost_ref[...])
    x1_o[...] = x1
    h2 = _rmsnorm(x1, gpre_ref[...]) * (1.0 + sc2) + sh2
    h2_o[...] = h2
    logits = _dot3(h2, rw_ref[...]) + rb_ref[...]
    lane = lax.broadcasted_iota(jnp.int32, logits.shape, 1)
    vals, idxs = [], []
    for _ in range(TOP_K):
        m = jnp.max(logits, axis=1, keepdims=True)
        idx = jnp.min(jnp.where(logits == m, lane, N_EXPERTS), axis=1, keepdims=True)
        vals.append(m)
        idxs.append(idx)
        logits = jnp.where(lane == idx, -jnp.inf, logits)
    e = jnp.exp(jnp.concatenate(vals, axis=1) - vals[0])
    tw_o[...] = e / jnp.sum(e, axis=1, keepdims=True)
    ti_o[...] = jnp.concatenate(idxs, axis=1)


def _post_mix(is_prompt, tm, yr, g, bonus, ya, x, mod, lnw, lnb, wout, gpost, gpre, rw, rb):
    m, d = x.shape
    row = lambda i: (i, 0)
    if is_prompt:
        mod_rows = 8
        mod_map = lambda c: (lambda i: (0, c))
    else:
        mod_rows = tm
        mod_map = lambda c: (lambda i: (i, c))
    wide = pl.BlockSpec((tm, RWKV_WIDTH), row)
    in_specs = [wide, wide, wide, wide, pl.BlockSpec((tm, d), row),
                pl.BlockSpec((mod_rows, d), mod_map(2)), pl.BlockSpec((mod_rows, d), mod_map(3)),
                pl.BlockSpec((mod_rows, d), mod_map(4)),
                _resident(lnw.shape), _resident(lnb.shape), _resident(wout.shape), _resident(gpost.shape),
                _resident(gpre.shape), _resident(rw.shape), _resident(rb.shape)]
    out_shape = [jax.ShapeDtypeStruct((m, d), F32), jax.ShapeDtypeStruct((m, d), F32),
                 jax.ShapeDtypeStruct((m, TOP_K), jnp.int32), jax.ShapeDtypeStruct((m, TOP_K), F32)]
    out_specs = [pl.BlockSpec((tm, d), row), pl.BlockSpec((tm, d), row),
                 pl.BlockSpec((tm, TOP_K), row), pl.BlockSpec((tm, TOP_K), row)]
    return pl.pallas_call(
        functools.partial(_post_mix_kernel, is_prompt),
        grid=(m // tm,), in_specs=in_specs, out_specs=out_specs, out_shape=out_shape,
        compiler_params=pltpu.CompilerParams(dimension_semantics=("arbitrary",), vmem_limit_bytes=VMEM_LIMIT),
        name="post_mix_prompt" if is_prompt else "post_mix_sample",
    )(yr, g, bonus, ya, x, mod, mod, mod, lnw, lnb, wout, gpost, gpre, rw, rb)


def _moe_kernel(ex_ref, rb_ref, ns_ref, tok_ref, h2_hbm, w1_ref, b1g_ref, b1l_ref, w2_ref, b2_ref, ys_hbm,
                xbuf, xb, acc, w1p, w2b, sem_in, sem_out):
    s = pl.program_id(0)
    f = pl.program_id(1)
    n_super = pl.num_programs(0)
    ns = ns_ref[s]
    rb = rb_ref[s]
    rows = MOE_ROWS
    grp = 2 * LANES
    share = rows // MOE_NF

    def gather_share(sup, n_tiles, col):
        base = rb_ref[sup] * rows

        def body(j, c):
            r0 = j * rows + col * share
            for u in range(share):
                pltpu.make_async_copy(h2_hbm.at[pl.ds(tok_ref[base + r0 + u], 1)], xbuf.at[pl.ds(r0 + u, 1)],
                                      sem_in).start()
            return c
        lax.fori_loop(0, n_tiles, body, 0)

    def gather_wait(j):
        sl = pl.ds(j * rows, rows)
        pltpu.make_async_copy(h2_hbm.at[pl.ds(0, rows)], xbuf.at[sl], sem_in).wait()

    def y_copy(j):
        return pltpu.make_async_copy(acc.at[pl.ds(j * rows, rows)], ys_hbm.at[pl.ds((rb + j) * rows, rows)], sem_out)

    def for_tiles(fn):
        def body(j, c):
            fn(j)
            return c
        lax.fori_loop(0, ns, body, 0)

    @pl.when(ns > 0)
    def _():
        @pl.when(f == 0)
        def _():
            @pl.when(s == 0)
            def _():
                for col in range(MOE_NF):
                    gather_share(0, ns, col)
            b2 = jnp.broadcast_to(b2_ref[0], (rows, D_MODEL))

            def init(j):
                acc[pl.ds(pl.multiple_of(j * rows, rows), rows), :] = b2
            for_tiles(init)

            for_tiles(gather_wait)

            def cast(j):
                sl = pl.ds(pl.multiple_of(j * rows, rows), rows)
                xb[sl, :] = xbuf[sl, :].astype(BF16)
            for_tiles(cast)

        nxt = jnp.minimum(s + 1, n_super - 1)
        gather_share(nxt, jnp.where(s + 1 < n_super, jnp.maximum(ns_ref[nxt], 0), 0), f)

        pr = lax.broadcasted_iota(jnp.int32, (grp, grp), 0)
        pc = lax.broadcasted_iota(jnp.int32, (grp, grp), 1)
        perm = jnp.where(pr == jnp.where(pc < LANES, 2 * pc, 2 * (pc - LANES) + 1), 1.0, 0.0).astype(BF16)
        n_grp = 2 * MOE_FT // grp
        for g in range(n_grp):
            gs = slice(grp * g, grp * (g + 1))
            w1p[:, gs] = _dot(w1_ref[0, :, gs].astype(BF16), perm).astype(BF16)
        w2b[...] = w2_ref[0].astype(BF16)
        b1g = b1g_ref[0]
        b1l = b1l_ref[0]

        def tile(j):
            sl = pl.ds(pl.multiple_of(j * rows, rows), rows)
            hh = _dot(xb[sl, :], w1p[...])
            glu = jnp.concatenate([hh[:, grp * g:grp * g + LANES] for g in range(n_grp)], axis=1) + b1g
            lin = jnp.concatenate([hh[:, grp * g + LANES:grp * (g + 1)] for g in range(n_grp)], axis=1) + b1l
            glu = jnp.minimum(glu, SWIGLU_LIMIT)
            lin = jnp.clip(lin, -SWIGLU_LIMIT, SWIGLU_LIMIT)
            act = glu * jax.nn.sigmoid(SWIGLU_ALPHA * glu) * (lin + 1.0)
            acc[sl, :] += _dot(act.astype(BF16), w2b[...])
        for_tiles(tile)

        @pl.when(f == pl.num_programs(1) - 1)
        def _():
            for_tiles(lambda j: y_copy(j).start())
            for_tiles(lambda j: y_copy(j).wait())

    @pl.when((ns < 0) & (f == 0))
    def _():
        acc[0:rows, :] = jnp.zeros((rows, D_MODEL), F32)

        def z_copy(j):
            return pltpu.make_async_copy(acc.at[pl.ds(0, rows)], ys_hbm.at[pl.ds((rb + j) * rows, rows)], sem_out)

        def body(j, c, op):
            op(z_copy(j))
            return c
        lax.fori_loop(0, -ns, functools.partial(body, op=lambda cp: cp.start()), 0)
        lax.fori_loop(0, -ns, functools.partial(body, op=lambda cp: cp.wait()), 0)


def _moe_experts(ex, rb, ns, src_tok, h2, w1, b1g, b1l, w2, b2):
    n_rows = src_tok.shape[0]
    n_super = ex.shape[0]
    nf = MOE_NF
    last = nf - 1
    sub_rows = MOE_SUB * MOE_ROWS

    def fcol(s, f, ns_):
        return jnp.where(ns_[s] > 0, f, last)

    grid_spec = pltpu.PrefetchScalarGridSpec(
        num_scalar_prefetch=4,
        grid=(n_super, nf),
        in_specs=[pl.BlockSpec(memory_space=pl.ANY),
                  pl.BlockSpec((1, D_MODEL, 2 * MOE_FT), lambda s, f, e_, r_, n_, t_: (e_[s], 0, fcol(s, f, n_))),
                  pl.BlockSpec((1, 1, MOE_FT), lambda s, f, e_, r_, n_, t_: (e_[s], 0, fcol(s, f, n_))),
                  pl.BlockSpec((1, 1, MOE_FT), lambda s, f, e_, r_, n_, t_: (e_[s], 0, fcol(s, f, n_))),
                  pl.BlockSpec((1, MOE_FT, D_MODEL), lambda s, f, e_, r_, n_, t_: (e_[s], fcol(s, f, n_), 0)),
                  pl.BlockSpec((1, 1, D_MODEL), lambda s, f, e_, r_, n_, t_: (e_[s], 0, 0))],
        out_specs=pl.BlockSpec(memory_space=pl.ANY),
        scratch_shapes=[pltpu.VMEM((sub_rows, D_MODEL), F32),
                        pltpu.VMEM((sub_rows, D_MODEL), BF16),
                        pltpu.VMEM((sub_rows, D_MODEL), F32),
                        pltpu.VMEM((D_MODEL, 2 * MOE_FT), BF16),
                        pltpu.VMEM((MOE_FT, D_MODEL), BF16),
                        pltpu.SemaphoreType.DMA, pltpu.SemaphoreType.DMA])
    return pl.pallas_call(
        _moe_kernel,
        grid_spec=grid_spec,
        out_shape=jax.ShapeDtypeStruct((n_rows, D_MODEL), F32),
        compiler_params=pltpu.CompilerParams(dimension_semantics=("arbitrary", "arbitrary"),
                                             vmem_limit_bytes=VMEM_LIMIT),
        name="moe_experts",
    )(ex, rb, ns, src_tok, h2, w1, b1g, b1l, w2, b2)


def _combine_kernel(is_prompt, idx_ref, nidx_ref, ys_hbm, x1_ref, tw_ref, gt2_ref, gpost_ref, o_ref, buf, sem):
    tk = COMBINE_TOK
    n = TOP_K * tk
    unroll = 16
    i = pl.program_id(0)
    slot = i % 2

    def start_rows(ref, buf_slot):
        def body(q, c):
            for u in range(unroll):
                j = q * unroll + u
                pltpu.make_async_copy(ys_hbm.at[pl.ds(ref[0, 0, j], 1)], buf.at[buf_slot, pl.ds(j, 1)],
                                      sem.at[buf_slot]).start()
            return c
        lax.fori_loop(0, n // unroll, body, 0)

    @pl.when(i == 0)
    def _():
        start_rows(idx_ref, 0)

    @pl.when(i + 1 < pl.num_programs(0))
    def _():
        start_rows(nidx_ref, 1 - slot)

    pltpu.make_async_copy(ys_hbm.at[pl.ds(0, n)], buf.at[slot], sem.at[slot]).wait()
    tw = tw_ref[...]
    f = tw[:, 0:1] * buf[slot, 0:tk, :]
    for k in range(1, TOP_K):
        f = f + tw[:, k:k + 1] * buf[slot, k * tk:(k + 1) * tk, :]
    gt2 = gt2_ref[...]
    if is_prompt:
        gt2 = gt2[0:1]
    o_ref[...] = x1_ref[...] + gt2 * _rmsnorm(f, gpost_ref[...])


def _combine(is_prompt, pos, ys, x1, tw, mod, gpost):
    m, d = x1.shape
    tk = COMBINE_TOK
    nblk = m // tk
    idx = pos.reshape(nblk, tk, TOP_K).transpose(0, 2, 1).reshape(nblk, 1, TOP_K * tk)
    row = lambda i: (i, 0)
    mod_spec = (pl.BlockSpec((8, d), lambda i: (0, 5)) if is_prompt else pl.BlockSpec((tk, d), lambda i: (i, 5)))
    return pl.pallas_call(
        functools.partial(_combine_kernel, is_prompt),
        grid=(nblk,),
        in_specs=[pl.BlockSpec((1, 1, TOP_K * tk), lambda i: (i, 0, 0), memory_space=pltpu.SMEM),
                  pl.BlockSpec((1, 1, TOP_K * tk), lambda i: (jnp.minimum(i + 1, nblk - 1), 0, 0),
                               memory_space=pltpu.SMEM),
                  pl.BlockSpec(memory_space=pl.ANY),
                  pl.BlockSpec((tk, d), row), pl.BlockSpec((tk, TOP_K), row), mod_spec,
                  pl.BlockSpec((1, d), lambda i: (0, 0))],
        out_specs=pl.BlockSpec((tk, d), row),
        out_shape=jax.ShapeDtypeStruct((m, d), F32),
        scratch_shapes=[pltpu.VMEM((2, TOP_K * tk, d), F32), pltpu.SemaphoreType.DMA((2,))],
        compiler_params=pltpu.CompilerParams(dimension_semantics=("arbitrary",)),
        name="moe_combine_prompt" if is_prompt else "moe_combine_sample",
    )(idx, idx, ys, x1, tw, mod, gpost)


def _routing_tables(top_i):
    n_assign = top_i.size
    e_flat = top_i.reshape(-1)
    onehot = (e_flat[:, None] == jnp.arange(N_EXPERTS, dtype=jnp.int32)[None, :]).astype(jnp.int32)
    counts = jnp.sum(onehot, axis=0)
    rank = jnp.sum((jnp.cumsum(onehot, axis=0) - onehot) * onehot, axis=1)
    tiles = (counts + MOE_ROWS - 1) // MOE_ROWS
    tile_start = jnp.cumsum(tiles) - tiles
    pos = tile_start[e_flat] * MOE_ROWS + rank
    n_rows = _padded_rows(n_assign)
    src_tok = jnp.zeros((n_rows,), jnp.int32).at[pos].set(jnp.arange(n_assign, dtype=jnp.int32) // TOP_K)
    supers = (tiles + MOE_SUB - 1) // MOE_SUB
    super_end = jnp.cumsum(supers)
    s_idx = jnp.arange(_max_supers(n_assign), dtype=jnp.int32)
    ex = jnp.minimum(jnp.searchsorted(super_end, s_idx, side='right'), N_EXPERTS - 1).astype(jnp.int32)
    j = s_idx - (super_end - supers)[ex]
    live = s_idx < super_end[-1]
    tail = jnp.sum(tiles) + MOE_SUB * (s_idx - super_end[-1])
    n_clear = jnp.clip(n_rows // MOE_ROWS - tail, 0, MOE_SUB)
    ns = jnp.where(live, jnp.clip(tiles[ex] - MOE_SUB * j, 0, MOE_SUB), -n_clear).astype(jnp.int32)
    rb = jnp.where(live, tile_start[ex] + MOE_SUB * j, tail).astype(jnp.int32)
    last_live = jnp.max(jnp.where(live, ex, 0))
    ex = jnp.where(live, ex, last_live).astype(jnp.int32)
    return pos.astype(jnp.int32), src_tok, ex, rb, ns


def _padded_rows(n_assign):
    rows = n_assign + N_EXPERTS * (MOE_ROWS - 1)
    return -(-rows // MOE_ROWS) * MOE_ROWS


def _max_supers(n_assign):
    return N_EXPERTS + -(-_padded_rows(n_assign) // (MOE_ROWS * MOE_SUB))


def kernel(x_prompt, x_sample, cache_k, cache_v, state_wkv, state_shift, c_prompt, c_sample, w_ada, b_ada, g_pre_mix, g_post_mix, g_pre_ffn, g_post_ffn, mu_shift, w_in, rwkv_w0, rwkv_w2, rwkv_a0, rwkv_a2, rwkv_g2, rwkv_k_k, rwkv_k_a, rwkv_r_k, rwkv_ln_w, rwkv_ln_b, attn_sinks, w_out, router_w, router_b, moe_w1, moe_b1, moe_w2, moe_b2):
    assert w_ada.shape[0] == 1, "single-layer step"
    d = D_MODEL
    t = x_prompt.shape[1]
    nb = x_sample.shape[0]
    xp = x_prompt.reshape(t, d)
    xs = x_sample.reshape(nb, d)

    c_all = jnp.concatenate([jnp.broadcast_to(c_prompt, (8, d)), c_sample], axis=0)
    mod = _adaln_mod(c_all, w_ada[0], b_ada[0].reshape(1, 6 * d))
    mod_p, mod_s = mod[:8], mod[8:]

    offs = [0, 1024, 2048, 3072, 3136, 3200, 3360]
    w_in0 = w_in[0]
    pad_to = lambda w, n: jnp.pad(w, ((0, 0), (0, n - w.shape[1])))
    wts = [w_in0[:, offs[0]:offs[1]], w_in0[:, offs[1]:offs[2]], w_in0[:, offs[2]:offs[3]],
           pad_to(w_in0[:, offs[3]:offs[4]], 128), pad_to(w_in0[:, offs[4]:offs[5]], 128),
           pad_to(w_in0[:, offs[5]:offs[6]], 256), w_in0[:, offs[6]:]]
    wts = [w.astype(BF16) for w in wts]
    pad_rows = lambda w, n: jnp.pad(w, ((0, n - w.shape[0]), (0, 0))).astype(BF16)
    row = lambda p: p.reshape(1, -1)
    rw = [row(rwkv_w0[0]), pad_rows(rwkv_w2[0], 128), row(rwkv_a0[0]), pad_rows(rwkv_a2[0], 128),
          pad_rows(rwkv_g2[0], 256), row(rwkv_k_k[0]), row(rwkv_k_a[0]), row(rwkv_r_k[0])]
    gpm = row(g_pre_mix[0])
    mu8 = jnp.pad(mu_shift[0], ((0, 8 - N_SHIFTED), (0, 0)))

    pr = _in_proj(True, 256, xp, xp, mod_p, gpm, mu8, wts, rw)
    sr = _in_proj(False, nb, xs, state_shift[0], mod_s, gpm, mu8, wts, rw)
    r_p, lw_p, kf_p, v_p, na_p, b_p, g_p, bonus_p, q_p, ka_p, va_p, hlast_p = pr
    r_s, lw_s, kf_s, v_s, na_s, b_s, g_s, bonus_s, q_s, ka_s, va_s, h_s = sr

    y_p, st_p = _rwkv_prompt(r_p, lw_p, kf_p, v_p, na_p, b_p)
    y_s, wkv_s = _rwkv_sample(r_s, lw_s, kf_s, v_s, na_s, b_s, state_wkv[0])
    y_s = y_s.reshape(nb, RWKV_WIDTH)
    sinks = attn_sinks[0]
    att_p = _swa_prompt(q_p, ka_p, va_p, sinks.reshape(1, N_Q_HEADS))
    wb = cache_k.shape[2]
    att_s, ck_new, cv_new = _swa_sample(q_s, ka_s, va_s, cache_k[0].reshape(nb, wb, KV_WIDTH),
                                        cache_v[0].reshape(nb, wb, KV_WIDTH), sinks.reshape(N_Q_HEADS, 1))
    att_s = att_s.reshape(nb, ATTN_WIDTH)

    post = [row(rwkv_ln_w[0]), row(rwkv_ln_b[0]), w_out[0].astype(BF16), row(g_post_mix[0]),
            row(g_pre_ffn[0]), router_w[0], row(router_b[0])]
    x1_p, h2_p, ti_p, tw_p = _post_mix(True, 256, y_p, g_p, bonus_p, att_p, xp, mod_p, *post)
    x1_s, h2_s, ti_s, tw_s = _post_mix(False, nb, y_s, g_s, bonus_s, att_s, xs, mod_s, *post)

    h2_all = jnp.concatenate([h2_p, h2_s], axis=0)
    pos, src_tok, ex, rb, ns = _routing_tables(jnp.concatenate([ti_p, ti_s], axis=0))
    b1 = moe_b1[0].reshape(N_EXPERTS, 1, D_FF, 2)
    ys_rows = _moe_experts(ex, rb, ns, src_tok, h2_all, moe_w1[0], b1[..., 0], b1[..., 1],
                           moe_w2[0], moe_b2[0].reshape(N_EXPERTS, 1, d))
    pos = pos.reshape(-1, TOP_K)
    gpf = row(g_post_ffn[0])
    out_p = _combine(True, pos[:t], ys_rows, x1_p, tw_p, mod_p, gpf)
    out_s = _combine(False, pos[t:], ys_rows, x1_s, tw_s, mod_s, gpf)

    n_keep = min(WINDOW, t)
    st_heads = jnp.stack([st_p[:, :HEAD_DIM, :HEAD_DIM], st_p[:, HEAD_DIM:, HEAD_DIM:]], axis=1)
    return (out_p.reshape(1, t, d),
            out_s.reshape(nb, 1, d),
            ka_p[t - n_keep:].reshape(1, 1, n_keep, N_KV_HEADS, HEAD_DIM),
            va_p[t - n_keep:].reshape(1, 1, n_keep, N_KV_HEADS, HEAD_DIM),
            st_heads.reshape(1, 1, N_RWKV_HEADS, HEAD_DIM, HEAD_DIM),
            hlast_p[7:8].reshape(1, 1, d),
            ck_new.reshape(1, nb, wb, N_KV_HEADS, HEAD_DIM),
            cv_new.reshape(1, nb, wb, N_KV_HEADS, HEAD_DIM),
            wkv_s.reshape(1, nb, N_RWKV_HEADS, HEAD_DIM, HEAD_DIM),
            h_s.reshape(1, nb, d))
```

```python
import functools
import math

import jax
import jax.numpy as jnp
from jax import lax
from jax.experimental import pallas as pl
from jax.experimental.pallas import tpu as pltpu

F32 = jnp.float32
BF16 = jnp.bfloat16

D_MODEL = 2048
HEAD_DIM = 64
RWKV_WIDTH = 1024
N_RWKV_HEADS = 16
ATTN_WIDTH = 1024
N_Q_HEADS = 16
N_KV_HEADS = 2
Q_PER_KV = 8
KV_WIDTH = 128
WINDOW = 128
N_SHIFTED = 6
N_EXPERTS = 32
TOP_K = 4
D_FF = 2048
SWIGLU_ALPHA = 1.702
SWIGLU_LIMIT = 7.0
NORM_EPS = 1e-6
GN_EPS = 64e-5
L2_EPS = 1e-12

LANES = 128
VMEM_LIMIT = 56 * 1024 * 1024

CHUNK = 64
RWKV_ROWS = 256
RWKV_PAIRS = 8
MOE_ROWS = 256
MOE_SUB = 6
MOE_FT = 256
MOE_NF = D_FF // MOE_FT
COMBINE_TOK = 128


def _dot(a, b):
    return jnp.dot(a, b, preferred_element_type=F32)


def _dot_nt(a, b):
    return lax.dot_general(a, b, (((1,), (1,)), ((), ())), preferred_element_type=F32)


def _split_bf16(x):
    hi = x.astype(BF16)
    lo = (x - hi.astype(F32)).astype(BF16)
    return hi, lo


def _dot3(a, b):
    ah, al = _split_bf16(a)
    bh, bl = _split_bf16(b)
    return _dot(ah, bh) + _dot(ah, bl) + _dot(al, bh)


def _seg_sum64(x):
    r = lax.broadcasted_iota(jnp.int32, (LANES, LANES), 0) // HEAD_DIM
    c = lax.broadcasted_iota(jnp.int32, (LANES, LANES), 1) // HEAD_DIM
    bd = jnp.where(r == c, 1.0, 0.0).astype(BF16)
    hi, lo = _split_bf16(x)
    outs = []
    for j in range(x.shape[1] // LANES):
        sl = slice(LANES * j, LANES * (j + 1))
        outs.append(_dot(hi[:, sl], bd) + _dot(lo[:, sl], bd))
    return jnp.concatenate(outs, axis=1)


def _rmsnorm(x, g):
    ms = jnp.mean(x * x, axis=-1, keepdims=True)
    return x * lax.rsqrt(ms + NORM_EPS) * g


def _resident(shape):
    nd = len(shape)
    return pl.BlockSpec(shape, lambda *_: (0,) * nd, pipeline_mode=pl.Buffered(1))


def _mod_kernel(c_ref, w_ref, b_ref, o_ref):
    c = c_ref[...]
    s = c * jax.nn.sigmoid(c)
    o_ref[...] = _dot3(s, w_ref[...]) + b_ref[...]


def _adaln_mod(c, w_ada, b_ada):
    rows, d = c.shape
    n = w_ada.shape[1]
    tn = 512
    return pl.pallas_call(
        _mod_kernel,
        grid=(n // tn,),
        in_specs=[pl.BlockSpec((rows, d), lambda j: (0, 0)),
                  pl.BlockSpec((d, tn), lambda j: (0, j)),
                  pl.BlockSpec((1, tn), lambda j: (0, j))],
        out_specs=pl.BlockSpec((rows, tn), lambda j: (0, j)),
        out_shape=jax.ShapeDtypeStruct((rows, n), F32),
        compiler_params=pltpu.CompilerParams(dimension_semantics=("arbitrary",), vmem_limit_bytes=VMEM_LIMIT),
        name="adaln_mod",
    )(c, w_ada, b_ada)


def _inproj_kernel(is_prompt, tm,
                   x_ref, prev_ref, sh_ref, sc_ref, gpm_ref, mu_ref,
                   wr_ref, wk_ref, wv_ref, wwl_ref, wal_ref, wgl_ref, wqkv_ref,
                   w0_ref, w2_ref, a0_ref, a2_ref, g2_ref, kk_ref, ka_ref, rk_ref,
                   r_o, lw_o, kf_o, v_o, na_o, b_o, g_o, bonus_o, q_o, kat_o, vat_o, h_o):
    i = pl.program_id(0)
    gpm = gpm_ref[...]
    sh = sh_ref[...]
    sc = sc_ref[...]
    if is_prompt:
        sh = sh[0:1]
        sc = sc[0:1]

    def modnorm(x):
        return _rmsnorm(x, gpm) * (1.0 + sc) + sh

    h = modnorm(x_ref[...])
    if is_prompt:
        hp = modnorm(prev_ref[...])[7:8, :]
        hp = jnp.where(i > 0, hp, 0.0)
        row = lax.broadcasted_iota(jnp.int32, h.shape, 0)
        hprev = jnp.where(row == 0, hp, pltpu.roll(h, 1, axis=0))
        h_o[...] = h[tm - 8:tm, :]
    else:
        hprev = prev_ref[...]
        h_o[...] = h
    dx = hprev - h
    mu = mu_ref[...]

    def branch(j, w_ref):
        xi = (h + dx * mu[j:j + 1, :]).astype(BF16)
        return _dot(xi, w_ref[...])

    r = branch(0, wr_ref)
    k = branch(1, wk_ref)
    v = branch(2, wv_ref)
    wl = branch(3, wwl_ref)
    al = branch(4, wal_ref)
    gl = branch(5, wgl_ref)
    qkv = _dot(h.astype(BF16), wqkv_ref[...])
    q_o[...] = qkv[:, :ATTN_WIDTH]
    kat_o[...] = qkv[:, ATTN_WIDTH:ATTN_WIDTH + KV_WIDTH]
    vat_o[...] = qkv[:, ATTN_WIDTH + KV_WIDTH:]

    z = w0_ref[...] + _dot(jnp.tanh(wl).astype(BF16), w2_ref[...])
    w_raw = -jnp.logaddexp(-z, 0.0) - 0.5
    lw_o[...] = -jnp.exp(w_raw)
    a = jax.nn.sigmoid(a0_ref[...] + _dot(al.astype(BF16), a2_ref[...]))
    g_o[...] = _dot(jax.nn.sigmoid(gl).astype(BF16), g2_ref[...])
    kk = k * kk_ref[...]
    kk = kk / jnp.maximum(jnp.sqrt(_seg_sum64(kk * kk)), L2_EPS)
    kf = k * (1.0 + (a - 1.0) * ka_ref[...])
    r_o[...] = r
    kf_o[...] = kf
    v_o[...] = v
    na_o[...] = -kk
    b_o[...] = kk * a
    bonus_o[...] = _seg_sum64(r * kf * rk_ref[...]) * v


def _in_proj(is_prompt, tm, x, prev, mod, gpm, mu8, wts, rw):
    m, d = x.shape
    grid = (m // tm,)
    row = lambda i: (i, 0)
    if is_prompt:
        prev_spec = pl.BlockSpec((8, d), lambda i: (jnp.maximum(i * (tm // 8) - 1, 0), 0))
        mod_rows = 8
        mod_map = lambda c: (lambda i: (0, c))
        h_shape, h_spec = (8, d), pl.BlockSpec((8, d), lambda i: (0, 0))
    else:
        prev_spec = pl.BlockSpec((tm, d), row)
        mod_rows = tm
        mod_map = lambda c: (lambda i: (i, c))
        h_shape, h_spec = (m, d), pl.BlockSpec((tm, d), row)
    in_specs = [pl.BlockSpec((tm, d), row), prev_spec,
                pl.BlockSpec((mod_rows, d), mod_map(0)), pl.BlockSpec((mod_rows, d), mod_map(1)),
                _resident((1, d)), _resident((8, d))]
    in_specs += [_resident(w.shape) for w in wts]
    in_specs += [_resident(p.shape) for p in rw]
    wide = jax.ShapeDtypeStruct((m, RWKV_WIDTH), F32)
    wide_spec = pl.BlockSpec((tm, RWKV_WIDTH), row)
    kv = jax.ShapeDtypeStruct((m, KV_WIDTH), F32)
    kv_spec = pl.BlockSpec((tm, KV_WIDTH), row)
    out_shape = [wide] * 9 + [kv, kv, jax.ShapeDtypeStruct(h_shape, F32)]
    out_specs = [wide_spec] * 9 + [kv_spec, kv_spec, h_spec]
    return pl.pallas_call(
        functools.partial(_inproj_kernel, is_prompt, tm),
        grid=grid, in_specs=in_specs, out_specs=out_specs, out_shape=out_shape,
        compiler_params=pltpu.CompilerParams(dimension_semantics=("arbitrary",), vmem_limit_bytes=VMEM_LIMIT),
        name="in_proj_prompt" if is_prompt else "in_proj_sample",
    )(x, prev, mod, mod, gpm, mu8, *wts, *rw)


def _rwkv_chunk_kernel(r_ref, lw_ref, k_ref, v_ref, a_ref, b_ref, y_ref, s_ref, st_ref):
    t = pl.program_id(1)
    C = CHUNK
    P = 2 * HEAD_DIM

    @pl.when(t == 0)
    def _():
        st_ref[...] = jnp.zeros_like(st_ref)

    ri = lax.broadcasted_iota(jnp.int32, (P, P), 0)
    ci = lax.broadcasted_iota(jnp.int32, (P, P), 1)
    bd = (ri // C) == (ci // C)
    tril_s = bd & ((ri % C) > (ci % C))
    tril_i = bd & ((ri % C) >= (ci % C))
    eye = jnp.where(ri == ci, 1.0, 0.0)
    lane0 = lax.broadcasted_iota(jnp.int32, (C, P), 1) < HEAD_DIM
    trow = lax.broadcasted_iota(jnp.int32, (C, P), 0)

    def stack(x):
        return jnp.concatenate([jnp.where(lane0, x, 0.0), jnp.where(lane0, 0.0, x)], axis=0)

    def dup(x):
        return jnp.concatenate([x, x], axis=0)

    def prep(sl, pp):
        ln = slice(P * pp, P * (pp + 1))
        lw = lw_ref[sl, ln]
        cw = lw
        for s in (1, 2, 4, 8, 16, 32):
            cw = cw + jnp.where(trow >= s, pltpu.roll(cw, s, axis=0), 0.0)
        cw_last = cw[C - 1:C, :]
        e_neg = jnp.exp(-cw)
        e_end = jnp.exp(cw_last - cw)
        k = k_ref[sl, ln]
        v = v_ref[sl, ln]
        b = b_ref[sl, ln]
        a2 = stack(a_ref[sl, ln] * jnp.exp(cw - lw))
        r2 = stack(r_ref[sl, ln] * jnp.exp(cw))
        return dict(
            v2=stack(v),
            lhs=jnp.concatenate([a2, r2], axis=0).astype(BF16),
            rhs=jnp.concatenate([dup(k * e_neg), dup(b * e_neg)], axis=0).astype(BF16),
            kbh=jnp.concatenate([dup(b * e_end), dup(k * e_end)], axis=0).astype(BF16),
            decay=jnp.exp(cw_last))

    def chunk(c, carry):
        sl = pl.ds(pl.multiple_of(c * C, C), C)
        pairs = range(RWKV_PAIRS)
        st = [st_ref[pp] for pp in pairs]
        d = [prep(sl, pp) for pp in pairs]
        gram = [_dot_nt(d[pp]['lhs'], d[pp]['rhs']) for pp in pairs]
        l2 = [jnp.where(tril_s, gram[pp][0:P, P:2 * P], 0.0) for pp in pairs]
        a_s = [_dot_nt(d[pp]['lhs'], st[pp].astype(BF16)) for pp in pairs]
        v2b = [d[pp]['v2'].astype(BF16) for pp in pairs]
        rhs_u = [a_s[pp][0:P] + _dot(jnp.where(tril_s, gram[pp][0:P, 0:P], 0.0).astype(BF16), v2b[pp])
                 for pp in pairs]
        inv = [eye + l2[pp] for pp in pairs]
        lp = l2
        for _ in range(5):
            lpb = [lp[pp].astype(BF16) for pp in pairs]
            lp = [_dot(lpb[pp], lpb[pp]) for pp in pairs]
            inv = [inv[pp] + _dot(lp[pp].astype(BF16), inv[pp].astype(BF16)) for pp in pairs]
        u2 = [_dot(inv[pp].astype(BF16), rhs_u[pp].astype(BF16)) for pp in pairs]
        p_cat = [jnp.concatenate([jnp.where(tril_i, gram[pp][P:2 * P, P:2 * P], 0.0),
                                  jnp.where(tril_i, gram[pp][P:2 * P, 0:P], 0.0)], axis=1).astype(BF16)
                 for pp in pairs]
        y2 = [a_s[pp][P:2 * P] + _dot(p_cat[pp], jnp.concatenate([u2[pp].astype(BF16), v2b[pp]], axis=0))
              for pp in pairs]
        uvt = [jnp.concatenate([u2[pp], d[pp]['v2']], axis=0).T.astype(BF16) for pp in pairs]
        st_new = [jnp.where(bd, st[pp] * d[pp]['decay'] + _dot(uvt[pp], d[pp]['kbh']), 0.0) for pp in pairs]
        for pp in pairs:
            y_ref[sl, P * pp:P * (pp + 1)] = y2[pp][0:C] + y2[pp][C:2 * C]
            st_ref[pp] = st_new[pp]
        return carry

    lax.fori_loop(0, RWKV_ROWS // C, chunk, 0)

    @pl.when(t == pl.num_programs(1) - 1)
    def _():
        s_ref[...] = st_ref[...]


def _rwkv_prompt(r, lw, kf, v, na, b):
    t = r.shape[0]
    n_pairs = RWKV_WIDTH // LANES
    width = RWKV_PAIRS * LANES
    spec = pl.BlockSpec((RWKV_ROWS, width), lambda p, i: (i, p))
    return pl.pallas_call(
        _rwkv_chunk_kernel,
        grid=(n_pairs // RWKV_PAIRS, t // RWKV_ROWS),
        in_specs=[spec] * 6,
        out_specs=[spec, pl.BlockSpec((RWKV_PAIRS, LANES, LANES), lambda p, i: (p, 0, 0))],
        out_shape=[jax.ShapeDtypeStruct((t, RWKV_WIDTH), F32),
                   jax.ShapeDtypeStruct((n_pairs, LANES, LANES), F32)],
        scratch_shapes=[pltpu.VMEM((RWKV_PAIRS, LANES, LANES), F32)],
        compiler_params=pltpu.CompilerParams(dimension_semantics=("arbitrary", "arbitrary")),
        name="rwkv_chunked",
    )(r, lw, kf, v, na, b)


def _rwkv_step_kernel(bb, r_ref, lw_ref, k_ref, v_ref, a_ref, b_ref, s_ref, y_ref, so_ref):
    n = HEAD_DIM
    eye = jnp.where(lax.broadcasted_iota(jnp.int32, (n, n), 0) == lax.broadcasted_iota(jnp.int32, (n, n), 1), 1.0, 0.0)

    def body(bi, carry):
        rb = r_ref[bi]
        dec = jnp.exp(lw_ref[bi])
        kb = k_ref[bi]
        vb = v_ref[bi]
        ab = a_ref[bi]
        bb_ = b_ref[bi]
        heads = range(N_RWKV_HEADS)
        row = lambda x, h: x[h:h + 1]
        s = [s_ref[bi, h] for h in heads]
        sa = [jnp.sum(s[h] * row(ab, h), axis=1, keepdims=True) for h in heads]
        vcol = [jnp.sum(eye * row(vb, h), axis=1, keepdims=True) for h in heads]
        s2 = [s[h] * row(dec, h) + sa[h] * row(bb_, h) + vcol[h] * row(kb, h) for h in heads]
        ycol = [jnp.sum(s2[h] * row(rb, h), axis=1, keepdims=True) for h in heads]
        ys = [jnp.sum(eye * ycol[h], axis=0, keepdims=True) for h in heads]
        for h in heads:
            so_ref[bi, h] = s2[h]
        y_ref[bi] = jnp.concatenate(ys, axis=0)
        return carry

    lax.fori_loop(0, bb, body, 0)


def _rwkv_sample(r, lw, kf, v, na, b, state):
    nb = r.shape[0]
    bb = 8
    vec = lambda x: x.reshape(nb, N_RWKV_HEADS, HEAD_DIM)
    vspec = pl.BlockSpec((bb, N_RWKV_HEADS, HEAD_DIM), lambda i: (i, 0, 0))
    sspec = pl.BlockSpec((bb, N_RWKV_HEADS, HEAD_DIM, HEAD_DIM), lambda i: (i, 0, 0, 0))
    return pl.pallas_call(
        functools.partial(_rwkv_step_kernel, bb),
        grid=(nb // bb,),
        in_specs=[vspec] * 6 + [sspec],
        out_specs=[vspec, sspec],
        out_shape=[jax.ShapeDtypeStruct((nb, N_RWKV_HEADS, HEAD_DIM), F32),
                   jax.ShapeDtypeStruct(state.shape, F32)],
        compiler_params=pltpu.CompilerParams(dimension_semantics=("arbitrary",)),
        name="rwkv_step",
    )(vec(r), vec(lw), vec(kf), vec(v), vec(na), vec(b), state)


def _alibi_slope(head):
    return 2.0 ** (-8.0 * (head + 1) / N_Q_HEADS)


def _swa_prompt_kernel(q_ref, kc_ref, kp_ref, vc_ref, vp_ref, sink_ref, o_ref):
    n = pl.program_id(0)
    w = WINDOW
    kcat = jnp.concatenate([kp_ref[...], kc_ref[...]], axis=0)
    vcat = jnp.concatenate([vp_ref[...], vc_ref[...]], axis=0)
    lane_k = lax.broadcasted_iota(jnp.int32, kcat.shape, 1) < HEAD_DIM
    kswap = pltpu.roll(kcat, HEAD_DIM, axis=1)
    vswap = pltpu.roll(vcat, HEAD_DIM, axis=1)
    kdup = [jnp.where(lane_k, kcat, kswap).astype(BF16), jnp.where(lane_k, kswap, kcat).astype(BF16)]
    vdup = [jnp.where(lane_k, vcat, vswap).astype(BF16), jnp.where(lane_k, vswap, vcat).astype(BF16)]
    qi = lax.broadcasted_iota(jnp.int32, (w, 2 * w), 0)
    kj = lax.broadcasted_iota(jnp.int32, (w, 2 * w), 1)
    dist = qi + w - kj
    valid = (dist >= 0) & (dist <= w) & ((n > 0) | (kj >= w))
    distf = dist.astype(F32)
    lane_q = lax.broadcasted_iota(jnp.int32, (w, LANES), 1) < HEAD_DIM
    sinks = sink_ref[...]
    group = 4
    for j0 in range(0, N_Q_HEADS // 2, group):
        js = range(j0, j0 + group)
        heads = [2 * j + half for j in js for half in range(2)]
        kvh = {h: h // Q_PER_KV for h in heads}
        q2 = {}
        for j in js:
            qp = q_ref[:, LANES * j:LANES * (j + 1)] * (1.0 / math.sqrt(HEAD_DIM))
            q2[j] = jnp.concatenate([jnp.where(lane_q, qp, 0.0), jnp.where(lane_q, 0.0, qp)], axis=0).astype(BF16)
        s2 = {j: _dot_nt(q2[j], kdup[kvh[2 * j]]) for j in js}
        s = {h: jnp.where(valid, s2[h // 2][w * (h % 2):w * (h % 2 + 1)] - _alibi_slope(h) * distf, -jnp.inf)
             for h in heads}
        sink = {h: sinks[0:1, h:h + 1] for h in heads}
        m = {h: jnp.maximum(jnp.max(s[h], axis=-1, keepdims=True), sink[h]) for h in heads}
        p = {h: jnp.exp(s[h] - m[h]) for h in heads}
        den = {h: jnp.sum(p[h], axis=-1, keepdims=True) + jnp.exp(sink[h] - m[h]) for h in heads}
        o = {h: _dot(p[h].astype(BF16), vdup[kvh[h]]) / den[h] for h in heads}
        for j in js:
            o_ref[:, LANES * j:LANES * (j + 1)] = jnp.where(lane_q, o[2 * j], o[2 * j + 1])


def _swa_prompt(q, ka, va, sinks):
    t = q.shape[0]
    w = WINDOW
    cur = lambda n: (n, 0)
    prv = lambda n: (jnp.maximum(n - 1, 0), 0)
    kvs = lambda f: pl.BlockSpec((w, KV_WIDTH), f)
    return pl.pallas_call(
        _swa_prompt_kernel,
        grid=(t // w,),
        in_specs=[pl.BlockSpec((w, ATTN_WIDTH), cur), kvs(cur), kvs(prv), kvs(cur), kvs(prv),
                  pl.BlockSpec((1, N_Q_HEADS), lambda n: (0, 0))],
        out_specs=pl.BlockSpec((w, ATTN_WIDTH), cur),
        out_shape=jax.ShapeDtypeStruct((t, ATTN_WIDTH), F32),
        compiler_params=pltpu.CompilerParams(dimension_semantics=("arbitrary",)),
        name="swa_prompt",
    )(q, ka, ka, va, va, sinks)


def _swa_sample_kernel(q_ref, kn_ref, vn_ref, ck_ref, cv_ref, sink_ref, o_ref, ko_ref, vo_ref):
    wb = ck_ref.shape[1]
    q = q_ref[...] * (1.0 / math.sqrt(HEAD_DIM))
    q2 = jnp.concatenate([q, q], axis=2)
    rowh = lax.broadcasted_iota(jnp.int32, q2.shape, 1) // Q_PER_KV
    laneh = lax.broadcasted_iota(jnp.int32, q2.shape, 2) // HEAD_DIM
    qb = jnp.where(rowh == laneh, q2, 0.0)
    kn = kn_ref[...]
    vn = vn_ref[...]
    ck = ck_ref[...]
    cv = cv_ref[...]
    s = jnp.einsum('bqd,bkd->bqk', qb.astype(BF16), ck.astype(BF16), preferred_element_type=F32)
    s_self = jnp.sum(qb * kn, axis=2, keepdims=True)
    head = lax.broadcasted_iota(jnp.int32, (1, N_Q_HEADS, 1), 1).astype(F32)
    slope = jnp.exp2(-8.0 * (head + 1.0) / N_Q_HEADS)
    dist = (wb - lax.broadcasted_iota(jnp.int32, (1, 1, wb), 2)).astype(F32)
    s = s - slope * dist
    sink = sink_ref[...][None]
    m = jnp.maximum(jnp.maximum(jnp.max(s, axis=2, keepdims=True), s_self), sink)
    p = jnp.exp(s - m)
    p_self = jnp.exp(s_self - m)
    den = jnp.sum(p, axis=2, keepdims=True) + p_self + jnp.exp(sink - m)
    o = jnp.einsum('bqk,bkd->bqd', p.astype(BF16), cv.astype(BF16), preferred_element_type=F32)
    o = (o + p_self * vn) / den
    sel = lax.broadcasted_iota(jnp.int32, (1, N_Q_HEADS, HEAD_DIM), 1) < Q_PER_KV
    o_ref[...] = jnp.where(sel, o[:, :, :HEAD_DIM], o[:, :, HEAD_DIM:])
    ko_ref[:, 0:wb - 1, :] = ck_ref[:, 1:wb, :]
    ko_ref[:, wb - 1:wb, :] = kn
    vo_ref[:, 0:wb - 1, :] = cv_ref[:, 1:wb, :]
    vo_ref[:, wb - 1:wb, :] = vn


def _swa_sample(q, ka, va, cache_k, cache_v, sinks_col):
    nb, wb = cache_k.shape[0], cache_k.shape[1]
    bb = 16
    b3 = lambda i: (i, 0, 0)
    nspec = pl.BlockSpec((bb, 1, KV_WIDTH), b3)
    cspec = pl.BlockSpec((bb, wb, KV_WIDTH), b3)
    qspec = pl.BlockSpec((bb, N_Q_HEADS, HEAD_DIM), b3)
    return pl.pallas_call(
        _swa_sample_kernel,
        grid=(nb // bb,),
        in_specs=[qspec, nspec, nspec, cspec, cspec, pl.BlockSpec((N_Q_HEADS, 1), lambda i: (0, 0))],
        out_specs=[qspec, cspec, cspec],
        out_shape=[jax.ShapeDtypeStruct((nb, N_Q_HEADS, HEAD_DIM), F32),
                   jax.ShapeDtypeStruct(cache_k.shape, F32), jax.ShapeDtypeStruct(cache_v.shape, F32)],
        compiler_params=pltpu.CompilerParams(dimension_semantics=("arbitrary",)),
        name="swa_sample",
    )(q.reshape(nb, N_Q_HEADS, HEAD_DIM), ka.reshape(nb, 1, KV_WIDTH), va.reshape(nb, 1, KV_WIDTH),
      cache_k, cache_v, sinks_col)


def _post_mix_kernel(is_prompt, yr_ref, g_ref, bonus_ref, ya_ref, x_ref, gt1_ref, sh2_ref, sc2_ref,
                     lnw_ref, lnb_ref, wout_ref, gpost_ref, gpre_ref, rw_ref, rb_ref,
                     x1_o, h2_o, ti_o, tw_o):
    gt1 = gt1_ref[...]
    sh2 = sh2_ref[...]
    sc2 = sc2_ref[...]
    if is_prompt:
        gt1, sh2, sc2 = gt1[0:1], sh2[0:1], sc2[0:1]
    y = yr_ref[...]
    mean = _seg_sum64(y) * (1.0 / HEAD_DIM)
    dlt = y - mean
    var = _seg_sum64(dlt * dlt) * (1.0 / HEAD_DIM)
    yn = dlt * lax.rsqrt(var + GN_EPS) * lnw_ref[...] + lnb_ref[...]
    yr = (yn + bonus_ref[...]) * g_ref[...]
    mix = _dot(jnp.concatenate([yr, ya_ref[...]], axis=1).astype(BF16), wout_ref[...])
    x1 = x_ref[...] + gt1 * _rmsnorm(mix, gpost_ref[...])
    x1_o[...] = x1
    h2 = _rmsnorm(x1, gpre_ref[...]) * (1.0 + sc2) + sh2
    h2_o[...] = h2
    logits = _dot3(h2, rw_ref[...]) + rb_ref[...]
    lane = lax.broadcasted_iota(jnp.int32, logits.shape, 1)
    vals, idxs = [], []
    for _ in range(TOP_K):
        m = jnp.max(logits, axis=1, keepdims=True)
        idx = jnp.min(jnp.where(logits == m, lane, N_EXPERTS), axis=1, keepdims=True)
        vals.append(m)
        idxs.append(idx)
        logits = jnp.where(lane == idx, -jnp.inf, logits)
    e = jnp.exp(jnp.concatenate(vals, axis=1) - vals[0])
    tw_o[...] = e / jnp.sum(e, axis=1, keepdims=True)
    ti_o[...] = jnp.concatenate(idxs, axis=1)


def _post_mix(is_prompt, tm, yr, g, bonus, ya, x, mod, lnw, lnb, wout, gpost, gpre, rw, rb):
    m, d = x.shape
    row = lambda i: (i, 0)
    if is_prompt:
        mod_rows = 8
        mod_map = lambda c: (lambda i: (0, c))
    else:
        mod_rows = tm
        mod_map = lambda c: (lambda i: (i, c))
    wide = pl.BlockSpec((tm, RWKV_WIDTH), row)
    in_specs = [wide, wide, wide, wide, pl.BlockSpec((tm, d), row),
                pl.BlockSpec((mod_rows, d), mod_map(2)), pl.BlockSpec((mod_rows, d), mod_map(3)),
                pl.BlockSpec((mod_rows, d), mod_map(4)),
                _resident(lnw.shape), _resident(lnb.shape), _resident(wout.shape), _resident(gpost.shape),
                _resident(gpre.shape), _resident(rw.shape), _resident(rb.shape)]
    out_shape = [jax.ShapeDtypeStruct((m, d), F32), jax.ShapeDtypeStruct((m, d), F32),
                 jax.ShapeDtypeStruct((m, TOP_K), jnp.int32), jax.ShapeDtypeStruct((m, TOP_K), F32)]
    out_specs = [pl.BlockSpec((tm, d), row), pl.BlockSpec((tm, d), row),
                 pl.BlockSpec((tm, TOP_K), row), pl.BlockSpec((tm, TOP_K), row)]
    return pl.pallas_call(
        functools.partial(_post_mix_kernel, is_prompt),
        grid=(m // tm,), in_specs=in_specs, out_specs=out_specs, out_shape=out_shape,
        compiler_params=pltpu.CompilerParams(dimension_semantics=("arbitrary",), vmem_limit_bytes=VMEM_LIMIT),
        name="post_mix_prompt" if is_prompt else "post_mix_sample",
    )(yr, g, bonus, ya, x, mod, mod, mod, lnw, lnb, wout, gpost, gpre, rw, rb)


def _moe_kernel(ex_ref, rb_ref, ns_ref, tok_ref, h2_hbm, w1_ref, b1g_ref, b1l_ref, w2_ref, b2_ref, ys_hbm,
                xbuf, xb, acc, w1p, w2b, sem_in, sem_out):
    s = pl.program_id(0)
    f = pl.program_id(1)
    n_super = pl.num_programs(0)
    ns = ns_ref[s]
    rb = rb_ref[s]
    rows = MOE_ROWS
    grp = 2 * LANES
    share = rows // MOE_NF

    def gather_share(sup, n_tiles, col):
        base = rb_ref[sup] * rows

        def body(j, c):
            r0 = j * rows + col * share
            for u in range(share):
                pltpu.make_async_copy(h2_hbm.at[pl.ds(tok_ref[base + r0 + u], 1)], xbuf.at[pl.ds(r0 + u, 1)],
                                      sem_in).start()
            return c
        lax.fori_loop(0, n_tiles, body, 0)

    def gather_wait(j):
        sl = pl.ds(j * rows, rows)
        pltpu.make_async_copy(h2_hbm.at[pl.ds(0, rows)], xbuf.at[sl], sem_in).wait()

    def y_copy(j):
        return pltpu.make_async_copy(acc.at[pl.ds(j * rows, rows)], ys_hbm.at[pl.ds((rb + j) * rows, rows)], sem_out)

    def for_tiles(fn):
        def body(j, c):
            fn(j)
            return c
        lax.fori_loop(0, ns, body, 0)

    @pl.when(ns > 0)
    def _():
        @pl.when(f == 0)
        def _():
            @pl.when(s == 0)
            def _():
                for col in range(MOE_NF):
                    gather_share(0, ns, col)
            b2 = jnp.broadcast_to(b2_ref[0], (rows, D_MODEL))

            def init(j):
                acc[pl.ds(pl.multiple_of(j * rows, rows), rows), :] = b2
            for_tiles(init)

            for_tiles(gather_wait)

            def cast(j):
                sl = pl.ds(pl.multiple_of(j * rows, rows), rows)
                xb[sl, :] = xbuf[sl, :].astype(BF16)
            for_tiles(cast)

        nxt = jnp.minimum(s + 1, n_super - 1)
        gather_share(nxt, jnp.where(s + 1 < n_super, jnp.maximum(ns_ref[nxt], 0), 0), f)

        pr = lax.broadcasted_iota(jnp.int32, (grp, grp), 0)
        pc = lax.broadcasted_iota(jnp.int32, (grp, grp), 1)
        perm = jnp.where(pr == jnp.where(pc < LANES, 2 * pc, 2 * (pc - LANES) + 1), 1.0, 0.0).astype(BF16)
        n_grp = 2 * MOE_FT // grp
        for g in range(n_grp):
            gs = slice(grp * g, grp * (g + 1))
            w1p[:, gs] = _dot(w1_ref[0, :, gs].astype(BF16), perm).astype(BF16)
        w2b[...] = w2_ref[0].astype(BF16)
        b1g = b1g_ref[0]
        b1l = b1l_ref[0]

        last_col = f == pl.num_programs(1) - 1

        def swiglu(hh):
            glu = jnp.concatenate([hh[:, grp * g:grp * g + LANES] for g in range(n_grp)], axis=1) + b1g
            lin = jnp.concatenate([hh[:, grp * g + LANES:grp * (g + 1)] for g in range(n_grp)], axis=1) + b1l
            glu = jnp.minimum(glu, SWIGLU_LIMIT)
            lin = jnp.clip(lin, -SWIGLU_LIMIT, SWIGLU_LIMIT)
            return (glu * jax.nn.sigmoid(SWIGLU_ALPHA * glu) * (lin + 1.0)).astype(BF16)

        def tiles(js):
            sls = [pl.ds(pl.multiple_of(j * rows, rows), rows) for j in js]
            hh = [_dot(xb[sl, :], w1p[...]) for sl in sls]
            act = [swiglu(h) for h in hh]
            out = [_dot(a, w2b[...]) for a in act]
            for sl, o in zip(sls, out):
                acc[sl, :] += o

            @pl.when(last_col)
            def _():
                for j in js:
                    y_copy(j).start()

        def tile_pair(q, c):
            tiles([2 * q, 2 * q + 1])
            return c
        lax.fori_loop(0, ns // 2, tile_pair, 0)

        @pl.when(ns % 2 == 1)
        def _():
            tiles([ns - 1])

        @pl.when(last_col)
        def _():
            for_tiles(lambda j: y_copy(j).wait())

    @pl.when((ns < 0) & (f == 0))
    def _():
        acc[0:rows, :] = jnp.zeros((rows, D_MODEL), F32)

        def z_copy(j):
            return pltpu.make_async_copy(acc.at[pl.ds(0, rows)], ys_hbm.at[pl.ds((rb + j) * rows, rows)], sem_out)

        def body(j, c, op):
            op(z_copy(j))
            return c
        lax.fori_loop(0, -ns, functools.partial(body, op=lambda cp: cp.start()), 0)
        lax.fori_loop(0, -ns, functools.partial(body, op=lambda cp: cp.wait()), 0)


def _moe_experts(ex, rb, ns, src_tok, h2, w1, b1g, b1l, w2, b2):
    n_rows = src_tok.shape[0]
    n_super = ex.shape[0]
    nf = MOE_NF
    last = nf - 1
    sub_rows = MOE_SUB * MOE_ROWS

    def fcol(s, f, ns_):
        return jnp.where(ns_[s] > 0, f, last)

    grid_spec = pltpu.PrefetchScalarGridSpec(
        num_scalar_prefetch=4,
        grid=(n_super, nf),
        in_specs=[pl.BlockSpec(memory_space=pl.ANY),
                  pl.BlockSpec((1, D_MODEL, 2 * MOE_FT), lambda s, f, e_, r_, n_, t_: (e_[s], 0, fcol(s, f, n_))),
                  pl.BlockSpec((1, 1, MOE_FT), lambda s, f, e_, r_, n_, t_: (e_[s], 0, fcol(s, f, n_))),
                  pl.BlockSpec((1, 1, MOE_FT), lambda s, f, e_, r_, n_, t_: (e_[s], 0, fcol(s, f, n_))),
                  pl.BlockSpec((1, MOE_FT, D_MODEL), lambda s, f, e_, r_, n_, t_: (e_[s], fcol(s, f, n_), 0)),
                  pl.BlockSpec((1, 1, D_MODEL), lambda s, f, e_, r_, n_, t_: (e_[s], 0, 0))],
        out_specs=pl.BlockSpec(memory_space=pl.ANY),
        scratch_shapes=[pltpu.VMEM((sub_rows, D_MODEL), F32),
                        pltpu.VMEM((sub_rows, D_MODEL), BF16),
                        pltpu.VMEM((sub_rows, D_MODEL), F32),
                        pltpu.VMEM((D_MODEL, 2 * MOE_FT), BF16),
                        pltpu.VMEM((MOE_FT, D_MODEL), BF16),
                        pltpu.SemaphoreType.DMA, pltpu.SemaphoreType.DMA])
    return pl.pallas_call(
        _moe_kernel,
        grid_spec=grid_spec,
        out_shape=jax.ShapeDtypeStruct((n_rows, D_MODEL), F32),
        compiler_params=pltpu.CompilerParams(dimension_semantics=("arbitrary", "arbitrary"),
                                             vmem_limit_bytes=VMEM_LIMIT),
        name="moe_experts",
    )(ex, rb, ns, src_tok, h2, w1, b1g, b1l, w2, b2)


def _combine_kernel(is_prompt, idx_ref, nidx_ref, ys_hbm, x1_ref, tw_ref, gt2_ref, gpost_ref, o_ref, buf, sem):
    tk = COMBINE_TOK
    n = TOP_K * tk
    unroll = 16
    i = pl.program_id(0)
    slot = i % 2

    def start_rows(ref, buf_slot):
        def body(q, c):
            for u in range(unroll):
                j = q * unroll + u
                pltpu.make_async_copy(ys_hbm.at[pl.ds(ref[0, 0, j], 1)], buf.at[buf_slot, pl.ds(j, 1)],
                                      sem.at[buf_slot]).start()
            return c
        lax.fori_loop(0, n // unroll, body, 0)

    @pl.when(i == 0)
    def _():
        start_rows(idx_ref, 0)

    @pl.when(i + 1 < pl.num_programs(0))
    def _():
        start_rows(nidx_ref, 1 - slot)

    pltpu.make_async_copy(ys_hbm.at[pl.ds(0, n)], buf.at[slot], sem.at[slot]).wait()
    tw = tw_ref[...]
    f = tw[:, 0:1] * buf[slot, 0:tk, :]
    for k in range(1, TOP_K):
        f = f + tw[:, k:k + 1] * buf[slot, k * tk:(k + 1) * tk, :]
    gt2 = gt2_ref[...]
    if is_prompt:
        gt2 = gt2[0:1]
    o_ref[...] = x1_ref[...] + gt2 * _rmsnorm(f, gpost_ref[...])


def _combine(is_prompt, pos, ys, x1, tw, mod, gpost):
    m, d = x1.shape
    tk = COMBINE_TOK
    nblk = m // tk
    idx = pos.reshape(nblk, tk, TOP_K).transpose(0, 2, 1).reshape(nblk, 1, TOP_K * tk)
    row = lambda i: (i, 0)
    mod_spec = (pl.BlockSpec((8, d), lambda i: (0, 5)) if is_prompt else pl.BlockSpec((tk, d), lambda i: (i, 5)))
    return pl.pallas_call(
        functools.partial(_combine_kernel, is_prompt),
        grid=(nblk,),
        in_specs=[pl.BlockSpec((1, 1, TOP_K * tk), lambda i: (i, 0, 0), memory_space=pltpu.SMEM),
                  pl.BlockSpec((1, 1, TOP_K * tk), lambda i: (jnp.minimum(i + 1, nblk - 1), 0, 0),
                               memory_space=pltpu.SMEM),
                  pl.BlockSpec(memory_space=pl.ANY),
                  pl.BlockSpec((tk, d), row), pl.BlockSpec((tk, TOP_K), row), mod_spec,
                  pl.BlockSpec((1, d), lambda i: (0, 0))],
        out_specs=pl.BlockSpec((tk, d), row),
        out_shape=jax.ShapeDtypeStruct((m, d), F32),
        scratch_shapes=[pltpu.VMEM((2, TOP_K * tk, d), F32), pltpu.SemaphoreType.DMA((2,))],
        compiler_params=pltpu.CompilerParams(dimension_semantics=("arbitrary",)),
        name="moe_combine_prompt" if is_prompt else "moe_combine_sample",
    )(idx, idx, ys, x1, tw, mod, gpost)


def _routing_tables(top_i):
    n_assign = top_i.size
    e_flat = top_i.reshape(-1)
    onehot = (e_flat[:, None] == jnp.arange(N_EXPERTS, dtype=jnp.int32)[None, :]).astype(jnp.int32)
    counts = jnp.sum(onehot, axis=0)
    rank = jnp.sum((jnp.cumsum(onehot, axis=0) - onehot) * onehot, axis=1)
    tiles = (counts + MOE_ROWS - 1) // MOE_ROWS
    tile_start = jnp.cumsum(tiles) - tiles
    pos = tile_start[e_flat] * MOE_ROWS + rank
    n_rows = _padded_rows(n_assign)
    assert n_assign < (1 << 16)
    order = jnp.sort(e_flat * (1 << 16) + jnp.arange(n_assign, dtype=jnp.int32)) & 0xFFFF
    p = jnp.arange(n_rows, dtype=jnp.int32)
    e_of_p = jnp.minimum(jnp.searchsorted(jnp.cumsum(tiles), p // MOE_ROWS, side='right'), N_EXPERTS - 1)
    r_of_p = p - tile_start[e_of_p] * MOE_ROWS
    src = order[jnp.minimum((jnp.cumsum(counts) - counts)[e_of_p] + r_of_p, n_assign - 1)] // TOP_K
    src_tok = jnp.where(r_of_p < counts[e_of_p], src, 0).astype(jnp.int32)
    supers = (tiles + MOE_SUB - 1) // MOE_SUB
    super_end = jnp.cumsum(supers)
    s_idx = jnp.arange(_max_supers(n_assign), dtype=jnp.int32)
    ex = jnp.minimum(jnp.searchsorted(super_end, s_idx, side='right'), N_EXPERTS - 1).astype(jnp.int32)
    j = s_idx - (super_end - supers)[ex]
    live = s_idx < super_end[-1]
    tail = jnp.sum(tiles) + MOE_SUB * (s_idx - super_end[-1])
    n_clear = jnp.clip(n_rows // MOE_ROWS - tail, 0, MOE_SUB)
    ns = jnp.where(live, jnp.clip(tiles[ex] - MOE_SUB * j, 0, MOE_SUB), -n_clear).astype(jnp.int32)
    rb = jnp.where(live, tile_start[ex] + MOE_SUB * j, tail).astype(jnp.int32)
    last_live = jnp.max(jnp.where(live, ex, 0))
    ex = jnp.where(live, ex, last_live).astype(jnp.int32)
    return pos.astype(jnp.int32), src_tok, ex, rb, ns


def _padded_rows(n_assign):
    rows = n_assign + N_EXPERTS * (MOE_ROWS - 1)
    return -(-rows // MOE_ROWS) * MOE_ROWS


def _max_supers(n_assign):
    return N_EXPERTS + -(-_padded_rows(n_assign) // (MOE_ROWS * MOE_SUB))


def kernel(x_prompt, x_sample, cache_k, cache_v, state_wkv, state_shift, c_prompt, c_sample, w_ada, b_ada, g_pre_mix, g_post_mix, g_pre_ffn, g_post_ffn, mu_shift, w_in, rwkv_w0, rwkv_w2, rwkv_a0, rwkv_a2, rwkv_g2, rwkv_k_k, rwkv_k_a, rwkv_r_k, rwkv_ln_w, rwkv_ln_b, attn_sinks, w_out, router_w, router_b, moe_w1, moe_b1, moe_w2, moe_b2):
    assert w_ada.shape[0] == 1, "single-layer step"
    d = D_MODEL
    t = x_prompt.shape[1]
    nb = x_sample.shape[0]
    xp = x_prompt.reshape(t, d)
    xs = x_sample.reshape(nb, d)

    c_all = jnp.concatenate([jnp.broadcast_to(c_prompt, (8, d)), c_sample], axis=0)
    mod = _adaln_mod(c_all, w_ada[0], b_ada[0].reshape(1, 6 * d))
    mod_p, mod_s = mod[:8], mod[8:]

    offs = [0, 1024, 2048, 3072, 3136, 3200, 3360]
    w_in0 = w_in[0]
    pad_to = lambda w, n: jnp.pad(w, ((0, 0), (0, n - w.shape[1])))
    wts = [w_in0[:, offs[0]:offs[1]], w_in0[:, offs[1]:offs[2]], w_in0[:, offs[2]:offs[3]],
           pad_to(w_in0[:, offs[3]:offs[4]], 128), pad_to(w_in0[:, offs[4]:offs[5]], 128),
           pad_to(w_in0[:, offs[5]:offs[6]], 256), w_in0[:, offs[6]:]]
    wts = [w.astype(BF16) for w in wts]
    pad_rows = lambda w, n: jnp.pad(w, ((0, n - w.shape[0]), (0, 0))).astype(BF16)
    row = lambda p: p.reshape(1, -1)
    rw = [row(rwkv_w0[0]), pad_rows(rwkv_w2[0], 128), row(rwkv_a0[0]), pad_rows(rwkv_a2[0], 128),
          pad_rows(rwkv_g2[0], 256), row(rwkv_k_k[0]), row(rwkv_k_a[0]), row(rwkv_r_k[0])]
    gpm = row(g_pre_mix[0])
    mu8 = jnp.pad(mu_shift[0], ((0, 8 - N_SHIFTED), (0, 0)))

    pr = _in_proj(True, 256, xp, xp, mod_p, gpm, mu8, wts, rw)
    sr = _in_proj(False, nb, xs, state_shift[0], mod_s, gpm, mu8, wts, rw)
    r_p, lw_p, kf_p, v_p, na_p, b_p, g_p, bonus_p, q_p, ka_p, va_p, hlast_p = pr
    r_s, lw_s, kf_s, v_s, na_s, b_s, g_s, bonus_s, q_s, ka_s, va_s, h_s = sr

    y_p, st_p = _rwkv_prompt(r_p, lw_p, kf_p, v_p, na_p, b_p)
    y_s, wkv_s = _rwkv_sample(r_s, lw_s, kf_s, v_s, na_s, b_s, state_wkv[0])
    y_s = y_s.reshape(nb, RWKV_WIDTH)
    sinks = attn_sinks[0]
    att_p = _swa_prompt(q_p, ka_p, va_p, sinks.reshape(1, N_Q_HEADS))
    wb = cache_k.shape[2]
    att_s, ck_new, cv_new = _swa_sample(q_s, ka_s, va_s, cache_k[0].reshape(nb, wb, KV_WIDTH),
                                        cache_v[0].reshape(nb, wb, KV_WIDTH), sinks.reshape(N_Q_HEADS, 1))
    att_s = att_s.reshape(nb, ATTN_WIDTH)

    post = [row(rwkv_ln_w[0]), row(rwkv_ln_b[0]), w_out[0].astype(BF16), row(g_post_mix[0]),
            row(g_pre_ffn[0]), router_w[0], row(router_b[0])]
    x1_p, h2_p, ti_p, tw_p = _post_mix(True, 256, y_p, g_p, bonus_p, att_p, xp, mod_p, *post)
    x1_s, h2_s, ti_s, tw_s = _post_mix(False, nb, y_s, g_s, bonus_s, att_s, xs, mod_s, *post)

    h2_all = jnp.concatenate([h2_p, h2_s], axis=0)
    pos, src_tok, ex, rb, ns = _routing_tables(jnp.concatenate([ti_p, ti_s], axis=0))
    b1 = moe_b1[0].reshape(N_EXPERTS, 1, D_FF, 2)
    ys_rows = _moe_experts(ex, rb, ns, src_tok, h2_all, moe_w1[0], b1[..., 0], b1[..., 1],
                           moe_w2[0], moe_b2[0].reshape(N_EXPERTS, 1, d))
    pos = pos.reshape(-1, TOP_K)
    gpf = row(g_post_ffn[0])
    out_p = _combine(True, pos[:t], ys_rows, x1_p, tw_p, mod_p, gpf)
    out_s = _combine(False, pos[t:], ys_rows, x1_s, tw_s, mod_s, gpf)

    n_keep = min(WINDOW, t)
    st_heads = jnp.stack([st_p[:, :HEAD_DIM, :HEAD_DIM], st_p[:, HEAD_DIM:, HEAD_DIM:]], axis=1)
    return (out_p.reshape(1, t, d),
            out_s.reshape(nb, 1, d),
            ka_p[t - n_keep:].reshape(1, 1, n_keep, N_KV_HEADS, HEAD_DIM),
            va_p[t - n_keep:].reshape(1, 1, n_keep, N_KV_HEADS, HEAD_DIM),
            st_heads.reshape(1, 1, N_RWKV_HEADS, HEAD_DIM, HEAD_DIM),
            hlast_p[7:8].reshape(1, 1, d),
            ck_new.reshape(1, nb, wb, N_KV_HEADS, HEAD_DIM),
            cv_new.reshape(1, nb, wb, N_KV_HEADS, HEAD_DIM),
            wkv_s.reshape(1, nb, N_RWKV_HEADS, HEAD_DIM, HEAD_DIM),
            h_s.reshape(1, nb, d))
```

```python
import functools
import math

import jax
import jax.numpy as jnp
from jax import lax
from jax.experimental import pallas as pl
from jax.experimental.pallas import tpu as pltpu

F32 = jnp.float32
BF16 = jnp.bfloat16

D_MODEL = 2048
HEAD_DIM = 64
RWKV_WIDTH = 1024
N_RWKV_HEADS = 16
ATTN_WIDTH = 1024
N_Q_HEADS = 16
N_KV_HEADS = 2
Q_PER_KV = 8
KV_WIDTH = 128
WINDOW = 128
N_SHIFTED = 6
N_EXPERTS = 32
TOP_K = 4
D_FF = 2048
SWIGLU_ALPHA = 1.702
SWIGLU_LIMIT = 7.0
NORM_EPS = 1e-6
GN_EPS = 64e-5
L2_EPS = 1e-12

LANES = 128
VMEM_LIMIT = 56 * 1024 * 1024

CHUNK = 64
RWKV_ROWS = 256
RWKV_PAIRS = 8
MOE_ROWS = 256
MOE_SUB = 6
MOE_FT = 256
MOE_NF = D_FF // MOE_FT
COMBINE_TOK = 128


def _dot(a, b):
    return jnp.dot(a, b, preferred_element_type=F32)


def _dot_nt(a, b):
    return lax.dot_general(a, b, (((1,), (1,)), ((), ())), preferred_element_type=F32)


def _split_bf16(x):
    hi = x.astype(BF16)
    lo = (x - hi.astype(F32)).astype(BF16)
    return hi, lo


def _dot3(a, b):
    ah, al = _split_bf16(a)
    bh, bl = _split_bf16(b)
    return _dot(ah, bh) + _dot(ah, bl) + _dot(al, bh)


def _seg_sum64(x):
    r = lax.broadcasted_iota(jnp.int32, (LANES, LANES), 0) // HEAD_DIM
    c = lax.broadcasted_iota(jnp.int32, (LANES, LANES), 1) // HEAD_DIM
    bd = jnp.where(r == c, 1.0, 0.0).astype(BF16)
    hi, lo = _split_bf16(x)
    outs = []
    for j in range(x.shape[1] // LANES):
        sl = slice(LANES * j, LANES * (j + 1))
        outs.append(_dot(hi[:, sl], bd) + _dot(lo[:, sl], bd))
    return jnp.concatenate(outs, axis=1)


def _rmsnorm(x, g):
    ms = jnp.mean(x * x, axis=-1, keepdims=True)
    return x * lax.rsqrt(ms + NORM_EPS) * g


def _resident(shape):
    nd = len(shape)
    return pl.BlockSpec(shape, lambda *_: (0,) * nd, pipeline_mode=pl.Buffered(1))


def _mod_kernel(c_ref, w_ref, b_ref, o_ref):
    c = c_ref[...]
    s = c * jax.nn.sigmoid(c)
    o_ref[...] = _dot3(s, w_ref[...]) + b_ref[...]


def _adaln_mod(c, w_ada, b_ada):
    rows, d = c.shape
    n = w_ada.shape[1]
    tn = 512
    return pl.pallas_call(
        _mod_kernel,
        grid=(n // tn,),
        in_specs=[pl.BlockSpec((rows, d), lambda j: (0, 0)),
                  pl.BlockSpec((d, tn), lambda j: (0, j)),
                  pl.BlockSpec((1, tn), lambda j: (0, j))],
        out_specs=pl.BlockSpec((rows, tn), lambda j: (0, j)),
        out_shape=jax.ShapeDtypeStruct((rows, n), F32),
        compiler_params=pltpu.CompilerParams(dimension_semantics=("arbitrary",), vmem_limit_bytes=VMEM_LIMIT),
        name="adaln_mod",
    )(c, w_ada, b_ada)


def _inproj_kernel(is_prompt, tm,
                   x_ref, prev_ref, sh_ref, sc_ref, gpm_ref, mu_ref,
                   wr_ref, wk_ref, wv_ref, wwl_ref, wal_ref, wgl_ref, wqkv_ref,
                   w0_ref, w2_ref, a0_ref, a2_ref, g2_ref, kk_ref, ka_ref, rk_ref,
                   r_o, lw_o, kf_o, v_o, na_o, b_o, g_o, bonus_o, q_o, kat_o, vat_o, h_o):
    i = pl.program_id(0)
    gpm = gpm_ref[...]
    sh = sh_ref[...]
    sc = sc_ref[...]
    if is_prompt:
        sh = sh[0:1]
        sc = sc[0:1]

    def modnorm(x):
        return _rmsnorm(x, gpm) * (1.0 + sc) + sh

    h = modnorm(x_ref[...])
    if is_prompt:
        hp = modnorm(prev_ref[...])[7:8, :]
        hp = jnp.where(i > 0, hp, 0.0)
        row = lax.broadcasted_iota(jnp.int32, h.shape, 0)
        hprev = jnp.where(row == 0, hp, pltpu.roll(h, 1, axis=0))
        h_o[...] = h[tm - 8:tm, :]
    else:
        hprev = prev_ref[...]
        h_o[...] = h
    dx = hprev - h
    mu = mu_ref[...]

    def branch(j, w_ref):
        xi = (h + dx * mu[j:j + 1, :]).astype(BF16)
        return _dot(xi, w_ref[...])

    r = branch(0, wr_ref)
    k = branch(1, wk_ref)
    v = branch(2, wv_ref)
    wl = branch(3, wwl_ref)
    al = branch(4, wal_ref)
    gl = branch(5, wgl_ref)
    qkv = _dot(h.astype(BF16), wqkv_ref[...])
    q_o[...] = qkv[:, :ATTN_WIDTH]
    kat_o[...] = qkv[:, ATTN_WIDTH:ATTN_WIDTH + KV_WIDTH]
    vat_o[...] = qkv[:, ATTN_WIDTH + KV_WIDTH:]

    z = w0_ref[...] + _dot(jnp.tanh(wl).astype(BF16), w2_ref[...])
    w_raw = -jnp.logaddexp(-z, 0.0) - 0.5
    lw_o[...] = -jnp.exp(w_raw)
    a = jax.nn.sigmoid(a0_ref[...] + _dot(al.astype(BF16), a2_ref[...]))
    g_o[...] = _dot(jax.nn.sigmoid(gl).astype(BF16), g2_ref[...])
    kk = k * kk_ref[...]
    kk = kk / jnp.maximum(jnp.sqrt(_seg_sum64(kk * kk)), L2_EPS)
    kf = k * (1.0 + (a - 1.0) * ka_ref[...])
    r_o[...] = r
    kf_o[...] = kf
    v_o[...] = v
    na_o[...] = -kk
    b_o[...] = kk * a
    bonus_o[...] = _seg_sum64(r * kf * rk_ref[...]) * v


def _in_proj(is_prompt, tm, x, prev, mod, gpm, mu8, wts, rw):
    m, d = x.shape
    grid = (m // tm,)
    row = lambda i: (i, 0)
    if is_prompt:
        prev_spec = pl.BlockSpec((8, d), lambda i: (jnp.maximum(i * (tm // 8) - 1, 0), 0))
        mod_rows = 8
        mod_map = lambda c: (lambda i: (0, c))
        h_shape, h_spec = (8, d), pl.BlockSpec((8, d), lambda i: (0, 0))
    else:
        prev_spec = pl.BlockSpec((tm, d), row)
        mod_rows = tm
        mod_map = lambda c: (lambda i: (i, c))
        h_shape, h_spec = (m, d), pl.BlockSpec((tm, d), row)
    in_specs = [pl.BlockSpec((tm, d), row), prev_spec,
                pl.BlockSpec((mod_rows, d), mod_map(0)), pl.BlockSpec((mod_rows, d), mod_map(1)),
                _resident((1, d)), _resident((8, d))]
    in_specs += [_resident(w.shape) for w in wts]
    in_specs += [_resident(p.shape) for p in rw]
    wide = jax.ShapeDtypeStruct((m, RWKV_WIDTH), F32)
    wide_spec = pl.BlockSpec((tm, RWKV_WIDTH), row)
    kv = jax.ShapeDtypeStruct((m, KV_WIDTH), F32)
    kv_spec = pl.BlockSpec((tm, KV_WIDTH), row)
    out_shape = [wide] * 9 + [kv, kv, jax.ShapeDtypeStruct(h_shape, F32)]
    out_specs = [wide_spec] * 9 + [kv_spec, kv_spec, h_spec]
    return pl.pallas_call(
        functools.partial(_inproj_kernel, is_prompt, tm),
        grid=grid, in_specs=in_specs, out_specs=out_specs, out_shape=out_shape,
        compiler_params=pltpu.CompilerParams(dimension_semantics=("arbitrary",), vmem_limit_bytes=VMEM_LIMIT),
        name="in_proj_prompt" if is_prompt else "in_proj_sample",
    )(x, prev, mod, mod, gpm, mu8, *wts, *rw)


def _rwkv_chunk_kernel(r_ref, lw_ref, k_ref, v_ref, a_ref, b_ref, y_ref, s_ref, st_ref):
    t = pl.program_id(1)
    C = CHUNK
    P = 2 * HEAD_DIM

    @pl.when(t == 0)
    def _():
        st_ref[...] = jnp.zeros_like(st_ref)

    ri = lax.broadcasted_iota(jnp.int32, (P, P), 0)
    ci = lax.broadcasted_iota(jnp.int32, (P, P), 1)
    bd = (ri // C) == (ci // C)
    tril_s = bd & ((ri % C) > (ci % C))
    tril_i = bd & ((ri % C) >= (ci % C))
    eye = jnp.where(ri == ci, 1.0, 0.0)
    lane0 = lax.broadcasted_iota(jnp.int32, (C, P), 1) < HEAD_DIM
    trow = lax.broadcasted_iota(jnp.int32, (C, P), 0)

    def stack(x):
        return jnp.concatenate([jnp.where(lane0, x, 0.0), jnp.where(lane0, 0.0, x)], axis=0)

    def dup(x):
        return jnp.concatenate([x, x], axis=0)

    def prep(sl, pp):
        ln = slice(P * pp, P * (pp + 1))
        lw = lw_ref[sl, ln]
        cw = lw
        for s in (1, 2, 4, 8, 16, 32):
            cw = cw + jnp.where(trow >= s, pltpu.roll(cw, s, axis=0), 0.0)
        cw_last = cw[C - 1:C, :]
        e_neg = jnp.exp(-cw)
        e_end = jnp.exp(cw_last - cw)
        k = k_ref[sl, ln]
        v = v_ref[sl, ln]
        b = b_ref[sl, ln]
        a2 = stack(a_ref[sl, ln] * jnp.exp(cw - lw))
        r2 = stack(r_ref[sl, ln] * jnp.exp(cw))
        return dict(
            v2=stack(v),
            lhs=jnp.concatenate([a2, r2], axis=0).astype(BF16),
            rhs=jnp.concatenate([dup(k * e_neg), dup(b * e_neg)], axis=0).astype(BF16),
            kbh=jnp.concatenate([dup(b * e_end), dup(k * e_end)], axis=0).astype(BF16),
            decay=jnp.exp(cw_last))

    def chunk(c, carry):
        sl = pl.ds(pl.multiple_of(c * C, C), C)
        pairs = range(RWKV_PAIRS)
        st = [st_ref[pp] for pp in pairs]
        d = [prep(sl, pp) for pp in pairs]
        gram = [_dot_nt(d[pp]['lhs'], d[pp]['rhs']) for pp in pairs]
        l2 = [jnp.where(tril_s, gram[pp][0:P, P:2 * P], 0.0) for pp in pairs]
        a_s = [_dot_nt(d[pp]['lhs'], st[pp].astype(BF16)) for pp in pairs]
        v2b = [d[pp]['v2'].astype(BF16) for pp in pairs]
        rhs_u = [a_s[pp][0:P] + _dot(jnp.where(tril_s, gram[pp][0:P, 0:P], 0.0).astype(BF16), v2b[pp])
                 for pp in pairs]
        inv = [eye + l2[pp] for pp in pairs]
        lp = l2
        for _ in range(5):
            lpb = [lp[pp].astype(BF16) for pp in pairs]
            lp = [_dot(lpb[pp], lpb[pp]) for pp in pairs]
            inv = [inv[pp] + _dot(lp[pp].astype(BF16), inv[pp].astype(BF16)) for pp in pairs]
        u2 = [_dot(inv[pp].astype(BF16), rhs_u[pp].astype(BF16)) for pp in pairs]
        p_cat = [jnp.concatenate([jnp.where(tril_i, gram[pp][P:2 * P, P:2 * P], 0.0),
                                  jnp.where(tril_i, gram[pp][P:2 * P, 0:P], 0.0)], axis=1).astype(BF16)
                 for pp in pairs]
        y2 = [a_s[pp][P:2 * P] + _dot(p_cat[pp], jnp.concatenate([u2[pp].astype(BF16), v2b[pp]], axis=0))
              for pp in pairs]
        uvt = [jnp.concatenate([u2[pp], d[pp]['v2']], axis=0).T.astype(BF16) for pp in pairs]
        st_new = [jnp.where(bd, st[pp] * d[pp]['decay'] + _dot(uvt[pp], d[pp]['kbh']), 0.0) for pp in pairs]
        for pp in pairs:
            y_ref[sl, P * pp:P * (pp + 1)] = y2[pp][0:C] + y2[pp][C:2 * C]
            st_ref[pp] = st_new[pp]
        return carry

    lax.fori_loop(0, RWKV_ROWS // C, chunk, 0)

    @pl.when(t == pl.num_programs(1) - 1)
    def _():
        s_ref[...] = st_ref[...]


def _rwkv_prompt(r, lw, kf, v, na, b):
    t = r.shape[0]
    n_pairs = RWKV_WIDTH // LANES
    width = RWKV_PAIRS * LANES
    spec = pl.BlockSpec((RWKV_ROWS, width), lambda p, i: (i, p))
    return pl.pallas_call(
        _rwkv_chunk_kernel,
        grid=(n_pairs // RWKV_PAIRS, t // RWKV_ROWS),
        in_specs=[spec] * 6,
        out_specs=[spec, pl.BlockSpec((RWKV_PAIRS, LANES, LANES), lambda p, i: (p, 0, 0))],
        out_shape=[jax.ShapeDtypeStruct((t, RWKV_WIDTH), F32),
                   jax.ShapeDtypeStruct((n_pairs, LANES, LANES), F32)],
        scratch_shapes=[pltpu.VMEM((RWKV_PAIRS, LANES, LANES), F32)],
        compiler_params=pltpu.CompilerParams(dimension_semantics=("arbitrary", "arbitrary")),
        name="rwkv_chunked",
    )(r, lw, kf, v, na, b)


def _rwkv_step_kernel(bb, r_ref, lw_ref, k_ref, v_ref, a_ref, b_ref, s_ref, y_ref, so_ref):
    n = HEAD_DIM
    eye = jnp.where(lax.broadcasted_iota(jnp.int32, (n, n), 0) == lax.broadcasted_iota(jnp.int32, (n, n), 1), 1.0, 0.0)

    def body(bi, carry):
        rb = r_ref[bi]
        dec = jnp.exp(lw_ref[bi])
        kb = k_ref[bi]
        vb = v_ref[bi]
        ab = a_ref[bi]
        bb_ = b_ref[bi]
        heads = range(N_RWKV_HEADS)
        row = lambda x, h: x[h:h + 1]
        s = [s_ref[bi, h] for h in heads]
        sa = [jnp.sum(s[h] * row(ab, h), axis=1, keepdims=True) for h in heads]
        vcol = [jnp.sum(eye * row(vb, h), axis=1, keepdims=True) for h in heads]
        s2 = [s[h] * row(dec, h) + sa[h] * row(bb_, h) + vcol[h] * row(kb, h) for h in heads]
        ycol = [jnp.sum(s2[h] * row(rb, h), axis=1, keepdims=True) for h in heads]
        ys = [jnp.sum(eye * ycol[h], axis=0, keepdims=True) for h in heads]
        for h in heads:
            so_ref[bi, h] = s2[h]
        y_ref[bi] = jnp.concatenate(ys, axis=0)
        return carry

    lax.fori_loop(0, bb, body, 0)


def _rwkv_sample(r, lw, kf, v, na, b, state):
    nb = r.shape[0]
    bb = 8
    vec = lambda x: x.reshape(nb, N_RWKV_HEADS, HEAD_DIM)
    vspec = pl.BlockSpec((bb, N_RWKV_HEADS, HEAD_DIM), lambda i: (i, 0, 0))
    sspec = pl.BlockSpec((bb, N_RWKV_HEADS, HEAD_DIM, HEAD_DIM), lambda i: (i, 0, 0, 0))
    return pl.pallas_call(
        functools.partial(_rwkv_step_kernel, bb),
        grid=(nb // bb,),
        in_specs=[vspec] * 6 + [sspec],
        out_specs=[vspec, sspec],
        out_shape=[jax.ShapeDtypeStruct((nb, N_RWKV_HEADS, HEAD_DIM), F32),
                   jax.ShapeDtypeStruct(state.shape, F32)],
        compiler_params=pltpu.CompilerParams(dimension_semantics=("arbitrary",)),
        name="rwkv_step",
    )(vec(r), vec(lw), vec(kf), vec(v), vec(na), vec(b), state)


def _alibi_slope(head):
    return 2.0 ** (-8.0 * (head + 1) / N_Q_HEADS)


def _swa_prompt_kernel(q_ref, kc_ref, kp_ref, vc_ref, vp_ref, sink_ref, o_ref):
    n = pl.program_id(0)
    w = WINDOW
    kcat = jnp.concatenate([kp_ref[...], kc_ref[...]], axis=0)
    vcat = jnp.concatenate([vp_ref[...], vc_ref[...]], axis=0)
    lane_k = lax.broadcasted_iota(jnp.int32, kcat.shape, 1) < HEAD_DIM
    kswap = pltpu.roll(kcat, HEAD_DIM, axis=1)
    vswap = pltpu.roll(vcat, HEAD_DIM, axis=1)
    kdup = [jnp.where(lane_k, kcat, kswap).astype(BF16), jnp.where(lane_k, kswap, kcat).astype(BF16)]
    vdup = [jnp.where(lane_k, vcat, vswap).astype(BF16), jnp.where(lane_k, vswap, vcat).astype(BF16)]
    qi = lax.broadcasted_iota(jnp.int32, (w, 2 * w), 0)
    kj = lax.broadcasted_iota(jnp.int32, (w, 2 * w), 1)
    dist = qi + w - kj
    valid = (dist >= 0) & (dist <= w) & ((n > 0) | (kj >= w))
    distf = dist.astype(F32)
    lane_q = lax.broadcasted_iota(jnp.int32, (w, LANES), 1) < HEAD_DIM
    sinks = sink_ref[...]
    group = 4
    for j0 in range(0, N_Q_HEADS // 2, group):
        js = range(j0, j0 + group)
        heads = [2 * j + half for j in js for half in range(2)]
        kvh = {h: h // Q_PER_KV for h in heads}
        q2 = {}
        for j in js:
            qp = q_ref[:, LANES * j:LANES * (j + 1)] * (1.0 / math.sqrt(HEAD_DIM))
            q2[j] = jnp.concatenate([jnp.where(lane_q, qp, 0.0), jnp.where(lane_q, 0.0, qp)], axis=0).astype(BF16)
        s2 = {j: _dot_nt(q2[j], kdup[kvh[2 * j]]) for j in js}
        s = {h: jnp.where(valid, s2[h // 2][w * (h % 2):w * (h % 2 + 1)] - _alibi_slope(h) * distf, -jnp.inf)
             for h in heads}
        sink = {h: sinks[0:1, h:h + 1] for h in heads}
        m = {h: jnp.maximum(jnp.max(s[h], axis=-1, keepdims=True), sink[h]) for h in heads}
        p = {h: jnp.exp(s[h] - m[h]) for h in heads}
        den = {h: jnp.sum(p[h], axis=-1, keepdims=True) + jnp.exp(sink[h] - m[h]) for h in heads}
        o = {h: _dot(p[h].astype(BF16), vdup[kvh[h]]) / den[h] for h in heads}
        for j in js:
            o_ref[:, LANES * j:LANES * (j + 1)] = jnp.where(lane_q, o[2 * j], o[2 * j + 1])


def _swa_prompt(q, ka, va, sinks):
    t = q.shape[0]
    w = WINDOW
    cur = lambda n: (n, 0)
    prv = lambda n: (jnp.maximum(n - 1, 0), 0)
    kvs = lambda f: pl.BlockSpec((w, KV_WIDTH), f)
    return pl.pallas_call(
        _swa_prompt_kernel,
        grid=(t // w,),
        in_specs=[pl.BlockSpec((w, ATTN_WIDTH), cur), kvs(cur), kvs(prv), kvs(cur), kvs(prv),
                  pl.BlockSpec((1, N_Q_HEADS), lambda n: (0, 0))],
        out_specs=pl.BlockSpec((w, ATTN_WIDTH), cur),
        out_shape=jax.ShapeDtypeStruct((t, ATTN_WIDTH), F32),
        compiler_params=pltpu.CompilerParams(dimension_semantics=("arbitrary",)),
        name="swa_prompt",
    )(q, ka, ka, va, va, sinks)


def _swa_sample_kernel(q_ref, kn_ref, vn_ref, ck_ref, cv_ref, sink_ref, o_ref, ko_ref, vo_ref):
    wb = ck_ref.shape[1]
    q = q_ref[...] * (1.0 / math.sqrt(HEAD_DIM))
    q2 = jnp.concatenate([q, q], axis=2)
    rowh = lax.broadcasted_iota(jnp.int32, q2.shape, 1) // Q_PER_KV
    laneh = lax.broadcasted_iota(jnp.int32, q2.shape, 2) // HEAD_DIM
    qb = jnp.where(rowh == laneh, q2, 0.0)
    kn = kn_ref[...]
    vn = vn_ref[...]
    ck = ck_ref[...]
    cv = cv_ref[...]
    s = jnp.einsum('bqd,bkd->bqk', qb.astype(BF16), ck.astype(BF16), preferred_element_type=F32)
    s_self = jnp.sum(qb * kn, axis=2, keepdims=True)
    head = lax.broadcasted_iota(jnp.int32, (1, N_Q_HEADS, 1), 1).astype(F32)
    slope = jnp.exp2(-8.0 * (head + 1.0) / N_Q_HEADS)
    dist = (wb - lax.broadcasted_iota(jnp.int32, (1, 1, wb), 2)).astype(F32)
    s = s - slope * dist
    sink = sink_ref[...][None]
    m = jnp.maximum(jnp.maximum(jnp.max(s, axis=2, keepdims=True), s_self), sink)
    p = jnp.exp(s - m)
    p_self = jnp.exp(s_self - m)
    den = jnp.sum(p, axis=2, keepdims=True) + p_self + jnp.exp(sink - m)
    o = jnp.einsum('bqk,bkd->bqd', p.astype(BF16), cv.astype(BF16), preferred_element_type=F32)
    o = (o + p_self * vn) / den
    sel = lax.broadcasted_iota(jnp.int32, (1, N_Q_HEADS, HEAD_DIM), 1) < Q_PER_KV
    o_ref[...] = jnp.where(sel, o[:, :, :HEAD_DIM], o[:, :, HEAD_DIM:])
    ko_ref[:, 0:wb - 1, :] = ck_ref[:, 1:wb, :]
    ko_ref[:, wb - 1:wb, :] = kn
    vo_ref[:, 0:wb - 1, :] = cv_ref[:, 1:wb, :]
    vo_ref[:, wb - 1:wb, :] = vn


def _swa_sample(q, ka, va, cache_k, cache_v, sinks_col):
    nb, wb = cache_k.shape[0], cache_k.shape[1]
    bb = 16
    b3 = lambda i: (i, 0, 0)
    nspec = pl.BlockSpec((bb, 1, KV_WIDTH), b3)
    cspec = pl.BlockSpec((bb, wb, KV_WIDTH), b3)
    qspec = pl.BlockSpec((bb, N_Q_HEADS, HEAD_DIM), b3)
    return pl.pallas_call(
        _swa_sample_kernel,
        grid=(nb // bb,),
        in_specs=[qspec, nspec, nspec, cspec, cspec, pl.BlockSpec((N_Q_HEADS, 1), lambda i: (0, 0))],
        out_specs=[qspec, cspec, cspec],
        out_shape=[jax.ShapeDtypeStruct((nb, N_Q_HEADS, HEAD_DIM), F32),
                   jax.ShapeDtypeStruct(cache_k.shape, F32), jax.ShapeDtypeStruct(cache_v.shape, F32)],
        compiler_params=pltpu.CompilerParams(dimension_semantics=("arbitrary",)),
        name="swa_sample",
    )(q.reshape(nb, N_Q_HEADS, HEAD_DIM), ka.reshape(nb, 1, KV_WIDTH), va.reshape(nb, 1, KV_WIDTH),
      cache_k, cache_v, sinks_col)


def _post_mix_kernel(is_prompt, yr_ref, g_ref, bonus_ref, ya_ref, x_ref, gt1_ref, sh2_ref, sc2_ref,
                     lnw_ref, lnb_ref, wout_ref, gpost_ref, gpre_ref, rw_ref, rb_ref, cnt_ref,
                     x1_o, h2_o, ti_o, tw_o, rank_o, cnt_o, run_ref):
    @pl.when(pl.program_id(0) == 0)
    def _():
        run_ref[...] = cnt_ref[...]

    gt1 = gt1_ref[...]
    sh2 = sh2_ref[...]
    sc2 = sc2_ref[...]
    if is_prompt:
        gt1, sh2, sc2 = gt1[0:1], sh2[0:1], sc2[0:1]
    y = yr_ref[...]
    mean = _seg_sum64(y) * (1.0 / HEAD_DIM)
    dlt = y - mean
    var = _seg_sum64(dlt * dlt) * (1.0 / HEAD_DIM)
    yn = dlt * lax.rsqrt(var + GN_EPS) * lnw_ref[...] + lnb_ref[...]
    yr = (yn + bonus_ref[...]) * g_ref[...]
    mix = _dot(jnp.concatenate([yr, ya_ref[...]], axis=1).astype(BF16), wout_ref[...])
    x1 = x_ref[...] + gt1 * _rmsnorm(mix, gpost_ref[...])
    x1_o[...] = x1
    h2 = _rmsnorm(x1, gpre_ref[...]) * (1.0 + sc2) + sh2
    h2_o[...] = h2
    logits = _dot3(h2, rw_ref[...]) + rb_ref[...]
    lane = lax.broadcasted_iota(jnp.int32, logits.shape, 1)
    vals, idxs = [], []
    for _ in range(TOP_K):
        m = jnp.max(logits, axis=1, keepdims=True)
        idx = jnp.min(jnp.where(logits == m, lane, N_EXPERTS), axis=1, keepdims=True)
        vals.append(m)
        idxs.append(idx)
        logits = jnp.where(lane == idx, -jnp.inf, logits)
    e = jnp.exp(jnp.concatenate(vals, axis=1) - vals[0])
    tw_o[...] = e / jnp.sum(e, axis=1, keepdims=True)
    ti_o[...] = jnp.concatenate(idxs, axis=1)

    tm = logits.shape[0]
    lane_e = lax.broadcasted_iota(jnp.int32, (tm, LANES), 1)
    tri = jnp.where(lax.broadcasted_iota(jnp.int32, (tm, tm), 0) > lax.broadcasted_iota(jnp.int32, (tm, tm), 1),
                    1.0, 0.0).astype(BF16)
    run = run_ref[...]
    ranks = []
    for idx in idxs:
        onehot = jnp.where(lane_e == idx, 1.0, 0.0)
        before = _dot(tri, onehot.astype(BF16)) + run
        ranks.append(jnp.sum(onehot * before, axis=1, keepdims=True))
        run = run + jnp.sum(onehot, axis=0, keepdims=True)
    run_ref[...] = run
    cnt_o[...] = run
    rank_o[...] = jnp.concatenate(ranks, axis=1).astype(jnp.int32)


def _post_mix(is_prompt, tm, yr, g, bonus, ya, x, mod, lnw, lnb, wout, gpost, gpre, rw, rb, cnt):
    m, d = x.shape
    row = lambda i: (i, 0)
    if is_prompt:
        mod_rows = 8
        mod_map = lambda c: (lambda i: (0, c))
    else:
        mod_rows = tm
        mod_map = lambda c: (lambda i: (i, c))
    wide = pl.BlockSpec((tm, RWKV_WIDTH), row)
    in_specs = [wide, wide, wide, wide, pl.BlockSpec((tm, d), row),
                pl.BlockSpec((mod_rows, d), mod_map(2)), pl.BlockSpec((mod_rows, d), mod_map(3)),
                pl.BlockSpec((mod_rows, d), mod_map(4)),
                _resident(lnw.shape), _resident(lnb.shape), _resident(wout.shape), _resident(gpost.shape),
                _resident(gpre.shape), _resident(rw.shape), _resident(rb.shape), _resident(cnt.shape)]
    out_shape = [jax.ShapeDtypeStruct((m, d), F32), jax.ShapeDtypeStruct((m, d), F32),
                 jax.ShapeDtypeStruct((m, TOP_K), jnp.int32), jax.ShapeDtypeStruct((m, TOP_K), F32),
                 jax.ShapeDtypeStruct((m, TOP_K), jnp.int32), jax.ShapeDtypeStruct((1, LANES), F32)]
    out_specs = [pl.BlockSpec((tm, d), row), pl.BlockSpec((tm, d), row),
                 pl.BlockSpec((tm, TOP_K), row), pl.BlockSpec((tm, TOP_K), row),
                 pl.BlockSpec((tm, TOP_K), row), pl.BlockSpec((1, LANES), lambda i: (0, 0))]
    return pl.pallas_call(
        functools.partial(_post_mix_kernel, is_prompt),
        grid=(m // tm,), in_specs=in_specs, out_specs=out_specs, out_shape=out_shape,
        scratch_shapes=[pltpu.VMEM((1, LANES), F32)],
        compiler_params=pltpu.CompilerParams(dimension_semantics=("arbitrary",), vmem_limit_bytes=VMEM_LIMIT),
        name="post_mix_prompt" if is_prompt else "post_mix_sample",
    )(yr, g, bonus, ya, x, mod, mod, mod, lnw, lnb, wout, gpost, gpre, rw, rb, cnt)


def _moe_kernel(ex_ref, rb_ref, ns_ref, tok_ref, h2_hbm, w1_ref, b1g_ref, b1l_ref, w2_ref, b2_ref, ys_hbm,
                xbuf, xb, acc, w1p, w2b, sem_in, sem_out):
    s = pl.program_id(0)
    f = pl.program_id(1)
    n_super = pl.num_programs(0)
    ns = ns_ref[s]
    rb = rb_ref[s]
    rows = MOE_ROWS
    grp = 2 * LANES
    share = rows // MOE_NF

    def gather_share(sup, n_tiles, col):
        base = rb_ref[sup] * rows

        def body(j, c):
            r0 = j * rows + col * share
            for u in range(share):
                pltpu.make_async_copy(h2_hbm.at[pl.ds(tok_ref[base + r0 + u], 1)], xbuf.at[pl.ds(r0 + u, 1)],
                                      sem_in).start()
            return c
        lax.fori_loop(0, n_tiles, body, 0)

    def gather_wait(j):
        sl = pl.ds(j * rows, rows)
        pltpu.make_async_copy(h2_hbm.at[pl.ds(0, rows)], xbuf.at[sl], sem_in).wait()

    def y_copy(j):
        return pltpu.make_async_copy(acc.at[pl.ds(j * rows, rows)], ys_hbm.at[pl.ds((rb + j) * rows, rows)], sem_out)

    def for_tiles(fn):
        def body(j, c):
            fn(j)
            return c
        lax.fori_loop(0, ns, body, 0)

    @pl.when(ns > 0)
    def _():
        @pl.when(f == 0)
        def _():
            @pl.when(s == 0)
            def _():
                for col in range(MOE_NF):
                    gather_share(0, ns, col)
            b2 = jnp.broadcast_to(b2_ref[0], (rows, D_MODEL))

            def init(j):
                acc[pl.ds(pl.multiple_of(j * rows, rows), rows), :] = b2
            for_tiles(init)

            for_tiles(gather_wait)

            def cast(j):
                sl = pl.ds(pl.multiple_of(j * rows, rows), rows)
                xb[sl, :] = xbuf[sl, :].astype(BF16)
            for_tiles(cast)

        nxt = jnp.minimum(s + 1, n_super - 1)
        gather_share(nxt, jnp.where(s + 1 < n_super, jnp.maximum(ns_ref[nxt], 0), 0), f)

        pr = lax.broadcasted_iota(jnp.int32, (grp, grp), 0)
        pc = lax.broadcasted_iota(jnp.int32, (grp, grp), 1)
        perm = jnp.where(pr == jnp.where(pc < LANES, 2 * pc, 2 * (pc - LANES) + 1), 1.0, 0.0).astype(BF16)
        n_grp = 2 * MOE_FT // grp
        for g in range(n_grp):
            gs = slice(grp * g, grp * (g + 1))
            w1p[:, gs] = _dot(w1_ref[0, :, gs].astype(BF16), perm).astype(BF16)
        w2b[...] = w2_ref[0].astype(BF16)
        b1g = b1g_ref[0]
        b1l = b1l_ref[0]

        last_col = f == pl.num_programs(1) - 1

        def swiglu(hh):
            glu = jnp.concatenate([hh[:, grp * g:grp * g + LANES] for g in range(n_grp)], axis=1) + b1g
            lin = jnp.concatenate([hh[:, grp * g + LANES:grp * (g + 1)] for g in range(n_grp)], axis=1) + b1l
            glu = jnp.minimum(glu, SWIGLU_LIMIT)
            lin = jnp.clip(lin, -SWIGLU_LIMIT, SWIGLU_LIMIT)
            return (glu * jax.nn.sigmoid(SWIGLU_ALPHA * glu) * (lin + 1.0)).astype(BF16)

        def tiles(js):
            sls = [pl.ds(pl.multiple_of(j * rows, rows), rows) for j in js]
            hh = [_dot(xb[sl, :], w1p[...]) for sl in sls]
            act = [swiglu(h) for h in hh]
            out = [_dot(a, w2b[...]) for a in act]
            for sl, o in zip(sls, out):
                acc[sl, :] += o

            @pl.when(last_col)
            def _():
                for j in js:
                    y_copy(j).start()

        def tile_pair(q, c):
            tiles([2 * q, 2 * q + 1])
            return c
        lax.fori_loop(0, ns // 2, tile_pair, 0)

        @pl.when(ns % 2 == 1)
        def _():
            tiles([ns - 1])

        @pl.when(last_col)
        def _():
            for_tiles(lambda j: y_copy(j).wait())

    @pl.when((ns < 0) & (f == 0))
    def _():
        acc[0:rows, :] = jnp.zeros((rows, D_MODEL), F32)

        def z_copy(j):
            return pltpu.make_async_copy(acc.at[pl.ds(0, rows)], ys_hbm.at[pl.ds((rb + j) * rows, rows)], sem_out)

        def body(j, c, op):
            op(z_copy(j))
            return c
        lax.fori_loop(0, -ns, functools.partial(body, op=lambda cp: cp.start()), 0)
        lax.fori_loop(0, -ns, functools.partial(body, op=lambda cp: cp.wait()), 0)


def _moe_experts(ex, rb, ns, src_tok, h2, w1, b1g, b1l, w2, b2):
    n_rows = src_tok.shape[0]
    n_super = ex.shape[0]
    nf = MOE_NF
    last = nf - 1
    sub_rows = MOE_SUB * MOE_ROWS

    def fcol(s, f, ns_):
        return jnp.where(ns_[s] > 0, f, last)

    grid_spec = pltpu.PrefetchScalarGridSpec(
        num_scalar_prefetch=4,
        grid=(n_super, nf),
        in_specs=[pl.BlockSpec(memory_space=pl.ANY),
                  pl.BlockSpec((1, D_MODEL, 2 * MOE_FT), lambda s, f, e_, r_, n_, t_: (e_[s], 0, fcol(s, f, n_))),
                  pl.BlockSpec((1, 1, MOE_FT), lambda s, f, e_, r_, n_, t_: (e_[s], 0, fcol(s, f, n_))),
                  pl.BlockSpec((1, 1, MOE_FT), lambda s, f, e_, r_, n_, t_: (e_[s], 0, fcol(s, f, n_))),
                  pl.BlockSpec((1, MOE_FT, D_MODEL), lambda s, f, e_, r_, n_, t_: (e_[s], fcol(s, f, n_), 0)),
                  pl.BlockSpec((1, 1, D_MODEL), lambda s, f, e_, r_, n_, t_: (e_[s], 0, 0))],
        out_specs=pl.BlockSpec(memory_space=pl.ANY),
        scratch_shapes=[pltpu.VMEM((sub_rows, D_MODEL), F32),
                        pltpu.VMEM((sub_rows, D_MODEL), BF16),
                        pltpu.VMEM((sub_rows, D_MODEL), F32),
                        pltpu.VMEM((D_MODEL, 2 * MOE_FT), BF16),
                        pltpu.VMEM((MOE_FT, D_MODEL), BF16),
                        pltpu.SemaphoreType.DMA, pltpu.SemaphoreType.DMA])
    return pl.pallas_call(
        _moe_kernel,
        grid_spec=grid_spec,
        out_shape=jax.ShapeDtypeStruct((n_rows, D_MODEL), F32),
        compiler_params=pltpu.CompilerParams(dimension_semantics=("arbitrary", "arbitrary"),
                                             vmem_limit_bytes=VMEM_LIMIT),
        name="moe_experts",
    )(ex, rb, ns, src_tok, h2, w1, b1g, b1l, w2, b2)


def _combine_kernel(is_prompt, idx_ref, nidx_ref, ys_hbm, x1_ref, tw_ref, gt2_ref, gpost_ref, o_ref, buf, sem):
    tk = COMBINE_TOK
    n = TOP_K * tk
    unroll = 16
    i = pl.program_id(0)
    slot = i % 2

    def start_rows(ref, buf_slot):
        def body(q, c):
            for u in range(unroll):
                j = q * unroll + u
                pltpu.make_async_copy(ys_hbm.at[pl.ds(ref[0, 0, j], 1)], buf.at[buf_slot, pl.ds(j, 1)],
                                      sem.at[buf_slot]).start()
            return c
        lax.fori_loop(0, n // unroll, body, 0)

    @pl.when(i == 0)
    def _():
        start_rows(idx_ref, 0)

    @pl.when(i + 1 < pl.num_programs(0))
    def _():
        start_rows(nidx_ref, 1 - slot)

    pltpu.make_async_copy(ys_hbm.at[pl.ds(0, n)], buf.at[slot], sem.at[slot]).wait()
    tw = tw_ref[...]
    f = tw[:, 0:1] * buf[slot, 0:tk, :]
    for k in range(1, TOP_K):
        f = f + tw[:, k:k + 1] * buf[slot, k * tk:(k + 1) * tk, :]
    gt2 = gt2_ref[...]
    if is_prompt:
        gt2 = gt2[0:1]
    o_ref[...] = x1_ref[...] + gt2 * _rmsnorm(f, gpost_ref[...])


def _combine(is_prompt, pos, ys, x1, tw, mod, gpost):
    m, d = x1.shape
    tk = COMBINE_TOK
    nblk = m // tk
    idx = pos.reshape(nblk, tk, TOP_K).transpose(0, 2, 1).reshape(nblk, 1, TOP_K * tk)
    row = lambda i: (i, 0)
    mod_spec = (pl.BlockSpec((8, d), lambda i: (0, 5)) if is_prompt else pl.BlockSpec((tk, d), lambda i: (i, 5)))
    return pl.pallas_call(
        functools.partial(_combine_kernel, is_prompt),
        grid=(nblk,),
        in_specs=[pl.BlockSpec((1, 1, TOP_K * tk), lambda i: (i, 0, 0), memory_space=pltpu.SMEM),
                  pl.BlockSpec((1, 1, TOP_K * tk), lambda i: (jnp.minimum(i + 1, nblk - 1), 0, 0),
                               memory_space=pltpu.SMEM),
                  pl.BlockSpec(memory_space=pl.ANY),
                  pl.BlockSpec((tk, d), row), pl.BlockSpec((tk, TOP_K), row), mod_spec,
                  pl.BlockSpec((1, d), lambda i: (0, 0))],
        out_specs=pl.BlockSpec((tk, d), row),
        out_shape=jax.ShapeDtypeStruct((m, d), F32),
        scratch_shapes=[pltpu.VMEM((2, TOP_K * tk, d), F32), pltpu.SemaphoreType.DMA((2,))],
        compiler_params=pltpu.CompilerParams(dimension_semantics=("arbitrary",)),
        name="moe_combine_prompt" if is_prompt else "moe_combine_sample",
    )(idx, idx, ys, x1, tw, mod, gpost)


def _routing_tables(top_i, rank, counts):
    n_assign = top_i.size
    e_flat = top_i.reshape(-1)
    tiles = (counts + MOE_ROWS - 1) // MOE_ROWS
    tile_end = jnp.cumsum(tiles)
    tile_start = tile_end - tiles
    pos = tile_start[e_flat] * MOE_ROWS + rank.reshape(-1)
    n_rows = _padded_rows(n_assign)
    assert n_assign <= (1 << 16) and n_rows <= (1 << 16)
    key = pos.astype(jnp.uint32) * jnp.uint32(1 << 16) + jnp.arange(n_assign, dtype=jnp.uint32)
    order = (jnp.sort(key) & jnp.uint32(0xFFFF)).astype(jnp.int32)
    p = jnp.arange(n_rows, dtype=jnp.int32)
    e_of_p = jnp.minimum(jnp.sum((p[:, None] // MOE_ROWS >= tile_end[None, :]).astype(jnp.int32), axis=1),
                         N_EXPERTS - 1)
    r_of_p = p - tile_start[e_of_p] * MOE_ROWS
    src = order[jnp.minimum((jnp.cumsum(counts) - counts)[e_of_p] + r_of_p, n_assign - 1)] // TOP_K
    src_tok = jnp.where(r_of_p < counts[e_of_p], src, 0).astype(jnp.int32)
    supers = (tiles + MOE_SUB - 1) // MOE_SUB
    super_end = jnp.cumsum(supers)
    s_idx = jnp.arange(_max_supers(n_assign), dtype=jnp.int32)
    ex = jnp.minimum(jnp.sum((s_idx[:, None] >= super_end[None, :]).astype(jnp.int32), axis=1),
                     N_EXPERTS - 1).astype(jnp.int32)
    j = s_idx - (super_end - supers)[ex]
    live = s_idx < super_end[-1]
    tail = jnp.sum(tiles) + MOE_SUB * (s_idx - super_end[-1])
    n_clear = jnp.clip(n_rows // MOE_ROWS - tail, 0, MOE_SUB)
    ns = jnp.where(live, jnp.clip(tiles[ex] - MOE_SUB * j, 0, MOE_SUB), -n_clear).astype(jnp.int32)
    rb = jnp.where(live, tile_start[ex] + MOE_SUB * j, tail).astype(jnp.int32)
    last_live = jnp.max(jnp.where(live, ex, 0))
    ex = jnp.where(live, ex, last_live).astype(jnp.int32)
    return pos.astype(jnp.int32), src_tok, ex, rb, ns


def _padded_rows(n_assign):
    rows = n_assign + N_EXPERTS * (MOE_ROWS - 1)
    return -(-rows // MOE_ROWS) * MOE_ROWS


def _max_supers(n_assign):
    return N_EXPERTS + -(-_padded_rows(n_assign) // (MOE_ROWS * MOE_SUB))


def kernel(x_prompt, x_sample, cache_k, cache_v, state_wkv, state_shift, c_prompt, c_sample, w_ada, b_ada, g_pre_mix, g_post_mix, g_pre_ffn, g_post_ffn, mu_shift, w_in, rwkv_w0, rwkv_w2, rwkv_a0, rwkv_a2, rwkv_g2, rwkv_k_k, rwkv_k_a, rwkv_r_k, rwkv_ln_w, rwkv_ln_b, attn_sinks, w_out, router_w, router_b, moe_w1, moe_b1, moe_w2, moe_b2):
    assert w_ada.shape[0] == 1, "single-layer step"
    d = D_MODEL
    t = x_prompt.shape[1]
    nb = x_sample.shape[0]
    xp = x_prompt.reshape(t, d)
    xs = x_sample.reshape(nb, d)

    c_all = jnp.concatenate([jnp.broadcast_to(c_prompt, (8, d)), c_sample], axis=0)
    mod = _adaln_mod(c_all, w_ada[0], b_ada[0].reshape(1, 6 * d))
    mod_p, mod_s = mod[:8], mod[8:]

    offs = [0, 1024, 2048, 3072, 3136, 3200, 3360]
    w_in0 = w_in[0]
    pad_to = lambda w, n: jnp.pad(w, ((0, 0), (0, n - w.shape[1])))
    wts = [w_in0[:, offs[0]:offs[1]], w_in0[:, offs[1]:offs[2]], w_in0[:, offs[2]:offs[3]],
           pad_to(w_in0[:, offs[3]:offs[4]], 128), pad_to(w_in0[:, offs[4]:offs[5]], 128),
           pad_to(w_in0[:, offs[5]:offs[6]], 256), w_in0[:, offs[6]:]]
    wts = [w.astype(BF16) for w in wts]
    pad_rows = lambda w, n: jnp.pad(w, ((0, n - w.shape[0]), (0, 0))).astype(BF16)
    row = lambda p: p.reshape(1, -1)
    rw = [row(rwkv_w0[0]), pad_rows(rwkv_w2[0], 128), row(rwkv_a0[0]), pad_rows(rwkv_a2[0], 128),
          pad_rows(rwkv_g2[0], 256), row(rwkv_k_k[0]), row(rwkv_k_a[0]), row(rwkv_r_k[0])]
    gpm = row(g_pre_mix[0])
    mu8 = jnp.pad(mu_shift[0], ((0, 8 - N_SHIFTED), (0, 0)))

    pr = _in_proj(True, 256, xp, xp, mod_p, gpm, mu8, wts, rw)
    sr = _in_proj(False, nb, xs, state_shift[0], mod_s, gpm, mu8, wts, rw)
    r_p, lw_p, kf_p, v_p, na_p, b_p, g_p, bonus_p, q_p, ka_p, va_p, hlast_p = pr
    r_s, lw_s, kf_s, v_s, na_s, b_s, g_s, bonus_s, q_s, ka_s, va_s, h_s = sr

    y_p, st_p = _rwkv_prompt(r_p, lw_p, kf_p, v_p, na_p, b_p)
    y_s, wkv_s = _rwkv_sample(r_s, lw_s, kf_s, v_s, na_s, b_s, state_wkv[0])
    y_s = y_s.reshape(nb, RWKV_WIDTH)
    sinks = attn_sinks[0]
    att_p = _swa_prompt(q_p, ka_p, va_p, sinks.reshape(1, N_Q_HEADS))
    wb = cache_k.shape[2]
    att_s, ck_new, cv_new = _swa_sample(q_s, ka_s, va_s, cache_k[0].reshape(nb, wb, KV_WIDTH),
                                        cache_v[0].reshape(nb, wb, KV_WIDTH), sinks.reshape(N_Q_HEADS, 1))
    att_s = att_s.reshape(nb, ATTN_WIDTH)

    post = [row(rwkv_ln_w[0]), row(rwkv_ln_b[0]), w_out[0].astype(BF16), row(g_post_mix[0]),
            row(g_pre_ffn[0]), router_w[0], row(router_b[0])]
    x1_p, h2_p, ti_p, tw_p, rank_p, cnt_p = _post_mix(True, 256, y_p, g_p, bonus_p, att_p, xp, mod_p, *post,
                                                      jnp.zeros((1, LANES), F32))
    x1_s, h2_s, ti_s, tw_s, rank_s, cnt_s = _post_mix(False, nb, y_s, g_s, bonus_s, att_s, xs, mod_s, *post, cnt_p)

    h2_all = jnp.concatenate([h2_p, h2_s], axis=0)
    pos, src_tok, ex, rb, ns = _routing_tables(jnp.concatenate([ti_p, ti_s], axis=0),
                                               jnp.concatenate([rank_p, rank_s], axis=0),
                                               cnt_s[0, :N_EXPERTS].astype(jnp.int32))
    b1 = moe_b1[0].reshape(N_EXPERTS, 1, D_FF, 2)
    ys_rows = _moe_experts(ex, rb, ns, src_tok, h2_all, moe_w1[0], b1[..., 0], b1[..., 1],
                           moe_w2[0], moe_b2[0].reshape(N_EXPERTS, 1, d))
    pos = pos.reshape(-1, TOP_K)
    gpf = row(g_post_ffn[0])
    out_p = _combine(True, pos[:t], ys_rows, x1_p, tw_p, mod_p, gpf)
    out_s = _combine(False, pos[t:], ys_rows, x1_s, tw_s, mod_s, gpf)

    n_keep = min(WINDOW, t)
    st_heads = jnp.stack([st_p[:, :HEAD_DIM, :HEAD_DIM], st_p[:, HEAD_DIM:, HEAD_DIM:]], axis=1)
    return (out_p.reshape(1, t, d),
            out_s.reshape(nb, 1, d),
            ka_p[t - n_keep:].reshape(1, 1, n_keep, N_KV_HEADS, HEAD_DIM),
            va_p[t - n_keep:].reshape(1, 1, n_keep, N_KV_HEADS, HEAD_DIM),
            st_heads.reshape(1, 1, N_RWKV_HEADS, HEAD_DIM, HEAD_DIM),
            hlast_p[7:8].reshape(1, 1, d),
            ck_new.reshape(1, nb, wb, N_KV_HEADS, HEAD_DIM),
            cv_new.reshape(1, nb, wb, N_KV_HEADS, HEAD_DIM),
            wkv_s.reshape(1, nb, N_RWKV_HEADS, HEAD_DIM, HEAD_DIM),
            h_s.reshape(1, nb, d))
```

```python
import functools
import math

import jax
import jax.numpy as jnp
from jax import lax
from jax.experimental import pallas as pl
from jax.experimental.pallas import tpu as pltpu

F32 = jnp.float32
BF16 = jnp.bfloat16

D_MODEL = 2048
HEAD_DIM = 64
RWKV_WIDTH = 1024
N_RWKV_HEADS = 16
ATTN_WIDTH = 1024
N_Q_HEADS = 16
N_KV_HEADS = 2
Q_PER_KV = 8
KV_WIDTH = 128
WINDOW = 128
N_SHIFTED = 6
N_EXPERTS = 32
TOP_K = 4
D_FF = 2048
SWIGLU_ALPHA = 1.702
SWIGLU_LIMIT = 7.0
NORM_EPS = 1e-6
GN_EPS = 64e-5
L2_EPS = 1e-12

LANES = 128
VMEM_LIMIT = 56 * 1024 * 1024

CHUNK = 64
RWKV_ROWS = 256
RWKV_PAIRS = 8
MOE_ROWS = 128
MOE_SUB = 12
MOE_FT = 256
MOE_NF = D_FF // MOE_FT
COMBINE_TOK = 128


def _dot(a, b):
    return jnp.dot(a, b, preferred_element_type=F32)


def _dot_nt(a, b):
    return lax.dot_general(a, b, (((1,), (1,)), ((), ())), preferred_element_type=F32)


def _split_bf16(x):
    hi = x.astype(BF16)
    lo = (x - hi.astype(F32)).astype(BF16)
    return hi, lo


def _dot3(a, b):
    ah, al = _split_bf16(a)
    bh, bl = _split_bf16(b)
    return _dot(ah, bh) + _dot(ah, bl) + _dot(al, bh)


def _seg_sum64(x):
    r = lax.broadcasted_iota(jnp.int32, (LANES, LANES), 0) // HEAD_DIM
    c = lax.broadcasted_iota(jnp.int32, (LANES, LANES), 1) // HEAD_DIM
    bd = jnp.where(r == c, 1.0, 0.0).astype(BF16)
    hi, lo = _split_bf16(x)
    outs = []
    for j in range(x.shape[1] // LANES):
        sl = slice(LANES * j, LANES * (j + 1))
        outs.append(_dot(hi[:, sl], bd) + _dot(lo[:, sl], bd))
    return jnp.concatenate(outs, axis=1)


def _rmsnorm(x, g):
    ms = jnp.mean(x * x, axis=-1, keepdims=True)
    return x * lax.rsqrt(ms + NORM_EPS) * g


def _resident(shape):
    nd = len(shape)
    return pl.BlockSpec(shape, lambda *_: (0,) * nd, pipeline_mode=pl.Buffered(1))


def _mod_kernel(c_ref, w_ref, b_ref, o_ref):
    c = c_ref[...]
    s = c * jax.nn.sigmoid(c)
    o_ref[...] = _dot3(s, w_ref[...]) + b_ref[...]


def _adaln_mod(c, w_ada, b_ada):
    rows, d = c.shape
    n = w_ada.shape[1]
    tn = 512
    return pl.pallas_call(
        _mod_kernel,
        grid=(n // tn,),
        in_specs=[pl.BlockSpec((rows, d), lambda j: (0, 0)),
                  pl.BlockSpec((d, tn), lambda j: (0, j)),
                  pl.BlockSpec((1, tn), lambda j: (0, j))],
        out_specs=pl.BlockSpec((rows, tn), lambda j: (0, j)),
        out_shape=jax.ShapeDtypeStruct((rows, n), F32),
        compiler_params=pltpu.CompilerParams(dimension_semantics=("arbitrary",), vmem_limit_bytes=VMEM_LIMIT),
        name="adaln_mod",
    )(c, w_ada, b_ada)


def _inproj_kernel(is_prompt, tm,
                   x_ref, prev_ref, sh_ref, sc_ref, gpm_ref, mu_ref,
                   wr_ref, wk_ref, wv_ref, wwl_ref, wal_ref, wgl_ref, wqkv_ref,
                   w0_ref, w2_ref, a0_ref, a2_ref, g2_ref, kk_ref, ka_ref, rk_ref,
                   r_o, lw_o, kf_o, v_o, na_o, b_o, g_o, bonus_o, q_o, kat_o, vat_o, h_o):
    i = pl.program_id(0)
    gpm = gpm_ref[...]
    sh = sh_ref[...]
    sc = sc_ref[...]
    if is_prompt:
        sh = sh[0:1]
        sc = sc[0:1]

    def modnorm(x):
        return _rmsnorm(x, gpm) * (1.0 + sc) + sh

    h = modnorm(x_ref[...])
    if is_prompt:
        hp = modnorm(prev_ref[...])[7:8, :]
        hp = jnp.where(i > 0, hp, 0.0)
        row = lax.broadcasted_iota(jnp.int32, h.shape, 0)
        hprev = jnp.where(row == 0, hp, pltpu.roll(h, 1, axis=0))
        h_o[...] = h[tm - 8:tm, :]
    else:
        hprev = prev_ref[...]
        h_o[...] = h
    dx = hprev - h
    mu = mu_ref[...]

    def branch(j, w_ref):
        xi = (h + dx * mu[j:j + 1, :]).astype(BF16)
        return _dot(xi, w_ref[...])

    r = branch(0, wr_ref)
    k = branch(1, wk_ref)
    v = branch(2, wv_ref)
    wl = branch(3, wwl_ref)
    al = branch(4, wal_ref)
    gl = branch(5, wgl_ref)
    qkv = _dot(h.astype(BF16), wqkv_ref[...])
    q_o[...] = qkv[:, :ATTN_WIDTH]
    kat_o[...] = qkv[:, ATTN_WIDTH:ATTN_WIDTH + KV_WIDTH]
    vat_o[...] = qkv[:, ATTN_WIDTH + KV_WIDTH:]

    z = w0_ref[...] + _dot(jnp.tanh(wl).astype(BF16), w2_ref[...])
    w_raw = -jnp.logaddexp(-z, 0.0) - 0.5
    lw_o[...] = -jnp.exp(w_raw)
    a = jax.nn.sigmoid(a0_ref[...] + _dot(al.astype(BF16), a2_ref[...]))
    g_o[...] = _dot(jax.nn.sigmoid(gl).astype(BF16), g2_ref[...])
    kk = k * kk_ref[...]
    kk = kk / jnp.maximum(jnp.sqrt(_seg_sum64(kk * kk)), L2_EPS)
    kf = k * (1.0 + (a - 1.0) * ka_ref[...])
    r_o[...] = r
    kf_o[...] = kf
    v_o[...] = v
    na_o[...] = -kk
    b_o[...] = kk * a
    bonus_o[...] = _seg_sum64(r * kf * rk_ref[...]) * v


def _in_proj(is_prompt, tm, x, prev, mod, gpm, mu8, wts, rw):
    m, d = x.shape
    grid = (m // tm,)
    row = lambda i: (i, 0)
    if is_prompt:
        prev_spec = pl.BlockSpec((8, d), lambda i: (jnp.maximum(i * (tm // 8) - 1, 0), 0))
        mod_rows = 8
        mod_map = lambda c: (lambda i: (0, c))
        h_shape, h_spec = (8, d), pl.BlockSpec((8, d), lambda i: (0, 0))
    else:
        prev_spec = pl.BlockSpec((tm, d), row)
        mod_rows = tm
        mod_map = lambda c: (lambda i: (i, c))
        h_shape, h_spec = (m, d), pl.BlockSpec((tm, d), row)
    in_specs = [pl.BlockSpec((tm, d), row), prev_spec,
                pl.BlockSpec((mod_rows, d), mod_map(0)), pl.BlockSpec((mod_rows, d), mod_map(1)),
                _resident((1, d)), _resident((8, d))]
    in_specs += [_resident(w.shape) for w in wts]
    in_specs += [_resident(p.shape) for p in rw]
    wide = jax.ShapeDtypeStruct((m, RWKV_WIDTH), F32)
    wide_spec = pl.BlockSpec((tm, RWKV_WIDTH), row)
    kv = jax.ShapeDtypeStruct((m, KV_WIDTH), F32)
    kv_spec = pl.BlockSpec((tm, KV_WIDTH), row)
    out_shape = [wide] * 9 + [kv, kv, jax.ShapeDtypeStruct(h_shape, F32)]
    out_specs = [wide_spec] * 9 + [kv_spec, kv_spec, h_spec]
    return pl.pallas_call(
        functools.partial(_inproj_kernel, is_prompt, tm),
        grid=grid, in_specs=in_specs, out_specs=out_specs, out_shape=out_shape,
        compiler_params=pltpu.CompilerParams(dimension_semantics=("arbitrary",), vmem_limit_bytes=VMEM_LIMIT),
        name="in_proj_prompt" if is_prompt else "in_proj_sample",
    )(x, prev, mod, mod, gpm, mu8, *wts, *rw)


def _rwkv_chunk_kernel(r_ref, lw_ref, k_ref, v_ref, a_ref, b_ref, y_ref, s_ref, st_ref):
    t = pl.program_id(1)
    C = CHUNK
    P = 2 * HEAD_DIM

    @pl.when(t == 0)
    def _():
        st_ref[...] = jnp.zeros_like(st_ref)

    ri = lax.broadcasted_iota(jnp.int32, (P, P), 0)
    ci = lax.broadcasted_iota(jnp.int32, (P, P), 1)
    bd = (ri // C) == (ci // C)
    tril_s = bd & ((ri % C) > (ci % C))
    tril_i = bd & ((ri % C) >= (ci % C))
    eye = jnp.where(ri == ci, 1.0, 0.0)
    lane0 = lax.broadcasted_iota(jnp.int32, (C, P), 1) < HEAD_DIM
    trow = lax.broadcasted_iota(jnp.int32, (C, P), 0)

    def stack(x):
        return jnp.concatenate([jnp.where(lane0, x, 0.0), jnp.where(lane0, 0.0, x)], axis=0)

    def dup(x):
        return jnp.concatenate([x, x], axis=0)

    def prep(sl, pp):
        ln = slice(P * pp, P * (pp + 1))
        lw = lw_ref[sl, ln]
        cw = lw
        for s in (1, 2, 4, 8, 16, 32):
            cw = cw + jnp.where(trow >= s, pltpu.roll(cw, s, axis=0), 0.0)
        cw_last = cw[C - 1:C, :]
        e_neg = jnp.exp(-cw)
        e_end = jnp.exp(cw_last - cw)
        k = k_ref[sl, ln]
        v = v_ref[sl, ln]
        b = b_ref[sl, ln]
        a2 = stack(a_ref[sl, ln] * jnp.exp(cw - lw))
        r2 = stack(r_ref[sl, ln] * jnp.exp(cw))
        return dict(
            v2=stack(v),
            lhs=jnp.concatenate([a2, r2], axis=0).astype(BF16),
            rhs=jnp.concatenate([dup(k * e_neg), dup(b * e_neg)], axis=0).astype(BF16),
            kbh=jnp.concatenate([dup(b * e_end), dup(k * e_end)], axis=0).astype(BF16),
            decay=jnp.exp(cw_last))

    def chunk(c, carry):
        sl = pl.ds(pl.multiple_of(c * C, C), C)
        pairs = range(RWKV_PAIRS)
        st = [st_ref[pp] for pp in pairs]
        d = [prep(sl, pp) for pp in pairs]
        gram = [_dot_nt(d[pp]['lhs'], d[pp]['rhs']) for pp in pairs]
        l2 = [jnp.where(tril_s, gram[pp][0:P, P:2 * P], 0.0) for pp in pairs]
        a_s = [_dot_nt(d[pp]['lhs'], st[pp].astype(BF16)) for pp in pairs]
        v2b = [d[pp]['v2'].astype(BF16) for pp in pairs]
        rhs_u = [a_s[pp][0:P] + _dot(jnp.where(tril_s, gram[pp][0:P, 0:P], 0.0).astype(BF16), v2b[pp])
                 for pp in pairs]
        inv = [eye + l2[pp] for pp in pairs]
        lp = l2
        for _ in range(5):
            lpb = [lp[pp].astype(BF16) for pp in pairs]
            lp = [_dot(lpb[pp], lpb[pp]) for pp in pairs]
            inv = [inv[pp] + _dot(lp[pp].astype(BF16), inv[pp].astype(BF16)) for pp in pairs]
        u2 = [_dot(inv[pp].astype(BF16), rhs_u[pp].astype(BF16)) for pp in pairs]
        p_cat = [jnp.concatenate([jnp.where(tril_i, gram[pp][P:2 * P, P:2 * P], 0.0),
                                  jnp.where(tril_i, gram[pp][P:2 * P, 0:P], 0.0)], axis=1).astype(BF16)
                 for pp in pairs]
        y2 = [a_s[pp][P:2 * P] + _dot(p_cat[pp], jnp.concatenate([u2[pp].astype(BF16), v2b[pp]], axis=0))
              for pp in pairs]
        uvt = [jnp.concatenate([u2[pp], d[pp]['v2']], axis=0).T.astype(BF16) for pp in pairs]
        st_new = [jnp.where(bd, st[pp] * d[pp]['decay'] + _dot(uvt[pp], d[pp]['kbh']), 0.0) for pp in pairs]
        for pp in pairs:
            y_ref[sl, P * pp:P * (pp + 1)] = y2[pp][0:C] + y2[pp][C:2 * C]
            st_ref[pp] = st_new[pp]
        return carry

    lax.fori_loop(0, RWKV_ROWS // C, chunk, 0)

    @pl.when(t == pl.num_programs(1) - 1)
    def _():
        s_ref[...] = st_ref[...]


def _rwkv_prompt(r, lw, kf, v, na, b):
    t = r.shape[0]
    n_pairs = RWKV_WIDTH // LANES
    width = RWKV_PAIRS * LANES
    spec = pl.BlockSpec((RWKV_ROWS, width), lambda p, i: (i, p))
    return pl.pallas_call(
        _rwkv_chunk_kernel,
        grid=(n_pairs // RWKV_PAIRS, t // RWKV_ROWS),
        in_specs=[spec] * 6,
        out_specs=[spec, pl.BlockSpec((RWKV_PAIRS, LANES, LANES), lambda p, i: (p, 0, 0))],
        out_shape=[jax.ShapeDtypeStruct((t, RWKV_WIDTH), F32),
                   jax.ShapeDtypeStruct((n_pairs, LANES, LANES), F32)],
        scratch_shapes=[pltpu.VMEM((RWKV_PAIRS, LANES, LANES), F32)],
        compiler_params=pltpu.CompilerParams(dimension_semantics=("arbitrary", "arbitrary")),
        name="rwkv_chunked",
    )(r, lw, kf, v, na, b)


def _rwkv_step_kernel(bb, r_ref, lw_ref, k_ref, v_ref, a_ref, b_ref, s_ref, y_ref, so_ref):
    n = HEAD_DIM
    eye = jnp.where(lax.broadcasted_iota(jnp.int32, (n, n), 0) == lax.broadcasted_iota(jnp.int32, (n, n), 1), 1.0, 0.0)

    def body(bi, carry):
        rb = r_ref[bi]
        dec = jnp.exp(lw_ref[bi])
        kb = k_ref[bi]
        vb = v_ref[bi]
        ab = a_ref[bi]
        bb_ = b_ref[bi]
        heads = range(N_RWKV_HEADS)
        row = lambda x, h: x[h:h + 1]
        s = [s_ref[bi, h] for h in heads]
        sa = [jnp.sum(s[h] * row(ab, h), axis=1, keepdims=True) for h in heads]
        vcol = [jnp.sum(eye * row(vb, h), axis=1, keepdims=True) for h in heads]
        s2 = [s[h] * row(dec, h) + sa[h] * row(bb_, h) + vcol[h] * row(kb, h) for h in heads]
        ycol = [jnp.sum(s2[h] * row(rb, h), axis=1, keepdims=True) for h in heads]
        ys = [jnp.sum(eye * ycol[h], axis=0, keepdims=True) for h in heads]
        for h in heads:
            so_ref[bi, h] = s2[h]
        y_ref[bi] = jnp.concatenate(ys, axis=0)
        return carry

    lax.fori_loop(0, bb, body, 0)


def _rwkv_sample(r, lw, kf, v, na, b, state):
    nb = r.shape[0]
    bb = 8
    vec = lambda x: x.reshape(nb, N_RWKV_HEADS, HEAD_DIM)
    vspec = pl.BlockSpec((bb, N_RWKV_HEADS, HEAD_DIM), lambda i: (i, 0, 0))
    sspec = pl.BlockSpec((bb, N_RWKV_HEADS, HEAD_DIM, HEAD_DIM), lambda i: (i, 0, 0, 0))
    return pl.pallas_call(
        functools.partial(_rwkv_step_kernel, bb),
        grid=(nb // bb,),
        in_specs=[vspec] * 6 + [sspec],
        out_specs=[vspec, sspec],
        out_shape=[jax.ShapeDtypeStruct((nb, N_RWKV_HEADS, HEAD_DIM), F32),
                   jax.ShapeDtypeStruct(state.shape, F32)],
        compiler_params=pltpu.CompilerParams(dimension_semantics=("arbitrary",)),
        name="rwkv_step",
    )(vec(r), vec(lw), vec(kf), vec(v), vec(na), vec(b), state)


def _alibi_slope(head):
    return 2.0 ** (-8.0 * (head + 1) / N_Q_HEADS)


def _swa_prompt_kernel(q_ref, kc_ref, kp_ref, vc_ref, vp_ref, sink_ref, o_ref):
    n = pl.program_id(0)
    w = WINDOW
    kcat = jnp.concatenate([kp_ref[...], kc_ref[...]], axis=0)
    vcat = jnp.concatenate([vp_ref[...], vc_ref[...]], axis=0)
    lane_k = lax.broadcasted_iota(jnp.int32, kcat.shape, 1) < HEAD_DIM
    kswap = pltpu.roll(kcat, HEAD_DIM, axis=1)
    vswap = pltpu.roll(vcat, HEAD_DIM, axis=1)
    kdup = [jnp.where(lane_k, kcat, kswap).astype(BF16), jnp.where(lane_k, kswap, kcat).astype(BF16)]
    vdup = [jnp.where(lane_k, vcat, vswap).astype(BF16), jnp.where(lane_k, vswap, vcat).astype(BF16)]
    qi = lax.broadcasted_iota(jnp.int32, (w, 2 * w), 0)
    kj = lax.broadcasted_iota(jnp.int32, (w, 2 * w), 1)
    dist = qi + w - kj
    valid = (dist >= 0) & (dist <= w) & ((n > 0) | (kj >= w))
    distf = dist.astype(F32)
    lane_q = lax.broadcasted_iota(jnp.int32, (w, LANES), 1) < HEAD_DIM
    sinks = sink_ref[...]
    group = 4
    for j0 in range(0, N_Q_HEADS // 2, group):
        js = range(j0, j0 + group)
        heads = [2 * j + half for j in js for half in range(2)]
        kvh = {h: h // Q_PER_KV for h in heads}
        q2 = {}
        for j in js:
            qp = q_ref[:, LANES * j:LANES * (j + 1)] * (1.0 / math.sqrt(HEAD_DIM))
            q2[j] = jnp.concatenate([jnp.where(lane_q, qp, 0.0), jnp.where(lane_q, 0.0, qp)], axis=0).astype(BF16)
        s2 = {j: _dot_nt(q2[j], kdup[kvh[2 * j]]) for j in js}
        s = {h: jnp.where(valid, s2[h // 2][w * (h % 2):w * (h % 2 + 1)] - _alibi_slope(h) * distf, -jnp.inf)
             for h in heads}
        sink = {h: sinks[0:1, h:h + 1] for h in heads}
        m = {h: jnp.maximum(jnp.max(s[h], axis=-1, keepdims=True), sink[h]) for h in heads}
        p = {h: jnp.exp(s[h] - m[h]) for h in heads}
        den = {h: jnp.sum(p[h], axis=-1, keepdims=True) + jnp.exp(sink[h] - m[h]) for h in heads}
        o = {h: _dot(p[h].astype(BF16), vdup[kvh[h]]) / den[h] for h in heads}
        for j in js:
            o_ref[:, LANES * j:LANES * (j + 1)] = jnp.where(lane_q, o[2 * j], o[2 * j + 1])


def _swa_prompt(q, ka, va, sinks):
    t = q.shape[0]
    w = WINDOW
    cur = lambda n: (n, 0)
    prv = lambda n: (jnp.maximum(n - 1, 0), 0)
    kvs = lambda f: pl.BlockSpec((w, KV_WIDTH), f)
    return pl.pallas_call(
        _swa_prompt_kernel,
        grid=(t // w,),
        in_specs=[pl.BlockSpec((w, ATTN_WIDTH), cur), kvs(cur), kvs(prv), kvs(cur), kvs(prv),
                  pl.BlockSpec((1, N_Q_HEADS), lambda n: (0, 0))],
        out_specs=pl.BlockSpec((w, ATTN_WIDTH), cur),
        out_shape=jax.ShapeDtypeStruct((t, ATTN_WIDTH), F32),
        compiler_params=pltpu.CompilerParams(dimension_semantics=("arbitrary",)),
        name="swa_prompt",
    )(q, ka, ka, va, va, sinks)


def _swa_sample_kernel(q_ref, kn_ref, vn_ref, ck_ref, cv_ref, sink_ref, o_ref, ko_ref, vo_ref):
    wb = ck_ref.shape[1]
    q = q_ref[...] * (1.0 / math.sqrt(HEAD_DIM))
    q2 = jnp.concatenate([q, q], axis=2)
    rowh = lax.broadcasted_iota(jnp.int32, q2.shape, 1) // Q_PER_KV
    laneh = lax.broadcasted_iota(jnp.int32, q2.shape, 2) // HEAD_DIM
    qb = jnp.where(rowh == laneh, q2, 0.0)
    kn = kn_ref[...]
    vn = vn_ref[...]
    ck = ck_ref[...]
    cv = cv_ref[...]
    s = jnp.einsum('bqd,bkd->bqk', qb.astype(BF16), ck.astype(BF16), preferred_element_type=F32)
    s_self = jnp.sum(qb * kn, axis=2, keepdims=True)
    head = lax.broadcasted_iota(jnp.int32, (1, N_Q_HEADS, 1), 1).astype(F32)
    slope = jnp.exp2(-8.0 * (head + 1.0) / N_Q_HEADS)
    dist = (wb - lax.broadcasted_iota(jnp.int32, (1, 1, wb), 2)).astype(F32)
    s = s - slope * dist
    sink = sink_ref[...][None]
    m = jnp.maximum(jnp.maximum(jnp.max(s, axis=2, keepdims=True), s_self), sink)
    p = jnp.exp(s - m)
    p_self = jnp.exp(s_self - m)
    den = jnp.sum(p, axis=2, keepdims=True) + p_self + jnp.exp(sink - m)
    o = jnp.einsum('bqk,bkd->bqd', p.astype(BF16), cv.astype(BF16), preferred_element_type=F32)
    o = (o + p_self * vn) / den
    sel = lax.broadcasted_iota(jnp.int32, (1, N_Q_HEADS, HEAD_DIM), 1) < Q_PER_KV
    o_ref[...] = jnp.where(sel, o[:, :, :HEAD_DIM], o[:, :, HEAD_DIM:])
    ko_ref[:, 0:wb - 1, :] = ck_ref[:, 1:wb, :]
    ko_ref[:, wb - 1:wb, :] = kn
    vo_ref[:, 0:wb - 1, :] = cv_ref[:, 1:wb, :]
    vo_ref[:, wb - 1:wb, :] = vn


def _swa_sample(q, ka, va, cache_k, cache_v, sinks_col):
    nb, wb = cache_k.shape[0], cache_k.shape[1]
    bb = 16
    b3 = lambda i: (i, 0, 0)
    nspec = pl.BlockSpec((bb, 1, KV_WIDTH), b3)
    cspec = pl.BlockSpec((bb, wb, KV_WIDTH), b3)
    qspec = pl.BlockSpec((bb, N_Q_HEADS, HEAD_DIM), b3)
    return pl.pallas_call(
        _swa_sample_kernel,
        grid=(nb // bb,),
        in_specs=[qspec, nspec, nspec, cspec, cspec, pl.BlockSpec((N_Q_HEADS, 1), lambda i: (0, 0))],
        out_specs=[qspec, cspec, cspec],
        out_shape=[jax.ShapeDtypeStruct((nb, N_Q_HEADS, HEAD_DIM), F32),
                   jax.ShapeDtypeStruct(cache_k.shape, F32), jax.ShapeDtypeStruct(cache_v.shape, F32)],
        compiler_params=pltpu.CompilerParams(dimension_semantics=("arbitrary",)),
        name="swa_sample",
    )(q.reshape(nb, N_Q_HEADS, HEAD_DIM), ka.reshape(nb, 1, KV_WIDTH), va.reshape(nb, 1, KV_WIDTH),
      cache_k, cache_v, sinks_col)


def _post_mix_kernel(is_prompt, yr_ref, g_ref, bonus_ref, ya_ref, x_ref, gt1_ref, sh2_ref, sc2_ref,
                     lnw_ref, lnb_ref, wout_ref, gpost_ref, gpre_ref, rw_ref, rb_ref, cnt_ref,
                     x1_o, h2_o, ti_o, tw_o, rank_o, cnt_o, run_ref):
    @pl.when(pl.program_id(0) == 0)
    def _():
        run_ref[...] = cnt_ref[...]

    gt1 = gt1_ref[...]
    sh2 = sh2_ref[...]
    sc2 = sc2_ref[...]
    if is_prompt:
        gt1, sh2, sc2 = gt1[0:1], sh2[0:1], sc2[0:1]
    y = yr_ref[...]
    mean = _seg_sum64(y) * (1.0 / HEAD_DIM)
    dlt = y - mean
    var = _seg_sum64(dlt * dlt) * (1.0 / HEAD_DIM)
    yn = dlt * lax.rsqrt(var + GN_EPS) * lnw_ref[...] + lnb_ref[...]
    yr = (yn + bonus_ref[...]) * g_ref[...]
    mix = _dot(jnp.concatenate([yr, ya_ref[...]], axis=1).astype(BF16), wout_ref[...])
    x1 = x_ref[...] + gt1 * _rmsnorm(mix, gpost_ref[...])
    x1_o[...] = x1
    h2 = _rmsnorm(x1, gpre_ref[...]) * (1.0 + sc2) + sh2
    h2_o[...] = h2
    logits = _dot3(h2, rw_ref[...]) + rb_ref[...]
    lane = lax.broadcasted_iota(jnp.int32, logits.shape, 1)
    vals, idxs = [], []
    for _ in range(TOP_K):
        m = jnp.max(logits, axis=1, keepdims=True)
        idx = jnp.min(jnp.where(logits == m, lane, N_EXPERTS), axis=1, keepdims=True)
        vals.append(m)
        idxs.append(idx)
        logits = jnp.where(lane == idx, -jnp.inf, logits)
    e = jnp.exp(jnp.concatenate(vals, axis=1) - vals[0])
    tw_o[...] = e / jnp.sum(e, axis=1, keepdims=True)
    ti_o[...] = jnp.concatenate(idxs, axis=1)

    tm = logits.shape[0]
    lane_e = lax.broadcasted_iota(jnp.int32, (tm, LANES), 1)
    tri = jnp.where(lax.broadcasted_iota(jnp.int32, (tm, tm), 0) > lax.broadcasted_iota(jnp.int32, (tm, tm), 1),
                    1.0, 0.0).astype(BF16)
    run = run_ref[...]
    ranks = []
    for idx in idxs:
        onehot = jnp.where(lane_e == idx, 1.0, 0.0)
        before = _dot(tri, onehot.astype(BF16)) + run
        ranks.append(jnp.sum(onehot * before, axis=1, keepdims=True))
        run = run + jnp.sum(onehot, axis=0, keepdims=True)
    run_ref[...] = run
    cnt_o[...] = run
    rank_o[...] = jnp.concatenate(ranks, axis=1).astype(jnp.int32)


def _post_mix(is_prompt, tm, yr, g, bonus, ya, x, mod, lnw, lnb, wout, gpost, gpre, rw, rb, cnt):
    m, d = x.shape
    row = lambda i: (i, 0)
    if is_prompt:
        mod_rows = 8
        mod_map = lambda c: (lambda i: (0, c))
    else:
        mod_rows = tm
        mod_map = lambda c: (lambda i: (i, c))
    wide = pl.BlockSpec((tm, RWKV_WIDTH), row)
    in_specs = [wide, wide, wide, wide, pl.BlockSpec((tm, d), row),
                pl.BlockSpec((mod_rows, d), mod_map(2)), pl.BlockSpec((mod_rows, d), mod_map(3)),
                pl.BlockSpec((mod_rows, d), mod_map(4)),
                _resident(lnw.shape), _resident(lnb.shape), _resident(wout.shape), _resident(gpost.shape),
                _resident(gpre.shape), _resident(rw.shape), _resident(rb.shape), _resident(cnt.shape)]
    out_shape = [jax.ShapeDtypeStruct((m, d), F32), jax.ShapeDtypeStruct((m, d), F32),
                 jax.ShapeDtypeStruct((m, TOP_K), jnp.int32), jax.ShapeDtypeStruct((m, TOP_K), F32),
                 jax.ShapeDtypeStruct((m, TOP_K), jnp.int32), jax.ShapeDtypeStruct((1, LANES), F32)]
    out_specs = [pl.BlockSpec((tm, d), row), pl.BlockSpec((tm, d), row),
                 pl.BlockSpec((tm, TOP_K), row), pl.BlockSpec((tm, TOP_K), row),
                 pl.BlockSpec((tm, TOP_K), row), pl.BlockSpec((1, LANES), lambda i: (0, 0))]
    return pl.pallas_call(
        functools.partial(_post_mix_kernel, is_prompt),
        grid=(m // tm,), in_specs=in_specs, out_specs=out_specs, out_shape=out_shape,
        scratch_shapes=[pltpu.VMEM((1, LANES), F32)],
        compiler_params=pltpu.CompilerParams(dimension_semantics=("arbitrary",), vmem_limit_bytes=VMEM_LIMIT),
        name="post_mix_prompt" if is_prompt else "post_mix_sample",
    )(yr, g, bonus, ya, x, mod, mod, mod, lnw, lnb, wout, gpost, gpre, rw, rb, cnt)


def _moe_kernel(ex_ref, rb_ref, ns_ref, cb_ref, tok_ref, h2_hbm, w1_ref, b1g_ref, b1l_ref, w2_ref, b2_ref, ys_hbm,
                xbuf, xb, acc, w1p, w2b, sem_in, sem_out):
    s = pl.program_id(0)
    f = pl.program_id(1)
    n_super = pl.num_programs(0)
    ns = ns_ref[s]
    rb = rb_ref[s]
    rows = MOE_ROWS
    grp = 2 * LANES
    share = rows // MOE_NF

    def gather_share(sup, n_gran, col):
        base = cb_ref[sup]

        def body(j, c):
            r0 = pl.multiple_of(j * rows + col * share, share)
            for u in range(share):
                pltpu.make_async_copy(h2_hbm.at[pl.ds(tok_ref[base + r0 + u], 1)], xbuf.at[pl.ds(r0 + u, 1)],
                                      sem_in).start()
            return c
        lax.fori_loop(0, n_gran, body, 0)

    def gather_wait(j):
        sl = pl.ds(j * rows, rows)
        pltpu.make_async_copy(h2_hbm.at[pl.ds(0, rows)], xbuf.at[sl], sem_in).wait()

    def y_copy(r0, n):
        return pltpu.make_async_copy(acc.at[pl.ds(r0, n)], ys_hbm.at[pl.ds(rb * rows + r0, n)], sem_out)

    def for_tiles(fn):
        def body(j, c):
            fn(j)
            return c
        lax.fori_loop(0, ns, body, 0)

    @pl.when(ns > 0)
    def _():
        @pl.when(f == 0)
        def _():
            @pl.when(s == 0)
            def _():
                for col in range(MOE_NF):
                    gather_share(0, ns, col)
            b2 = jnp.broadcast_to(b2_ref[0], (rows, D_MODEL))

            def init(j):
                acc[pl.ds(pl.multiple_of(j * rows, rows), rows), :] = b2
            for_tiles(init)

            for_tiles(gather_wait)

            def cast(j):
                sl = pl.ds(pl.multiple_of(j * rows, rows), rows)
                xb[sl, :] = xbuf[sl, :].astype(BF16)
            for_tiles(cast)

        nxt = jnp.minimum(s + 1, n_super - 1)
        gather_share(nxt, jnp.where(s + 1 < n_super, jnp.maximum(ns_ref[nxt], 0), 0), f)

        pr = lax.broadcasted_iota(jnp.int32, (grp, grp), 0)
        pc = lax.broadcasted_iota(jnp.int32, (grp, grp), 1)
        perm = jnp.where(pr == jnp.where(pc < LANES, 2 * pc, 2 * (pc - LANES) + 1), 1.0, 0.0).astype(BF16)
        n_grp = 2 * MOE_FT // grp
        for g in range(n_grp):
            gs = slice(grp * g, grp * (g + 1))
            w1p[:, gs] = _dot(w1_ref[0, :, gs].astype(BF16), perm).astype(BF16)
        w2b[...] = w2_ref[0].astype(BF16)
        b1g = b1g_ref[0]
        b1l = b1l_ref[0]

        last_col = f == pl.num_programs(1) - 1

        def swiglu(hh):
            glu = jnp.concatenate([hh[:, grp * g:grp * g + LANES] for g in range(n_grp)], axis=1) + b1g
            lin = jnp.concatenate([hh[:, grp * g + LANES:grp * (g + 1)] for g in range(n_grp)], axis=1) + b1l
            glu = jnp.minimum(glu, SWIGLU_LIMIT)
            lin = jnp.clip(lin, -SWIGLU_LIMIT, SWIGLU_LIMIT)
            return (glu * jax.nn.sigmoid(SWIGLU_ALPHA * glu) * (lin + 1.0)).astype(BF16)

        def tiles(starts, n):
            sls = [pl.ds(pl.multiple_of(r0, rows), n) for r0 in starts]
            hh = [_dot(xb[sl, :], w1p[...]) for sl in sls]
            act = [swiglu(h) for h in hh]
            out = [_dot(a, w2b[...]) for a in act]
            for sl, o in zip(sls, out):
                acc[sl, :] += o

            @pl.when(last_col)
            def _():
                for r0 in starts:
                    y_copy(r0, n).start()

        big = 2 * rows

        def tile_pair(q, c):
            tiles([2 * big * q, 2 * big * q + big], big)
            return c
        lax.fori_loop(0, ns // 4, tile_pair, 0)

        @pl.when(ns % 4 >= 2)
        def _():
            tiles([(ns // 4) * 2 * big], big)

        @pl.when(ns % 2 == 1)
        def _():
            tiles([(ns - 1) * rows], rows)

        @pl.when(last_col)
        def _():
            for_tiles(lambda j: y_copy(j * rows, rows).wait())

    @pl.when((ns < 0) & (f == 0))
    def _():
        acc[0:rows, :] = jnp.zeros((rows, D_MODEL), F32)

        def z_copy(j):
            return pltpu.make_async_copy(acc.at[pl.ds(0, rows)], ys_hbm.at[pl.ds((rb + j) * rows, rows)], sem_out)

        def body(j, c, op):
            op(z_copy(j))
            return c
        lax.fori_loop(0, -ns, functools.partial(body, op=lambda cp: cp.start()), 0)
        lax.fori_loop(0, -ns, functools.partial(body, op=lambda cp: cp.wait()), 0)


def _moe_experts(n_rows, ex, rb, ns, cb, tok_sorted, h2, w1, b1g, b1l, w2, b2):
    n_super = ex.shape[0]
    nf = MOE_NF
    last = nf - 1
    sub_rows = MOE_SUB * MOE_ROWS

    def fcol(s, f, ns_):
        return jnp.where(ns_[s] > 0, f, last)

    grid_spec = pltpu.PrefetchScalarGridSpec(
        num_scalar_prefetch=5,
        grid=(n_super, nf),
        in_specs=[pl.BlockSpec(memory_space=pl.ANY),
                  pl.BlockSpec((1, D_MODEL, 2 * MOE_FT), lambda s, f, e_, r_, n_, c_, t_: (e_[s], 0, fcol(s, f, n_))),
                  pl.BlockSpec((1, 1, MOE_FT), lambda s, f, e_, r_, n_, c_, t_: (e_[s], 0, fcol(s, f, n_))),
                  pl.BlockSpec((1, 1, MOE_FT), lambda s, f, e_, r_, n_, c_, t_: (e_[s], 0, fcol(s, f, n_))),
                  pl.BlockSpec((1, MOE_FT, D_MODEL), lambda s, f, e_, r_, n_, c_, t_: (e_[s], fcol(s, f, n_), 0)),
                  pl.BlockSpec((1, 1, D_MODEL), lambda s, f, e_, r_, n_, c_, t_: (e_[s], 0, 0))],
        out_specs=pl.BlockSpec(memory_space=pl.ANY),
        scratch_shapes=[pltpu.VMEM((sub_rows, D_MODEL), F32),
                        pltpu.VMEM((sub_rows, D_MODEL), BF16),
                        pltpu.VMEM((sub_rows, D_MODEL), F32),
                        pltpu.VMEM((D_MODEL, 2 * MOE_FT), BF16),
                        pltpu.VMEM((MOE_FT, D_MODEL), BF16),
                        pltpu.SemaphoreType.DMA, pltpu.SemaphoreType.DMA])
    return pl.pallas_call(
        _moe_kernel,
        grid_spec=grid_spec,
        out_shape=jax.ShapeDtypeStruct((n_rows, D_MODEL), F32),
        compiler_params=pltpu.CompilerParams(dimension_semantics=("arbitrary", "arbitrary"),
                                             vmem_limit_bytes=VMEM_LIMIT),
        name="moe_experts",
    )(ex, rb, ns, cb, tok_sorted, h2, w1, b1g, b1l, w2, b2)


def _combine_kernel(is_prompt, idx_ref, nidx_ref, ys_hbm, x1_ref, tw_ref, gt2_ref, gpost_ref, o_ref, buf, sem):
    tk = COMBINE_TOK
    n = TOP_K * tk
    unroll = 16
    i = pl.program_id(0)
    slot = i % 2

    def start_rows(ref, buf_slot):
        def body(q, c):
            j0 = pl.multiple_of(q * unroll, unroll)
            for u in range(unroll):
                j = j0 + u
                pltpu.make_async_copy(ys_hbm.at[pl.ds(ref[0, 0, j], 1)], buf.at[buf_slot, pl.ds(j, 1)],
                                      sem.at[buf_slot]).start()
            return c
        lax.fori_loop(0, n // unroll, body, 0)

    @pl.when(i == 0)
    def _():
        start_rows(idx_ref, 0)

    @pl.when(i + 1 < pl.num_programs(0))
    def _():
        start_rows(nidx_ref, 1 - slot)

    pltpu.make_async_copy(ys_hbm.at[pl.ds(0, n)], buf.at[slot], sem.at[slot]).wait()
    tw = tw_ref[...]
    f = tw[:, 0:1] * buf[slot, 0:tk, :]
    for k in range(1, TOP_K):
        f = f + tw[:, k:k + 1] * buf[slot, k * tk:(k + 1) * tk, :]
    gt2 = gt2_ref[...]
    if is_prompt:
        gt2 = gt2[0:1]
    o_ref[...] = x1_ref[...] + gt2 * _rmsnorm(f, gpost_ref[...])


def _combine(is_prompt, pos, ys, x1, tw, mod, gpost):
    m, d = x1.shape
    tk = COMBINE_TOK
    nblk = m // tk
    idx = pos.reshape(nblk, tk, TOP_K).transpose(0, 2, 1).reshape(nblk, 1, TOP_K * tk)
    row = lambda i: (i, 0)
    mod_spec = (pl.BlockSpec((8, d), lambda i: (0, 5)) if is_prompt else pl.BlockSpec((tk, d), lambda i: (i, 5)))
    return pl.pallas_call(
        functools.partial(_combine_kernel, is_prompt),
        grid=(nblk,),
        in_specs=[pl.BlockSpec((1, 1, TOP_K * tk), lambda i: (i, 0, 0), memory_space=pltpu.SMEM),
                  pl.BlockSpec((1, 1, TOP_K * tk), lambda i: (jnp.minimum(i + 1, nblk - 1), 0, 0),
                               memory_space=pltpu.SMEM),
                  pl.BlockSpec(memory_space=pl.ANY),
                  pl.BlockSpec((tk, d), row), pl.BlockSpec((tk, TOP_K), row), mod_spec,
                  pl.BlockSpec((1, d), lambda i: (0, 0))],
        out_specs=pl.BlockSpec((tk, d), row),
        out_shape=jax.ShapeDtypeStruct((m, d), F32),
        scratch_shapes=[pltpu.VMEM((2, TOP_K * tk, d), F32), pltpu.SemaphoreType.DMA((2,))],
        compiler_params=pltpu.CompilerParams(dimension_semantics=("arbitrary",)),
        name="moe_combine_prompt" if is_prompt else "moe_combine_sample",
    )(idx, idx, ys, x1, tw, mod, gpost)


def _routing_tables(top_i, rank, counts):
    n_assign = top_i.size
    e_flat = top_i.reshape(-1)
    tiles = (counts + MOE_ROWS - 1) // MOE_ROWS
    tile_start = jnp.cumsum(tiles) - tiles
    group_start = jnp.cumsum(counts) - counts
    pos = tile_start[e_flat] * MOE_ROWS + rank.reshape(-1)
    n_rows = _padded_rows(n_assign)
    assert n_assign <= (1 << 16) and n_rows <= (1 << 16)
    key = pos.astype(jnp.uint32) * jnp.uint32(1 << 16) + jnp.arange(n_assign, dtype=jnp.uint32)
    tok_sorted = (jnp.sort(key) & jnp.uint32(0xFFFF)).astype(jnp.int32) // TOP_K
    tok_sorted = jnp.pad(tok_sorted, (0, n_rows - n_assign))
    supers = (tiles + MOE_SUB - 1) // MOE_SUB
    super_end = jnp.cumsum(supers)
    s_idx = jnp.arange(_max_supers(n_assign), dtype=jnp.int32)
    ex = jnp.minimum(jnp.sum((s_idx[:, None] >= super_end[None, :]).astype(jnp.int32), axis=1),
                     N_EXPERTS - 1).astype(jnp.int32)
    j = s_idx - (super_end - supers)[ex]
    live = s_idx < super_end[-1]
    tail = jnp.sum(tiles) + MOE_SUB * (s_idx - super_end[-1])
    n_clear = jnp.clip(n_rows // MOE_ROWS - tail, 0, MOE_SUB)
    ns = jnp.where(live, jnp.clip(tiles[ex] - MOE_SUB * j, 0, MOE_SUB), -n_clear).astype(jnp.int32)
    rb = jnp.where(live, tile_start[ex] + MOE_SUB * j, tail).astype(jnp.int32)
    cb = jnp.where(live, group_start[ex] + MOE_SUB * MOE_ROWS * j, 0).astype(jnp.int32)
    last_live = jnp.max(jnp.where(live, ex, 0))
    ex = jnp.where(live, ex, last_live).astype(jnp.int32)
    return pos.astype(jnp.int32), tok_sorted, ex, rb, ns, cb


def _padded_rows(n_assign):
    rows = n_assign + N_EXPERTS * (MOE_ROWS - 1)
    return -(-rows // MOE_ROWS) * MOE_ROWS


def _max_supers(n_assign):
    return N_EXPERTS + -(-_padded_rows(n_assign) // (MOE_ROWS * MOE_SUB))


def kernel(x_prompt, x_sample, cache_k, cache_v, state_wkv, state_shift, c_prompt, c_sample, w_ada, b_ada, g_pre_mix, g_post_mix, g_pre_ffn, g_post_ffn, mu_shift, w_in, rwkv_w0, rwkv_w2, rwkv_a0, rwkv_a2, rwkv_g2, rwkv_k_k, rwkv_k_a, rwkv_r_k, rwkv_ln_w, rwkv_ln_b, attn_sinks, w_out, router_w, router_b, moe_w1, moe_b1, moe_w2, moe_b2):
    assert w_ada.shape[0] == 1, "single-layer step"
    d = D_MODEL
    t = x_prompt.shape[1]
    nb = x_sample.shape[0]
    xp = x_prompt.reshape(t, d)
    xs = x_sample.reshape(nb, d)

    c_all = jnp.concatenate([jnp.broadcast_to(c_prompt, (8, d)), c_sample], axis=0)
    mod = _adaln_mod(c_all, w_ada[0], b_ada[0].reshape(1, 6 * d))
    mod_p, mod_s = mod[:8], mod[8:]

    offs = [0, 1024, 2048, 3072, 3136, 3200, 3360]
    w_in0 = w_in[0]
    pad_to = lambda w, n: jnp.pad(w, ((0, 0), (0, n - w.shape[1])))
    wts = [w_in0[:, offs[0]:offs[1]], w_in0[:, offs[1]:offs[2]], w_in0[:, offs[2]:offs[3]],
           pad_to(w_in0[:, offs[3]:offs[4]], 128), pad_to(w_in0[:, offs[4]:offs[5]], 128),
           pad_to(w_in0[:, offs[5]:offs[6]], 256), w_in0[:, offs[6]:]]
    wts = [w.astype(BF16) for w in wts]
    pad_rows = lambda w, n: jnp.pad(w, ((0, n - w.shape[0]), (0, 0))).astype(BF16)
    row = lambda p: p.reshape(1, -1)
    rw = [row(rwkv_w0[0]), pad_rows(rwkv_w2[0], 128), row(rwkv_a0[0]), pad_rows(rwkv_a2[0], 128),
          pad_rows(rwkv_g2[0], 256), row(rwkv_k_k[0]), row(rwkv_k_a[0]), row(rwkv_r_k[0])]
    gpm = row(g_pre_mix[0])
    mu8 = jnp.pad(mu_shift[0], ((0, 8 - N_SHIFTED), (0, 0)))

    pr = _in_proj(True, 256, xp, xp, mod_p, gpm, mu8, wts, rw)
    sr = _in_proj(False, nb, xs, state_shift[0], mod_s, gpm, mu8, wts, rw)
    r_p, lw_p, kf_p, v_p, na_p, b_p, g_p, bonus_p, q_p, ka_p, va_p, hlast_p = pr
    r_s, lw_s, kf_s, v_s, na_s, b_s, g_s, bonus_s, q_s, ka_s, va_s, h_s = sr

    y_p, st_p = _rwkv_prompt(r_p, lw_p, kf_p, v_p, na_p, b_p)
    y_s, wkv_s = _rwkv_sample(r_s, lw_s, kf_s, v_s, na_s, b_s, state_wkv[0])
    y_s = y_s.reshape(nb, RWKV_WIDTH)
    sinks = attn_sinks[0]
    att_p = _swa_prompt(q_p, ka_p, va_p, sinks.reshape(1, N_Q_HEADS))
    wb = cache_k.shape[2]
    att_s, ck_new, cv_new = _swa_sample(q_s, ka_s, va_s, cache_k[0].reshape(nb, wb, KV_WIDTH),
                                        cache_v[0].reshape(nb, wb, KV_WIDTH), sinks.reshape(N_Q_HEADS, 1))
    att_s = att_s.reshape(nb, ATTN_WIDTH)

    post = [row(rwkv_ln_w[0]), row(rwkv_ln_b[0]), w_out[0].astype(BF16), row(g_post_mix[0]),
            row(g_pre_ffn[0]), router_w[0], row(router_b[0])]
    x1_p, h2_p, ti_p, tw_p, rank_p, cnt_p = _post_mix(True, 256, y_p, g_p, bonus_p, att_p, xp, mod_p, *post,
                                                      jnp.zeros((1, LANES), F32))
    x1_s, h2_s, ti_s, tw_s, rank_s, cnt_s = _post_mix(False, nb, y_s, g_s, bonus_s, att_s, xs, mod_s, *post, cnt_p)

    h2_all = jnp.concatenate([h2_p, h2_s], axis=0)
    pos, tok_sorted, ex, rb, ns, cb = _routing_tables(jnp.concatenate([ti_p, ti_s], axis=0),
                                                      jnp.concatenate([rank_p, rank_s], axis=0),
                                                      cnt_s[0, :N_EXPERTS].astype(jnp.int32))
    b1 = moe_b1[0].reshape(N_EXPERTS, 1, D_FF, 2)
    ys_rows = _moe_experts(tok_sorted.shape[0], ex, rb, ns, cb, tok_sorted, h2_all, moe_w1[0], b1[..., 0], b1[..., 1],
                           moe_w2[0], moe_b2[0].reshape(N_EXPERTS, 1, d))
    pos = pos.reshape(-1, TOP_K)
    gpf = row(g_post_ffn[0])
    out_p = _combine(True, pos[:t], ys_rows, x1_p, tw_p, mod_p, gpf)
    out_s = _combine(False, pos[t:], ys_rows, x1_s, tw_s, mod_s, gpf)

    n_keep = min(WINDOW, t)
    st_heads = jnp.stack([st_p[:, :HEAD_DIM, :HEAD_DIM], st_p[:, HEAD_DIM:, HEAD_DIM:]], axis=1)
    return (out_p.reshape(1, t, d),
            out_s.reshape(nb, 1, d),
            ka_p[t - n_keep:].reshape(1, 1, n_keep, N_KV_HEADS, HEAD_DIM),
            va_p[t - n_keep:].reshape(1, 1, n_keep, N_KV_HEADS, HEAD_DIM),
            st_heads.reshape(1, 1, N_RWKV_HEADS, HEAD_DIM, HEAD_DIM),
            hlast_p[7:8].reshape(1, 1, d),
            ck_new.reshape(1, nb, wb, N_KV_HEADS, HEAD_DIM),
            cv_new.reshape(1, nb, wb, N_KV_HEADS, HEAD_DIM),
            wkv_s.reshape(1, nb, N_RWKV_HEADS, HEAD_DIM, HEAD_DIM),
            h_s.reshape(1, nb, d))
```

```python
import functools
import math

import jax
import jax.numpy as jnp
from jax import lax
from jax.experimental import pallas as pl
from jax.experimental.pallas import tpu as pltpu

F32 = jnp.float32
BF16 = jnp.bfloat16

D_MODEL = 2048
HEAD_DIM = 64
RWKV_WIDTH = 1024
N_RWKV_HEADS = 16
ATTN_WIDTH = 1024
N_Q_HEADS = 16
N_KV_HEADS = 2
Q_PER_KV = 8
KV_WIDTH = 128
WINDOW = 128
N_SHIFTED = 6
N_EXPERTS = 32
TOP_K = 4
D_FF = 2048
SWIGLU_ALPHA = 1.702
SWIGLU_LIMIT = 7.0
NORM_EPS = 1e-6
GN_EPS = 64e-5
L2_EPS = 1e-12

LANES = 128
SUBLANES = 8
VMEM_LIMIT = 56 * 1024 * 1024

CHUNK = 64
RWKV_ROWS = 256
RWKV_PAIRS = 8
MOE_ROWS = 128
MOE_SUB = 12
MOE_FT = 256
MOE_NF = D_FF // MOE_FT
COMBINE_TOK = 128


def _dot(a, b):
    return jnp.dot(a, b, preferred_element_type=F32)


def _dot_nt(a, b):
    return lax.dot_general(a, b, (((1,), (1,)), ((), ())), preferred_element_type=F32)


def _split_bf16(x):
    hi = x.astype(BF16)
    lo = (x - hi.astype(F32)).astype(BF16)
    return hi, lo


def _dot3(a, b):
    ah, al = _split_bf16(a)
    bh, bl = _split_bf16(b)
    return _dot(ah, bh) + _dot(ah, bl) + _dot(al, bh)


def _seg_sum64(x):
    r = lax.broadcasted_iota(jnp.int32, (LANES, LANES), 0) // HEAD_DIM
    c = lax.broadcasted_iota(jnp.int32, (LANES, LANES), 1) // HEAD_DIM
    bd = jnp.where(r == c, 1.0, 0.0).astype(BF16)
    hi, lo = _split_bf16(x)
    outs = []
    for j in range(x.shape[1] // LANES):
        sl = slice(LANES * j, LANES * (j + 1))
        outs.append(_dot(hi[:, sl], bd) + _dot(lo[:, sl], bd))
    return jnp.concatenate(outs, axis=1)


def _rmsnorm(x, g):
    ms = jnp.mean(x * x, axis=-1, keepdims=True)
    return x * lax.rsqrt(ms + NORM_EPS) * g


def _resident(shape):
    nd = len(shape)
    return pl.BlockSpec(shape, lambda *_: (0,) * nd, pipeline_mode=pl.Buffered(1))


def _mod_kernel(c_ref, w_ref, b_ref, o_ref):
    c = c_ref[...]
    s = c * jax.nn.sigmoid(c)
    o_ref[...] = _dot3(s, w_ref[...]) + b_ref[...]


def _adaln_mod(c, w_ada, b_ada):
    rows, d = c.shape
    n = w_ada.shape[1]
    tn = 512
    return pl.pallas_call(
        _mod_kernel,
        grid=(n // tn,),
        in_specs=[pl.BlockSpec((rows, d), lambda j: (0, 0)),
                  pl.BlockSpec((d, tn), lambda j: (0, j)),
                  pl.BlockSpec((1, tn), lambda j: (0, j))],
        out_specs=pl.BlockSpec((rows, tn), lambda j: (0, j)),
        out_shape=jax.ShapeDtypeStruct((rows, n), F32),
        compiler_params=pltpu.CompilerParams(dimension_semantics=("arbitrary",), vmem_limit_bytes=VMEM_LIMIT),
        name="adaln_mod",
    )(c, w_ada, b_ada)


def _inproj_kernel(is_prompt, tm,
                   x_ref, prev_ref, sh_ref, sc_ref, gpm_ref, mu_ref,
                   wr_ref, wk_ref, wv_ref, wwl_ref, wal_ref, wgl_ref, wqkv_ref,
                   w0_ref, w2_ref, a0_ref, a2_ref, g2_ref, kk_ref, ka_ref, rk_ref,
                   r_o, lw_o, kf_o, v_o, na_o, b_o, g_o, bonus_o, q_o, kat_o, vat_o, h_o):
    i = pl.program_id(0)
    gpm = gpm_ref[...]
    sh = sh_ref[...]
    sc = sc_ref[...]
    if is_prompt:
        sh = sh[0:1]
        sc = sc[0:1]

    def modnorm(x):
        return _rmsnorm(x, gpm) * (1.0 + sc) + sh

    h = modnorm(x_ref[...])
    if is_prompt:
        hp = modnorm(prev_ref[...])[7:8, :]
        hp = jnp.where(i > 0, hp, 0.0)
        row = lax.broadcasted_iota(jnp.int32, h.shape, 0)
        hprev = jnp.where(row == 0, hp, pltpu.roll(h, 1, axis=0))
        h_o[...] = h[tm - 8:tm, :]
    else:
        hprev = prev_ref[...]
        h_o[...] = h
    dx = hprev - h
    mu = mu_ref[...]

    def branch(j, w_ref):
        xi = (h + dx * mu[j:j + 1, :]).astype(BF16)
        return _dot(xi, w_ref[...])

    r = branch(0, wr_ref)
    k = branch(1, wk_ref)
    v = branch(2, wv_ref)
    wl = branch(3, wwl_ref)
    al = branch(4, wal_ref)
    gl = branch(5, wgl_ref)
    qkv = _dot(h.astype(BF16), wqkv_ref[...])
    q_o[...] = qkv[:, :ATTN_WIDTH]
    kat_o[...] = qkv[:, ATTN_WIDTH:ATTN_WIDTH + KV_WIDTH]
    vat_o[...] = qkv[:, ATTN_WIDTH + KV_WIDTH:]

    z = w0_ref[...] + _dot(jnp.tanh(wl).astype(BF16), w2_ref[...])
    w_raw = -jnp.logaddexp(-z, 0.0) - 0.5
    lw_o[...] = -jnp.exp(w_raw)
    a = jax.nn.sigmoid(a0_ref[...] + _dot(al.astype(BF16), a2_ref[...]))
    g_o[...] = _dot(jax.nn.sigmoid(gl).astype(BF16), g2_ref[...])
    kk = k * kk_ref[...]
    kk = kk / jnp.maximum(jnp.sqrt(_seg_sum64(kk * kk)), L2_EPS)
    kf = k * (1.0 + (a - 1.0) * ka_ref[...])
    r_o[...] = r
    kf_o[...] = kf
    v_o[...] = v
    na_o[...] = -kk
    b_o[...] = kk * a
    bonus_o[...] = _seg_sum64(r * kf * rk_ref[...]) * v


def _in_proj(is_prompt, tm, x, prev, mod, gpm, mu8, wts, rw):
    m, d = x.shape
    grid = (m // tm,)
    row = lambda i: (i, 0)
    if is_prompt:
        prev_spec = pl.BlockSpec((8, d), lambda i: (jnp.maximum(i * (tm // 8) - 1, 0), 0))
        mod_rows = 8
        mod_map = lambda c: (lambda i: (0, c))
        h_shape, h_spec = (8, d), pl.BlockSpec((8, d), lambda i: (0, 0))
    else:
        prev_spec = pl.BlockSpec((tm, d), row)
        mod_rows = tm
        mod_map = lambda c: (lambda i: (i, c))
        h_shape, h_spec = (m, d), pl.BlockSpec((tm, d), row)
    in_specs = [pl.BlockSpec((tm, d), row), prev_spec,
                pl.BlockSpec((mod_rows, d), mod_map(0)), pl.BlockSpec((mod_rows, d), mod_map(1)),
                _resident((1, d)), _resident((8, d))]
    in_specs += [_resident(w.shape) for w in wts]
    in_specs += [_resident(p.shape) for p in rw]
    wide = jax.ShapeDtypeStruct((m, RWKV_WIDTH), F32)
    wide_spec = pl.BlockSpec((tm, RWKV_WIDTH), row)
    kv = jax.ShapeDtypeStruct((m, KV_WIDTH), F32)
    kv_spec = pl.BlockSpec((tm, KV_WIDTH), row)
    out_shape = [wide] * 9 + [kv, kv, jax.ShapeDtypeStruct(h_shape, F32)]
    out_specs = [wide_spec] * 9 + [kv_spec, kv_spec, h_spec]
    return pl.pallas_call(
        functools.partial(_inproj_kernel, is_prompt, tm),
        grid=grid, in_specs=in_specs, out_specs=out_specs, out_shape=out_shape,
        compiler_params=pltpu.CompilerParams(dimension_semantics=("arbitrary",), vmem_limit_bytes=VMEM_LIMIT),
        name="in_proj_prompt" if is_prompt else "in_proj_sample",
    )(x, prev, mod, mod, gpm, mu8, *wts, *rw)


def _rwkv_chunk_kernel(r_ref, lw_ref, k_ref, v_ref, a_ref, b_ref, y_ref, s_ref, st_ref):
    t = pl.program_id(1)
    C = CHUNK
    P = 2 * HEAD_DIM

    @pl.when(t == 0)
    def _():
        st_ref[...] = jnp.zeros_like(st_ref)

    ri = lax.broadcasted_iota(jnp.int32, (P, P), 0)
    ci = lax.broadcasted_iota(jnp.int32, (P, P), 1)
    bd = (ri // C) == (ci // C)
    tril_s = bd & ((ri % C) > (ci % C))
    tril_i = bd & ((ri % C) >= (ci % C))
    eye = jnp.where(ri == ci, 1.0, 0.0)
    lane0 = lax.broadcasted_iota(jnp.int32, (C, P), 1) < HEAD_DIM
    trow = lax.broadcasted_iota(jnp.int32, (C, P), 0)

    def stack(x):
        return jnp.concatenate([jnp.where(lane0, x, 0.0), jnp.where(lane0, 0.0, x)], axis=0)

    def dup(x):
        return jnp.concatenate([x, x], axis=0)

    def prep(sl, pp):
        ln = slice(P * pp, P * (pp + 1))
        lw = lw_ref[sl, ln]
        cw = lw
        for s in (1, 2, 4, 8, 16, 32):
            cw = cw + jnp.where(trow >= s, pltpu.roll(cw, s, axis=0), 0.0)
        cw_last = cw[C - 1:C, :]
        e_neg = jnp.exp(-cw)
        e_end = jnp.exp(cw_last - cw)
        k = k_ref[sl, ln]
        v = v_ref[sl, ln]
        b = b_ref[sl, ln]
        a2 = stack(a_ref[sl, ln] * jnp.exp(cw - lw))
        r2 = stack(r_ref[sl, ln] * jnp.exp(cw))
        return dict(
            v2=stack(v),
            lhs=jnp.concatenate([a2, r2], axis=0).astype(BF16),
            rhs=jnp.concatenate([dup(k * e_neg), dup(b * e_neg)], axis=0).astype(BF16),
            kbh=jnp.concatenate([dup(b * e_end), dup(k * e_end)], axis=0).astype(BF16),
            decay=jnp.exp(cw_last))

    def chunk(c, carry):
        sl = pl.ds(pl.multiple_of(c * C, C), C)
        pairs = range(RWKV_PAIRS)
        st = [st_ref[pp] for pp in pairs]
        d = [prep(sl, pp) for pp in pairs]
        gram = [_dot_nt(d[pp]['lhs'], d[pp]['rhs']) for pp in pairs]
        l2 = [jnp.where(tril_s, gram[pp][0:P, P:2 * P], 0.0) for pp in pairs]
        a_s = [_dot_nt(d[pp]['lhs'], st[pp].astype(BF16)) for pp in pairs]
        v2b = [d[pp]['v2'].astype(BF16) for pp in pairs]
        rhs_u = [a_s[pp][0:P] + _dot(jnp.where(tril_s, gram[pp][0:P, 0:P], 0.0).astype(BF16), v2b[pp])
                 for pp in pairs]
        inv = [eye + l2[pp] for pp in pairs]
        lp = l2
        for _ in range(5):
            lpb = [lp[pp].astype(BF16) for pp in pairs]
            lp = [_dot(lpb[pp], lpb[pp]) for pp in pairs]
            inv = [inv[pp] + _dot(lp[pp].astype(BF16), inv[pp].astype(BF16)) for pp in pairs]
        u2 = [_dot(inv[pp].astype(BF16), rhs_u[pp].astype(BF16)) for pp in pairs]
        p_cat = [jnp.concatenate([jnp.where(tril_i, gram[pp][P:2 * P, P:2 * P], 0.0),
                                  jnp.where(tril_i, gram[pp][P:2 * P, 0:P], 0.0)], axis=1).astype(BF16)
                 for pp in pairs]
        y2 = [a_s[pp][P:2 * P] + _dot(p_cat[pp], jnp.concatenate([u2[pp].astype(BF16), v2b[pp]], axis=0))
              for pp in pairs]
        uvt = [jnp.concatenate([u2[pp], d[pp]['v2']], axis=0).T.astype(BF16) for pp in pairs]
        st_new = [jnp.where(bd, st[pp] * d[pp]['decay'] + _dot(uvt[pp], d[pp]['kbh']), 0.0) for pp in pairs]
        for pp in pairs:
            y_ref[sl, P * pp:P * (pp + 1)] = y2[pp][0:C] + y2[pp][C:2 * C]
            st_ref[pp] = st_new[pp]
        return carry

    lax.fori_loop(0, RWKV_ROWS // C, chunk, 0)

    @pl.when(t == pl.num_programs(1) - 1)
    def _():
        s_ref[...] = st_ref[...]


def _rwkv_prompt(r, lw, kf, v, na, b):
    t = r.shape[0]
    n_pairs = RWKV_WIDTH // LANES
    width = RWKV_PAIRS * LANES
    spec = pl.BlockSpec((RWKV_ROWS, width), lambda p, i: (i, p))
    return pl.pallas_call(
        _rwkv_chunk_kernel,
        grid=(n_pairs // RWKV_PAIRS, t // RWKV_ROWS),
        in_specs=[spec] * 6,
        out_specs=[spec, pl.BlockSpec((RWKV_PAIRS, LANES, LANES), lambda p, i: (p, 0, 0))],
        out_shape=[jax.ShapeDtypeStruct((t, RWKV_WIDTH), F32),
                   jax.ShapeDtypeStruct((n_pairs, LANES, LANES), F32)],
        scratch_shapes=[pltpu.VMEM((RWKV_PAIRS, LANES, LANES), F32)],
        compiler_params=pltpu.CompilerParams(dimension_semantics=("arbitrary", "arbitrary")),
        name="rwkv_chunked",
    )(r, lw, kf, v, na, b)


def _rwkv_step_kernel(bb, r_ref, lw_ref, k_ref, v_ref, a_ref, b_ref, s_ref, y_ref, so_ref):
    n = HEAD_DIM
    eye = jnp.where(lax.broadcasted_iota(jnp.int32, (n, n), 0) == lax.broadcasted_iota(jnp.int32, (n, n), 1), 1.0, 0.0)

    def body(bi, carry):
        rb = r_ref[bi]
        dec = jnp.exp(lw_ref[bi])
        kb = k_ref[bi]
        vb = v_ref[bi]
        ab = a_ref[bi]
        bb_ = b_ref[bi]
        heads = range(N_RWKV_HEADS)
        row = lambda x, h: x[h:h + 1]
        s = [s_ref[bi, h] for h in heads]
        sa = [jnp.sum(s[h] * row(ab, h), axis=1, keepdims=True) for h in heads]
        vcol = [jnp.sum(eye * row(vb, h), axis=1, keepdims=True) for h in heads]
        s2 = [s[h] * row(dec, h) + sa[h] * row(bb_, h) + vcol[h] * row(kb, h) for h in heads]
        ycol = [jnp.sum(s2[h] * row(rb, h), axis=1, keepdims=True) for h in heads]
        ys = [jnp.sum(eye * ycol[h], axis=0, keepdims=True) for h in heads]
        for h in heads:
            so_ref[bi, h] = s2[h]
        y_ref[bi] = jnp.concatenate(ys, axis=0)
        return carry

    lax.fori_loop(0, bb, body, 0)


def _rwkv_sample(r, lw, kf, v, na, b, state):
    nb = r.shape[0]
    bb = 8
    vec = lambda x: x.reshape(nb, N_RWKV_HEADS, HEAD_DIM)
    vspec = pl.BlockSpec((bb, N_RWKV_HEADS, HEAD_DIM), lambda i: (i, 0, 0))
    sspec = pl.BlockSpec((bb, N_RWKV_HEADS, HEAD_DIM, HEAD_DIM), lambda i: (i, 0, 0, 0))
    return pl.pallas_call(
        functools.partial(_rwkv_step_kernel, bb),
        grid=(nb // bb,),
        in_specs=[vspec] * 6 + [sspec],
        out_specs=[vspec, sspec],
        out_shape=[jax.ShapeDtypeStruct((nb, N_RWKV_HEADS, HEAD_DIM), F32),
                   jax.ShapeDtypeStruct(state.shape, F32)],
        compiler_params=pltpu.CompilerParams(dimension_semantics=("arbitrary",)),
        name="rwkv_step",
    )(vec(r), vec(lw), vec(kf), vec(v), vec(na), vec(b), state)


def _alibi_slope(head):
    return 2.0 ** (-8.0 * (head + 1) / N_Q_HEADS)


def _swa_prompt_kernel(q_ref, kc_ref, kp_ref, vc_ref, vp_ref, sink_ref, o_ref):
    n = pl.program_id(0)
    w = WINDOW
    kcat = jnp.concatenate([kp_ref[...], kc_ref[...]], axis=0)
    vcat = jnp.concatenate([vp_ref[...], vc_ref[...]], axis=0)
    lane_k = lax.broadcasted_iota(jnp.int32, kcat.shape, 1) < HEAD_DIM
    kswap = pltpu.roll(kcat, HEAD_DIM, axis=1)
    vswap = pltpu.roll(vcat, HEAD_DIM, axis=1)
    kdup = [jnp.where(lane_k, kcat, kswap).astype(BF16), jnp.where(lane_k, kswap, kcat).astype(BF16)]
    vdup = [jnp.where(lane_k, vcat, vswap).astype(BF16), jnp.where(lane_k, vswap, vcat).astype(BF16)]
    qi = lax.broadcasted_iota(jnp.int32, (w, 2 * w), 0)
    kj = lax.broadcasted_iota(jnp.int32, (w, 2 * w), 1)
    dist = qi + w - kj
    valid = (dist >= 0) & (dist <= w) & ((n > 0) | (kj >= w))
    distf = dist.astype(F32)
    lane_q = lax.broadcasted_iota(jnp.int32, (w, LANES), 1) < HEAD_DIM
    sinks = sink_ref[...]
    group = 4
    for j0 in range(0, N_Q_HEADS // 2, group):
        js = range(j0, j0 + group)
        heads = [2 * j + half for j in js for half in range(2)]
        kvh = {h: h // Q_PER_KV for h in heads}
        q2 = {}
        for j in js:
            qp = q_ref[:, LANES * j:LANES * (j + 1)] * (1.0 / math.sqrt(HEAD_DIM))
            q2[j] = jnp.concatenate([jnp.where(lane_q, qp, 0.0), jnp.where(lane_q, 0.0, qp)], axis=0).astype(BF16)
        s2 = {j: _dot_nt(q2[j], kdup[kvh[2 * j]]) for j in js}
        s = {h: jnp.where(valid, s2[h // 2][w * (h % 2):w * (h % 2 + 1)] - _alibi_slope(h) * distf, -jnp.inf)
             for h in heads}
        sink = {h: sinks[0:1, h:h + 1] for h in heads}
        m = {h: jnp.maximum(jnp.max(s[h], axis=-1, keepdims=True), sink[h]) for h in heads}
        p = {h: jnp.exp(s[h] - m[h]) for h in heads}
        den = {h: jnp.sum(p[h], axis=-1, keepdims=True) + jnp.exp(sink[h] - m[h]) for h in heads}
        o = {h: _dot(p[h].astype(BF16), vdup[kvh[h]]) / den[h] for h in heads}
        for j in js:
            o_ref[:, LANES * j:LANES * (j + 1)] = jnp.where(lane_q, o[2 * j], o[2 * j + 1])


def _swa_prompt(q, ka, va, sinks):
    t = q.shape[0]
    w = WINDOW
    cur = lambda n: (n, 0)
    prv = lambda n: (jnp.maximum(n - 1, 0), 0)
    kvs = lambda f: pl.BlockSpec((w, KV_WIDTH), f)
    return pl.pallas_call(
        _swa_prompt_kernel,
        grid=(t // w,),
        in_specs=[pl.BlockSpec((w, ATTN_WIDTH), cur), kvs(cur), kvs(prv), kvs(cur), kvs(prv),
                  pl.BlockSpec((1, N_Q_HEADS), lambda n: (0, 0))],
        out_specs=pl.BlockSpec((w, ATTN_WIDTH), cur),
        out_shape=jax.ShapeDtypeStruct((t, ATTN_WIDTH), F32),
        compiler_params=pltpu.CompilerParams(dimension_semantics=("arbitrary",)),
        name="swa_prompt",
    )(q, ka, ka, va, va, sinks)


def _swa_sample_kernel(q_ref, kn_ref, vn_ref, ck_ref, cv_ref, sink_ref, o_ref, ko_ref, vo_ref):
    wb = ck_ref.shape[1]
    q = q_ref[...] * (1.0 / math.sqrt(HEAD_DIM))
    q2 = jnp.concatenate([q, q], axis=2)
    rowh = lax.broadcasted_iota(jnp.int32, q2.shape, 1) // Q_PER_KV
    laneh = lax.broadcasted_iota(jnp.int32, q2.shape, 2) // HEAD_DIM
    qb = jnp.where(rowh == laneh, q2, 0.0)
    kn = kn_ref[...]
    vn = vn_ref[...]
    ck = ck_ref[...]
    cv = cv_ref[...]
    s = jnp.einsum('bqd,bkd->bqk', qb.astype(BF16), ck.astype(BF16), preferred_element_type=F32)
    s_self = jnp.sum(qb * kn, axis=2, keepdims=True)
    head = lax.broadcasted_iota(jnp.int32, (1, N_Q_HEADS, 1), 1).astype(F32)
    slope = jnp.exp2(-8.0 * (head + 1.0) / N_Q_HEADS)
    dist = (wb - lax.broadcasted_iota(jnp.int32, (1, 1, wb), 2)).astype(F32)
    s = s - slope * dist
    sink = sink_ref[...][None]
    m = jnp.maximum(jnp.maximum(jnp.max(s, axis=2, keepdims=True), s_self), sink)
    p = jnp.exp(s - m)
    p_self = jnp.exp(s_self - m)
    den = jnp.sum(p, axis=2, keepdims=True) + p_self + jnp.exp(sink - m)
    o = jnp.einsum('bqk,bkd->bqd', p.astype(BF16), cv.astype(BF16), preferred_element_type=F32)
    o = (o + p_self * vn) / den
    sel = lax.broadcasted_iota(jnp.int32, (1, N_Q_HEADS, HEAD_DIM), 1) < Q_PER_KV
    o_ref[...] = jnp.where(sel, o[:, :, :HEAD_DIM], o[:, :, HEAD_DIM:])
    ko_ref[:, 0:wb - 1, :] = ck_ref[:, 1:wb, :]
    ko_ref[:, wb - 1:wb, :] = kn
    vo_ref[:, 0:wb - 1, :] = cv_ref[:, 1:wb, :]
    vo_ref[:, wb - 1:wb, :] = vn


def _swa_sample(q, ka, va, cache_k, cache_v, sinks_col):
    nb, wb = cache_k.shape[0], cache_k.shape[1]
    bb = 16
    b3 = lambda i: (i, 0, 0)
    nspec = pl.BlockSpec((bb, 1, KV_WIDTH), b3)
    cspec = pl.BlockSpec((bb, wb, KV_WIDTH), b3)
    qspec = pl.BlockSpec((bb, N_Q_HEADS, HEAD_DIM), b3)
    return pl.pallas_call(
        _swa_sample_kernel,
        grid=(nb // bb,),
        in_specs=[qspec, nspec, nspec, cspec, cspec, pl.BlockSpec((N_Q_HEADS, 1), lambda i: (0, 0))],
        out_specs=[qspec, cspec, cspec],
        out_shape=[jax.ShapeDtypeStruct((nb, N_Q_HEADS, HEAD_DIM), F32),
                   jax.ShapeDtypeStruct(cache_k.shape, F32), jax.ShapeDtypeStruct(cache_v.shape, F32)],
        compiler_params=pltpu.CompilerParams(dimension_semantics=("arbitrary",)),
        name="swa_sample",
    )(q.reshape(nb, N_Q_HEADS, HEAD_DIM), ka.reshape(nb, 1, KV_WIDTH), va.reshape(nb, 1, KV_WIDTH),
      cache_k, cache_v, sinks_col)


def _post_mix_kernel(is_prompt, n_main, n_tail, *refs):
    if n_tail:
        tail_ref, refs = refs[16], refs[:16] + refs[17:]
    i = pl.program_id(0)

    @pl.when(i < n_main)
    def _():
        _post_mix_body(is_prompt, *refs)

    if n_tail:
        h2_o = refs[17]

        @pl.when(i == n_main)
        def _():
            h2_o[0:n_tail, :] = tail_ref[...]
            h2_o[n_tail:, :] = jnp.zeros((h2_o.shape[0] - n_tail, h2_o.shape[1]), F32)


def _post_mix_body(is_prompt, yr_ref, g_ref, bonus_ref, ya_ref, x_ref, gt1_ref, sh2_ref, sc2_ref,
                   lnw_ref, lnb_ref, wout_ref, gpost_ref, gpre_ref, rw_ref, rb_ref, cnt_ref,
                   x1_o, h2_o, ti_o, tw_o, rank_o, cnt_o, run_ref):
    @pl.when(pl.program_id(0) == 0)
    def _():
        run_ref[...] = cnt_ref[...]

    gt1 = gt1_ref[...]
    sh2 = sh2_ref[...]
    sc2 = sc2_ref[...]
    if is_prompt:
        gt1, sh2, sc2 = gt1[0:1], sh2[0:1], sc2[0:1]
    y = yr_ref[...]
    mean = _seg_sum64(y) * (1.0 / HEAD_DIM)
    dlt = y - mean
    var = _seg_sum64(dlt * dlt) * (1.0 / HEAD_DIM)
    yn = dlt * lax.rsqrt(var + GN_EPS) * lnw_ref[...] + lnb_ref[...]
    yr = (yn + bonus_ref[...]) * g_ref[...]
    mix = _dot(jnp.concatenate([yr, ya_ref[...]], axis=1).astype(BF16), wout_ref[...])
    x1 = x_ref[...] + gt1 * _rmsnorm(mix, gpost_ref[...])
    x1_o[...] = x1
    h2 = _rmsnorm(x1, gpre_ref[...]) * (1.0 + sc2) + sh2
    h2_o[...] = h2
    logits = _dot3(h2, rw_ref[...]) + rb_ref[...]
    lane = lax.broadcasted_iota(jnp.int32, logits.shape, 1)
    vals, idxs = [], []
    for _ in range(TOP_K):
        m = jnp.max(logits, axis=1, keepdims=True)
        idx = jnp.min(jnp.where(logits == m, lane, N_EXPERTS), axis=1, keepdims=True)
        vals.append(m)
        idxs.append(idx)
        logits = jnp.where(lane == idx, -jnp.inf, logits)
    e = jnp.exp(jnp.concatenate(vals, axis=1) - vals[0])
    tw_o[...] = e / jnp.sum(e, axis=1, keepdims=True)
    ti_o[...] = jnp.concatenate(idxs, axis=1)

    tm = logits.shape[0]
    lane_e = lax.broadcasted_iota(jnp.int32, (tm, LANES), 1)
    tri = jnp.where(lax.broadcasted_iota(jnp.int32, (tm, tm), 0) > lax.broadcasted_iota(jnp.int32, (tm, tm), 1),
                    1.0, 0.0).astype(BF16)
    run = run_ref[...]
    ranks = []
    for idx in idxs:
        onehot = jnp.where(lane_e == idx, 1.0, 0.0)
        before = _dot(tri, onehot.astype(BF16)) + run
        ranks.append(jnp.sum(onehot * before, axis=1, keepdims=True))
        run = run + jnp.sum(onehot, axis=0, keepdims=True)
    run_ref[...] = run
    cnt_o[...] = run
    rank_o[...] = jnp.concatenate(ranks, axis=1).astype(jnp.int32)


def _post_mix(is_prompt, tm, yr, g, bonus, ya, x, mod, lnw, lnb, wout, gpost, gpre, rw, rb, cnt, h2_tail=None):
    m, d = x.shape
    n_main = m // tm
    n_tail = 0 if h2_tail is None else h2_tail.shape[0]
    assert n_tail <= tm
    row = lambda i: (jnp.minimum(i, n_main - 1), 0)
    if is_prompt:
        mod_rows = 8
        mod_map = lambda c: (lambda i: (0, c))
    else:
        mod_rows = tm
        mod_map = lambda c: (lambda i: (jnp.minimum(i, n_main - 1), c))
    wide = pl.BlockSpec((tm, RWKV_WIDTH), row)
    in_specs = [wide, wide, wide, wide, pl.BlockSpec((tm, d), row),
                pl.BlockSpec((mod_rows, d), mod_map(2)), pl.BlockSpec((mod_rows, d), mod_map(3)),
                pl.BlockSpec((mod_rows, d), mod_map(4)),
                _resident(lnw.shape), _resident(lnb.shape), _resident(wout.shape), _resident(gpost.shape),
                _resident(gpre.shape), _resident(rw.shape), _resident(rb.shape), _resident(cnt.shape)]
    args = [yr, g, bonus, ya, x, mod, mod, mod, lnw, lnb, wout, gpost, gpre, rw, rb, cnt]
    if n_tail:
        in_specs.append(_resident(h2_tail.shape))
        args.append(h2_tail)
    out_shape = [jax.ShapeDtypeStruct((m, d), F32), jax.ShapeDtypeStruct((m + n_tail, d), F32),
                 jax.ShapeDtypeStruct((m, TOP_K), jnp.int32), jax.ShapeDtypeStruct((m, TOP_K), F32),
                 jax.ShapeDtypeStruct((m, TOP_K), jnp.int32), jax.ShapeDtypeStruct((1, LANES), F32)]
    out_specs = [pl.BlockSpec((tm, d), row), pl.BlockSpec((tm, d), lambda i: (i, 0)),
                 pl.BlockSpec((tm, TOP_K), row), pl.BlockSpec((tm, TOP_K), row),
                 pl.BlockSpec((tm, TOP_K), row), pl.BlockSpec((1, LANES), lambda i: (0, 0))]
    return pl.pallas_call(
        functools.partial(_post_mix_kernel, is_prompt, n_main, n_tail),
        grid=(n_main + (1 if n_tail else 0),), in_specs=in_specs, out_specs=out_specs, out_shape=out_shape,
        scratch_shapes=[pltpu.VMEM((1, LANES), F32)],
        compiler_params=pltpu.CompilerParams(dimension_semantics=("arbitrary",), vmem_limit_bytes=VMEM_LIMIT),
        name="post_mix_prompt" if is_prompt else "post_mix_sample",
    )(*args)


def _moe_kernel(ex_ref, rb_ref, ns_ref, cb_ref, tok_ref, h2_hbm, w1_ref, b1g_ref, b1l_ref, w2_ref, b2_ref, ys_hbm,
                xbuf, xb, acc, w1p, w2b, sem_in, sem_out):
    s = pl.program_id(0)
    f = pl.program_id(1)
    n_super = pl.num_programs(0)
    ns = ns_ref[s]
    rb = rb_ref[s]
    rows = MOE_ROWS
    grp = 2 * LANES
    share = rows // MOE_NF

    def gather_share(sup, n_gran, col):
        base = cb_ref[sup]

        def body(j, c):
            r0 = j * rows + col * share
            g0 = j * (rows // SUBLANES) + col * (share // SUBLANES)
            for u in range(share):
                tok = tok_ref[base + r0 + u]
                pltpu.make_async_copy(h2_hbm.at[tok >> 3, pl.ds(tok & (SUBLANES - 1), 1)],
                                      xbuf.at[g0 + u // SUBLANES, pl.ds(u % SUBLANES, 1)], sem_in).start()
            return c
        lax.fori_loop(0, n_gran, body, 0)

    def gather_wait(j):
        n8 = rows // SUBLANES
        pltpu.make_async_copy(h2_hbm.at[pl.ds(0, n8)], xbuf.at[pl.ds(j * n8, n8)], sem_in).wait()

    def y_copy(r0, n):
        return pltpu.make_async_copy(acc.at[pl.ds(r0, n)], ys_hbm.at[pl.ds(rb * rows + r0, n)], sem_out)

    def for_tiles(fn):
        def body(j, c):
            fn(j)
            return c
        lax.fori_loop(0, ns, body, 0)

    @pl.when(ns > 0)
    def _():
        @pl.when(f == 0)
        def _():
            @pl.when(s == 0)
            def _():
                for col in range(MOE_NF):
                    gather_share(0, ns, col)
            b2 = jnp.broadcast_to(b2_ref[0], (rows, D_MODEL))

            def init(j):
                acc[pl.ds(pl.multiple_of(j * rows, rows), rows), :] = b2
            for_tiles(init)

            for_tiles(gather_wait)

            def cast(j):
                n8 = rows // SUBLANES
                x = xbuf[pl.ds(pl.multiple_of(j * n8, n8), n8)].reshape(rows, D_MODEL)
                xb[pl.ds(pl.multiple_of(j * rows, rows), rows), :] = x.astype(BF16)
            for_tiles(cast)

        nxt = jnp.minimum(s + 1, n_super - 1)
        gather_share(nxt, jnp.where(s + 1 < n_super, jnp.maximum(ns_ref[nxt], 0), 0), f)

        pr = lax.broadcasted_iota(jnp.int32, (grp, grp), 0)
        pc = lax.broadcasted_iota(jnp.int32, (grp, grp), 1)
        perm = jnp.where(pr == jnp.where(pc < LANES, 2 * pc, 2 * (pc - LANES) + 1), 1.0, 0.0).astype(BF16)
        n_grp = 2 * MOE_FT // grp
        for g in range(n_grp):
            gs = slice(grp * g, grp * (g + 1))
            w1p[:, gs] = _dot(w1_ref[0, :, gs].astype(BF16), perm).astype(BF16)
        w2b[...] = w2_ref[0].astype(BF16)
        b1g = b1g_ref[0]
        b1l = b1l_ref[0]

        last_col = f == pl.num_programs(1) - 1

        def swiglu(hh):
            glu = jnp.concatenate([hh[:, grp * g:grp * g + LANES] for g in range(n_grp)], axis=1) + b1g
            lin = jnp.concatenate([hh[:, grp * g + LANES:grp * (g + 1)] for g in range(n_grp)], axis=1) + b1l
            glu = jnp.minimum(glu, SWIGLU_LIMIT)
            lin = jnp.clip(lin, -SWIGLU_LIMIT, SWIGLU_LIMIT)
            return (glu * jax.nn.sigmoid(SWIGLU_ALPHA * glu) * (lin + 1.0)).astype(BF16)

        def tiles(starts, n):
            sls = [pl.ds(pl.multiple_of(r0, rows), n) for r0 in starts]
            hh = [_dot(xb[sl, :], w1p[...]) for sl in sls]
            act = [swiglu(h) for h in hh]
            out = [_dot(a, w2b[...]) for a in act]
            for sl, o in zip(sls, out):
                acc[sl, :] += o

            @pl.when(last_col)
            def _():
                for r0 in starts:
                    y_copy(r0, n).start()

        big = 2 * rows

        def tile_pair(q, c):
            tiles([2 * big * q, 2 * big * q + big], big)
            return c
        lax.fori_loop(0, ns // 4, tile_pair, 0)

        @pl.when(ns % 4 >= 2)
        def _():
            tiles([(ns // 4) * 2 * big], big)

        @pl.when(ns % 2 == 1)
        def _():
            tiles([(ns - 1) * rows], rows)

        @pl.when(last_col)
        def _():
            for_tiles(lambda j: y_copy(j * rows, rows).wait())

    @pl.when((ns < 0) & (f == 0))
    def _():
        acc[0:rows, :] = jnp.zeros((rows, D_MODEL), F32)

        def z_copy(j):
            return pltpu.make_async_copy(acc.at[pl.ds(0, rows)], ys_hbm.at[pl.ds((rb + j) * rows, rows)], sem_out)

        def body(j, c, op):
            op(z_copy(j))
            return c
        lax.fori_loop(0, -ns, functools.partial(body, op=lambda cp: cp.start()), 0)
        lax.fori_loop(0, -ns, functools.partial(body, op=lambda cp: cp.wait()), 0)


def _moe_experts(n_rows, ex, rb, ns, cb, tok_sorted, h2, w1, b1g, b1l, w2, b2):
    n_super = ex.shape[0]
    nf = MOE_NF
    last = nf - 1
    sub_rows = MOE_SUB * MOE_ROWS

    def fcol(s, f, ns_):
        return jnp.where(ns_[s] > 0, f, last)

    grid_spec = pltpu.PrefetchScalarGridSpec(
        num_scalar_prefetch=5,
        grid=(n_super, nf),
        in_specs=[pl.BlockSpec(memory_space=pl.ANY),
                  pl.BlockSpec((1, D_MODEL, 2 * MOE_FT), lambda s, f, e_, r_, n_, c_, t_: (e_[s], 0, fcol(s, f, n_))),
                  pl.BlockSpec((1, 1, MOE_FT), lambda s, f, e_, r_, n_, c_, t_: (e_[s], 0, fcol(s, f, n_))),
                  pl.BlockSpec((1, 1, MOE_FT), lambda s, f, e_, r_, n_, c_, t_: (e_[s], 0, fcol(s, f, n_))),
                  pl.BlockSpec((1, MOE_FT, D_MODEL), lambda s, f, e_, r_, n_, c_, t_: (e_[s], fcol(s, f, n_), 0)),
                  pl.BlockSpec((1, 1, D_MODEL), lambda s, f, e_, r_, n_, c_, t_: (e_[s], 0, 0))],
        out_specs=pl.BlockSpec(memory_space=pl.ANY),
        scratch_shapes=[pltpu.VMEM((sub_rows // SUBLANES, SUBLANES, D_MODEL), F32),
                        pltpu.VMEM((sub_rows, D_MODEL), BF16),
                        pltpu.VMEM((sub_rows, D_MODEL), F32),
                        pltpu.VMEM((D_MODEL, 2 * MOE_FT), BF16),
                        pltpu.VMEM((MOE_FT, D_MODEL), BF16),
                        pltpu.SemaphoreType.DMA, pltpu.SemaphoreType.DMA])
    return pl.pallas_call(
        _moe_kernel,
        grid_spec=grid_spec,
        out_shape=jax.ShapeDtypeStruct((n_rows, D_MODEL), F32),
        compiler_params=pltpu.CompilerParams(dimension_semantics=("arbitrary", "arbitrary"),
                                             vmem_limit_bytes=VMEM_LIMIT),
        name="moe_experts",
    )(ex, rb, ns, cb, tok_sorted, h2.reshape(-1, SUBLANES, D_MODEL), w1, b1g, b1l, w2, b2)


def _combine_kernel(is_prompt, idx_ref, nidx_ref, ys_hbm, x1_ref, tw_ref, gt2_ref, gpost_ref, o_ref, buf, sem):
    tk = COMBINE_TOK
    n = TOP_K * tk
    unroll = 16
    i = pl.program_id(0)
    slot = i % 2

    def start_rows(ref, buf_slot):
        def body(q, c):
            for u in range(unroll):
                p = ref[0, 0, q * unroll + u]
                pltpu.make_async_copy(ys_hbm.at[p >> 3, pl.ds(p & (SUBLANES - 1), 1)],
                                      buf.at[buf_slot, q * (unroll // SUBLANES) + u // SUBLANES,
                                             pl.ds(u % SUBLANES, 1)],
                                      sem.at[buf_slot]).start()
            return c
        lax.fori_loop(0, n // unroll, body, 0)

    @pl.when(i == 0)
    def _():
        start_rows(idx_ref, 0)

    @pl.when(i + 1 < pl.num_programs(0))
    def _():
        start_rows(nidx_ref, 1 - slot)

    pltpu.make_async_copy(ys_hbm.at[pl.ds(0, n // SUBLANES)], buf.at[slot], sem.at[slot]).wait()
    tw = tw_ref[...]
    t8 = tk // SUBLANES
    rows_of = lambda k: buf[slot, k * t8:(k + 1) * t8].reshape(tk, D_MODEL)
    f = tw[:, 0:1] * rows_of(0)
    for k in range(1, TOP_K):
        f = f + tw[:, k:k + 1] * rows_of(k)
    gt2 = gt2_ref[...]
    if is_prompt:
        gt2 = gt2[0:1]
    o_ref[...] = x1_ref[...] + gt2 * _rmsnorm(f, gpost_ref[...])


def _combine(is_prompt, pos, ys, x1, tw, mod, gpost):
    m, d = x1.shape
    tk = COMBINE_TOK
    nblk = m // tk
    idx = pos.reshape(nblk, tk, TOP_K).transpose(0, 2, 1).reshape(nblk, 1, TOP_K * tk)
    row = lambda i: (i, 0)
    mod_spec = (pl.BlockSpec((8, d), lambda i: (0, 5)) if is_prompt else pl.BlockSpec((tk, d), lambda i: (i, 5)))
    return pl.pallas_call(
        functools.partial(_combine_kernel, is_prompt),
        grid=(nblk,),
        in_specs=[pl.BlockSpec((1, 1, TOP_K * tk), lambda i: (i, 0, 0), memory_space=pltpu.SMEM),
                  pl.BlockSpec((1, 1, TOP_K * tk), lambda i: (jnp.minimum(i + 1, nblk - 1), 0, 0),
                               memory_space=pltpu.SMEM),
                  pl.BlockSpec(memory_space=pl.ANY),
                  pl.BlockSpec((tk, d), row), pl.BlockSpec((tk, TOP_K), row), mod_spec,
                  pl.BlockSpec((1, d), lambda i: (0, 0))],
        out_specs=pl.BlockSpec((tk, d), row),
        out_shape=jax.ShapeDtypeStruct((m, d), F32),
        scratch_shapes=[pltpu.VMEM((2, TOP_K * tk // SUBLANES, SUBLANES, d), F32), pltpu.SemaphoreType.DMA((2,))],
        compiler_params=pltpu.CompilerParams(dimension_semantics=("arbitrary",)),
        name="moe_combine_prompt" if is_prompt else "moe_combine_sample",
    )(idx, idx, ys.reshape(-1, SUBLANES, d), x1, tw, mod, gpost)


def _routing_tables(top_i, rank, counts):
    n_assign = top_i.size
    e_flat = top_i.reshape(-1)
    tiles = (counts + MOE_ROWS - 1) // MOE_ROWS
    tile_start = jnp.cumsum(tiles) - tiles
    group_start = jnp.cumsum(counts) - counts
    pos = tile_start[e_flat] * MOE_ROWS + rank.reshape(-1)
    n_rows = _padded_rows(n_assign)
    assert n_assign <= (1 << 16) and n_rows <= (1 << 16)
    key = pos.astype(jnp.uint32) * jnp.uint32(1 << 16) + jnp.arange(n_assign, dtype=jnp.uint32)
    tok_sorted = (jnp.sort(key) & jnp.uint32(0xFFFF)).astype(jnp.int32) // TOP_K
    tok_sorted = jnp.pad(tok_sorted, (0, n_rows - n_assign))
    supers = (tiles + MOE_SUB - 1) // MOE_SUB
    super_end = jnp.cumsum(supers)
    s_idx = jnp.arange(_max_supers(n_assign), dtype=jnp.int32)
    ex = jnp.minimum(jnp.sum((s_idx[:, None] >= super_end[None, :]).astype(jnp.int32), axis=1),
                     N_EXPERTS - 1).astype(jnp.int32)
    j = s_idx - (super_end - supers)[ex]
    live = s_idx < super_end[-1]
    tail = jnp.sum(tiles) + MOE_SUB * (s_idx - super_end[-1])
    n_clear = jnp.clip(n_rows // MOE_ROWS - tail, 0, MOE_SUB)
    ns = jnp.where(live, jnp.clip(tiles[ex] - MOE_SUB * j, 0, MOE_SUB), -n_clear).astype(jnp.int32)
    rb = jnp.where(live, tile_start[ex] + MOE_SUB * j, tail).astype(jnp.int32)
    cb = jnp.where(live, group_start[ex] + MOE_SUB * MOE_ROWS * j, 0).astype(jnp.int32)
    last_live = jnp.max(jnp.where(live, ex, 0))
    ex = jnp.where(live, ex, last_live).astype(jnp.int32)
    return pos.astype(jnp.int32), tok_sorted, ex, rb, ns, cb


def _padded_rows(n_assign):
    rows = n_assign + N_EXPERTS * (MOE_ROWS - 1)
    return -(-rows // MOE_ROWS) * MOE_ROWS


def _max_supers(n_assign):
    return N_EXPERTS + -(-_padded_rows(n_assign) // (MOE_ROWS * MOE_SUB))


def kernel(x_prompt, x_sample, cache_k, cache_v, state_wkv, state_shift, c_prompt, c_sample, w_ada, b_ada, g_pre_mix, g_post_mix, g_pre_ffn, g_post_ffn, mu_shift, w_in, rwkv_w0, rwkv_w2, rwkv_a0, rwkv_a2, rwkv_g2, rwkv_k_k, rwkv_k_a, rwkv_r_k, rwkv_ln_w, rwkv_ln_b, attn_sinks, w_out, router_w, router_b, moe_w1, moe_b1, moe_w2, moe_b2):
    assert w_ada.shape[0] == 1, "single-layer step"
    d = D_MODEL
    t = x_prompt.shape[1]
    nb = x_sample.shape[0]
    xp = x_prompt.reshape(t, d)
    xs = x_sample.reshape(nb, d)

    c_all = jnp.concatenate([jnp.broadcast_to(c_prompt, (8, d)), c_sample], axis=0)
    mod = _adaln_mod(c_all, w_ada[0], b_ada[0].reshape(1, 6 * d))
    mod_p, mod_s = mod[:8], mod[8:]

    offs = [0, 1024, 2048, 3072, 3136, 3200, 3360]
    w_in0 = w_in[0]
    pad_to = lambda w, n: jnp.pad(w, ((0, 0), (0, n - w.shape[1])))
    wts = [w_in0[:, offs[0]:offs[1]], w_in0[:, offs[1]:offs[2]], w_in0[:, offs[2]:offs[3]],
           pad_to(w_in0[:, offs[3]:offs[4]], 128), pad_to(w_in0[:, offs[4]:offs[5]], 128),
           pad_to(w_in0[:, offs[5]:offs[6]], 256), w_in0[:, offs[6]:]]
    wts = [w.astype(BF16) for w in wts]
    pad_rows = lambda w, n: jnp.pad(w, ((0, n - w.shape[0]), (0, 0))).astype(BF16)
    row = lambda p: p.reshape(1, -1)
    rw = [row(rwkv_w0[0]), pad_rows(rwkv_w2[0], 128), row(rwkv_a0[0]), pad_rows(rwkv_a2[0], 128),
          pad_rows(rwkv_g2[0], 256), row(rwkv_k_k[0]), row(rwkv_k_a[0]), row(rwkv_r_k[0])]
    gpm = row(g_pre_mix[0])
    mu8 = jnp.pad(mu_shift[0], ((0, 8 - N_SHIFTED), (0, 0)))

    pr = _in_proj(True, 256, xp, xp, mod_p, gpm, mu8, wts, rw)
    sr = _in_proj(False, nb, xs, state_shift[0], mod_s, gpm, mu8, wts, rw)
    r_p, lw_p, kf_p, v_p, na_p, b_p, g_p, bonus_p, q_p, ka_p, va_p, hlast_p = pr
    r_s, lw_s, kf_s, v_s, na_s, b_s, g_s, bonus_s, q_s, ka_s, va_s, h_s = sr

    y_p, st_p = _rwkv_prompt(r_p, lw_p, kf_p, v_p, na_p, b_p)
    y_s, wkv_s = _rwkv_sample(r_s, lw_s, kf_s, v_s, na_s, b_s, state_wkv[0])
    y_s = y_s.reshape(nb, RWKV_WIDTH)
    sinks = attn_sinks[0]
    att_p = _swa_prompt(q_p, ka_p, va_p, sinks.reshape(1, N_Q_HEADS))
    wb = cache_k.shape[2]
    att_s, ck_new, cv_new = _swa_sample(q_s, ka_s, va_s, cache_k[0].reshape(nb, wb, KV_WIDTH),
                                        cache_v[0].reshape(nb, wb, KV_WIDTH), sinks.reshape(N_Q_HEADS, 1))
    att_s = att_s.reshape(nb, ATTN_WIDTH)

    post = [row(rwkv_ln_w[0]), row(rwkv_ln_b[0]), w_out[0].astype(BF16), row(g_post_mix[0]),
            row(g_pre_ffn[0]), router_w[0], row(router_b[0])]
    x1_s, h2_s, ti_s, tw_s, rank_s, cnt_s = _post_mix(False, nb, y_s, g_s, bonus_s, att_s, xs, mod_s, *post,
                                                      jnp.zeros((1, LANES), F32))
    x1_p, h2_all, ti_p, tw_p, rank_p, cnt_all = _post_mix(True, 256, y_p, g_p, bonus_p, att_p, xp, mod_p, *post,
                                                          cnt_s, h2_tail=h2_s)

    pos, tok_sorted, ex, rb, ns, cb = _routing_tables(jnp.concatenate([ti_p, ti_s], axis=0),
                                                      jnp.concatenate([rank_p, rank_s], axis=0),
                                                      cnt_all[0, :N_EXPERTS].astype(jnp.int32))
    b1 = moe_b1[0].reshape(N_EXPERTS, 1, D_FF, 2)
    ys_rows = _moe_experts(tok_sorted.shape[0], ex, rb, ns, cb, tok_sorted, h2_all, moe_w1[0], b1[..., 0], b1[..., 1],
                           moe_w2[0], moe_b2[0].reshape(N_EXPERTS, 1, d))
    pos = pos.reshape(-1, TOP_K)
    gpf = row(g_post_ffn[0])
    out_p = _combine(True, pos[:t], ys_rows, x1_p, tw_p, mod_p, gpf)
    out_s = _combine(False, pos[t:], ys_rows, x1_s, tw_s, mod_s, gpf)

    n_keep = min(WINDOW, t)
    st_heads = jnp.stack([st_p[:, :HEAD_DIM, :HEAD_DIM], st_p[:, HEAD_DIM:, HEAD_DIM:]], axis=1)
    return (out_p.reshape(1, t, d),
            out_s.reshape(nb, 1, d),
            ka_p[t - n_keep:].reshape(1, 1, n_keep, N_KV_HEADS, HEAD_DIM),
            va_p[t - n_keep:].reshape(1, 1, n_keep, N_KV_HEADS, HEAD_DIM),
            st_heads.reshape(1, 1, N_RWKV_HEADS, HEAD_DIM, HEAD_DIM),
            hlast_p[7:8].reshape(1, 1, d),
            ck_new.reshape(1, nb, wb, N_KV_HEADS, HEAD_DIM),
            cv_new.reshape(1, nb, wb, N_KV_HEADS, HEAD_DIM),
            wkv_s.reshape(1, nb, N_RWKV_HEADS, HEAD_DIM, HEAD_DIM),
            h_s.reshape(1, nb, d))
```

```python
import functools
import math

import jax
import jax.numpy as jnp
from jax import lax
from jax.experimental import pallas as pl
from jax.experimental.pallas import tpu as pltpu

F32 = jnp.float32
BF16 = jnp.bfloat16

D_MODEL = 2048
HEAD_DIM = 64
RWKV_WIDTH = 1024
N_RWKV_HEADS = 16
ATTN_WIDTH = 1024
N_Q_HEADS = 16
N_KV_HEADS = 2
Q_PER_KV = 8
KV_WIDTH = 128
WINDOW = 128
N_SHIFTED = 6
N_EXPERTS = 32
TOP_K = 4
D_FF = 2048
SWIGLU_ALPHA = 1.702
SWIGLU_LIMIT = 7.0
NORM_EPS = 1e-6
GN_EPS = 64e-5
L2_EPS = 1e-12

LANES = 128
SUBLANES = 8
VMEM_LIMIT = 56 * 1024 * 1024

CHUNK = 64
RWKV_ROWS = 256
RWKV_PAIRS = 8
MOE_ROWS = 128
MOE_SUB = 12
MOE_FT = 256
MOE_NF = D_FF // MOE_FT
COMBINE_TOK = 128


def _dot(a, b):
    return jnp.dot(a, b, preferred_element_type=F32)


def _dot_nt(a, b):
    return lax.dot_general(a, b, (((1,), (1,)), ((), ())), preferred_element_type=F32)


def _split_bf16(x):
    hi = x.astype(BF16)
    lo = (x - hi.astype(F32)).astype(BF16)
    return hi, lo


def _dot3(a, b):
    ah, al = _split_bf16(a)
    bh, bl = _split_bf16(b)
    return _dot(ah, bh) + _dot(ah, bl) + _dot(al, bh)


def _seg_sum64(x):
    r = lax.broadcasted_iota(jnp.int32, (LANES, LANES), 0) // HEAD_DIM
    c = lax.broadcasted_iota(jnp.int32, (LANES, LANES), 1) // HEAD_DIM
    bd = jnp.where(r == c, 1.0, 0.0).astype(BF16)
    hi, lo = _split_bf16(x)
    outs = []
    for j in range(x.shape[1] // LANES):
        sl = slice(LANES * j, LANES * (j + 1))
        outs.append(_dot(hi[:, sl], bd) + _dot(lo[:, sl], bd))
    return jnp.concatenate(outs, axis=1)


def _rmsnorm(x, g):
    ms = jnp.mean(x * x, axis=-1, keepdims=True)
    return x * lax.rsqrt(ms + NORM_EPS) * g


def _resident(shape):
    nd = len(shape)
    return pl.BlockSpec(shape, lambda *_: (0,) * nd, pipeline_mode=pl.Buffered(1))


def _mod_kernel(c_ref, w_ref, b_ref, o_ref):
    c = c_ref[...]
    s = c * jax.nn.sigmoid(c)
    o_ref[...] = _dot3(s, w_ref[...]) + b_ref[...]


def _adaln_mod(c, w_ada, b_ada):
    rows, d = c.shape
    n = w_ada.shape[1]
    tn = 512
    return pl.pallas_call(
        _mod_kernel,
        grid=(n // tn,),
        in_specs=[pl.BlockSpec((rows, d), lambda j: (0, 0)),
                  pl.BlockSpec((d, tn), lambda j: (0, j)),
                  pl.BlockSpec((1, tn), lambda j: (0, j))],
        out_specs=pl.BlockSpec((rows, tn), lambda j: (0, j)),
        out_shape=jax.ShapeDtypeStruct((rows, n), F32),
        compiler_params=pltpu.CompilerParams(dimension_semantics=("arbitrary",), vmem_limit_bytes=VMEM_LIMIT),
        name="adaln_mod",
    )(c, w_ada, b_ada)


def _inproj_kernel(is_prompt, tm,
                   x_ref, prev_ref, sh_ref, sc_ref, gpm_ref, mu_ref,
                   wr_ref, wk_ref, wv_ref, wwl_ref, wal_ref, wgl_ref, wqkv_ref,
                   w0_ref, w2_ref, a0_ref, a2_ref, g2_ref, kk_ref, ka_ref, rk_ref,
                   r_o, lw_o, kf_o, v_o, na_o, b_o, g_o, bonus_o, q_o, kat_o, vat_o, h_o):
    i = pl.program_id(0)
    gpm = gpm_ref[...]
    sh = sh_ref[...]
    sc = sc_ref[...]
    if is_prompt:
        sh = sh[0:1]
        sc = sc[0:1]

    def modnorm(x):
        return _rmsnorm(x, gpm) * (1.0 + sc) + sh

    h = modnorm(x_ref[...])
    if is_prompt:
        hp = modnorm(prev_ref[...])[7:8, :]
        hp = jnp.where(i > 0, hp, 0.0)
        row = lax.broadcasted_iota(jnp.int32, h.shape, 0)
        hprev = jnp.where(row == 0, hp, pltpu.roll(h, 1, axis=0))
        h_o[...] = h[tm - 8:tm, :]
    else:
        hprev = prev_ref[...]
        h_o[...] = h
    dx = hprev - h
    mu = mu_ref[...]

    def branch(j, w_ref):
        xi = (h + dx * mu[j:j + 1, :]).astype(BF16)
        return _dot(xi, w_ref[...])

    r = branch(0, wr_ref)
    k = branch(1, wk_ref)
    v = branch(2, wv_ref)
    wl = branch(3, wwl_ref)
    al = branch(4, wal_ref)
    gl = branch(5, wgl_ref)
    qkv = _dot(h.astype(BF16), wqkv_ref[...])
    q_o[...] = qkv[:, :ATTN_WIDTH]
    kat_o[...] = qkv[:, ATTN_WIDTH:ATTN_WIDTH + KV_WIDTH]
    vat_o[...] = qkv[:, ATTN_WIDTH + KV_WIDTH:]

    z = w0_ref[...] + _dot(jnp.tanh(wl).astype(BF16), w2_ref[...])
    w_raw = -jnp.logaddexp(-z, 0.0) - 0.5
    lw_o[...] = -jnp.exp(w_raw)
    a = jax.nn.sigmoid(a0_ref[...] + _dot(al.astype(BF16), a2_ref[...]))
    g_o[...] = _dot(jax.nn.sigmoid(gl).astype(BF16), g2_ref[...])
    kk = k * kk_ref[...]
    kk = kk / jnp.maximum(jnp.sqrt(_seg_sum64(kk * kk)), L2_EPS)
    kf = k * (1.0 + (a - 1.0) * ka_ref[...])
    r_o[...] = r
    kf_o[...] = kf
    v_o[...] = v
    na_o[...] = -kk
    b_o[...] = kk * a
    bonus_o[...] = _seg_sum64(r * kf * rk_ref[...]) * v


def _in_proj(is_prompt, tm, x, prev, mod, gpm, mu8, wts, rw):
    m, d = x.shape
    grid = (m // tm,)
    row = lambda i: (i, 0)
    if is_prompt:
        prev_spec = pl.BlockSpec((8, d), lambda i: (jnp.maximum(i * (tm // 8) - 1, 0), 0))
        mod_rows = 8
        mod_map = lambda c: (lambda i: (0, c))
        h_shape, h_spec = (8, d), pl.BlockSpec((8, d), lambda i: (0, 0))
    else:
        prev_spec = pl.BlockSpec((tm, d), row)
        mod_rows = tm
        mod_map = lambda c: (lambda i: (i, c))
        h_shape, h_spec = (m, d), pl.BlockSpec((tm, d), row)
    in_specs = [pl.BlockSpec((tm, d), row), prev_spec,
                pl.BlockSpec((mod_rows, d), mod_map(0)), pl.BlockSpec((mod_rows, d), mod_map(1)),
                _resident((1, d)), _resident((8, d))]
    in_specs += [_resident(w.shape) for w in wts]
    in_specs += [_resident(p.shape) for p in rw]
    wide = jax.ShapeDtypeStruct((m, RWKV_WIDTH), F32)
    wide_spec = pl.BlockSpec((tm, RWKV_WIDTH), row)
    kv = jax.ShapeDtypeStruct((m, KV_WIDTH), F32)
    kv_spec = pl.BlockSpec((tm, KV_WIDTH), row)
    out_shape = [wide] * 9 + [kv, kv, jax.ShapeDtypeStruct(h_shape, F32)]
    out_specs = [wide_spec] * 9 + [kv_spec, kv_spec, h_spec]
    return pl.pallas_call(
        functools.partial(_inproj_kernel, is_prompt, tm),
        grid=grid, in_specs=in_specs, out_specs=out_specs, out_shape=out_shape,
        compiler_params=pltpu.CompilerParams(dimension_semantics=("arbitrary",), vmem_limit_bytes=VMEM_LIMIT),
        name="in_proj_prompt" if is_prompt else "in_proj_sample",
    )(x, prev, mod, mod, gpm, mu8, *wts, *rw)


def _rwkv_chunk_kernel(r_ref, lw_ref, k_ref, v_ref, a_ref, b_ref, y_ref, s_ref, st_ref):
    t = pl.program_id(1)
    C = CHUNK
    P = 2 * HEAD_DIM

    @pl.when(t == 0)
    def _():
        st_ref[...] = jnp.zeros_like(st_ref)

    ri = lax.broadcasted_iota(jnp.int32, (P, P), 0)
    ci = lax.broadcasted_iota(jnp.int32, (P, P), 1)
    bd = (ri // C) == (ci // C)
    tril_s = bd & ((ri % C) > (ci % C))
    tril_i = bd & ((ri % C) >= (ci % C))
    eye = jnp.where(ri == ci, 1.0, 0.0)
    lane0 = lax.broadcasted_iota(jnp.int32, (C, P), 1) < HEAD_DIM
    trow = lax.broadcasted_iota(jnp.int32, (C, P), 0)

    def stack(x):
        return jnp.concatenate([jnp.where(lane0, x, 0.0), jnp.where(lane0, 0.0, x)], axis=0)

    def dup(x):
        return jnp.concatenate([x, x], axis=0)

    def prep(sl, pp):
        ln = slice(P * pp, P * (pp + 1))
        lw = lw_ref[sl, ln]
        cw = lw
        for s in (1, 2, 4, 8, 16, 32):
            cw = cw + jnp.where(trow >= s, pltpu.roll(cw, s, axis=0), 0.0)
        cw_last = cw[C - 1:C, :]
        e_neg = jnp.exp(-cw)
        e_end = jnp.exp(cw_last - cw)
        k = k_ref[sl, ln]
        v = v_ref[sl, ln]
        b = b_ref[sl, ln]
        a2 = stack(a_ref[sl, ln] * jnp.exp(cw - lw))
        r2 = stack(r_ref[sl, ln] * jnp.exp(cw))
        return dict(
            v2=stack(v),
            lhs=jnp.concatenate([a2, r2], axis=0).astype(BF16),
            rhs=jnp.concatenate([dup(k * e_neg), dup(b * e_neg)], axis=0).astype(BF16),
            kbh=jnp.concatenate([dup(b * e_end), dup(k * e_end)], axis=0).astype(BF16),
            decay=jnp.exp(cw_last))

    def chunk(c, carry):
        sl = pl.ds(pl.multiple_of(c * C, C), C)
        pairs = range(RWKV_PAIRS)
        st = [st_ref[pp] for pp in pairs]
        d = [prep(sl, pp) for pp in pairs]
        gram = [_dot_nt(d[pp]['lhs'], d[pp]['rhs']) for pp in pairs]
        l2 = [jnp.where(tril_s, gram[pp][0:P, P:2 * P], 0.0) for pp in pairs]
        a_s = [_dot_nt(d[pp]['lhs'], st[pp].astype(BF16)) for pp in pairs]
        v2b = [d[pp]['v2'].astype(BF16) for pp in pairs]
        rhs_u = [a_s[pp][0:P] + _dot(jnp.where(tril_s, gram[pp][0:P, 0:P], 0.0).astype(BF16), v2b[pp])
                 for pp in pairs]
        inv = [eye + l2[pp] for pp in pairs]
        lp = l2
        for _ in range(5):
            lpb = [lp[pp].astype(BF16) for pp in pairs]
            lp = [_dot(lpb[pp], lpb[pp]) for pp in pairs]
            inv = [inv[pp] + _dot(lp[pp].astype(BF16), inv[pp].astype(BF16)) for pp in pairs]
        u2 = [_dot(inv[pp].astype(BF16), rhs_u[pp].astype(BF16)) for pp in pairs]
        p_cat = [jnp.concatenate([jnp.where(tril_i, gram[pp][P:2 * P, P:2 * P], 0.0),
                                  jnp.where(tril_i, gram[pp][P:2 * P, 0:P], 0.0)], axis=1).astype(BF16)
                 for pp in pairs]
        y2 = [a_s[pp][P:2 * P] + _dot(p_cat[pp], jnp.concatenate([u2[pp].astype(BF16), v2b[pp]], axis=0))
              for pp in pairs]
        uvt = [jnp.concatenate([u2[pp], d[pp]['v2']], axis=0).T.astype(BF16) for pp in pairs]
        st_new = [jnp.where(bd, st[pp] * d[pp]['decay'] + _dot(uvt[pp], d[pp]['kbh']), 0.0) for pp in pairs]
        for pp in pairs:
            y_ref[sl, P * pp:P * (pp + 1)] = y2[pp][0:C] + y2[pp][C:2 * C]
            st_ref[pp] = st_new[pp]
        return carry

    lax.fori_loop(0, RWKV_ROWS // C, chunk, 0)

    @pl.when(t == pl.num_programs(1) - 1)
    def _():
        s_ref[...] = st_ref[...]


def _rwkv_prompt(r, lw, kf, v, na, b):
    t = r.shape[0]
    n_pairs = RWKV_WIDTH // LANES
    width = RWKV_PAIRS * LANES
    spec = pl.BlockSpec((RWKV_ROWS, width), lambda p, i: (i, p))
    return pl.pallas_call(
        _rwkv_chunk_kernel,
        grid=(n_pairs // RWKV_PAIRS, t // RWKV_ROWS),
        in_specs=[spec] * 6,
        out_specs=[spec, pl.BlockSpec((RWKV_PAIRS, LANES, LANES), lambda p, i: (p, 0, 0))],
        out_shape=[jax.ShapeDtypeStruct((t, RWKV_WIDTH), F32),
                   jax.ShapeDtypeStruct((n_pairs, LANES, LANES), F32)],
        scratch_shapes=[pltpu.VMEM((RWKV_PAIRS, LANES, LANES), F32)],
        compiler_params=pltpu.CompilerParams(dimension_semantics=("arbitrary", "arbitrary")),
        name="rwkv_chunked",
    )(r, lw, kf, v, na, b)


def _rwkv_step_kernel(bb, r_ref, lw_ref, k_ref, v_ref, a_ref, b_ref, s_ref, y_ref, so_ref):
    n = HEAD_DIM
    eye = jnp.where(lax.broadcasted_iota(jnp.int32, (n, n), 0) == lax.broadcasted_iota(jnp.int32, (n, n), 1), 1.0, 0.0)

    def body(bi, carry):
        rb = r_ref[bi]
        dec = jnp.exp(lw_ref[bi])
        kb = k_ref[bi]
        vb = v_ref[bi]
        ab = a_ref[bi]
        bb_ = b_ref[bi]
        heads = range(N_RWKV_HEADS)
        row = lambda x, h: x[h:h + 1]
        s = [s_ref[bi, h] for h in heads]
        sa = [jnp.sum(s[h] * row(ab, h), axis=1, keepdims=True) for h in heads]
        vcol = [jnp.sum(eye * row(vb, h), axis=1, keepdims=True) for h in heads]
        s2 = [s[h] * row(dec, h) + sa[h] * row(bb_, h) + vcol[h] * row(kb, h) for h in heads]
        ycol = [jnp.sum(s2[h] * row(rb, h), axis=1, keepdims=True) for h in heads]
        ys = [jnp.sum(eye * ycol[h], axis=0, keepdims=True) for h in heads]
        for h in heads:
            so_ref[bi, h] = s2[h]
        y_ref[bi] = jnp.concatenate(ys, axis=0)
        return carry

    lax.fori_loop(0, bb, body, 0)


def _rwkv_sample(r, lw, kf, v, na, b, state):
    nb = r.shape[0]
    bb = 8
    vec = lambda x: x.reshape(nb, N_RWKV_HEADS, HEAD_DIM)
    vspec = pl.BlockSpec((bb, N_RWKV_HEADS, HEAD_DIM), lambda i: (i, 0, 0))
    sspec = pl.BlockSpec((bb, N_RWKV_HEADS, HEAD_DIM, HEAD_DIM), lambda i: (i, 0, 0, 0))
    return pl.pallas_call(
        functools.partial(_rwkv_step_kernel, bb),
        grid=(nb // bb,),
        in_specs=[vspec] * 6 + [sspec],
        out_specs=[vspec, sspec],
        out_shape=[jax.ShapeDtypeStruct((nb, N_RWKV_HEADS, HEAD_DIM), F32),
                   jax.ShapeDtypeStruct(state.shape, F32)],
        compiler_params=pltpu.CompilerParams(dimension_semantics=("arbitrary",)),
        name="rwkv_step",
    )(vec(r), vec(lw), vec(kf), vec(v), vec(na), vec(b), state)


def _alibi_slope(head):
    return 2.0 ** (-8.0 * (head + 1) / N_Q_HEADS)


def _swa_prompt_kernel(q_ref, kc_ref, kp_ref, vc_ref, vp_ref, sink_ref, o_ref):
    n = pl.program_id(0)
    w = WINDOW
    kcat = jnp.concatenate([kp_ref[...], kc_ref[...]], axis=0)
    vcat = jnp.concatenate([vp_ref[...], vc_ref[...]], axis=0)
    lane_k = lax.broadcasted_iota(jnp.int32, kcat.shape, 1) < HEAD_DIM
    kswap = pltpu.roll(kcat, HEAD_DIM, axis=1)
    vswap = pltpu.roll(vcat, HEAD_DIM, axis=1)
    kdup = [jnp.where(lane_k, kcat, kswap).astype(BF16), jnp.where(lane_k, kswap, kcat).astype(BF16)]
    vdup = [jnp.where(lane_k, vcat, vswap).astype(BF16), jnp.where(lane_k, vswap, vcat).astype(BF16)]
    qi = lax.broadcasted_iota(jnp.int32, (w, 2 * w), 0)
    kj = lax.broadcasted_iota(jnp.int32, (w, 2 * w), 1)
    dist = qi + w - kj
    valid = (dist >= 0) & (dist <= w) & ((n > 0) | (kj >= w))
    distf = dist.astype(F32)
    lane_q = lax.broadcasted_iota(jnp.int32, (w, LANES), 1) < HEAD_DIM
    sinks = sink_ref[...]
    group = 4
    for j0 in range(0, N_Q_HEADS // 2, group):
        js = range(j0, j0 + group)
        heads = [2 * j + half for j in js for half in range(2)]
        kvh = {h: h // Q_PER_KV for h in heads}
        q2 = {}
        for j in js:
            qp = q_ref[:, LANES * j:LANES * (j + 1)] * (1.0 / math.sqrt(HEAD_DIM))
            q2[j] = jnp.concatenate([jnp.where(lane_q, qp, 0.0), jnp.where(lane_q, 0.0, qp)], axis=0).astype(BF16)
        s2 = {j: _dot_nt(q2[j], kdup[kvh[2 * j]]) for j in js}
        s = {h: jnp.where(valid, s2[h // 2][w * (h % 2):w * (h % 2 + 1)] - _alibi_slope(h) * distf, -jnp.inf)
             for h in heads}
        sink = {h: sinks[0:1, h:h + 1] for h in heads}
        m = {h: jnp.maximum(jnp.max(s[h], axis=-1, keepdims=True), sink[h]) for h in heads}
        p = {h: jnp.exp(s[h] - m[h]) for h in heads}
        den = {h: jnp.sum(p[h], axis=-1, keepdims=True) + jnp.exp(sink[h] - m[h]) for h in heads}
        o = {h: _dot(p[h].astype(BF16), vdup[kvh[h]]) / den[h] for h in heads}
        for j in js:
            o_ref[:, LANES * j:LANES * (j + 1)] = jnp.where(lane_q, o[2 * j], o[2 * j + 1])


def _swa_prompt(q, ka, va, sinks):
    t = q.shape[0]
    w = WINDOW
    cur = lambda n: (n, 0)
    prv = lambda n: (jnp.maximum(n - 1, 0), 0)
    kvs = lambda f: pl.BlockSpec((w, KV_WIDTH), f)
    return pl.pallas_call(
        _swa_prompt_kernel,
        grid=(t // w,),
        in_specs=[pl.BlockSpec((w, ATTN_WIDTH), cur), kvs(cur), kvs(prv), kvs(cur), kvs(prv),
                  pl.BlockSpec((1, N_Q_HEADS), lambda n: (0, 0))],
        out_specs=pl.BlockSpec((w, ATTN_WIDTH), cur),
        out_shape=jax.ShapeDtypeStruct((t, ATTN_WIDTH), F32),
        compiler_params=pltpu.CompilerParams(dimension_semantics=("arbitrary",)),
        name="swa_prompt",
    )(q, ka, ka, va, va, sinks)


def _swa_sample_kernel(q_ref, kn_ref, vn_ref, ck_ref, cv_ref, sink_ref, o_ref, ko_ref, vo_ref):
    wb = ck_ref.shape[1]
    q = q_ref[...] * (1.0 / math.sqrt(HEAD_DIM))
    q2 = jnp.concatenate([q, q], axis=2)
    rowh = lax.broadcasted_iota(jnp.int32, q2.shape, 1) // Q_PER_KV
    laneh = lax.broadcasted_iota(jnp.int32, q2.shape, 2) // HEAD_DIM
    qb = jnp.where(rowh == laneh, q2, 0.0)
    kn = kn_ref[...]
    vn = vn_ref[...]
    ck = ck_ref[...]
    cv = cv_ref[...]
    s = jnp.einsum('bqd,bkd->bqk', qb.astype(BF16), ck.astype(BF16), preferred_element_type=F32)
    s_self = jnp.sum(qb * kn, axis=2, keepdims=True)
    head = lax.broadcasted_iota(jnp.int32, (1, N_Q_HEADS, 1), 1).astype(F32)
    slope = jnp.exp2(-8.0 * (head + 1.0) / N_Q_HEADS)
    dist = (wb - lax.broadcasted_iota(jnp.int32, (1, 1, wb), 2)).astype(F32)
    s = s - slope * dist
    sink = sink_ref[...][None]
    m = jnp.maximum(jnp.maximum(jnp.max(s, axis=2, keepdims=True), s_self), sink)
    p = jnp.exp(s - m)
    p_self = jnp.exp(s_self - m)
    den = jnp.sum(p, axis=2, keepdims=True) + p_self + jnp.exp(sink - m)
    o = jnp.einsum('bqk,bkd->bqd', p.astype(BF16), cv.astype(BF16), preferred_element_type=F32)
    o = (o + p_self * vn) / den
    sel = lax.broadcasted_iota(jnp.int32, (1, N_Q_HEADS, HEAD_DIM), 1) < Q_PER_KV
    o_ref[...] = jnp.where(sel, o[:, :, :HEAD_DIM], o[:, :, HEAD_DIM:])
    ko_ref[:, 0:wb - 1, :] = ck_ref[:, 1:wb, :]
    ko_ref[:, wb - 1:wb, :] = kn
    vo_ref[:, 0:wb - 1, :] = cv_ref[:, 1:wb, :]
    vo_ref[:, wb - 1:wb, :] = vn


def _swa_sample(q, ka, va, cache_k, cache_v, sinks_col):
    nb, wb = cache_k.shape[0], cache_k.shape[1]
    bb = 16
    b3 = lambda i: (i, 0, 0)
    nspec = pl.BlockSpec((bb, 1, KV_WIDTH), b3)
    cspec = pl.BlockSpec((bb, wb, KV_WIDTH), b3)
    qspec = pl.BlockSpec((bb, N_Q_HEADS, HEAD_DIM), b3)
    return pl.pallas_call(
        _swa_sample_kernel,
        grid=(nb // bb,),
        in_specs=[qspec, nspec, nspec, cspec, cspec, pl.BlockSpec((N_Q_HEADS, 1), lambda i: (0, 0))],
        out_specs=[qspec, cspec, cspec],
        out_shape=[jax.ShapeDtypeStruct((nb, N_Q_HEADS, HEAD_DIM), F32),
                   jax.ShapeDtypeStruct(cache_k.shape, F32), jax.ShapeDtypeStruct(cache_v.shape, F32)],
        compiler_params=pltpu.CompilerParams(dimension_semantics=("arbitrary",)),
        name="swa_sample",
    )(q.reshape(nb, N_Q_HEADS, HEAD_DIM), ka.reshape(nb, 1, KV_WIDTH), va.reshape(nb, 1, KV_WIDTH),
      cache_k, cache_v, sinks_col)


def _post_mix_kernel(is_prompt, n_main, n_tail, *refs):
    if n_tail:
        tail_ref, refs = refs[16], refs[:16] + refs[17:]
    i = pl.program_id(0)

    @pl.when(i < n_main)
    def _():
        _post_mix_body(is_prompt, *refs)

    if n_tail:
        h2_o = refs[17]

        @pl.when(i == n_main)
        def _():
            h2_o[0:n_tail, :] = tail_ref[...]
            h2_o[n_tail:, :] = jnp.zeros((h2_o.shape[0] - n_tail, h2_o.shape[1]), F32)


def _post_mix_body(is_prompt, yr_ref, g_ref, bonus_ref, ya_ref, x_ref, gt1_ref, sh2_ref, sc2_ref,
                   lnw_ref, lnb_ref, wout_ref, gpost_ref, gpre_ref, rw_ref, rb_ref, cnt_ref,
                   x1_o, h2_o, ti_o, tw_o, rank_o, cnt_o, run_ref):
    @pl.when(pl.program_id(0) == 0)
    def _():
        run_ref[...] = cnt_ref[...]

    gt1 = gt1_ref[...]
    sh2 = sh2_ref[...]
    sc2 = sc2_ref[...]
    if is_prompt:
        gt1, sh2, sc2 = gt1[0:1], sh2[0:1], sc2[0:1]
    y = yr_ref[...]
    mean = _seg_sum64(y) * (1.0 / HEAD_DIM)
    dlt = y - mean
    var = _seg_sum64(dlt * dlt) * (1.0 / HEAD_DIM)
    yn = dlt * lax.rsqrt(var + GN_EPS) * lnw_ref[...] + lnb_ref[...]
    yr = (yn + bonus_ref[...]) * g_ref[...]
    mix = _dot(jnp.concatenate([yr, ya_ref[...]], axis=1).astype(BF16), wout_ref[...])
    x1 = x_ref[...] + gt1 * _rmsnorm(mix, gpost_ref[...])
    x1_o[...] = x1
    h2 = _rmsnorm(x1, gpre_ref[...]) * (1.0 + sc2) + sh2
    h2_o[...] = h2
    logits = _dot3(h2, rw_ref[...]) + rb_ref[...]
    lane = lax.broadcasted_iota(jnp.int32, logits.shape, 1)
    vals, idxs = [], []
    for _ in range(TOP_K):
        m = jnp.max(logits, axis=1, keepdims=True)
        idx = jnp.min(jnp.where(logits == m, lane, N_EXPERTS), axis=1, keepdims=True)
        vals.append(m)
        idxs.append(idx)
        logits = jnp.where(lane == idx, -jnp.inf, logits)
    e = jnp.exp(jnp.concatenate(vals, axis=1) - vals[0])
    tw_o[...] = e / jnp.sum(e, axis=1, keepdims=True)
    ti_o[...] = jnp.concatenate(idxs, axis=1)

    tm = logits.shape[0]
    lane_e = lax.broadcasted_iota(jnp.int32, (tm, LANES), 1)
    tri = jnp.where(lax.broadcasted_iota(jnp.int32, (tm, tm), 0) > lax.broadcasted_iota(jnp.int32, (tm, tm), 1),
                    1.0, 0.0).astype(BF16)
    run = run_ref[...]
    ranks = []
    for idx in idxs:
        onehot = jnp.where(lane_e == idx, 1.0, 0.0)
        before = _dot(tri, onehot.astype(BF16)) + run
        ranks.append(jnp.sum(onehot * before, axis=1, keepdims=True))
        run = run + jnp.sum(onehot, axis=0, keepdims=True)
    run_ref[...] = run
    cnt_o[...] = run
    rank_o[...] = jnp.concatenate(ranks, axis=1).astype(jnp.int32)


def _post_mix(is_prompt, tm, yr, g, bonus, ya, x, mod, lnw, lnb, wout, gpost, gpre, rw, rb, cnt, h2_tail=None):
    m, d = x.shape
    n_main = m // tm
    n_tail = 0 if h2_tail is None else h2_tail.shape[0]
    assert n_tail <= tm
    row = lambda i: (jnp.minimum(i, n_main - 1), 0)
    if is_prompt:
        mod_rows = 8
        mod_map = lambda c: (lambda i: (0, c))
    else:
        mod_rows = tm
        mod_map = lambda c: (lambda i: (jnp.minimum(i, n_main - 1), c))
    wide = pl.BlockSpec((tm, RWKV_WIDTH), row)
    in_specs = [wide, wide, wide, wide, pl.BlockSpec((tm, d), row),
                pl.BlockSpec((mod_rows, d), mod_map(2)), pl.BlockSpec((mod_rows, d), mod_map(3)),
                pl.BlockSpec((mod_rows, d), mod_map(4)),
                _resident(lnw.shape), _resident(lnb.shape), _resident(wout.shape), _resident(gpost.shape),
                _resident(gpre.shape), _resident(rw.shape), _resident(rb.shape), _resident(cnt.shape)]
    args = [yr, g, bonus, ya, x, mod, mod, mod, lnw, lnb, wout, gpost, gpre, rw, rb, cnt]
    if n_tail:
        in_specs.append(_resident(h2_tail.shape))
        args.append(h2_tail)
    out_shape = [jax.ShapeDtypeStruct((m, d), F32), jax.ShapeDtypeStruct((m + n_tail, d), F32),
                 jax.ShapeDtypeStruct((m, TOP_K), jnp.int32), jax.ShapeDtypeStruct((m, TOP_K), F32),
                 jax.ShapeDtypeStruct((m, TOP_K), jnp.int32), jax.ShapeDtypeStruct((1, LANES), F32)]
    out_specs = [pl.BlockSpec((tm, d), row), pl.BlockSpec((tm, d), lambda i: (i, 0)),
                 pl.BlockSpec((tm, TOP_K), row), pl.BlockSpec((tm, TOP_K), row),
                 pl.BlockSpec((tm, TOP_K), row), pl.BlockSpec((1, LANES), lambda i: (0, 0))]
    return pl.pallas_call(
        functools.partial(_post_mix_kernel, is_prompt, n_main, n_tail),
        grid=(n_main + (1 if n_tail else 0),), in_specs=in_specs, out_specs=out_specs, out_shape=out_shape,
        scratch_shapes=[pltpu.VMEM((1, LANES), F32)],
        compiler_params=pltpu.CompilerParams(dimension_semantics=("arbitrary",), vmem_limit_bytes=VMEM_LIMIT),
        name="post_mix_prompt" if is_prompt else "post_mix_sample",
    )(*args)


def _moe_kernel(ex_ref, rb_ref, ns_ref, cb_ref, tok_ref, h2_hbm, w1_ref, b1g_ref, b1l_ref, w2_ref, b2_ref, ys_hbm,
                xbuf, xb, acc, w1p, w2b, sem_in, sem_out):
    s = pl.program_id(0)
    f = pl.program_id(1)
    n_super = pl.num_programs(0)
    ns = ns_ref[s]
    rb = rb_ref[s]
    rows = MOE_ROWS
    grp = 2 * LANES
    share = rows // MOE_NF

    def gather_share(sup, col):
        base = cb_ref[sup] + col * share
        g_col = col * (share // SUBLANES)
        for j in range(MOE_SUB):
            for u in range(share):
                tok = tok_ref[base + (j * rows + u)]
                pltpu.make_async_copy(h2_hbm.at[tok >> 3, pl.ds(tok & (SUBLANES - 1), 1)],
                                      xbuf.at[g_col + (j * (rows // SUBLANES) + u // SUBLANES),
                                              pl.ds(u % SUBLANES, 1)], sem_in).start()

    def gather_wait():
        pltpu.make_async_copy(h2_hbm.at[pl.ds(0, xbuf.shape[0])], xbuf, sem_in).wait()

    def y_copy(r0, n):
        return pltpu.make_async_copy(acc.at[pl.ds(r0, n)], ys_hbm.at[pl.ds(rb * rows + r0, n)], sem_out)

    def for_tiles(fn):
        def body(j, c):
            fn(j)
            return c
        lax.fori_loop(0, ns, body, 0)

    @pl.when(ns > 0)
    def _():
        @pl.when(f == 0)
        def _():
            @pl.when(s == 0)
            def _():
                def first(col, c):
                    gather_share(0, col)
                    return c
                lax.fori_loop(0, MOE_NF, first, 0)
            b2 = jnp.broadcast_to(b2_ref[0], (rows, D_MODEL))

            def init(j):
                acc[pl.ds(pl.multiple_of(j * rows, rows), rows), :] = b2
            for_tiles(init)

            gather_wait()

            def cast(j):
                n8 = rows // SUBLANES
                x = xbuf[pl.ds(pl.multiple_of(j * n8, n8), n8)].reshape(rows, D_MODEL)
                xb[pl.ds(pl.multiple_of(j * rows, rows), rows), :] = x.astype(BF16)
            for_tiles(cast)

        n_grp = 2 * MOE_FT // grp

        def prep_weights():
            pr = lax.broadcasted_iota(jnp.int32, (grp, grp), 0)
            pc = lax.broadcasted_iota(jnp.int32, (grp, grp), 1)
            perm = jnp.where(pr == jnp.where(pc < LANES, 2 * pc, 2 * (pc - LANES) + 1), 1.0, 0.0).astype(BF16)
            for g in range(n_grp):
                gs = slice(grp * g, grp * (g + 1))
                w1p[:, gs] = _dot(w1_ref[0, :, gs].astype(BF16), perm).astype(BF16)
            w2b[...] = w2_ref[0].astype(BF16)

        nxt = jnp.minimum(s + 1, n_super - 1)
        next_live = (s + 1 < n_super) & (ns_ref[nxt] > 0)

        @pl.when(next_live)
        def _():
            gather_share(nxt, f)
            prep_weights()

        @pl.when(jnp.logical_not(next_live))
        def _():
            prep_weights()

        b1g = b1g_ref[0]
        b1l = b1l_ref[0]

        last_col = f == pl.num_programs(1) - 1

        def swiglu(hh):
            glu = jnp.concatenate([hh[:, grp * g:grp * g + LANES] for g in range(n_grp)], axis=1) + b1g
            lin = jnp.concatenate([hh[:, grp * g + LANES:grp * (g + 1)] for g in range(n_grp)], axis=1) + b1l
            glu = jnp.minimum(glu, SWIGLU_LIMIT)
            lin = jnp.clip(lin, -SWIGLU_LIMIT, SWIGLU_LIMIT)
            return (glu * jax.nn.sigmoid(SWIGLU_ALPHA * glu) * (lin + 1.0)).astype(BF16)

        def tiles(starts, n):
            sls = [pl.ds(pl.multiple_of(r0, rows), n) for r0 in starts]
            hh = [_dot(xb[sl, :], w1p[...]) for sl in sls]
            act = [swiglu(h) for h in hh]
            out = [_dot(a, w2b[...]) for a in act]
            for sl, o in zip(sls, out):
                acc[sl, :] += o

            @pl.when(last_col)
            def _():
                for r0 in starts:
                    y_copy(r0, n).start()

        big = 2 * rows

        def tile_pair(q, c):
            tiles([2 * big * q, 2 * big * q + big], big)
            return c
        lax.fori_loop(0, ns // 4, tile_pair, 0)

        @pl.when(ns % 4 >= 2)
        def _():
            tiles([(ns // 4) * 2 * big], big)

        @pl.when(ns % 2 == 1)
        def _():
            tiles([(ns - 1) * rows], rows)

        @pl.when(last_col)
        def _():
            for_tiles(lambda j: y_copy(j * rows, rows).wait())

    @pl.when((ns < 0) & (f == 0))
    def _():
        acc[0:rows, :] = jnp.zeros((rows, D_MODEL), F32)

        def z_copy(j):
            return pltpu.make_async_copy(acc.at[pl.ds(0, rows)], ys_hbm.at[pl.ds((rb + j) * rows, rows)], sem_out)

        def body(j, c, op):
            op(z_copy(j))
            return c
        lax.fori_loop(0, -ns, functools.partial(body, op=lambda cp: cp.start()), 0)
        lax.fori_loop(0, -ns, functools.partial(body, op=lambda cp: cp.wait()), 0)


def _moe_experts(n_rows, ex, rb, ns, cb, tok_sorted, h2, w1, b1g, b1l, w2, b2):
    n_super = ex.shape[0]
    nf = MOE_NF
    last = nf - 1
    sub_rows = MOE_SUB * MOE_ROWS

    def fcol(s, f, ns_):
        return jnp.where(ns_[s] > 0, f, last)

    grid_spec = pltpu.PrefetchScalarGridSpec(
        num_scalar_prefetch=5,
        grid=(n_super, nf),
        in_specs=[pl.BlockSpec(memory_space=pl.ANY),
                  pl.BlockSpec((1, D_MODEL, 2 * MOE_FT), lambda s, f, e_, r_, n_, c_, t_: (e_[s], 0, fcol(s, f, n_))),
                  pl.BlockSpec((1, 1, MOE_FT), lambda s, f, e_, r_, n_, c_, t_: (e_[s], 0, fcol(s, f, n_))),
                  pl.BlockSpec((1, 1, MOE_FT), lambda s, f, e_, r_, n_, c_, t_: (e_[s], 0, fcol(s, f, n_))),
                  pl.BlockSpec((1, MOE_FT, D_MODEL), lambda s, f, e_, r_, n_, c_, t_: (e_[s], fcol(s, f, n_), 0)),
                  pl.BlockSpec((1, 1, D_MODEL), lambda s, f, e_, r_, n_, c_, t_: (e_[s], 0, 0))],
        out_specs=pl.BlockSpec(memory_space=pl.ANY),
        scratch_shapes=[pltpu.VMEM((sub_rows // SUBLANES, SUBLANES, D_MODEL), F32),
                        pltpu.VMEM((sub_rows, D_MODEL), BF16),
                        pltpu.VMEM((sub_rows, D_MODEL), F32),
                        pltpu.VMEM((D_MODEL, 2 * MOE_FT), BF16),
                        pltpu.VMEM((MOE_FT, D_MODEL), BF16),
                        pltpu.SemaphoreType.DMA, pltpu.SemaphoreType.DMA])
    return pl.pallas_call(
        _moe_kernel,
        grid_spec=grid_spec,
        out_shape=jax.ShapeDtypeStruct((n_rows, D_MODEL), F32),
        compiler_params=pltpu.CompilerParams(dimension_semantics=("arbitrary", "arbitrary"),
                                             vmem_limit_bytes=VMEM_LIMIT),
        name="moe_experts",
    )(ex, rb, ns, cb, tok_sorted, h2.reshape(-1, SUBLANES, D_MODEL), w1, b1g, b1l, w2, b2)


def _combine_kernel(is_prompt, idx_ref, nidx_ref, ys_hbm, x1_ref, tw_ref, gt2_ref, gpost_ref, o_ref, buf, sem):
    tk = COMBINE_TOK
    n = TOP_K * tk
    unroll = 16
    i = pl.program_id(0)
    slot = i % 2

    def start_rows(ref, buf_slot):
        def body(q, c):
            for u in range(unroll):
                p = ref[0, 0, q * unroll + u]
                pltpu.make_async_copy(ys_hbm.at[p >> 3, pl.ds(p & (SUBLANES - 1), 1)],
                                      buf.at[buf_slot, q * (unroll // SUBLANES) + u // SUBLANES,
                                             pl.ds(u % SUBLANES, 1)],
                                      sem.at[buf_slot]).start()
            return c
        lax.fori_loop(0, n // unroll, body, 0)

    @pl.when(i == 0)
    def _():
        start_rows(idx_ref, 0)

    @pl.when(i + 1 < pl.num_programs(0))
    def _():
        start_rows(nidx_ref, 1 - slot)

    pltpu.make_async_copy(ys_hbm.at[pl.ds(0, n // SUBLANES)], buf.at[slot], sem.at[slot]).wait()
    tw = tw_ref[...]
    t8 = tk // SUBLANES
    rows_of = lambda k: buf[slot, k * t8:(k + 1) * t8].reshape(tk, D_MODEL)
    f = tw[:, 0:1] * rows_of(0)
    for k in range(1, TOP_K):
        f = f + tw[:, k:k + 1] * rows_of(k)
    gt2 = gt2_ref[...]
    if is_prompt:
        gt2 = gt2[0:1]
    o_ref[...] = x1_ref[...] + gt2 * _rmsnorm(f, gpost_ref[...])


def _combine(is_prompt, pos, ys, x1, tw, mod, gpost):
    m, d = x1.shape
    tk = COMBINE_TOK
    nblk = m // tk
    idx = pos.reshape(nblk, tk, TOP_K).transpose(0, 2, 1).reshape(nblk, 1, TOP_K * tk)
    row = lambda i: (i, 0)
    mod_spec = (pl.BlockSpec((8, d), lambda i: (0, 5)) if is_prompt else pl.BlockSpec((tk, d), lambda i: (i, 5)))
    return pl.pallas_call(
        functools.partial(_combine_kernel, is_prompt),
        grid=(nblk,),
        in_specs=[pl.BlockSpec((1, 1, TOP_K * tk), lambda i: (i, 0, 0), memory_space=pltpu.SMEM),
                  pl.BlockSpec((1, 1, TOP_K * tk), lambda i: (jnp.minimum(i + 1, nblk - 1), 0, 0),
                               memory_space=pltpu.SMEM),
                  pl.BlockSpec(memory_space=pl.ANY),
                  pl.BlockSpec((tk, d), row), pl.BlockSpec((tk, TOP_K), row), mod_spec,
                  pl.BlockSpec((1, d), lambda i: (0, 0))],
        out_specs=pl.BlockSpec((tk, d), row),
        out_shape=jax.ShapeDtypeStruct((m, d), F32),
        scratch_shapes=[pltpu.VMEM((2, TOP_K * tk // SUBLANES, SUBLANES, d), F32), pltpu.SemaphoreType.DMA((2,))],
        compiler_params=pltpu.CompilerParams(dimension_semantics=("arbitrary",)),
        name="moe_combine_prompt" if is_prompt else "moe_combine_sample",
    )(idx, idx, ys.reshape(-1, SUBLANES, d), x1, tw, mod, gpost)


def _routing_tables(top_i, rank, counts):
    n_assign = top_i.size
    e_flat = top_i.reshape(-1)
    tiles = (counts + MOE_ROWS - 1) // MOE_ROWS
    tile_start = jnp.cumsum(tiles) - tiles
    group_start = jnp.cumsum(counts) - counts
    pos = tile_start[e_flat] * MOE_ROWS + rank.reshape(-1)
    n_rows = _padded_rows(n_assign)
    assert n_assign <= (1 << 16) and n_rows <= (1 << 16)
    assert n_rows >= n_assign + MOE_SUB * MOE_ROWS
    key = pos.astype(jnp.uint32) * jnp.uint32(1 << 16) + jnp.arange(n_assign, dtype=jnp.uint32)
    tok_sorted = (jnp.sort(key) & jnp.uint32(0xFFFF)).astype(jnp.int32) // TOP_K
    tok_sorted = jnp.pad(tok_sorted, (0, n_rows - n_assign))
    supers = (tiles + MOE_SUB - 1) // MOE_SUB
    super_end = jnp.cumsum(supers)
    s_idx = jnp.arange(_max_supers(n_assign), dtype=jnp.int32)
    ex = jnp.minimum(jnp.sum((s_idx[:, None] >= super_end[None, :]).astype(jnp.int32), axis=1),
                     N_EXPERTS - 1).astype(jnp.int32)
    j = s_idx - (super_end - supers)[ex]
    live = s_idx < super_end[-1]
    tail = jnp.sum(tiles) + MOE_SUB * (s_idx - super_end[-1])
    n_clear = jnp.clip(n_rows // MOE_ROWS - tail, 0, MOE_SUB)
    ns = jnp.where(live, jnp.clip(tiles[ex] - MOE_SUB * j, 0, MOE_SUB), -n_clear).astype(jnp.int32)
    rb = jnp.where(live, tile_start[ex] + MOE_SUB * j, tail).astype(jnp.int32)
    cb = jnp.where(live, group_start[ex] + MOE_SUB * MOE_ROWS * j, 0).astype(jnp.int32)
    last_live = jnp.max(jnp.where(live, ex, 0))
    ex = jnp.where(live, ex, last_live).astype(jnp.int32)
    return pos.astype(jnp.int32), tok_sorted, ex, rb, ns, cb


def _padded_rows(n_assign):
    rows = n_assign + N_EXPERTS * (MOE_ROWS - 1)
    return -(-rows // MOE_ROWS) * MOE_ROWS


def _max_supers(n_assign):
    return N_EXPERTS + -(-_padded_rows(n_assign) // (MOE_ROWS * MOE_SUB))


def kernel(x_prompt, x_sample, cache_k, cache_v, state_wkv, state_shift, c_prompt, c_sample, w_ada, b_ada, g_pre_mix, g_post_mix, g_pre_ffn, g_post_ffn, mu_shift, w_in, rwkv_w0, rwkv_w2, rwkv_a0, rwkv_a2, rwkv_g2, rwkv_k_k, rwkv_k_a, rwkv_r_k, rwkv_ln_w, rwkv_ln_b, attn_sinks, w_out, router_w, router_b, moe_w1, moe_b1, moe_w2, moe_b2):
    assert w_ada.shape[0] == 1, "single-layer step"
    d = D_MODEL
    t = x_prompt.shape[1]
    nb = x_sample.shape[0]
    xp = x_prompt.reshape(t, d)
    xs = x_sample.reshape(nb, d)

    c_all = jnp.concatenate([jnp.broadcast_to(c_prompt, (8, d)), c_sample], axis=0)
    mod = _adaln_mod(c_all, w_ada[0], b_ada[0].reshape(1, 6 * d))
    mod_p, mod_s = mod[:8], mod[8:]

    offs = [0, 1024, 2048, 3072, 3136, 3200, 3360]
    w_in0 = w_in[0]
    pad_to = lambda w, n: jnp.pad(w, ((0, 0), (0, n - w.shape[1])))
    wts = [w_in0[:, offs[0]:offs[1]], w_in0[:, offs[1]:offs[2]], w_in0[:, offs[2]:offs[3]],
           pad_to(w_in0[:, offs[3]:offs[4]], 128), pad_to(w_in0[:, offs[4]:offs[5]], 128),
           pad_to(w_in0[:, offs[5]:offs[6]], 256), w_in0[:, offs[6]:]]
    wts = [w.astype(BF16) for w in wts]
    pad_rows = lambda w, n: jnp.pad(w, ((0, n - w.shape[0]), (0, 0))).astype(BF16)
    row = lambda p: p.reshape(1, -1)
    rw = [row(rwkv_w0[0]), pad_rows(rwkv_w2[0], 128), row(rwkv_a0[0]), pad_rows(rwkv_a2[0], 128),
          pad_rows(rwkv_g2[0], 256), row(rwkv_k_k[0]), row(rwkv_k_a[0]), row(rwkv_r_k[0])]
    gpm = row(g_pre_mix[0])
    mu8 = jnp.pad(mu_shift[0], ((0, 8 - N_SHIFTED), (0, 0)))

    pr = _in_proj(True, 256, xp, xp, mod_p, gpm, mu8, wts, rw)
    sr = _in_proj(False, nb, xs, state_shift[0], mod_s, gpm, mu8, wts, rw)
    r_p, lw_p, kf_p, v_p, na_p, b_p, g_p, bonus_p, q_p, ka_p, va_p, hlast_p = pr
    r_s, lw_s, kf_s, v_s, na_s, b_s, g_s, bonus_s, q_s, ka_s, va_s, h_s = sr

    y_p, st_p = _rwkv_prompt(r_p, lw_p, kf_p, v_p, na_p, b_p)
    y_s, wkv_s = _rwkv_sample(r_s, lw_s, kf_s, v_s, na_s, b_s, state_wkv[0])
    y_s = y_s.reshape(nb, RWKV_WIDTH)
    sinks = attn_sinks[0]
    att_p = _swa_prompt(q_p, ka_p, va_p, sinks.reshape(1, N_Q_HEADS))
    wb = cache_k.shape[2]
    att_s, ck_new, cv_new = _swa_sample(q_s, ka_s, va_s, cache_k[0].reshape(nb, wb, KV_WIDTH),
                                        cache_v[0].reshape(nb, wb, KV_WIDTH), sinks.reshape(N_Q_HEADS, 1))
    att_s = att_s.reshape(nb, ATTN_WIDTH)

    post = [row(rwkv_ln_w[0]), row(rwkv_ln_b[0]), w_out[0].astype(BF16), row(g_post_mix[0]),
            row(g_pre_ffn[0]), router_w[0], row(router_b[0])]
    x1_s, h2_s, ti_s, tw_s, rank_s, cnt_s = _post_mix(False, nb, y_s, g_s, bonus_s, att_s, xs, mod_s, *post,
                                                      jnp.zeros((1, LANES), F32))
    x1_p, h2_all, ti_p, tw_p, rank_p, cnt_all = _post_mix(True, 256, y_p, g_p, bonus_p, att_p, xp, mod_p, *post,
                                                          cnt_s, h2_tail=h2_s)

    pos, tok_sorted, ex, rb, ns, cb = _routing_tables(jnp.concatenate([ti_p, ti_s], axis=0),
                                                      jnp.concatenate([rank_p, rank_s], axis=0),
                                                      cnt_all[0, :N_EXPERTS].astype(jnp.int32))
    b1 = moe_b1[0].reshape(N_EXPERTS, 1, D_FF, 2)
    ys_rows = _moe_experts(tok_sorted.shape[0], ex, rb, ns, cb, tok_sorted, h2_all, moe_w1[0], b1[..., 0], b1[..., 1],
                           moe_w2[0], moe_b2[0].reshape(N_EXPERTS, 1, d))
    pos = pos.reshape(-1, TOP_K)
    gpf = row(g_post_ffn[0])
    out_p = _combine(True, pos[:t], ys_rows, x1_p, tw_p, mod_p, gpf)
    out_s = _combine(False, pos[t:], ys_rows, x1_s, tw_s, mod_s, gpf)

    n_keep = min(WINDOW, t)
    st_heads = jnp.stack([st_p[:, :HEAD_DIM, :HEAD_DIM], st_p[:, HEAD_DIM:, HEAD_DIM:]], axis=1)
    return (out_p.reshape(1, t, d),
            out_s.reshape(nb, 1, d),
            ka_p[t - n_keep:].reshape(1, 1, n_keep, N_KV_HEADS, HEAD_DIM),
            va_p[t - n_keep:].reshape(1, 1, n_keep, N_KV_HEADS, HEAD_DIM),
            st_heads.reshape(1, 1, N_RWKV_HEADS, HEAD_DIM, HEAD_DIM),
            hlast_p[7:8].reshape(1, 1, d),
            ck_new.reshape(1, nb, wb, N_KV_HEADS, HEAD_DIM),
            cv_new.reshape(1, nb, wb, N_KV_HEADS, HEAD_DIM),
            wkv_s.reshape(1, nb, N_RWKV_HEADS, HEAD_DIM, HEAD_DIM),
            h_s.reshape(1, nb, d))
```

```python
import functools
import math

import jax
import jax.numpy as jnp
from jax import lax
from jax.experimental import pallas as pl
from jax.experimental.pallas import tpu as pltpu

F32 = jnp.float32
BF16 = jnp.bfloat16

D_MODEL = 2048
HEAD_DIM = 64
RWKV_WIDTH = 1024
N_RWKV_HEADS = 16
ATTN_WIDTH = 1024
N_Q_HEADS = 16
N_KV_HEADS = 2
Q_PER_KV = 8
KV_WIDTH = 128
WINDOW = 128
N_SHIFTED = 6
N_EXPERTS = 32
TOP_K = 4
D_FF = 2048
SWIGLU_ALPHA = 1.702
SWIGLU_LIMIT = 7.0
NORM_EPS = 1e-6
GN_EPS = 64e-5
L2_EPS = 1e-12

LANES = 128
SUBLANES = 8
VMEM_LIMIT = 56 * 1024 * 1024

CHUNK = 64
RWKV_ROWS = 256
RWKV_PAIRS = 8
MOE_ROWS = 128
MOE_SUB = 12
MOE_FT = 256
MOE_NF = D_FF // MOE_FT
COMBINE_TOK = 128


def _dot(a, b):
    return jnp.dot(a, b, preferred_element_type=F32)


def _dot_nt(a, b):
    return lax.dot_general(a, b, (((1,), (1,)), ((), ())), preferred_element_type=F32)


def _split_bf16(x):
    hi = x.astype(BF16)
    lo = (x - hi.astype(F32)).astype(BF16)
    return hi, lo


def _dot3(a, b):
    ah, al = _split_bf16(a)
    bh, bl = _split_bf16(b)
    return _dot(ah, bh) + _dot(ah, bl) + _dot(al, bh)


def _seg_sum64(x):
    r = lax.broadcasted_iota(jnp.int32, (LANES, LANES), 0) // HEAD_DIM
    c = lax.broadcasted_iota(jnp.int32, (LANES, LANES), 1) // HEAD_DIM
    bd = jnp.where(r == c, 1.0, 0.0).astype(BF16)
    hi, lo = _split_bf16(x)
    outs = []
    for j in range(x.shape[1] // LANES):
        sl = slice(LANES * j, LANES * (j + 1))
        outs.append(_dot(hi[:, sl], bd) + _dot(lo[:, sl], bd))
    return jnp.concatenate(outs, axis=1)


def _rmsnorm(x, g):
    ms = jnp.mean(x * x, axis=-1, keepdims=True)
    return x * lax.rsqrt(ms + NORM_EPS) * g


def _resident(shape):
    nd = len(shape)
    return pl.BlockSpec(shape, lambda *_: (0,) * nd, pipeline_mode=pl.Buffered(1))


def _mod_kernel(c_ref, w_ref, b_ref, o_ref):
    c = c_ref[...]
    s = c * jax.nn.sigmoid(c)
    o_ref[...] = _dot3(s, w_ref[...]) + b_ref[...]


def _adaln_mod(c, w_ada, b_ada):
    rows, d = c.shape
    n = w_ada.shape[1]
    tn = 512
    return pl.pallas_call(
        _mod_kernel,
        grid=(n // tn,),
        in_specs=[pl.BlockSpec((rows, d), lambda j: (0, 0)),
                  pl.BlockSpec((d, tn), lambda j: (0, j)),
                  pl.BlockSpec((1, tn), lambda j: (0, j))],
        out_specs=pl.BlockSpec((rows, tn), lambda j: (0, j)),
        out_shape=jax.ShapeDtypeStruct((rows, n), F32),
        compiler_params=pltpu.CompilerParams(dimension_semantics=("arbitrary",), vmem_limit_bytes=VMEM_LIMIT),
        name="adaln_mod",
    )(c, w_ada, b_ada)


def _inproj_kernel(is_prompt, tm,
                   x_ref, prev_ref, sh_ref, sc_ref, gpm_ref, mu_ref,
                   wr_ref, wk_ref, wv_ref, wwl_ref, wal_ref, wgl_ref, wqkv_ref,
                   w0_ref, w2_ref, a0_ref, a2_ref, g2_ref, kk_ref, ka_ref, rk_ref,
                   r_o, lw_o, kf_o, v_o, na_o, b_o, g_o, bonus_o, q_o, kat_o, vat_o, h_o):
    i = pl.program_id(0)
    gpm = gpm_ref[...]
    sh = sh_ref[...]
    sc = sc_ref[...]
    if is_prompt:
        sh = sh[0:1]
        sc = sc[0:1]

    def modnorm(x):
        return _rmsnorm(x, gpm) * (1.0 + sc) + sh

    h = modnorm(x_ref[...])
    if is_prompt:
        hp = modnorm(prev_ref[...])[7:8, :]
        hp = jnp.where(i > 0, hp, 0.0)
        row = lax.broadcasted_iota(jnp.int32, h.shape, 0)
        hprev = jnp.where(row == 0, hp, pltpu.roll(h, 1, axis=0))
        h_o[...] = h[tm - 8:tm, :]
    else:
        hprev = prev_ref[...]
        h_o[...] = h
    dx = hprev - h
    mu = mu_ref[...]

    def branch(j, w_ref):
        xi = (h + dx * mu[j:j + 1, :]).astype(BF16)
        return _dot(xi, w_ref[...])

    r = branch(0, wr_ref)
    k = branch(1, wk_ref)
    v = branch(2, wv_ref)
    wl = branch(3, wwl_ref)
    al = branch(4, wal_ref)
    gl = branch(5, wgl_ref)
    qkv = _dot(h.astype(BF16), wqkv_ref[...])
    q_o[...] = qkv[:, :ATTN_WIDTH]
    kat_o[...] = qkv[:, ATTN_WIDTH:ATTN_WIDTH + KV_WIDTH]
    vat_o[...] = qkv[:, ATTN_WIDTH + KV_WIDTH:]

    z = w0_ref[...] + _dot(jnp.tanh(wl).astype(BF16), w2_ref[...])
    w_raw = -jnp.logaddexp(-z, 0.0) - 0.5
    lw_o[...] = -jnp.exp(w_raw)
    a = jax.nn.sigmoid(a0_ref[...] + _dot(al.astype(BF16), a2_ref[...]))
    g_o[...] = _dot(jax.nn.sigmoid(gl).astype(BF16), g2_ref[...])
    kk = k * kk_ref[...]
    kk = kk / jnp.maximum(jnp.sqrt(_seg_sum64(kk * kk)), L2_EPS)
    kf = k * (1.0 + (a - 1.0) * ka_ref[...])
    r_o[...] = r
    kf_o[...] = kf
    v_o[...] = v
    na_o[...] = -kk
    b_o[...] = kk * a
    bonus_o[...] = _seg_sum64(r * kf * rk_ref[...]) * v


def _in_proj(is_prompt, tm, x, prev, mod, gpm, mu8, wts, rw):
    m, d = x.shape
    grid = (m // tm,)
    row = lambda i: (i, 0)
    if is_prompt:
        prev_spec = pl.BlockSpec((8, d), lambda i: (jnp.maximum(i * (tm // 8) - 1, 0), 0))
        mod_rows = 8
        mod_map = lambda c: (lambda i: (0, c))
        h_shape, h_spec = (8, d), pl.BlockSpec((8, d), lambda i: (0, 0))
    else:
        prev_spec = pl.BlockSpec((tm, d), row)
        mod_rows = tm
        mod_map = lambda c: (lambda i: (i, c))
        h_shape, h_spec = (m, d), pl.BlockSpec((tm, d), row)
    in_specs = [pl.BlockSpec((tm, d), row), prev_spec,
                pl.BlockSpec((mod_rows, d), mod_map(0)), pl.BlockSpec((mod_rows, d), mod_map(1)),
                _resident((1, d)), _resident((8, d))]
    in_specs += [_resident(w.shape) for w in wts]
    in_specs += [_resident(p.shape) for p in rw]
    wide = jax.ShapeDtypeStruct((m, RWKV_WIDTH), F32)
    wide_spec = pl.BlockSpec((tm, RWKV_WIDTH), row)
    kv = jax.ShapeDtypeStruct((m, KV_WIDTH), F32)
    kv_spec = pl.BlockSpec((tm, KV_WIDTH), row)
    out_shape = [wide] * 9 + [kv, kv, jax.ShapeDtypeStruct(h_shape, F32)]
    out_specs = [wide_spec] * 9 + [kv_spec, kv_spec, h_spec]
    return pl.pallas_call(
        functools.partial(_inproj_kernel, is_prompt, tm),
        grid=grid, in_specs=in_specs, out_specs=out_specs, out_shape=out_shape,
        compiler_params=pltpu.CompilerParams(dimension_semantics=("arbitrary",), vmem_limit_bytes=VMEM_LIMIT),
        name="in_proj_prompt" if is_prompt else "in_proj_sample",
    )(x, prev, mod, mod, gpm, mu8, *wts, *rw)


def _rwkv_chunk_kernel(r_ref, lw_ref, k_ref, v_ref, a_ref, b_ref, y_ref, s_ref, st_ref):
    t = pl.program_id(1)
    C = CHUNK
    P = 2 * HEAD_DIM

    @pl.when(t == 0)
    def _():
        st_ref[...] = jnp.zeros_like(st_ref)

    ri = lax.broadcasted_iota(jnp.int32, (P, P), 0)
    ci = lax.broadcasted_iota(jnp.int32, (P, P), 1)
    bd = (ri // C) == (ci // C)
    tril_s = bd & ((ri % C) > (ci % C))
    tril_i = bd & ((ri % C) >= (ci % C))
    eye = jnp.where(ri == ci, 1.0, 0.0)
    lane0 = lax.broadcasted_iota(jnp.int32, (C, P), 1) < HEAD_DIM
    trow = lax.broadcasted_iota(jnp.int32, (C, P), 0)

    def stack(x):
        return jnp.concatenate([jnp.where(lane0, x, 0.0), jnp.where(lane0, 0.0, x)], axis=0)

    def dup(x):
        return jnp.concatenate([x, x], axis=0)

    def prep(sl, pp):
        ln = slice(P * pp, P * (pp + 1))
        lw = lw_ref[sl, ln]
        cw = lw
        for s in (1, 2, 4, 8, 16, 32):
            cw = cw + jnp.where(trow >= s, pltpu.roll(cw, s, axis=0), 0.0)
        cw_last = cw[C - 1:C, :]
        e_neg = jnp.exp(-cw)
        e_end = jnp.exp(cw_last - cw)
        k = k_ref[sl, ln]
        v = v_ref[sl, ln]
        b = b_ref[sl, ln]
        a2 = stack(a_ref[sl, ln] * jnp.exp(cw - lw))
        r2 = stack(r_ref[sl, ln] * jnp.exp(cw))
        return dict(
            v2=stack(v),
            lhs=jnp.concatenate([a2, r2], axis=0).astype(BF16),
            rhs=jnp.concatenate([dup(k * e_neg), dup(b * e_neg)], axis=0).astype(BF16),
            kbh=jnp.concatenate([dup(b * e_end), dup(k * e_end)], axis=0).astype(BF16),
            decay=jnp.exp(cw_last))

    def chunk(c, carry):
        sl = pl.ds(pl.multiple_of(c * C, C), C)
        pairs = range(RWKV_PAIRS)
        st = [st_ref[pp] for pp in pairs]
        d = [prep(sl, pp) for pp in pairs]
        gram = [_dot_nt(d[pp]['lhs'], d[pp]['rhs']) for pp in pairs]
        l2 = [jnp.where(tril_s, gram[pp][0:P, P:2 * P], 0.0) for pp in pairs]
        a_s = [_dot_nt(d[pp]['lhs'], st[pp].astype(BF16)) for pp in pairs]
        v2b = [d[pp]['v2'].astype(BF16) for pp in pairs]
        rhs_u = [a_s[pp][0:P] + _dot(jnp.where(tril_s, gram[pp][0:P, 0:P], 0.0).astype(BF16), v2b[pp])
                 for pp in pairs]
        inv = [eye + l2[pp] for pp in pairs]
        lp = l2
        for _ in range(5):
            lpb = [lp[pp].astype(BF16) for pp in pairs]
            lp = [_dot(lpb[pp], lpb[pp]) for pp in pairs]
            inv = [inv[pp] + _dot(lp[pp].astype(BF16), inv[pp].astype(BF16)) for pp in pairs]
        u2 = [_dot(inv[pp].astype(BF16), rhs_u[pp].astype(BF16)) for pp in pairs]
        p_cat = [jnp.concatenate([jnp.where(tril_i, gram[pp][P:2 * P, P:2 * P], 0.0),
                                  jnp.where(tril_i, gram[pp][P:2 * P, 0:P], 0.0)], axis=1).astype(BF16)
                 for pp in pairs]
        y2 = [a_s[pp][P:2 * P] + _dot(p_cat[pp], jnp.concatenate([u2[pp].astype(BF16), v2b[pp]], axis=0))
              for pp in pairs]
        uvt = [jnp.concatenate([u2[pp], d[pp]['v2']], axis=0).T.astype(BF16) for pp in pairs]
        st_new = [jnp.where(bd, st[pp] * d[pp]['decay'] + _dot(uvt[pp], d[pp]['kbh']), 0.0) for pp in pairs]
        for pp in pairs:
            y_ref[sl, P * pp:P * (pp + 1)] = y2[pp][0:C] + y2[pp][C:2 * C]
            st_ref[pp] = st_new[pp]
        return carry

    lax.fori_loop(0, RWKV_ROWS // C, chunk, 0)

    @pl.when(t == pl.num_programs(1) - 1)
    def _():
        s_ref[...] = st_ref[...]


def _rwkv_prompt(r, lw, kf, v, na, b):
    t = r.shape[0]
    n_pairs = RWKV_WIDTH // LANES
    width = RWKV_PAIRS * LANES
    spec = pl.BlockSpec((RWKV_ROWS, width), lambda p, i: (i, p))
    return pl.pallas_call(
        _rwkv_chunk_kernel,
        grid=(n_pairs // RWKV_PAIRS, t // RWKV_ROWS),
        in_specs=[spec] * 6,
        out_specs=[spec, pl.BlockSpec((RWKV_PAIRS, LANES, LANES), lambda p, i: (p, 0, 0))],
        out_shape=[jax.ShapeDtypeStruct((t, RWKV_WIDTH), F32),
                   jax.ShapeDtypeStruct((n_pairs, LANES, LANES), F32)],
        scratch_shapes=[pltpu.VMEM((RWKV_PAIRS, LANES, LANES), F32)],
        compiler_params=pltpu.CompilerParams(dimension_semantics=("arbitrary", "arbitrary")),
        name="rwkv_chunked",
    )(r, lw, kf, v, na, b)


def _rwkv_step_kernel(r_ref, lw_ref, k_ref, v_ref, a_ref, b_ref, s_ref, y_ref, so_ref):
    n = HEAD_DIM
    tr = lambda ref: ref[...].T
    rt, kt, vt, at, bt = tr(r_ref), tr(k_ref), tr(v_ref), tr(a_ref), tr(b_ref)
    dt = jnp.exp(tr(lw_ref))
    y_rows = []
    for h in range(2):
        hs = slice(h * n, (h + 1) * n)
        r_h, k_h, a_h, b_h, d_h = rt[hs], kt[hs], at[hs], bt[hs], dt[hs]
        for v in range(n):
            s = s_ref[h, v]
            sa = jnp.sum(s * a_h, axis=0, keepdims=True)
            s2 = s * d_h + sa * b_h + vt[h * n + v:h * n + v + 1] * k_h
            so_ref[h, v] = s2
            y_rows.append(jnp.sum(s2 * r_h, axis=0, keepdims=True))
    y_ref[...] = jnp.concatenate(y_rows, axis=0).T


def _rwkv_sample(r, lw, kf, v, na, b, state):
    nb = r.shape[0]
    assert nb == LANES, "the step kernel puts the whole batch on the lane axis"
    state_t = jnp.transpose(state, (1, 2, 3, 0))
    vspec = pl.BlockSpec((nb, LANES), lambda p: (0, p))
    sspec = pl.BlockSpec((2, HEAD_DIM, HEAD_DIM, nb), lambda p: (p, 0, 0, 0))
    y, state_new = pl.pallas_call(
        _rwkv_step_kernel,
        grid=(N_RWKV_HEADS // 2,),
        in_specs=[vspec] * 6 + [sspec],
        out_specs=[vspec, sspec],
        out_shape=[jax.ShapeDtypeStruct((nb, RWKV_WIDTH), F32), jax.ShapeDtypeStruct(state_t.shape, F32)],
        compiler_params=pltpu.CompilerParams(dimension_semantics=("arbitrary",)),
        name="rwkv_step",
    )(r, lw, kf, v, na, b, state_t)
    return y, jnp.transpose(state_new, (3, 0, 1, 2))


def _alibi_slope(head):
    return 2.0 ** (-8.0 * (head + 1) / N_Q_HEADS)


def _swa_prompt_kernel(q_ref, kc_ref, kp_ref, vc_ref, vp_ref, sink_ref, o_ref):
    n = pl.program_id(0)
    w = WINDOW
    kcat = jnp.concatenate([kp_ref[...], kc_ref[...]], axis=0)
    vcat = jnp.concatenate([vp_ref[...], vc_ref[...]], axis=0)
    lane_k = lax.broadcasted_iota(jnp.int32, kcat.shape, 1) < HEAD_DIM
    kswap = pltpu.roll(kcat, HEAD_DIM, axis=1)
    vswap = pltpu.roll(vcat, HEAD_DIM, axis=1)
    kdup = [jnp.where(lane_k, kcat, kswap).astype(BF16), jnp.where(lane_k, kswap, kcat).astype(BF16)]
    vdup = [jnp.where(lane_k, vcat, vswap).astype(BF16), jnp.where(lane_k, vswap, vcat).astype(BF16)]
    qi = lax.broadcasted_iota(jnp.int32, (w, 2 * w), 0)
    kj = lax.broadcasted_iota(jnp.int32, (w, 2 * w), 1)
    dist = qi + w - kj
    valid = (dist >= 0) & (dist <= w) & ((n > 0) | (kj >= w))
    distf = dist.astype(F32)
    lane_q = lax.broadcasted_iota(jnp.int32, (w, LANES), 1) < HEAD_DIM
    sinks = sink_ref[...]
    group = 4
    for j0 in range(0, N_Q_HEADS // 2, group):
        js = range(j0, j0 + group)
        heads = [2 * j + half for j in js for half in range(2)]
        kvh = {h: h // Q_PER_KV for h in heads}
        q2 = {}
        for j in js:
            qp = q_ref[:, LANES * j:LANES * (j + 1)] * (1.0 / math.sqrt(HEAD_DIM))
            q2[j] = jnp.concatenate([jnp.where(lane_q, qp, 0.0), jnp.where(lane_q, 0.0, qp)], axis=0).astype(BF16)
        s2 = {j: _dot_nt(q2[j], kdup[kvh[2 * j]]) for j in js}
        s = {h: jnp.where(valid, s2[h // 2][w * (h % 2):w * (h % 2 + 1)] - _alibi_slope(h) * distf, -jnp.inf)
             for h in heads}
        sink = {h: sinks[0:1, h:h + 1] for h in heads}
        m = {h: jnp.maximum(jnp.max(s[h], axis=-1, keepdims=True), sink[h]) for h in heads}
        p = {h: jnp.exp(s[h] - m[h]) for h in heads}
        den = {h: jnp.sum(p[h], axis=-1, keepdims=True) + jnp.exp(sink[h] - m[h]) for h in heads}
        o = {h: _dot(p[h].astype(BF16), vdup[kvh[h]]) / den[h] for h in heads}
        for j in js:
            o_ref[:, LANES * j:LANES * (j + 1)] = jnp.where(lane_q, o[2 * j], o[2 * j + 1])


def _swa_prompt(q, ka, va, sinks):
    t = q.shape[0]
    w = WINDOW
    cur = lambda n: (n, 0)
    prv = lambda n: (jnp.maximum(n - 1, 0), 0)
    kvs = lambda f: pl.BlockSpec((w, KV_WIDTH), f)
    return pl.pallas_call(
        _swa_prompt_kernel,
        grid=(t // w,),
        in_specs=[pl.BlockSpec((w, ATTN_WIDTH), cur), kvs(cur), kvs(prv), kvs(cur), kvs(prv),
                  pl.BlockSpec((1, N_Q_HEADS), lambda n: (0, 0))],
        out_specs=pl.BlockSpec((w, ATTN_WIDTH), cur),
        out_shape=jax.ShapeDtypeStruct((t, ATTN_WIDTH), F32),
        compiler_params=pltpu.CompilerParams(dimension_semantics=("arbitrary",)),
        name="swa_prompt",
    )(q, ka, ka, va, va, sinks)


def _swa_sample_kernel(q_ref, kn_ref, vn_ref, ck_ref, cv_ref, sink_ref, o_ref, ko_ref, vo_ref):
    wb = ck_ref.shape[1]
    q = q_ref[...] * (1.0 / math.sqrt(HEAD_DIM))
    q2 = jnp.concatenate([q, q], axis=2)
    rowh = lax.broadcasted_iota(jnp.int32, q2.shape, 1) // Q_PER_KV
    laneh = lax.broadcasted_iota(jnp.int32, q2.shape, 2) // HEAD_DIM
    qb = jnp.where(rowh == laneh, q2, 0.0)
    kn = kn_ref[...]
    vn = vn_ref[...]
    ck = ck_ref[...]
    cv = cv_ref[...]
    s = jnp.einsum('bqd,bkd->bqk', qb.astype(BF16), ck.astype(BF16), preferred_element_type=F32)
    s_self = jnp.sum(qb * kn, axis=2, keepdims=True)
    head = lax.broadcasted_iota(jnp.int32, (1, N_Q_HEADS, 1), 1).astype(F32)
    slope = jnp.exp2(-8.0 * (head + 1.0) / N_Q_HEADS)
    dist = (wb - lax.broadcasted_iota(jnp.int32, (1, 1, wb), 2)).astype(F32)
    s = s - slope * dist
    sink = sink_ref[...][None]
    m = jnp.maximum(jnp.maximum(jnp.max(s, axis=2, keepdims=True), s_self), sink)
    p = jnp.exp(s - m)
    p_self = jnp.exp(s_self - m)
    den = jnp.sum(p, axis=2, keepdims=True) + p_self + jnp.exp(sink - m)
    o = jnp.einsum('bqk,bkd->bqd', p.astype(BF16), cv.astype(BF16), preferred_element_type=F32)
    o = (o + p_self * vn) / den
    sel = lax.broadcasted_iota(jnp.int32, (1, N_Q_HEADS, HEAD_DIM), 1) < Q_PER_KV
    o_ref[...] = jnp.where(sel, o[:, :, :HEAD_DIM], o[:, :, HEAD_DIM:])
    ko_ref[:, 0:wb - 1, :] = ck_ref[:, 1:wb, :]
    ko_ref[:, wb - 1:wb, :] = kn
    vo_ref[:, 0:wb - 1, :] = cv_ref[:, 1:wb, :]
    vo_ref[:, wb - 1:wb, :] = vn


def _swa_sample(q, ka, va, cache_k, cache_v, sinks_col):
    nb, wb = cache_k.shape[0], cache_k.shape[1]
    bb = 16
    b3 = lambda i: (i, 0, 0)
    nspec = pl.BlockSpec((bb, 1, KV_WIDTH), b3)
    cspec = pl.BlockSpec((bb, wb, KV_WIDTH), b3)
    qspec = pl.BlockSpec((bb, N_Q_HEADS, HEAD_DIM), b3)
    return pl.pallas_call(
        _swa_sample_kernel,
        grid=(nb // bb,),
        in_specs=[qspec, nspec, nspec, cspec, cspec, pl.BlockSpec((N_Q_HEADS, 1), lambda i: (0, 0))],
        out_specs=[qspec, cspec, cspec],
        out_shape=[jax.ShapeDtypeStruct((nb, N_Q_HEADS, HEAD_DIM), F32),
                   jax.ShapeDtypeStruct(cache_k.shape, F32), jax.ShapeDtypeStruct(cache_v.shape, F32)],
        compiler_params=pltpu.CompilerParams(dimension_semantics=("arbitrary",)),
        name="swa_sample",
    )(q.reshape(nb, N_Q_HEADS, HEAD_DIM), ka.reshape(nb, 1, KV_WIDTH), va.reshape(nb, 1, KV_WIDTH),
      cache_k, cache_v, sinks_col)


def _post_mix_kernel(is_prompt, n_main, n_tail, *refs):
    if n_tail:
        tail_ref, refs = refs[16], refs[:16] + refs[17:]
    i = pl.program_id(0)

    @pl.when(i < n_main)
    def _():
        _post_mix_body(is_prompt, *refs)

    if n_tail:
        h2_o = refs[17]

        @pl.when(i == n_main)
        def _():
            h2_o[0:n_tail, :] = tail_ref[...]
            h2_o[n_tail:, :] = jnp.zeros((h2_o.shape[0] - n_tail, h2_o.shape[1]), F32)


def _post_mix_body(is_prompt, yr_ref, g_ref, bonus_ref, ya_ref, x_ref, gt1_ref, sh2_ref, sc2_ref,
                   lnw_ref, lnb_ref, wout_ref, gpost_ref, gpre_ref, rw_ref, rb_ref, cnt_ref,
                   x1_o, h2_o, ti_o, tw_o, rank_o, cnt_o, run_ref):
    @pl.when(pl.program_id(0) == 0)
    def _():
        run_ref[...] = cnt_ref[...]

    gt1 = gt1_ref[...]
    sh2 = sh2_ref[...]
    sc2 = sc2_ref[...]
    if is_prompt:
        gt1, sh2, sc2 = gt1[0:1], sh2[0:1], sc2[0:1]
    y = yr_ref[...]
    mean = _seg_sum64(y) * (1.0 / HEAD_DIM)
    dlt = y - mean
    var = _seg_sum64(dlt * dlt) * (1.0 / HEAD_DIM)
    yn = dlt * lax.rsqrt(var + GN_EPS) * lnw_ref[...] + lnb_ref[...]
    yr = (yn + bonus_ref[...]) * g_ref[...]
    mix = _dot(jnp.concatenate([yr, ya_ref[...]], axis=1).astype(BF16), wout_ref[...])
    x1 = x_ref[...] + gt1 * _rmsnorm(mix, gpost_ref[...])
    x1_o[...] = x1
    h2 = _rmsnorm(x1, gpre_ref[...]) * (1.0 + sc2) + sh2
    h2_o[...] = h2
    logits = _dot3(h2, rw_ref[...]) + rb_ref[...]
    lane = lax.broadcasted_iota(jnp.int32, logits.shape, 1)
    vals, idxs = [], []
    for _ in range(TOP_K):
        m = jnp.max(logits, axis=1, keepdims=True)
        idx = jnp.min(jnp.where(logits == m, lane, N_EXPERTS), axis=1, keepdims=True)
        vals.append(m)
        idxs.append(idx)
        logits = jnp.where(lane == idx, -jnp.inf, logits)
    e = jnp.exp(jnp.concatenate(vals, axis=1) - vals[0])
    tw_o[...] = e / jnp.sum(e, axis=1, keepdims=True)
    ti_o[...] = jnp.concatenate(idxs, axis=1)

    tm = logits.shape[0]
    lane_e = lax.broadcasted_iota(jnp.int32, (tm, LANES), 1)
    tri = jnp.where(lax.broadcasted_iota(jnp.int32, (tm, tm), 0) > lax.broadcasted_iota(jnp.int32, (tm, tm), 1),
                    1.0, 0.0).astype(BF16)
    run = run_ref[...]
    ranks = []
    for idx in idxs:
        onehot = jnp.where(lane_e == idx, 1.0, 0.0)
        before = _dot(tri, onehot.astype(BF16)) + run
        ranks.append(jnp.sum(onehot * before, axis=1, keepdims=True))
        run = run + jnp.sum(onehot, axis=0, keepdims=True)
    run_ref[...] = run
    cnt_o[...] = run
    rank_o[...] = jnp.concatenate(ranks, axis=1).astype(jnp.int32)


def _post_mix(is_prompt, tm, yr, g, bonus, ya, x, mod, lnw, lnb, wout, gpost, gpre, rw, rb, cnt, h2_tail=None):
    m, d = x.shape
    n_main = m // tm
    n_tail = 0 if h2_tail is None else h2_tail.shape[0]
    assert n_tail <= tm
    row = lambda i: (jnp.minimum(i, n_main - 1), 0)
    if is_prompt:
        mod_rows = 8
        mod_map = lambda c: (lambda i: (0, c))
    else:
        mod_rows = tm
        mod_map = lambda c: (lambda i: (jnp.minimum(i, n_main - 1), c))
    wide = pl.BlockSpec((tm, RWKV_WIDTH), row)
    in_specs = [wide, wide, wide, wide, pl.BlockSpec((tm, d), row),
                pl.BlockSpec((mod_rows, d), mod_map(2)), pl.BlockSpec((mod_rows, d), mod_map(3)),
                pl.BlockSpec((mod_rows, d), mod_map(4)),
                _resident(lnw.shape), _resident(lnb.shape), _resident(wout.shape), _resident(gpost.shape),
                _resident(gpre.shape), _resident(rw.shape), _resident(rb.shape), _resident(cnt.shape)]
    args = [yr, g, bonus, ya, x, mod, mod, mod, lnw, lnb, wout, gpost, gpre, rw, rb, cnt]
    if n_tail:
        in_specs.append(_resident(h2_tail.shape))
        args.append(h2_tail)
    out_shape = [jax.ShapeDtypeStruct((m, d), F32), jax.ShapeDtypeStruct((m + n_tail, d), F32),
                 jax.ShapeDtypeStruct((m, TOP_K), jnp.int32), jax.ShapeDtypeStruct((m, TOP_K), F32),
                 jax.ShapeDtypeStruct((m, TOP_K), jnp.int32), jax.ShapeDtypeStruct((1, LANES), F32)]
    out_specs = [pl.BlockSpec((tm, d), row), pl.BlockSpec((tm, d), lambda i: (i, 0)),
                 pl.BlockSpec((tm, TOP_K), row), pl.BlockSpec((tm, TOP_K), row),
                 pl.BlockSpec((tm, TOP_K), row), pl.BlockSpec((1, LANES), lambda i: (0, 0))]
    return pl.pallas_call(
        functools.partial(_post_mix_kernel, is_prompt, n_main, n_tail),
        grid=(n_main + (1 if n_tail else 0),), in_specs=in_specs, out_specs=out_specs, out_shape=out_shape,
        scratch_shapes=[pltpu.VMEM((1, LANES), F32)],
        compiler_params=pltpu.CompilerParams(dimension_semantics=("arbitrary",), vmem_limit_bytes=VMEM_LIMIT),
        name="post_mix_prompt" if is_prompt else "post_mix_sample",
    )(*args)


def _moe_kernel(ex_ref, rb_ref, ns_ref, cb_ref, tok_ref, h2_hbm, w1_ref, b1g_ref, b1l_ref, w2_ref, b2_ref, ys_hbm,
                xbuf, xb, acc, w1p, w2b, sem_in, sem_out):
    s = pl.program_id(0)
    f = pl.program_id(1)
    n_super = pl.num_programs(0)
    ns = ns_ref[s]
    rb = rb_ref[s]
    rows = MOE_ROWS
    grp = 2 * LANES
    share = rows // MOE_NF

    def gather_share(sup, col):
        base = cb_ref[sup] + col * share
        g_col = col * (share // SUBLANES)
        for j in range(MOE_SUB):
            for u in range(share):
                tok = tok_ref[base + (j * rows + u)]
                pltpu.make_async_copy(h2_hbm.at[tok >> 3, pl.ds(tok & (SUBLANES - 1), 1)],
                                      xbuf.at[g_col + (j * (rows // SUBLANES) + u // SUBLANES),
                                              pl.ds(u % SUBLANES, 1)], sem_in).start()

    def gather_wait():
        pltpu.make_async_copy(h2_hbm.at[pl.ds(0, xbuf.shape[0])], xbuf, sem_in).wait()

    def y_copy(r0, n):
        return pltpu.make_async_copy(acc.at[pl.ds(r0, n)], ys_hbm.at[pl.ds(rb * rows + r0, n)], sem_out)

    def for_tiles(fn):
        def body(j, c):
            fn(j)
            return c
        lax.fori_loop(0, ns, body, 0)

    @pl.when(ns > 0)
    def _():
        @pl.when(f == 0)
        def _():
            @pl.when(s == 0)
            def _():
                def first(col, c):
                    gather_share(0, col)
                    return c
                lax.fori_loop(0, MOE_NF, first, 0)
            b2 = jnp.broadcast_to(b2_ref[0], (rows, D_MODEL))

            def init(j):
                acc[pl.ds(pl.multiple_of(j * rows, rows), rows), :] = b2
            for_tiles(init)

            gather_wait()

            def cast(j):
                n8 = rows // SUBLANES
                x = xbuf[pl.ds(pl.multiple_of(j * n8, n8), n8)].reshape(rows, D_MODEL)
                xb[pl.ds(pl.multiple_of(j * rows, rows), rows), :] = x.astype(BF16)
            for_tiles(cast)

        n_grp = 2 * MOE_FT // grp

        def prep_weights():
            pr = lax.broadcasted_iota(jnp.int32, (grp, grp), 0)
            pc = lax.broadcasted_iota(jnp.int32, (grp, grp), 1)
            perm = jnp.where(pr == jnp.where(pc < LANES, 2 * pc, 2 * (pc - LANES) + 1), 1.0, 0.0).astype(BF16)
            for g in range(n_grp):
                gs = slice(grp * g, grp * (g + 1))
                w1p[:, gs] = _dot(w1_ref[0, :, gs].astype(BF16), perm).astype(BF16)
            w2b[...] = w2_ref[0].astype(BF16)

        nxt = jnp.minimum(s + 1, n_super - 1)
        next_live = (s + 1 < n_super) & (ns_ref[nxt] > 0)

        @pl.when(next_live)
        def _():
            gather_share(nxt, f)
            prep_weights()

        @pl.when(jnp.logical_not(next_live))
        def _():
            prep_weights()

        b1g = b1g_ref[0]
        b1l = b1l_ref[0]

        last_col = f == pl.num_programs(1) - 1

        def swiglu(hh):
            glu = jnp.concatenate([hh[:, grp * g:grp * g + LANES] for g in range(n_grp)], axis=1) + b1g
            lin = jnp.concatenate([hh[:, grp * g + LANES:grp * (g + 1)] for g in range(n_grp)], axis=1) + b1l
            glu = jnp.minimum(glu, SWIGLU_LIMIT)
            lin = jnp.clip(lin, -SWIGLU_LIMIT, SWIGLU_LIMIT)
            return (glu * jax.nn.sigmoid(SWIGLU_ALPHA * glu) * (lin + 1.0)).astype(BF16)

        def tiles(starts, n):
            sls = [pl.ds(pl.multiple_of(r0, rows), n) for r0 in starts]
            hh = [_dot(xb[sl, :], w1p[...]) for sl in sls]
            act = [swiglu(h) for h in hh]
            out = [_dot(a, w2b[...]) for a in act]
            for sl, o in zip(sls, out):
                acc[sl, :] += o

            @pl.when(last_col)
            def _():
                for r0 in starts:
                    y_copy(r0, n).start()

        big = 2 * rows

        def tile_pair(q, c):
            tiles([2 * big * q, 2 * big * q + big], big)
            return c
        lax.fori_loop(0, ns // 4, tile_pair, 0)

        @pl.when(ns % 4 >= 2)
        def _():
            tiles([(ns // 4) * 2 * big], big)

        @pl.when(ns % 2 == 1)
        def _():
            tiles([(ns - 1) * rows], rows)

        @pl.when(last_col)
        def _():
            for_tiles(lambda j: y_copy(j * rows, rows).wait())

    @pl.when((ns < 0) & (f == 0))
    def _():
        acc[0:rows, :] = jnp.zeros((rows, D_MODEL), F32)

        def z_copy(j):
            return pltpu.make_async_copy(acc.at[pl.ds(0, rows)], ys_hbm.at[pl.ds((rb + j) * rows, rows)], sem_out)

        def body(j, c, op):
            op(z_copy(j))
            return c
        lax.fori_loop(0, -ns, functools.partial(body, op=lambda cp: cp.start()), 0)
        lax.fori_loop(0, -ns, functools.partial(body, op=lambda cp: cp.wait()), 0)


def _moe_experts(n_rows, ex, rb, ns, cb, tok_sorted, h2, w1, b1g, b1l, w2, b2):
    n_super = ex.shape[0]
    nf = MOE_NF
    last = nf - 1
    sub_rows = MOE_SUB * MOE_ROWS

    def fcol(s, f, ns_):
        return jnp.where(ns_[s] > 0, f, last)

    grid_spec = pltpu.PrefetchScalarGridSpec(
        num_scalar_prefetch=5,
        grid=(n_super, nf),
        in_specs=[pl.BlockSpec(memory_space=pl.ANY),
                  pl.BlockSpec((1, D_MODEL, 2 * MOE_FT), lambda s, f, e_, r_, n_, c_, t_: (e_[s], 0, fcol(s, f, n_))),
                  pl.BlockSpec((1, 1, MOE_FT), lambda s, f, e_, r_, n_, c_, t_: (e_[s], 0, fcol(s, f, n_))),
                  pl.BlockSpec((1, 1, MOE_FT), lambda s, f, e_, r_, n_, c_, t_: (e_[s], 0, fcol(s, f, n_))),
                  pl.BlockSpec((1, MOE_FT, D_MODEL), lambda s, f, e_, r_, n_, c_, t_: (e_[s], fcol(s, f, n_), 0)),
                  pl.BlockSpec((1, 1, D_MODEL), lambda s, f, e_, r_, n_, c_, t_: (e_[s], 0, 0))],
        out_specs=pl.BlockSpec(memory_space=pl.ANY),
        scratch_shapes=[pltpu.VMEM((sub_rows // SUBLANES, SUBLANES, D_MODEL), F32),
                        pltpu.VMEM((sub_rows, D_MODEL), BF16),
                        pltpu.VMEM((sub_rows, D_MODEL), F32),
                        pltpu.VMEM((D_MODEL, 2 * MOE_FT), BF16),
                        pltpu.VMEM((MOE_FT, D_MODEL), BF16),
                        pltpu.SemaphoreType.DMA, pltpu.SemaphoreType.DMA])
    return pl.pallas_call(
        _moe_kernel,
        grid_spec=grid_spec,
        out_shape=jax.ShapeDtypeStruct((n_rows, D_MODEL), F32),
        compiler_params=pltpu.CompilerParams(dimension_semantics=("arbitrary", "arbitrary"),
                                             vmem_limit_bytes=VMEM_LIMIT),
        name="moe_experts",
    )(ex, rb, ns, cb, tok_sorted, h2.reshape(-1, SUBLANES, D_MODEL), w1, b1g, b1l, w2, b2)


def _combine_kernel(is_prompt, idx_ref, nidx_ref, ys_hbm, x1_ref, tw_ref, gt2_ref, gpost_ref, o_ref, buf, sem):
    tk = COMBINE_TOK
    n = TOP_K * tk
    unroll = 16
    i = pl.program_id(0)
    slot = i % 2

    def start_rows(ref, buf_slot):
        def body(q, c):
            for u in range(unroll):
                p = ref[0, 0, q * unroll + u]
                pltpu.make_async_copy(ys_hbm.at[p >> 3, pl.ds(p & (SUBLANES - 1), 1)],
                                      buf.at[buf_slot, q * (unroll // SUBLANES) + u // SUBLANES,
                                             pl.ds(u % SUBLANES, 1)],
                                      sem.at[buf_slot]).start()
            return c
        lax.fori_loop(0, n // unroll, body, 0)

    @pl.when(i == 0)
    def _():
        start_rows(idx_ref, 0)

    @pl.when(i + 1 < pl.num_programs(0))
    def _():
        start_rows(nidx_ref, 1 - slot)

    pltpu.make_async_copy(ys_hbm.at[pl.ds(0, n // SUBLANES)], buf.at[slot], sem.at[slot]).wait()
    tw = tw_ref[...]
    t8 = tk // SUBLANES
    rows_of = lambda k: buf[slot, k * t8:(k + 1) * t8].reshape(tk, D_MODEL)
    f = tw[:, 0:1] * rows_of(0)
    for k in range(1, TOP_K):
        f = f + tw[:, k:k + 1] * rows_of(k)
    gt2 = gt2_ref[...]
    if is_prompt:
        gt2 = gt2[0:1]
    o_ref[...] = x1_ref[...] + gt2 * _rmsnorm(f, gpost_ref[...])


def _combine(is_prompt, pos, ys, x1, tw, mod, gpost):
    m, d = x1.shape
    tk = COMBINE_TOK
    nblk = m // tk
    idx = pos.reshape(nblk, tk, TOP_K).transpose(0, 2, 1).reshape(nblk, 1, TOP_K * tk)
    row = lambda i: (i, 0)
    mod_spec = (pl.BlockSpec((8, d), lambda i: (0, 5)) if is_prompt else pl.BlockSpec((tk, d), lambda i: (i, 5)))
    return pl.pallas_call(
        functools.partial(_combine_kernel, is_prompt),
        grid=(nblk,),
        in_specs=[pl.BlockSpec((1, 1, TOP_K * tk), lambda i: (i, 0, 0), memory_space=pltpu.SMEM),
                  pl.BlockSpec((1, 1, TOP_K * tk), lambda i: (jnp.minimum(i + 1, nblk - 1), 0, 0),
                               memory_space=pltpu.SMEM),
                  pl.BlockSpec(memory_space=pl.ANY),
                  pl.BlockSpec((tk, d), row), pl.BlockSpec((tk, TOP_K), row), mod_spec,
                  pl.BlockSpec((1, d), lambda i: (0, 0))],
        out_specs=pl.BlockSpec((tk, d), row),
        out_shape=jax.ShapeDtypeStruct((m, d), F32),
        scratch_shapes=[pltpu.VMEM((2, TOP_K * tk // SUBLANES, SUBLANES, d), F32), pltpu.SemaphoreType.DMA((2,))],
        compiler_params=pltpu.CompilerParams(dimension_semantics=("arbitrary",)),
        name="moe_combine_prompt" if is_prompt else "moe_combine_sample",
    )(idx, idx, ys.reshape(-1, SUBLANES, d), x1, tw, mod, gpost)


def _routing_tables(top_i, rank, counts):
    n_assign = top_i.size
    e_flat = top_i.reshape(-1)
    tiles = (counts + MOE_ROWS - 1) // MOE_ROWS
    tile_start = jnp.cumsum(tiles) - tiles
    group_start = jnp.cumsum(counts) - counts
    pos = tile_start[e_flat] * MOE_ROWS + rank.reshape(-1)
    n_rows = _padded_rows(n_assign)
    assert n_assign <= (1 << 16) and n_rows <= (1 << 16)
    assert n_rows >= n_assign + MOE_SUB * MOE_ROWS
    key = pos.astype(jnp.uint32) * jnp.uint32(1 << 16) + jnp.arange(n_assign, dtype=jnp.uint32)
    tok_sorted = (jnp.sort(key) & jnp.uint32(0xFFFF)).astype(jnp.int32) // TOP_K
    tok_sorted = jnp.pad(tok_sorted, (0, n_rows - n_assign))
    supers = (tiles + MOE_SUB - 1) // MOE_SUB
    super_end = jnp.cumsum(supers)
    s_idx = jnp.arange(_max_supers(n_assign), dtype=jnp.int32)
    ex = jnp.minimum(jnp.sum((s_idx[:, None] >= super_end[None, :]).astype(jnp.int32), axis=1),
                     N_EXPERTS - 1).astype(jnp.int32)
    j = s_idx - (super_end - supers)[ex]
    live = s_idx < super_end[-1]
    tail = jnp.sum(tiles) + MOE_SUB * (s_idx - super_end[-1])
    n_clear = jnp.clip(n_rows // MOE_ROWS - tail, 0, MOE_SUB)
    ns = jnp.where(live, jnp.clip(tiles[ex] - MOE_SUB * j, 0, MOE_SUB), -n_clear).astype(jnp.int32)
    rb = jnp.where(live, tile_start[ex] + MOE_SUB * j, tail).astype(jnp.int32)
    cb = jnp.where(live, group_start[ex] + MOE_SUB * MOE_ROWS * j, 0).astype(jnp.int32)
    last_live = jnp.max(jnp.where(live, ex, 0))
    ex = jnp.where(live, ex, last_live).astype(jnp.int32)
    return pos.astype(jnp.int32), tok_sorted, ex, rb, ns, cb


def _padded_rows(n_assign):
    rows = n_assign + N_EXPERTS * (MOE_ROWS - 1)
    return -(-rows // MOE_ROWS) * MOE_ROWS


def _max_supers(n_assign):
    return N_EXPERTS + -(-_padded_rows(n_assign) // (MOE_ROWS * MOE_SUB))


def kernel(x_prompt, x_sample, cache_k, cache_v, state_wkv, state_shift, c_prompt, c_sample, w_ada, b_ada, g_pre_mix, g_post_mix, g_pre_ffn, g_post_ffn, mu_shift, w_in, rwkv_w0, rwkv_w2, rwkv_a0, rwkv_a2, rwkv_g2, rwkv_k_k, rwkv_k_a, rwkv_r_k, rwkv_ln_w, rwkv_ln_b, attn_sinks, w_out, router_w, router_b, moe_w1, moe_b1, moe_w2, moe_b2):
    assert w_ada.shape[0] == 1, "single-layer step"
    d = D_MODEL
    t = x_prompt.shape[1]
    nb = x_sample.shape[0]
    xp = x_prompt.reshape(t, d)
    xs = x_sample.reshape(nb, d)

    c_all = jnp.concatenate([jnp.broadcast_to(c_prompt, (8, d)), c_sample], axis=0)
    mod = _adaln_mod(c_all, w_ada[0], b_ada[0].reshape(1, 6 * d))
    mod_p, mod_s = mod[:8], mod[8:]

    offs = [0, 1024, 2048, 3072, 3136, 3200, 3360]
    w_in0 = w_in[0]
    pad_to = lambda w, n: jnp.pad(w, ((0, 0), (0, n - w.shape[1])))
    wts = [w_in0[:, offs[0]:offs[1]], w_in0[:, offs[1]:offs[2]], w_in0[:, offs[2]:offs[3]],
           pad_to(w_in0[:, offs[3]:offs[4]], 128), pad_to(w_in0[:, offs[4]:offs[5]], 128),
           pad_to(w_in0[:, offs[5]:offs[6]], 256), w_in0[:, offs[6]:]]
    wts = [w.astype(BF16) for w in wts]
    pad_rows = lambda w, n: jnp.pad(w, ((0, n - w.shape[0]), (0, 0))).astype(BF16)
    row = lambda p: p.reshape(1, -1)
    rw = [row(rwkv_w0[0]), pad_rows(rwkv_w2[0], 128), row(rwkv_a0[0]), pad_rows(rwkv_a2[0], 128),
          pad_rows(rwkv_g2[0], 256), row(rwkv_k_k[0]), row(rwkv_k_a[0]), row(rwkv_r_k[0])]
    gpm = row(g_pre_mix[0])
    mu8 = jnp.pad(mu_shift[0], ((0, 8 - N_SHIFTED), (0, 0)))

    pr = _in_proj(True, 256, xp, xp, mod_p, gpm, mu8, wts, rw)
    sr = _in_proj(False, nb, xs, state_shift[0], mod_s, gpm, mu8, wts, rw)
    r_p, lw_p, kf_p, v_p, na_p, b_p, g_p, bonus_p, q_p, ka_p, va_p, hlast_p = pr
    r_s, lw_s, kf_s, v_s, na_s, b_s, g_s, bonus_s, q_s, ka_s, va_s, h_s = sr

    y_p, st_p = _rwkv_prompt(r_p, lw_p, kf_p, v_p, na_p, b_p)
    y_s, wkv_s = _rwkv_sample(r_s, lw_s, kf_s, v_s, na_s, b_s, state_wkv[0])
    y_s = y_s.reshape(nb, RWKV_WIDTH)
    sinks = attn_sinks[0]
    att_p = _swa_prompt(q_p, ka_p, va_p, sinks.reshape(1, N_Q_HEADS))
    wb = cache_k.shape[2]
    att_s, ck_new, cv_new = _swa_sample(q_s, ka_s, va_s, cache_k[0].reshape(nb, wb, KV_WIDTH),
                                        cache_v[0].reshape(nb, wb, KV_WIDTH), sinks.reshape(N_Q_HEADS, 1))
    att_s = att_s.reshape(nb, ATTN_WIDTH)

    post = [row(rwkv_ln_w[0]), row(rwkv_ln_b[0]), w_out[0].astype(BF16), row(g_post_mix[0]),
            row(g_pre_ffn[0]), router_w[0], row(router_b[0])]
    x1_s, h2_s, ti_s, tw_s, rank_s, cnt_s = _post_mix(False, nb, y_s, g_s, bonus_s, att_s, xs, mod_s, *post,
                                                      jnp.zeros((1, LANES), F32))
    x1_p, h2_all, ti_p, tw_p, rank_p, cnt_all = _post_mix(True, 256, y_p, g_p, bonus_p, att_p, xp, mod_p, *post,
                                                          cnt_s, h2_tail=h2_s)

    pos, tok_sorted, ex, rb, ns, cb = _routing_tables(jnp.concatenate([ti_p, ti_s], axis=0),
                                                      jnp.concatenate([rank_p, rank_s], axis=0),
                                                      cnt_all[0, :N_EXPERTS].astype(jnp.int32))
    b1 = moe_b1[0].reshape(N_EXPERTS, 1, D_FF, 2)
    ys_rows = _moe_experts(tok_sorted.shape[0], ex, rb, ns, cb, tok_sorted, h2_all, moe_w1[0], b1[..., 0], b1[..., 1],
                           moe_w2[0], moe_b2[0].reshape(N_EXPERTS, 1, d))
    pos = pos.reshape(-1, TOP_K)
    gpf = row(g_post_ffn[0])
    out_p = _combine(True, pos[:t], ys_rows, x1_p, tw_p, mod_p, gpf)
    out_s = _combine(False, pos[t:], ys_rows, x1_s, tw_s, mod_s, gpf)

    n_keep = min(WINDOW, t)
    st_heads = jnp.stack([st_p[:, :HEAD_DIM, :HEAD_DIM], st_p[:, HEAD_DIM:, HEAD_DIM:]], axis=1)
    return (out_p.reshape(1, t, d),
            out_s.reshape(nb, 1, d),
            ka_p[t - n_keep:].reshape(1, 1, n_keep, N_KV_HEADS, HEAD_DIM),
            va_p[t - n_keep:].reshape(1, 1, n_keep, N_KV_HEADS, HEAD_DIM),
            st_heads.reshape(1, 1, N_RWKV_HEADS, HEAD_DIM, HEAD_DIM),
            hlast_p[7:8].reshape(1, 1, d),
            ck_new.reshape(1, nb, wb, N_KV_HEADS, HEAD_DIM),
            cv_new.reshape(1, nb, wb, N_KV_HEADS, HEAD_DIM),
            wkv_s.reshape(1, nb, N_RWKV_HEADS, HEAD_DIM, HEAD_DIM),
            h_s.reshape(1, nb, d))
```

```python
import functools
import math

import jax
import jax.numpy as jnp
from jax import lax
from jax.experimental import pallas as pl
from jax.experimental.pallas import tpu as pltpu

F32 = jnp.float32
BF16 = jnp.bfloat16

D_MODEL = 2048
HEAD_DIM = 64
RWKV_WIDTH = 1024
N_RWKV_HEADS = 16
ATTN_WIDTH = 1024
N_Q_HEADS = 16
N_KV_HEADS = 2
Q_PER_KV = 8
KV_WIDTH = 128
WINDOW = 128
N_SHIFTED = 6
N_EXPERTS = 32
TOP_K = 4
D_FF = 2048
SWIGLU_ALPHA = 1.702
SWIGLU_LIMIT = 7.0
NORM_EPS = 1e-6
GN_EPS = 64e-5
L2_EPS = 1e-12

LANES = 128
SUBLANES = 8
VMEM_LIMIT = 56 * 1024 * 1024

CHUNK = 64
RWKV_ROWS = 512
RWKV_PAIRS = 8
MOE_ROWS = 128
MOE_SUB = 12
MOE_FT = 256
MOE_NF = D_FF // MOE_FT
COMBINE_TOK = 128


def _dot(a, b):
    return jnp.dot(a, b, preferred_element_type=F32)


def _dot_nt(a, b):
    return lax.dot_general(a, b, (((1,), (1,)), ((), ())), preferred_element_type=F32)


def _split_bf16(x):
    hi = x.astype(BF16)
    lo = (x - hi.astype(F32)).astype(BF16)
    return hi, lo


def _dot3(a, b):
    ah, al = _split_bf16(a)
    bh, bl = _split_bf16(b)
    return _dot(ah, bh) + _dot(ah, bl) + _dot(al, bh)


def _seg_sum64(x):
    wide = LANES
    r = lax.broadcasted_iota(jnp.int32, (wide, wide), 0) // HEAD_DIM
    c = lax.broadcasted_iota(jnp.int32, (wide, wide), 1) // HEAD_DIM
    bd = jnp.where(r == c, 1.0, 0.0).astype(BF16)
    hi, lo = _split_bf16(x)
    outs = []
    for j in range(x.shape[1] // wide):
        sl = slice(wide * j, wide * (j + 1))
        outs.append(_dot(hi[:, sl], bd) + _dot(lo[:, sl], bd))
    return jnp.concatenate(outs, axis=1)


def _rmsnorm(x, g):
    ms = jnp.mean(x * x, axis=-1, keepdims=True)
    return x * lax.rsqrt(ms + NORM_EPS) * g


def _resident(shape):
    nd = len(shape)
    return pl.BlockSpec(shape, lambda *_: (0,) * nd, pipeline_mode=pl.Buffered(1))


def _mod_kernel(c_ref, w_ref, b_ref, o_ref):
    c = c_ref[...]
    s = c * jax.nn.sigmoid(c)
    o_ref[...] = _dot3(s, w_ref[...]) + b_ref[...]


def _adaln_mod(c, w_ada, b_ada):
    rows, d = c.shape
    n = w_ada.shape[1]
    tn = 512
    return pl.pallas_call(
        _mod_kernel,
        grid=(n // tn,),
        in_specs=[pl.BlockSpec((rows, d), lambda j: (0, 0)),
                  pl.BlockSpec((d, tn), lambda j: (0, j)),
                  pl.BlockSpec((1, tn), lambda j: (0, j))],
        out_specs=pl.BlockSpec((rows, tn), lambda j: (0, j)),
        out_shape=jax.ShapeDtypeStruct((rows, n), F32),
        compiler_params=pltpu.CompilerParams(dimension_semantics=("arbitrary",), vmem_limit_bytes=VMEM_LIMIT),
        name="adaln_mod",
    )(c, w_ada, b_ada)


def _inproj_kernel(is_prompt, tm,
                   x_ref, prev_ref, sh_ref, sc_ref, gpm_ref, mu_ref,
                   wr_ref, wk_ref, wv_ref, wwl_ref, wal_ref, wgl_ref, wqkv_ref,
                   w0_ref, w2_ref, a0_ref, a2_ref, g2_ref, kk_ref, ka_ref, rk_ref,
                   r_o, lw_o, kf_o, v_o, na_o, b_o, g_o, bonus_o, q_o, kat_o, vat_o, h_o):
    i = pl.program_id(0)
    gpm = gpm_ref[...]
    sh = sh_ref[...]
    sc = sc_ref[...]
    if is_prompt:
        sh = sh[0:1]
        sc = sc[0:1]

    def modnorm(x):
        return _rmsnorm(x, gpm) * (1.0 + sc) + sh

    h = modnorm(x_ref[...])
    if is_prompt:
        hp = modnorm(prev_ref[...])[7:8, :]
        hp = jnp.where(i > 0, hp, 0.0)
        row = lax.broadcasted_iota(jnp.int32, h.shape, 0)
        hprev = jnp.where(row == 0, hp, pltpu.roll(h, 1, axis=0))
        h_o[...] = h[tm - 8:tm, :]
    else:
        hprev = prev_ref[...]
        h_o[...] = h
    dx = hprev - h
    mu = mu_ref[...]

    def branch(j, w_ref):
        xi = (h + dx * mu[j:j + 1, :]).astype(BF16)
        return _dot(xi, w_ref[...])

    r = branch(0, wr_ref)
    k = branch(1, wk_ref)
    v = branch(2, wv_ref)
    wl = branch(3, wwl_ref)
    al = branch(4, wal_ref)
    gl = branch(5, wgl_ref)
    qkv = _dot(h.astype(BF16), wqkv_ref[...])
    q_o[...] = qkv[:, :ATTN_WIDTH]
    kat_o[...] = qkv[:, ATTN_WIDTH:ATTN_WIDTH + KV_WIDTH]
    vat_o[...] = qkv[:, ATTN_WIDTH + KV_WIDTH:]

    z = w0_ref[...] + _dot(jnp.tanh(wl).astype(BF16), w2_ref[...])
    w_raw = -jnp.logaddexp(-z, 0.0) - 0.5
    lw_o[...] = -jnp.exp(w_raw)
    a = jax.nn.sigmoid(a0_ref[...] + _dot(al.astype(BF16), a2_ref[...]))
    g_o[...] = _dot(jax.nn.sigmoid(gl).astype(BF16), g2_ref[...])
    kk = k * kk_ref[...]
    kk = kk / jnp.maximum(jnp.sqrt(_seg_sum64(kk * kk)), L2_EPS)
    kf = k * (1.0 + (a - 1.0) * ka_ref[...])
    r_o[...] = r
    kf_o[...] = kf
    v_o[...] = v
    na_o[...] = -kk
    b_o[...] = kk * a
    bonus_o[...] = _seg_sum64(r * kf * rk_ref[...]) * v


def _in_proj(is_prompt, tm, x, prev, mod, gpm, mu8, wts, rw):
    m, d = x.shape
    grid = (m // tm,)
    row = lambda i: (i, 0)
    if is_prompt:
        prev_spec = pl.BlockSpec((8, d), lambda i: (jnp.maximum(i * (tm // 8) - 1, 0), 0))
        mod_rows = 8
        mod_map = lambda c: (lambda i: (0, c))
        h_shape, h_spec = (8, d), pl.BlockSpec((8, d), lambda i: (0, 0))
    else:
        prev_spec = pl.BlockSpec((tm, d), row)
        mod_rows = tm
        mod_map = lambda c: (lambda i: (i, c))
        h_shape, h_spec = (m, d), pl.BlockSpec((tm, d), row)
    in_specs = [pl.BlockSpec((tm, d), row), prev_spec,
                pl.BlockSpec((mod_rows, d), mod_map(0)), pl.BlockSpec((mod_rows, d), mod_map(1)),
                _resident((1, d)), _resident((8, d))]
    in_specs += [_resident(w.shape) for w in wts]
    in_specs += [_resident(p.shape) for p in rw]
    wide = jax.ShapeDtypeStruct((m, RWKV_WIDTH), F32)
    wide_spec = pl.BlockSpec((tm, RWKV_WIDTH), row)
    kv = jax.ShapeDtypeStruct((m, KV_WIDTH), F32)
    kv_spec = pl.BlockSpec((tm, KV_WIDTH), row)
    out_shape = [wide] * 9 + [kv, kv, jax.ShapeDtypeStruct(h_shape, F32)]
    out_specs = [wide_spec] * 9 + [kv_spec, kv_spec, h_spec]
    return pl.pallas_call(
        functools.partial(_inproj_kernel, is_prompt, tm),
        grid=grid, in_specs=in_specs, out_specs=out_specs, out_shape=out_shape,
        compiler_params=pltpu.CompilerParams(dimension_semantics=("arbitrary",), vmem_limit_bytes=VMEM_LIMIT),
        name="in_proj_prompt" if is_prompt else "in_proj_sample",
    )(x, prev, mod, mod, gpm, mu8, *wts, *rw)


def _rwkv_chunk_kernel(r_ref, lw_ref, k_ref, v_ref, a_ref, b_ref, y_ref, s_ref, st_ref):
    t = pl.program_id(1)
    C = CHUNK
    P = 2 * HEAD_DIM

    @pl.when(t == 0)
    def _():
        st_ref[...] = jnp.zeros_like(st_ref)

    ri = lax.broadcasted_iota(jnp.int32, (P, P), 0)
    ci = lax.broadcasted_iota(jnp.int32, (P, P), 1)
    bd = (ri // C) == (ci // C)
    tril_s = bd & ((ri % C) > (ci % C))
    tril_i = bd & ((ri % C) >= (ci % C))
    eye = jnp.where(ri == ci, 1.0, 0.0)
    lane0 = lax.broadcasted_iota(jnp.int32, (C, P), 1) < HEAD_DIM
    trow = lax.broadcasted_iota(jnp.int32, (C, P), 0)

    def stack(x):
        return jnp.concatenate([jnp.where(lane0, x, 0.0), jnp.where(lane0, 0.0, x)], axis=0)

    def dup(x):
        return jnp.concatenate([x, x], axis=0)

    def prep(sl, pp):
        ln = slice(P * pp, P * (pp + 1))
        lw = lw_ref[sl, ln]
        cw = lw
        for s in (1, 2, 4, 8, 16, 32):
            cw = cw + jnp.where(trow >= s, pltpu.roll(cw, s, axis=0), 0.0)
        cw_last = cw[C - 1:C, :]
        e_neg = jnp.exp(-cw)
        e_end = jnp.exp(cw_last - cw)
        k = k_ref[sl, ln]
        v = v_ref[sl, ln]
        b = b_ref[sl, ln]
        a2 = stack(a_ref[sl, ln] * jnp.exp(cw - lw))
        r2 = stack(r_ref[sl, ln] * jnp.exp(cw))
        return dict(
            v2=stack(v),
            lhs=jnp.concatenate([a2, r2], axis=0).astype(BF16),
            rhs=jnp.concatenate([dup(k * e_neg), dup(b * e_neg)], axis=0).astype(BF16),
            kbh=jnp.concatenate([dup(b * e_end), dup(k * e_end)], axis=0).astype(BF16),
            decay=jnp.exp(cw_last))

    def chunk(c, carry):
        sl = pl.ds(pl.multiple_of(c * C, C), C)
        pairs = range(RWKV_PAIRS)
        st = [st_ref[pp] for pp in pairs]
        d = [prep(sl, pp) for pp in pairs]
        gram = [_dot_nt(d[pp]['lhs'], d[pp]['rhs']) for pp in pairs]
        l2 = [jnp.where(tril_s, gram[pp][0:P, P:2 * P], 0.0) for pp in pairs]
        a_s = [_dot_nt(d[pp]['lhs'], st[pp].astype(BF16)) for pp in pairs]
        v2b = [d[pp]['v2'].astype(BF16) for pp in pairs]
        rhs_u = [a_s[pp][0:P] + _dot(jnp.where(tril_s, gram[pp][0:P, 0:P], 0.0).astype(BF16), v2b[pp])
                 for pp in pairs]
        inv = [eye + l2[pp] for pp in pairs]
        lp = l2
        for _ in range(5):
            lpb = [lp[pp].astype(BF16) for pp in pairs]
            lp = [_dot(lpb[pp], lpb[pp]) for pp in pairs]
            inv = [inv[pp] + _dot(lp[pp].astype(BF16), inv[pp].astype(BF16)) for pp in pairs]
        u2 = [_dot(inv[pp].astype(BF16), rhs_u[pp].astype(BF16)) for pp in pairs]
        p_cat = [jnp.concatenate([jnp.where(tril_i, gram[pp][P:2 * P, P:2 * P], 0.0),
                                  jnp.where(tril_i, gram[pp][P:2 * P, 0:P], 0.0)], axis=1).astype(BF16)
                 for pp in pairs]
        y2 = [a_s[pp][P:2 * P] + _dot(p_cat[pp], jnp.concatenate([u2[pp].astype(BF16), v2b[pp]], axis=0))
              for pp in pairs]
        uvt = [jnp.concatenate([u2[pp], d[pp]['v2']], axis=0).T.astype(BF16) for pp in pairs]
        st_new = [jnp.where(bd, st[pp] * d[pp]['decay'] + _dot(uvt[pp], d[pp]['kbh']), 0.0) for pp in pairs]
        for pp in pairs:
            y_ref[sl, P * pp:P * (pp + 1)] = y2[pp][0:C] + y2[pp][C:2 * C]
            st_ref[pp] = st_new[pp]
        return carry

    lax.fori_loop(0, RWKV_ROWS // C, chunk, 0)

    @pl.when(t == pl.num_programs(1) - 1)
    def _():
        s_ref[...] = st_ref[...]


def _rwkv_prompt(r, lw, kf, v, na, b):
    t = r.shape[0]
    n_pairs = RWKV_WIDTH // LANES
    width = RWKV_PAIRS * LANES
    spec = pl.BlockSpec((RWKV_ROWS, width), lambda p, i: (i, p))
    return pl.pallas_call(
        _rwkv_chunk_kernel,
        grid=(n_pairs // RWKV_PAIRS, t // RWKV_ROWS),
        in_specs=[spec] * 6,
        out_specs=[spec, pl.BlockSpec((RWKV_PAIRS, LANES, LANES), lambda p, i: (p, 0, 0))],
        out_shape=[jax.ShapeDtypeStruct((t, RWKV_WIDTH), F32),
                   jax.ShapeDtypeStruct((n_pairs, LANES, LANES), F32)],
        scratch_shapes=[pltpu.VMEM((RWKV_PAIRS, LANES, LANES), F32)],
        compiler_params=pltpu.CompilerParams(dimension_semantics=("arbitrary", "arbitrary"),
                                             vmem_limit_bytes=VMEM_LIMIT),
        name="rwkv_chunked",
    )(r, lw, kf, v, na, b)


def _rwkv_step_kernel(r_ref, lw_ref, k_ref, v_ref, a_ref, b_ref, s_ref, y_ref, so_ref):
    n = HEAD_DIM
    tr = lambda ref: ref[...].T
    rt, kt, vt, at, bt = tr(r_ref), tr(k_ref), tr(v_ref), tr(a_ref), tr(b_ref)
    dt = jnp.exp(tr(lw_ref))
    y_rows = []
    for h in range(2):
        hs = slice(h * n, (h + 1) * n)
        r_h, k_h, a_h, b_h, d_h = rt[hs], kt[hs], at[hs], bt[hs], dt[hs]
        for v in range(n):
            s = s_ref[h, v]
            sa = jnp.sum(s * a_h, axis=0, keepdims=True)
            s2 = s * d_h + sa * b_h + vt[h * n + v:h * n + v + 1] * k_h
            so_ref[h, v] = s2
            y_rows.append(jnp.sum(s2 * r_h, axis=0, keepdims=True))
    y_ref[...] = jnp.concatenate(y_rows, axis=0).T


def _rwkv_sample(r, lw, kf, v, na, b, state):
    nb = r.shape[0]
    assert nb == LANES, "the step kernel puts the whole batch on the lane axis"
    state_t = jnp.transpose(state, (1, 2, 3, 0))
    vspec = pl.BlockSpec((nb, LANES), lambda p: (0, p))
    sspec = pl.BlockSpec((2, HEAD_DIM, HEAD_DIM, nb), lambda p: (p, 0, 0, 0))
    y, state_new = pl.pallas_call(
        _rwkv_step_kernel,
        grid=(N_RWKV_HEADS // 2,),
        in_specs=[vspec] * 6 + [sspec],
        out_specs=[vspec, sspec],
        out_shape=[jax.ShapeDtypeStruct((nb, RWKV_WIDTH), F32), jax.ShapeDtypeStruct(state_t.shape, F32)],
        compiler_params=pltpu.CompilerParams(dimension_semantics=("arbitrary",)),
        name="rwkv_step",
    )(r, lw, kf, v, na, b, state_t)
    return y, jnp.transpose(state_new, (3, 0, 1, 2))


def _alibi_slope(head):
    return 2.0 ** (-8.0 * (head + 1) / N_Q_HEADS)


def _swa_prompt_kernel(q_ref, kc_ref, kp_ref, vc_ref, vp_ref, sink_ref, o_ref):
    n = pl.program_id(0)
    w = WINDOW
    kcat = jnp.concatenate([kp_ref[...], kc_ref[...]], axis=0)
    vcat = jnp.concatenate([vp_ref[...], vc_ref[...]], axis=0)
    lane_k = lax.broadcasted_iota(jnp.int32, kcat.shape, 1) < HEAD_DIM
    kswap = pltpu.roll(kcat, HEAD_DIM, axis=1)
    vswap = pltpu.roll(vcat, HEAD_DIM, axis=1)
    kdup = [jnp.where(lane_k, kcat, kswap).astype(BF16), jnp.where(lane_k, kswap, kcat).astype(BF16)]
    vdup = [jnp.where(lane_k, vcat, vswap).astype(BF16), jnp.where(lane_k, vswap, vcat).astype(BF16)]
    qi = lax.broadcasted_iota(jnp.int32, (w, 2 * w), 0)
    kj = lax.broadcasted_iota(jnp.int32, (w, 2 * w), 1)
    dist = qi + w - kj
    valid = (dist >= 0) & (dist <= w) & ((n > 0) | (kj >= w))
    distf = dist.astype(F32)
    lane_q = lax.broadcasted_iota(jnp.int32, (w, LANES), 1) < HEAD_DIM
    ones = jnp.ones((2 * w, LANES), BF16)
    sinks = sink_ref[...]
    group = 4
    for j0 in range(0, N_Q_HEADS // 2, group):
        js = range(j0, j0 + group)
        heads = [2 * j + half for j in js for half in range(2)]
        kvh = {h: h // Q_PER_KV for h in heads}
        q2 = {}
        for j in js:
            qp = q_ref[:, LANES * j:LANES * (j + 1)] * (1.0 / math.sqrt(HEAD_DIM))
            q2[j] = jnp.concatenate([jnp.where(lane_q, qp, 0.0), jnp.where(lane_q, 0.0, qp)], axis=0).astype(BF16)
        s2 = {j: _dot_nt(q2[j], kdup[kvh[2 * j]]) for j in js}
        s = {h: jnp.where(valid, s2[h // 2][w * (h % 2):w * (h % 2 + 1)] - _alibi_slope(h) * distf, -jnp.inf)
             for h in heads}
        sink = {h: sinks[0:1, h:h + 1] for h in heads}
        m = {h: jnp.broadcast_to(jnp.maximum(jnp.max(s[h], axis=-1, keepdims=True), sink[h]), (w, LANES))
             for h in heads}
        p = {h: jnp.exp(s[h] - jnp.concatenate([m[h], m[h]], axis=1)).astype(BF16) for h in heads}
        den = {h: _dot(p[h], ones) + jnp.exp(sink[h] - m[h]) for h in heads}
        o = {h: _dot(p[h], vdup[kvh[h]]) / den[h] for h in heads}
        for j in js:
            o_ref[:, LANES * j:LANES * (j + 1)] = jnp.where(lane_q, o[2 * j], o[2 * j + 1])


def _swa_prompt(q, ka, va, sinks):
    t = q.shape[0]
    w = WINDOW
    cur = lambda n: (n, 0)
    prv = lambda n: (jnp.maximum(n - 1, 0), 0)
    kvs = lambda f: pl.BlockSpec((w, KV_WIDTH), f)
    return pl.pallas_call(
        _swa_prompt_kernel,
        grid=(t // w,),
        in_specs=[pl.BlockSpec((w, ATTN_WIDTH), cur), kvs(cur), kvs(prv), kvs(cur), kvs(prv),
                  pl.BlockSpec((1, N_Q_HEADS), lambda n: (0, 0))],
        out_specs=pl.BlockSpec((w, ATTN_WIDTH), cur),
        out_shape=jax.ShapeDtypeStruct((t, ATTN_WIDTH), F32),
        compiler_params=pltpu.CompilerParams(dimension_semantics=("arbitrary",)),
        name="swa_prompt",
    )(q, ka, ka, va, va, sinks)


def _swa_sample_kernel(q_ref, kn_ref, vn_ref, ck_ref, cv_ref, sink_ref, o_ref, ko_ref, vo_ref):
    wb = ck_ref.shape[1]
    q = q_ref[...] * (1.0 / math.sqrt(HEAD_DIM))
    q2 = jnp.concatenate([q, q], axis=2)
    rowh = lax.broadcasted_iota(jnp.int32, q2.shape, 1) // Q_PER_KV
    laneh = lax.broadcasted_iota(jnp.int32, q2.shape, 2) // HEAD_DIM
    qb = jnp.where(rowh == laneh, q2, 0.0)
    kn = kn_ref[...]
    vn = vn_ref[...]
    ck = ck_ref[...]
    cv = cv_ref[...]
    s = jnp.einsum('bqd,bkd->bqk', qb.astype(BF16), ck.astype(BF16), preferred_element_type=F32)
    s_self = jnp.sum(qb * kn, axis=2, keepdims=True)
    head = lax.broadcasted_iota(jnp.int32, (1, N_Q_HEADS, 1), 1).astype(F32)
    slope = jnp.exp2(-8.0 * (head + 1.0) / N_Q_HEADS)
    dist = (wb - lax.broadcasted_iota(jnp.int32, (1, 1, wb), 2)).astype(F32)
    s = s - slope * dist
    sink = sink_ref[...][None]
    m = jnp.maximum(jnp.maximum(jnp.max(s, axis=2, keepdims=True), s_self), sink)
    p = jnp.exp(s - m)
    p_self = jnp.exp(s_self - m)
    den = jnp.sum(p, axis=2, keepdims=True) + p_self + jnp.exp(sink - m)
    o = jnp.einsum('bqk,bkd->bqd', p.astype(BF16), cv.astype(BF16), preferred_element_type=F32)
    o = (o + p_self * vn) / den
    sel = lax.broadcasted_iota(jnp.int32, (1, N_Q_HEADS, HEAD_DIM), 1) < Q_PER_KV
    o_ref[...] = jnp.where(sel, o[:, :, :HEAD_DIM], o[:, :, HEAD_DIM:])
    ko_ref[:, 0:wb - 1, :] = ck_ref[:, 1:wb, :]
    ko_ref[:, wb - 1:wb, :] = kn
    vo_ref[:, 0:wb - 1, :] = cv_ref[:, 1:wb, :]
    vo_ref[:, wb - 1:wb, :] = vn


def _swa_sample(q, ka, va, cache_k, cache_v, sinks_col):
    nb, wb = cache_k.shape[0], cache_k.shape[1]
    bb = 16
    b3 = lambda i: (i, 0, 0)
    nspec = pl.BlockSpec((bb, 1, KV_WIDTH), b3)
    cspec = pl.BlockSpec((bb, wb, KV_WIDTH), b3)
    qspec = pl.BlockSpec((bb, N_Q_HEADS, HEAD_DIM), b3)
    return pl.pallas_call(
        _swa_sample_kernel,
        grid=(nb // bb,),
        in_specs=[qspec, nspec, nspec, cspec, cspec, pl.BlockSpec((N_Q_HEADS, 1), lambda i: (0, 0))],
        out_specs=[qspec, cspec, cspec],
        out_shape=[jax.ShapeDtypeStruct((nb, N_Q_HEADS, HEAD_DIM), F32),
                   jax.ShapeDtypeStruct(cache_k.shape, F32), jax.ShapeDtypeStruct(cache_v.shape, F32)],
        compiler_params=pltpu.CompilerParams(dimension_semantics=("arbitrary",)),
        name="swa_sample",
    )(q.reshape(nb, N_Q_HEADS, HEAD_DIM), ka.reshape(nb, 1, KV_WIDTH), va.reshape(nb, 1, KV_WIDTH),
      cache_k, cache_v, sinks_col)


def _post_mix_kernel(is_prompt, n_main, n_tail, *refs):
    if n_tail:
        tail_ref, refs = refs[16], refs[:16] + refs[17:]
    i = pl.program_id(0)

    @pl.when(i < n_main)
    def _():
        _post_mix_body(is_prompt, *refs)

    if n_tail:
        h2_o = refs[17]

        @pl.when(i == n_main)
        def _():
            h2_o[0:n_tail, :] = tail_ref[...]
            h2_o[n_tail:, :] = jnp.zeros((h2_o.shape[0] - n_tail, h2_o.shape[1]), F32)


def _post_mix_body(is_prompt, yr_ref, g_ref, bonus_ref, ya_ref, x_ref, gt1_ref, sh2_ref, sc2_ref,
                   lnw_ref, lnb_ref, wout_ref, gpost_ref, gpre_ref, rw_ref, rb_ref, cnt_ref,
                   x1_o, h2_o, ti_o, tw_o, rank_o, cnt_o, run_ref):
    @pl.when(pl.program_id(0) == 0)
    def _():
        run_ref[...] = cnt_ref[...]

    gt1 = gt1_ref[...]
    sh2 = sh2_ref[...]
    sc2 = sc2_ref[...]
    if is_prompt:
        gt1, sh2, sc2 = gt1[0:1], sh2[0:1], sc2[0:1]
    y = yr_ref[...]
    mean = _seg_sum64(y) * (1.0 / HEAD_DIM)
    dlt = y - mean
    var = _seg_sum64(dlt * dlt) * (1.0 / HEAD_DIM)
    yn = dlt * lax.rsqrt(var + GN_EPS) * lnw_ref[...] + lnb_ref[...]
    yr = (yn + bonus_ref[...]) * g_ref[...]
    mix = _dot(jnp.concatenate([yr, ya_ref[...]], axis=1).astype(BF16), wout_ref[...])
    x1 = x_ref[...] + gt1 * _rmsnorm(mix, gpost_ref[...])
    x1_o[...] = x1
    h2 = _rmsnorm(x1, gpre_ref[...]) * (1.0 + sc2) + sh2
    h2_o[...] = h2
    logits = _dot3(h2, rw_ref[...]) + rb_ref[...]
    lane = lax.broadcasted_iota(jnp.int32, logits.shape, 1)
    vals, idxs = [], []
    for _ in range(TOP_K):
        m = jnp.max(logits, axis=1, keepdims=True)
        idx = jnp.min(jnp.where(logits == m, lane, N_EXPERTS), axis=1, keepdims=True)
        vals.append(m)
        idxs.append(idx)
        logits = jnp.where(lane == idx, -jnp.inf, logits)
    e = jnp.exp(jnp.concatenate(vals, axis=1) - vals[0])
    tw_o[...] = e / jnp.sum(e, axis=1, keepdims=True)
    ti_o[...] = jnp.concatenate(idxs, axis=1)

    tm = logits.shape[0]
    lane_e = lax.broadcasted_iota(jnp.int32, (tm, LANES), 1)
    tri = jnp.where(lax.broadcasted_iota(jnp.int32, (tm, tm), 0) > lax.broadcasted_iota(jnp.int32, (tm, tm), 1),
                    1.0, 0.0).astype(BF16)
    run = run_ref[...]
    ranks = []
    for idx in idxs:
        onehot = jnp.where(lane_e == idx, 1.0, 0.0)
        before = _dot(tri, onehot.astype(BF16)) + run
        ranks.append(jnp.sum(onehot * before, axis=1, keepdims=True))
        run = run + jnp.sum(onehot, axis=0, keepdims=True)
    run_ref[...] = run
    cnt_o[...] = run
    rank_o[...] = jnp.concatenate(ranks, axis=1).astype(jnp.int32)


def _post_mix(is_prompt, tm, yr, g, bonus, ya, x, mod, lnw, lnb, wout, gpost, gpre, rw, rb, cnt, h2_tail=None):
    m, d = x.shape
    n_main = m // tm
    n_tail = 0 if h2_tail is None else h2_tail.shape[0]
    assert n_tail <= tm
    row = lambda i: (jnp.minimum(i, n_main - 1), 0)
    if is_prompt:
        mod_rows = 8
        mod_map = lambda c: (lambda i: (0, c))
    else:
        mod_rows = tm
        mod_map = lambda c: (lambda i: (jnp.minimum(i, n_main - 1), c))
    wide = pl.BlockSpec((tm, RWKV_WIDTH), row)
    in_specs = [wide, wide, wide, wide, pl.BlockSpec((tm, d), row),
                pl.BlockSpec((mod_rows, d), mod_map(2)), pl.BlockSpec((mod_rows, d), mod_map(3)),
                pl.BlockSpec((mod_rows, d), mod_map(4)),
                _resident(lnw.shape), _resident(lnb.shape), _resident(wout.shape), _resident(gpost.shape),
                _resident(gpre.shape), _resident(rw.shape), _resident(rb.shape), _resident(cnt.shape)]
    args = [yr, g, bonus, ya, x, mod, mod, mod, lnw, lnb, wout, gpost, gpre, rw, rb, cnt]
    if n_tail:
        in_specs.append(_resident(h2_tail.shape))
        args.append(h2_tail)
    out_shape = [jax.ShapeDtypeStruct((m, d), F32), jax.ShapeDtypeStruct((m + n_tail, d), F32),
                 jax.ShapeDtypeStruct((m, TOP_K), jnp.int32), jax.ShapeDtypeStruct((m, TOP_K), F32),
                 jax.ShapeDtypeStruct((m, TOP_K), jnp.int32), jax.ShapeDtypeStruct((1, LANES), F32)]
    out_specs = [pl.BlockSpec((tm, d), row), pl.BlockSpec((tm, d), lambda i: (i, 0)),
                 pl.BlockSpec((tm, TOP_K), row), pl.BlockSpec((tm, TOP_K), row),
                 pl.BlockSpec((tm, TOP_K), row), pl.BlockSpec((1, LANES), lambda i: (0, 0))]
    return pl.pallas_call(
        functools.partial(_post_mix_kernel, is_prompt, n_main, n_tail),
        grid=(n_main + (1 if n_tail else 0),), in_specs=in_specs, out_specs=out_specs, out_shape=out_shape,
        scratch_shapes=[pltpu.VMEM((1, LANES), F32)],
        compiler_params=pltpu.CompilerParams(dimension_semantics=("arbitrary",), vmem_limit_bytes=VMEM_LIMIT),
        name="post_mix_prompt" if is_prompt else "post_mix_sample",
    )(*args)


def _moe_kernel(ex_ref, rb_ref, ns_ref, cb_ref, tok_ref, h2_hbm, w1_ref, b1g_ref, b1l_ref, w2_ref, b2_ref, ys_hbm,
                xbuf, xb, acc, w1p, w2b, sem_in, sem_out):
    s = pl.program_id(0)
    f = pl.program_id(1)
    n_super = pl.num_programs(0)
    ns = ns_ref[s]
    rb = rb_ref[s]
    rows = MOE_ROWS
    grp = 2 * LANES
    share = rows // MOE_NF

    def gather_share(sup, col):
        base = cb_ref[sup] + col * share
        g_col = col * (share // SUBLANES)
        for j in range(MOE_SUB):
            for u in range(share):
                tok = tok_ref[base + (j * rows + u)]
                pltpu.make_async_copy(h2_hbm.at[tok >> 3, pl.ds(tok & (SUBLANES - 1), 1)],
                                      xbuf.at[g_col + (j * (rows // SUBLANES) + u // SUBLANES),
                                              pl.ds(u % SUBLANES, 1)], sem_in).start()

    def gather_wait():
        pltpu.make_async_copy(h2_hbm.at[pl.ds(0, xbuf.shape[0])], xbuf, sem_in).wait()

    def y_copy(r0, n):
        return pltpu.make_async_copy(acc.at[pl.ds(r0, n)], ys_hbm.at[pl.ds(rb * rows + r0, n)], sem_out)

    def for_tiles(fn):
        def body(j, c):
            fn(j)
            return c
        lax.fori_loop(0, ns, body, 0)

    @pl.when(ns > 0)
    def _():
        @pl.when(f == 0)
        def _():
            @pl.when(s == 0)
            def _():
                def first(col, c):
                    gather_share(0, col)
                    return c
                lax.fori_loop(0, MOE_NF, first, 0)
            b2 = jnp.broadcast_to(b2_ref[0], (rows, D_MODEL))

            def init(j):
                acc[pl.ds(pl.multiple_of(j * rows, rows), rows), :] = b2
            for_tiles(init)

            gather_wait()

            def cast(j):
                n8 = rows // SUBLANES
                x = xbuf[pl.ds(pl.multiple_of(j * n8, n8), n8)].reshape(rows, D_MODEL)
                xb[pl.ds(pl.multiple_of(j * rows, rows), rows), :] = x.astype(BF16)
            for_tiles(cast)

        n_grp = 2 * MOE_FT // grp

        def prep_weights():
            pr = lax.broadcasted_iota(jnp.int32, (grp, grp), 0)
            pc = lax.broadcasted_iota(jnp.int32, (grp, grp), 1)
            perm = jnp.where(pr == jnp.where(pc < LANES, 2 * pc, 2 * (pc - LANES) + 1), 1.0, 0.0).astype(BF16)
            for g in range(n_grp):
                gs = slice(grp * g, grp * (g + 1))
                w1p[:, gs] = _dot(w1_ref[0, :, gs].astype(BF16), perm).astype(BF16)
            w2b[...] = w2_ref[0].astype(BF16)

        nxt = jnp.minimum(s + 1, n_super - 1)
        next_live = (s + 1 < n_super) & (ns_ref[nxt] > 0)

        @pl.when(next_live)
        def _():
            gather_share(nxt, f)
            prep_weights()

        @pl.when(jnp.logical_not(next_live))
        def _():
            prep_weights()

        b1g = b1g_ref[0]
        b1l = b1l_ref[0]

        last_col = f == pl.num_programs(1) - 1

        def swiglu(hh):
            glu = jnp.concatenate([hh[:, grp * g:grp * g + LANES] for g in range(n_grp)], axis=1) + b1g
            lin = jnp.concatenate([hh[:, grp * g + LANES:grp * (g + 1)] for g in range(n_grp)], axis=1) + b1l
            glu = jnp.minimum(glu, SWIGLU_LIMIT)
            lin = jnp.clip(lin, -SWIGLU_LIMIT, SWIGLU_LIMIT)
            return (glu * jax.nn.sigmoid(SWIGLU_ALPHA * glu) * (lin + 1.0)).astype(BF16)

        def tiles(starts, n):
            sls = [pl.ds(pl.multiple_of(r0, rows), n) for r0 in starts]
            hh = [_dot(xb[sl, :], w1p[...]) for sl in sls]
            act = [swiglu(h) for h in hh]
            out = [_dot(a, w2b[...]) for a in act]
            for sl, o in zip(sls, out):
                acc[sl, :] += o

            @pl.when(last_col)
            def _():
                for r0 in starts:
                    y_copy(r0, n).start()

        big = 2 * rows

        def tile_pair(q, c):
            tiles([2 * big * q, 2 * big * q + big], big)
            return c
        lax.fori_loop(0, ns // 4, tile_pair, 0)

        @pl.when(ns % 4 >= 2)
        def _():
            tiles([(ns // 4) * 2 * big], big)

        @pl.when(ns % 2 == 1)
        def _():
            tiles([(ns - 1) * rows], rows)

        @pl.when(last_col)
        def _():
            for_tiles(lambda j: y_copy(j * rows, rows).wait())

    @pl.when((ns < 0) & (f == 0))
    def _():
        acc[0:rows, :] = jnp.zeros((rows, D_MODEL), F32)

        def z_copy(j):
            return pltpu.make_async_copy(acc.at[pl.ds(0, rows)], ys_hbm.at[pl.ds((rb + j) * rows, rows)], sem_out)

        def body(j, c, op):
            op(z_copy(j))
            return c
        lax.fori_loop(0, -ns, functools.partial(body, op=lambda cp: cp.start()), 0)
        lax.fori_loop(0, -ns, functools.partial(body, op=lambda cp: cp.wait()), 0)


def _moe_experts(n_rows, ex, rb, ns, cb, tok_sorted, h2, w1, b1g, b1l, w2, b2):
    n_super = ex.shape[0]
    nf = MOE_NF
    last = nf - 1
    sub_rows = MOE_SUB * MOE_ROWS

    def fcol(s, f, ns_):
        return jnp.where(ns_[s] > 0, f, last)

    grid_spec = pltpu.PrefetchScalarGridSpec(
        num_scalar_prefetch=5,
        grid=(n_super, nf),
        in_specs=[pl.BlockSpec(memory_space=pl.ANY),
                  pl.BlockSpec((1, D_MODEL, 2 * MOE_FT), lambda s, f, e_, r_, n_, c_, t_: (e_[s], 0, fcol(s, f, n_))),
                  pl.BlockSpec((1, 1, MOE_FT), lambda s, f, e_, r_, n_, c_, t_: (e_[s], 0, fcol(s, f, n_))),
                  pl.BlockSpec((1, 1, MOE_FT), lambda s, f, e_, r_, n_, c_, t_: (e_[s], 0, fcol(s, f, n_))),
                  pl.BlockSpec((1, MOE_FT, D_MODEL), lambda s, f, e_, r_, n_, c_, t_: (e_[s], fcol(s, f, n_), 0)),
                  pl.BlockSpec((1, 1, D_MODEL), lambda s, f, e_, r_, n_, c_, t_: (e_[s], 0, 0))],
        out_specs=pl.BlockSpec(memory_space=pl.ANY),
        scratch_shapes=[pltpu.VMEM((sub_rows // SUBLANES, SUBLANES, D_MODEL), F32),
                        pltpu.VMEM((sub_rows, D_MODEL), BF16),
                        pltpu.VMEM((sub_rows, D_MODEL), F32),
                        pltpu.VMEM((D_MODEL, 2 * MOE_FT), BF16),
                        pltpu.VMEM((MOE_FT, D_MODEL), BF16),
                        pltpu.SemaphoreType.DMA, pltpu.SemaphoreType.DMA])
    return pl.pallas_call(
        _moe_kernel,
        grid_spec=grid_spec,
        out_shape=jax.ShapeDtypeStruct((n_rows, D_MODEL), F32),
        compiler_params=pltpu.CompilerParams(dimension_semantics=("arbitrary", "arbitrary"),
                                             vmem_limit_bytes=VMEM_LIMIT),
        name="moe_experts",
    )(ex, rb, ns, cb, tok_sorted, h2.reshape(-1, SUBLANES, D_MODEL), w1, b1g, b1l, w2, b2)


def _combine_kernel(is_prompt, idx_ref, nidx_ref, ys_hbm, x1_ref, tw_ref, gt2_ref, gpost_ref, o_ref, buf, sem):
    tk = COMBINE_TOK
    n = TOP_K * tk
    unroll = 16
    i = pl.program_id(0)
    slot = i % 2

    def start_row(ref, buf_slot, j, g, sub):
        p = ref[0, 0, j]
        pltpu.make_async_copy(ys_hbm.at[p >> 3, pl.ds(p & (SUBLANES - 1), 1)],
                              buf.at[buf_slot, g, pl.ds(sub, 1)], sem.at[buf_slot]).start()

    @pl.when(i == 0)
    def _():
        def body(q, c):
            for u in range(unroll):
                start_row(idx_ref, 0, q * unroll + u, q * (unroll // SUBLANES) + u // SUBLANES, u % SUBLANES)
            return c
        lax.fori_loop(0, n // unroll, body, 0)

    pltpu.make_async_copy(ys_hbm.at[pl.ds(0, n // SUBLANES)], buf.at[slot], sem.at[slot]).wait()

    def combine():
        tw = tw_ref[...]
        t8 = tk // SUBLANES
        rows_of = lambda k: buf[slot, k * t8:(k + 1) * t8].reshape(tk, D_MODEL)
        f = tw[:, 0:1] * rows_of(0)
        for k in range(1, TOP_K):
            f = f + tw[:, k:k + 1] * rows_of(k)
        gt2 = gt2_ref[...]
        if is_prompt:
            gt2 = gt2[0:1]
        o_ref[...] = x1_ref[...] + gt2 * _rmsnorm(f, gpost_ref[...])

    @pl.when(i + 1 < pl.num_programs(0))
    def _():
        for j in range(n):
            start_row(nidx_ref, 1 - slot, j, j // SUBLANES, j % SUBLANES)
        combine()

    @pl.when(i + 1 >= pl.num_programs(0))
    def _():
        combine()


def _combine(is_prompt, pos, ys, x1, tw, mod, gpost):
    m, d = x1.shape
    tk = COMBINE_TOK
    nblk = m // tk
    idx = pos.reshape(nblk, tk, TOP_K).transpose(0, 2, 1).reshape(nblk, 1, TOP_K * tk)
    row = lambda i: (i, 0)
    mod_spec = (pl.BlockSpec((8, d), lambda i: (0, 5)) if is_prompt else pl.BlockSpec((tk, d), lambda i: (i, 5)))
    return pl.pallas_call(
        functools.partial(_combine_kernel, is_prompt),
        grid=(nblk,),
        in_specs=[pl.BlockSpec((1, 1, TOP_K * tk), lambda i: (i, 0, 0), memory_space=pltpu.SMEM),
                  pl.BlockSpec((1, 1, TOP_K * tk), lambda i: (jnp.minimum(i + 1, nblk - 1), 0, 0),
                               memory_space=pltpu.SMEM),
                  pl.BlockSpec(memory_space=pl.ANY),
                  pl.BlockSpec((tk, d), row), pl.BlockSpec((tk, TOP_K), row), mod_spec,
                  pl.BlockSpec((1, d), lambda i: (0, 0))],
        out_specs=pl.BlockSpec((tk, d), row),
        out_shape=jax.ShapeDtypeStruct((m, d), F32),
        scratch_shapes=[pltpu.VMEM((2, TOP_K * tk // SUBLANES, SUBLANES, d), F32), pltpu.SemaphoreType.DMA((2,))],
        compiler_params=pltpu.CompilerParams(dimension_semantics=("arbitrary",)),
        name="moe_combine_prompt" if is_prompt else "moe_combine_sample",
    )(idx, idx, ys.reshape(-1, SUBLANES, d), x1, tw, mod, gpost)


def _routing_tables(top_i, rank, counts):
    n_assign = top_i.size
    e_flat = top_i.reshape(-1)
    tiles = (counts + MOE_ROWS - 1) // MOE_ROWS
    tile_start = jnp.cumsum(tiles) - tiles
    group_start = jnp.cumsum(counts) - counts
    pos = tile_start[e_flat] * MOE_ROWS + rank.reshape(-1)
    n_rows = _padded_rows(n_assign)
    assert n_assign <= (1 << 16) and n_rows <= (1 << 16)
    assert n_rows >= n_assign + MOE_SUB * MOE_ROWS
    key = pos.astype(jnp.uint32) * jnp.uint32(1 << 16) + jnp.arange(n_assign, dtype=jnp.uint32)
    tok_sorted = (jnp.sort(key) & jnp.uint32(0xFFFF)).astype(jnp.int32) // TOP_K
    tok_sorted = jnp.pad(tok_sorted, (0, n_rows - n_assign))
    supers = (tiles + MOE_SUB - 1) // MOE_SUB
    super_end = jnp.cumsum(supers)
    s_idx = jnp.arange(_max_supers(n_assign), dtype=jnp.int32)
    ex = jnp.minimum(jnp.sum((s_idx[:, None] >= super_end[None, :]).astype(jnp.int32), axis=1),
                     N_EXPERTS - 1).astype(jnp.int32)
    j = s_idx - (super_end - supers)[ex]
    live = s_idx < super_end[-1]
    tail = jnp.sum(tiles) + MOE_SUB * (s_idx - super_end[-1])
    n_clear = jnp.clip(n_rows // MOE_ROWS - tail, 0, MOE_SUB)
    ns = jnp.where(live, jnp.clip(tiles[ex] - MOE_SUB * j, 0, MOE_SUB), -n_clear).astype(jnp.int32)
    rb = jnp.where(live, tile_start[ex] + MOE_SUB * j, tail).astype(jnp.int32)
    cb = jnp.where(live, group_start[ex] + MOE_SUB * MOE_ROWS * j, 0).astype(jnp.int32)
    last_live = jnp.max(jnp.where(live, ex, 0))
    ex = jnp.where(live, ex, last_live).astype(jnp.int32)
    return pos.astype(jnp.int32), tok_sorted, ex, rb, ns, cb


def _padded_rows(n_assign):
    rows = n_assign + N_EXPERTS * (MOE_ROWS - 1)
    return -(-rows // MOE_ROWS) * MOE_ROWS


def _max_supers(n_assign):
    return N_EXPERTS + -(-_padded_rows(n_assign) // (MOE_ROWS * MOE_SUB))


def kernel(x_prompt, x_sample, cache_k, cache_v, state_wkv, state_shift, c_prompt, c_sample, w_ada, b_ada, g_pre_mix, g_post_mix, g_pre_ffn, g_post_ffn, mu_shift, w_in, rwkv_w0, rwkv_w2, rwkv_a0, rwkv_a2, rwkv_g2, rwkv_k_k, rwkv_k_a, rwkv_r_k, rwkv_ln_w, rwkv_ln_b, attn_sinks, w_out, router_w, router_b, moe_w1, moe_b1, moe_w2, moe_b2):
    assert w_ada.shape[0] == 1, "single-layer step"
    d = D_MODEL
    t = x_prompt.shape[1]
    nb = x_sample.shape[0]
    xp = x_prompt.reshape(t, d)
    xs = x_sample.reshape(nb, d)

    c_all = jnp.concatenate([jnp.broadcast_to(c_prompt, (8, d)), c_sample], axis=0)
    mod = _adaln_mod(c_all, w_ada[0], b_ada[0].reshape(1, 6 * d))
    mod_p, mod_s = mod[:8], mod[8:]

    offs = [0, 1024, 2048, 3072, 3136, 3200, 3360]
    w_in0 = w_in[0]
    pad_to = lambda w, n: jnp.pad(w, ((0, 0), (0, n - w.shape[1])))
    wts = [w_in0[:, offs[0]:offs[1]], w_in0[:, offs[1]:offs[2]], w_in0[:, offs[2]:offs[3]],
           pad_to(w_in0[:, offs[3]:offs[4]], 128), pad_to(w_in0[:, offs[4]:offs[5]], 128),
           pad_to(w_in0[:, offs[5]:offs[6]], 256), w_in0[:, offs[6]:]]
    wts = [w.astype(BF16) for w in wts]
    pad_rows = lambda w, n: jnp.pad(w, ((0, n - w.shape[0]), (0, 0))).astype(BF16)
    row = lambda p: p.reshape(1, -1)
    rw = [row(rwkv_w0[0]), pad_rows(rwkv_w2[0], 128), row(rwkv_a0[0]), pad_rows(rwkv_a2[0], 128),
          pad_rows(rwkv_g2[0], 256), row(rwkv_k_k[0]), row(rwkv_k_a[0]), row(rwkv_r_k[0])]
    gpm = row(g_pre_mix[0])
    mu8 = jnp.pad(mu_shift[0], ((0, 8 - N_SHIFTED), (0, 0)))

    pr = _in_proj(True, 256, xp, xp, mod_p, gpm, mu8, wts, rw)
    sr = _in_proj(False, nb, xs, state_shift[0], mod_s, gpm, mu8, wts, rw)
    r_p, lw_p, kf_p, v_p, na_p, b_p, g_p, bonus_p, q_p, ka_p, va_p, hlast_p = pr
    r_s, lw_s, kf_s, v_s, na_s, b_s, g_s, bonus_s, q_s, ka_s, va_s, h_s = sr

    y_p, st_p = _rwkv_prompt(r_p, lw_p, kf_p, v_p, na_p, b_p)
    y_s, wkv_s = _rwkv_sample(r_s, lw_s, kf_s, v_s, na_s, b_s, state_wkv[0])
    y_s = y_s.reshape(nb, RWKV_WIDTH)
    sinks = attn_sinks[0]
    att_p = _swa_prompt(q_p, ka_p, va_p, sinks.reshape(1, N_Q_HEADS))
    wb = cache_k.shape[2]
    att_s, ck_new, cv_new = _swa_sample(q_s, ka_s, va_s, cache_k[0].reshape(nb, wb, KV_WIDTH),
                                        cache_v[0].reshape(nb, wb, KV_WIDTH), sinks.reshape(N_Q_HEADS, 1))
    att_s = att_s.reshape(nb, ATTN_WIDTH)

    post = [row(rwkv_ln_w[0]), row(rwkv_ln_b[0]), w_out[0].astype(BF16), row(g_post_mix[0]),
            row(g_pre_ffn[0]), router_w[0], row(router_b[0])]
    x1_s, h2_s, ti_s, tw_s, rank_s, cnt_s = _post_mix(False, nb, y_s, g_s, bonus_s, att_s, xs, mod_s, *post,
                                                      jnp.zeros((1, LANES), F32))
    x1_p, h2_all, ti_p, tw_p, rank_p, cnt_all = _post_mix(True, 256, y_p, g_p, bonus_p, att_p, xp, mod_p, *post,
                                                          cnt_s, h2_tail=h2_s)

    pos, tok_sorted, ex, rb, ns, cb = _routing_tables(jnp.concatenate([ti_p, ti_s], axis=0),
                                                      jnp.concatenate([rank_p, rank_s], axis=0),
                                                      cnt_all[0, :N_EXPERTS].astype(jnp.int32))
    b1 = moe_b1[0].reshape(N_EXPERTS, 1, D_FF, 2)
    ys_rows = _moe_experts(tok_sorted.shape[0], ex, rb, ns, cb, tok_sorted, h2_all, moe_w1[0], b1[..., 0], b1[..., 1],
                           moe_w2[0], moe_b2[0].reshape(N_EXPERTS, 1, d))
    pos = pos.reshape(-1, TOP_K)
    gpf = row(g_post_ffn[0])
    out_p = _combine(True, pos[:t], ys_rows, x1_p, tw_p, mod_p, gpf)
    out_s = _combine(False, pos[t:], ys_rows, x1_s, tw_s, mod_s, gpf)

    n_keep = min(WINDOW, t)
    st_heads = jnp.stack([st_p[:, :HEAD_DIM, :HEAD_DIM], st_p[:, HEAD_DIM:, HEAD_DIM:]], axis=1)
    return (out_p.reshape(1, t, d),
            out_s.reshape(nb, 1, d),
            ka_p[t - n_keep:].reshape(1, 1, n_keep, N_KV_HEADS, HEAD_DIM),
            va_p[t - n_keep:].reshape(1, 1, n_keep, N_KV_HEADS, HEAD_DIM),
            st_heads.reshape(1, 1, N_RWKV_HEADS, HEAD_DIM, HEAD_DIM),
            hlast_p[7:8].reshape(1, 1, d),
            ck_new.reshape(1, nb, wb, N_KV_HEADS, HEAD_DIM),
            cv_new.reshape(1, nb, wb, N_KV_HEADS, HEAD_DIM),
            wkv_s.reshape(1, nb, N_RWKV_HEADS, HEAD_DIM, HEAD_DIM),
            h_s.reshape(1, nb, d))
```

```python
import functools
import math

import jax
import jax.numpy as jnp
from jax import lax
from jax.experimental import pallas as pl
from jax.experimental.pallas import tpu as pltpu

F32 = jnp.float32
BF16 = jnp.bfloat16

D_MODEL = 2048
HEAD_DIM = 64
RWKV_WIDTH = 1024
N_RWKV_HEADS = 16
ATTN_WIDTH = 1024
N_Q_HEADS = 16
N_KV_HEADS = 2
Q_PER_KV = 8
KV_WIDTH = 128
WINDOW = 128
N_SHIFTED = 6
N_EXPERTS = 32
TOP_K = 4
D_FF = 2048
SWIGLU_ALPHA = 1.702
SWIGLU_LIMIT = 7.0
NORM_EPS = 1e-6
GN_EPS = 64e-5
L2_EPS = 1e-12

LANES = 128
SUBLANES = 8
VMEM_LIMIT = 56 * 1024 * 1024

CHUNK = 64
RWKV_ROWS = 512
RWKV_PAIRS = 8
MOE_ROWS = 128
MOE_SUB = 12
MOE_FT = 256
MOE_NF = D_FF // MOE_FT
COMBINE_TOK = 128


def _dot(a, b):
    return jnp.dot(a, b, preferred_element_type=F32)


def _dot_nt(a, b):
    return lax.dot_general(a, b, (((1,), (1,)), ((), ())), preferred_element_type=F32)


def _split_bf16(x):
    hi = x.astype(BF16)
    lo = (x - hi.astype(F32)).astype(BF16)
    return hi, lo


def _dot3(a, b):
    ah, al = _split_bf16(a)
    bh, bl = _split_bf16(b)
    return _dot(ah, bh) + _dot(ah, bl) + _dot(al, bh)


def _seg_sum64(x):
    wide = LANES
    r = lax.broadcasted_iota(jnp.int32, (wide, wide), 0) // HEAD_DIM
    c = lax.broadcasted_iota(jnp.int32, (wide, wide), 1) // HEAD_DIM
    bd = jnp.where(r == c, 1.0, 0.0).astype(BF16)
    hi, lo = _split_bf16(x)
    outs = []
    for j in range(x.shape[1] // wide):
        sl = slice(wide * j, wide * (j + 1))
        outs.append(_dot(hi[:, sl], bd) + _dot(lo[:, sl], bd))
    return jnp.concatenate(outs, axis=1)


def _rmsnorm(x, g):
    ms = jnp.mean(x * x, axis=-1, keepdims=True)
    return x * lax.rsqrt(ms + NORM_EPS) * g


def _resident(shape):
    nd = len(shape)
    return pl.BlockSpec(shape, lambda *_: (0,) * nd, pipeline_mode=pl.Buffered(1))


def _mod_kernel(c_ref, w_ref, b_ref, o_ref):
    c = c_ref[...]
    s = c * jax.nn.sigmoid(c)
    o_ref[...] = _dot3(s, w_ref[...]) + b_ref[...]


def _adaln_mod(c, w_ada, b_ada):
    rows, d = c.shape
    n = w_ada.shape[1]
    tn = 512
    return pl.pallas_call(
        _mod_kernel,
        grid=(n // tn,),
        in_specs=[pl.BlockSpec((rows, d), lambda j: (0, 0)),
                  pl.BlockSpec((d, tn), lambda j: (0, j)),
                  pl.BlockSpec((1, tn), lambda j: (0, j))],
        out_specs=pl.BlockSpec((rows, tn), lambda j: (0, j)),
        out_shape=jax.ShapeDtypeStruct((rows, n), F32),
        compiler_params=pltpu.CompilerParams(dimension_semantics=("arbitrary",), vmem_limit_bytes=VMEM_LIMIT),
        name="adaln_mod",
    )(c, w_ada, b_ada)


def _inproj_kernel(is_prompt, tm,
                   x_ref, prev_ref, sh_ref, sc_ref, gpm_ref, mu_ref,
                   wr_ref, wk_ref, wv_ref, wwl_ref, wal_ref, wgl_ref, wqkv_ref,
                   w0_ref, w2_ref, a0_ref, a2_ref, g2_ref, kk_ref, ka_ref, rk_ref,
                   r_o, lw_o, kf_o, v_o, na_o, b_o, g_o, bonus_o, q_o, kat_o, vat_o, h_o):
    i = pl.program_id(0)
    gpm = gpm_ref[...]
    sh = sh_ref[...]
    sc = sc_ref[...]
    if is_prompt:
        sh = sh[0:1]
        sc = sc[0:1]

    def modnorm(x):
        return _rmsnorm(x, gpm) * (1.0 + sc) + sh

    h = modnorm(x_ref[...])
    if is_prompt:
        hp = modnorm(prev_ref[...])[7:8, :]
        hp = jnp.where(i > 0, hp, 0.0)
        row = lax.broadcasted_iota(jnp.int32, h.shape, 0)
        hprev = jnp.where(row == 0, hp, pltpu.roll(h, 1, axis=0))
        h_o[...] = h[tm - 8:tm, :]
    else:
        hprev = prev_ref[...]
        h_o[...] = h
    dx = hprev - h
    mu = mu_ref[...]

    def branch(j, w_ref):
        xi = (h + dx * mu[j:j + 1, :]).astype(BF16)
        return _dot(xi, w_ref[...])

    r = branch(0, wr_ref)
    k = branch(1, wk_ref)
    v = branch(2, wv_ref)
    wl = branch(3, wwl_ref)
    al = branch(4, wal_ref)
    gl = branch(5, wgl_ref)
    qkv = _dot(h.astype(BF16), wqkv_ref[...])
    q_o[...] = qkv[:, :ATTN_WIDTH]
    kat_o[...] = qkv[:, ATTN_WIDTH:ATTN_WIDTH + KV_WIDTH]
    vat_o[...] = qkv[:, ATTN_WIDTH + KV_WIDTH:]

    z = w0_ref[...] + _dot(jnp.tanh(wl).astype(BF16), w2_ref[...])
    w_raw = -jnp.logaddexp(-z, 0.0) - 0.5
    lw_o[...] = -jnp.exp(w_raw)
    a = jax.nn.sigmoid(a0_ref[...] + _dot(al.astype(BF16), a2_ref[...]))
    g_o[...] = _dot(jax.nn.sigmoid(gl).astype(BF16), g2_ref[...])
    kk = k * kk_ref[...]
    kk = kk / jnp.maximum(jnp.sqrt(_seg_sum64(kk * kk)), L2_EPS)
    kf = k * (1.0 + (a - 1.0) * ka_ref[...])
    r_o[...] = r
    kf_o[...] = kf
    v_o[...] = v
    na_o[...] = -kk
    b_o[...] = kk * a
    bonus_o[...] = _seg_sum64(r * kf * rk_ref[...]) * v


def _in_proj(is_prompt, tm, x, prev, mod, gpm, mu8, wts, rw):
    m, d = x.shape
    grid = (m // tm,)
    row = lambda i: (i, 0)
    if is_prompt:
        prev_spec = pl.BlockSpec((8, d), lambda i: (jnp.maximum(i * (tm // 8) - 1, 0), 0))
        mod_rows = 8
        mod_map = lambda c: (lambda i: (0, c))
        h_shape, h_spec = (8, d), pl.BlockSpec((8, d), lambda i: (0, 0))
    else:
        prev_spec = pl.BlockSpec((tm, d), row)
        mod_rows = tm
        mod_map = lambda c: (lambda i: (i, c))
        h_shape, h_spec = (m, d), pl.BlockSpec((tm, d), row)
    in_specs = [pl.BlockSpec((tm, d), row), prev_spec,
                pl.BlockSpec((mod_rows, d), mod_map(0)), pl.BlockSpec((mod_rows, d), mod_map(1)),
                _resident((1, d)), _resident((8, d))]
    in_specs += [_resident(w.shape) for w in wts]
    in_specs += [_resident(p.shape) for p in rw]
    wide = jax.ShapeDtypeStruct((m, RWKV_WIDTH), F32)
    wide_spec = pl.BlockSpec((tm, RWKV_WIDTH), row)
    kv = jax.ShapeDtypeStruct((m, KV_WIDTH), F32)
    kv_spec = pl.BlockSpec((tm, KV_WIDTH), row)
    out_shape = [wide] * 9 + [kv, kv, jax.ShapeDtypeStruct(h_shape, F32)]
    out_specs = [wide_spec] * 9 + [kv_spec, kv_spec, h_spec]
    return pl.pallas_call(
        functools.partial(_inproj_kernel, is_prompt, tm),
        grid=grid, in_specs=in_specs, out_specs=out_specs, out_shape=out_shape,
        compiler_params=pltpu.CompilerParams(dimension_semantics=("arbitrary",), vmem_limit_bytes=VMEM_LIMIT),
        name="in_proj_prompt" if is_prompt else "in_proj_sample",
    )(x, prev, mod, mod, gpm, mu8, *wts, *rw)


def _rwkv_chunk_kernel(r_ref, lw_ref, k_ref, v_ref, a_ref, b_ref, y_ref, s_ref, st_ref):
    t = pl.program_id(1)
    C = CHUNK
    P = 2 * HEAD_DIM

    @pl.when(t == 0)
    def _():
        st_ref[...] = jnp.zeros_like(st_ref)

    ri = lax.broadcasted_iota(jnp.int32, (P, P), 0)
    ci = lax.broadcasted_iota(jnp.int32, (P, P), 1)
    bd = (ri // C) == (ci // C)
    tril_s = bd & ((ri % C) > (ci % C))
    tril_i = bd & ((ri % C) >= (ci % C))
    eye = jnp.where(ri == ci, 1.0, 0.0)
    lane0 = lax.broadcasted_iota(jnp.int32, (C, P), 1) < HEAD_DIM
    trow = lax.broadcasted_iota(jnp.int32, (C, P), 0)

    def stack(x):
        return jnp.concatenate([jnp.where(lane0, x, 0.0), jnp.where(lane0, 0.0, x)], axis=0)

    def dup(x):
        return jnp.concatenate([x, x], axis=0)

    def prep(sl, pp):
        ln = slice(P * pp, P * (pp + 1))
        lw = lw_ref[sl, ln]
        cw = lw
        for s in (1, 2, 4, 8, 16, 32):
            cw = cw + jnp.where(trow >= s, pltpu.roll(cw, s, axis=0), 0.0)
        cw_last = cw[C - 1:C, :]
        e_neg = jnp.exp(-cw)
        e_end = jnp.exp(cw_last - cw)
        k = k_ref[sl, ln]
        v = v_ref[sl, ln]
        b = b_ref[sl, ln]
        a2 = stack(a_ref[sl, ln] * jnp.exp(cw - lw))
        r2 = stack(r_ref[sl, ln] * jnp.exp(cw))
        return dict(
            v2=stack(v),
            lhs=jnp.concatenate([a2, r2], axis=0).astype(BF16),
            rhs=jnp.concatenate([dup(k * e_neg), dup(b * e_neg)], axis=0).astype(BF16),
            kbh=jnp.concatenate([dup(b * e_end), dup(k * e_end)], axis=0).astype(BF16),
            decay=jnp.exp(cw_last))

    def chunk(c, carry):
        sl = pl.ds(pl.multiple_of(c * C, C), C)
        pairs = range(RWKV_PAIRS)
        st = [st_ref[pp] for pp in pairs]
        d = [prep(sl, pp) for pp in pairs]
        gram = [_dot_nt(d[pp]['lhs'], d[pp]['rhs']) for pp in pairs]
        l2 = [jnp.where(tril_s, gram[pp][0:P, P:2 * P], 0.0) for pp in pairs]
        a_s = [_dot_nt(d[pp]['lhs'], st[pp].astype(BF16)) for pp in pairs]
        v2b = [d[pp]['v2'].astype(BF16) for pp in pairs]
        rhs_u = [a_s[pp][0:P] + _dot(jnp.where(tril_s, gram[pp][0:P, 0:P], 0.0).astype(BF16), v2b[pp])
                 for pp in pairs]
        inv = [eye + l2[pp] for pp in pairs]
        lp = l2
        for _ in range(5):
            lpb = [lp[pp].astype(BF16) for pp in pairs]
            lp = [_dot(lpb[pp], lpb[pp]) for pp in pairs]
            inv = [inv[pp] + _dot(lp[pp].astype(BF16), inv[pp].astype(BF16)) for pp in pairs]
        u2 = [_dot(inv[pp].astype(BF16), rhs_u[pp].astype(BF16)) for pp in pairs]
        p_cat = [jnp.concatenate([jnp.where(tril_i, gram[pp][P:2 * P, P:2 * P], 0.0),
                                  jnp.where(tril_i, gram[pp][P:2 * P, 0:P], 0.0)], axis=1).astype(BF16)
                 for pp in pairs]
        y2 = [a_s[pp][P:2 * P] + _dot(p_cat[pp], jnp.concatenate([u2[pp].astype(BF16), v2b[pp]], axis=0))
              for pp in pairs]
        uvt = [jnp.concatenate([u2[pp], d[pp]['v2']], axis=0).T.astype(BF16) for pp in pairs]
        st_new = [jnp.where(bd, st[pp] * d[pp]['decay'] + _dot(uvt[pp], d[pp]['kbh']), 0.0) for pp in pairs]
        for pp in pairs:
            y_ref[sl, P * pp:P * (pp + 1)] = y2[pp][0:C] + y2[pp][C:2 * C]
            st_ref[pp] = st_new[pp]
        return carry

    lax.fori_loop(0, RWKV_ROWS // C, chunk, 0)

    @pl.when(t == pl.num_programs(1) - 1)
    def _():
        s_ref[...] = st_ref[...]


def _rwkv_prompt(r, lw, kf, v, na, b):
    t = r.shape[0]
    n_pairs = RWKV_WIDTH // LANES
    width = RWKV_PAIRS * LANES
    spec = pl.BlockSpec((RWKV_ROWS, width), lambda p, i: (i, p))
    return pl.pallas_call(
        _rwkv_chunk_kernel,
        grid=(n_pairs // RWKV_PAIRS, t // RWKV_ROWS),
        in_specs=[spec] * 6,
        out_specs=[spec, pl.BlockSpec((RWKV_PAIRS, LANES, LANES), lambda p, i: (p, 0, 0))],
        out_shape=[jax.ShapeDtypeStruct((t, RWKV_WIDTH), F32),
                   jax.ShapeDtypeStruct((n_pairs, LANES, LANES), F32)],
        scratch_shapes=[pltpu.VMEM((RWKV_PAIRS, LANES, LANES), F32)],
        compiler_params=pltpu.CompilerParams(dimension_semantics=("arbitrary", "arbitrary"),
                                             vmem_limit_bytes=VMEM_LIMIT),
        name="rwkv_chunked",
    )(r, lw, kf, v, na, b)


def _rwkv_step_kernel(r_ref, lw_ref, k_ref, v_ref, a_ref, b_ref, s_ref, y_ref, so_ref):
    n = HEAD_DIM
    tr = lambda ref: ref[...].T
    rt, kt, vt, at, bt = tr(r_ref), tr(k_ref), tr(v_ref), tr(a_ref), tr(b_ref)
    dt = jnp.exp(tr(lw_ref))
    y_rows = []
    for h in range(2):
        hs = slice(h * n, (h + 1) * n)
        r_h, k_h, a_h, b_h, d_h = rt[hs], kt[hs], at[hs], bt[hs], dt[hs]
        for v in range(n):
            s = s_ref[h, v]
            sa = jnp.sum(s * a_h, axis=0, keepdims=True)
            s2 = s * d_h + sa * b_h + vt[h * n + v:h * n + v + 1] * k_h
            so_ref[h, v] = s2
            y_rows.append(jnp.sum(s2 * r_h, axis=0, keepdims=True))
    y_ref[...] = jnp.concatenate(y_rows, axis=0).T


def _rwkv_sample(r, lw, kf, v, na, b, state):
    nb = r.shape[0]
    assert nb == LANES, "the step kernel puts the whole batch on the lane axis"
    state_t = jnp.transpose(state, (1, 2, 3, 0))
    vspec = pl.BlockSpec((nb, LANES), lambda p: (0, p))
    sspec = pl.BlockSpec((2, HEAD_DIM, HEAD_DIM, nb), lambda p: (p, 0, 0, 0))
    y, state_new = pl.pallas_call(
        _rwkv_step_kernel,
        grid=(N_RWKV_HEADS // 2,),
        in_specs=[vspec] * 6 + [sspec],
        out_specs=[vspec, sspec],
        out_shape=[jax.ShapeDtypeStruct((nb, RWKV_WIDTH), F32), jax.ShapeDtypeStruct(state_t.shape, F32)],
        compiler_params=pltpu.CompilerParams(dimension_semantics=("arbitrary",)),
        name="rwkv_step",
    )(r, lw, kf, v, na, b, state_t)
    return y, jnp.transpose(state_new, (3, 0, 1, 2))


def _alibi_slope(head):
    return 2.0 ** (-8.0 * (head + 1) / N_Q_HEADS)


def _swa_prompt_kernel(q_ref, kc_ref, kp_ref, vc_ref, vp_ref, sink_ref, o_ref):
    n = pl.program_id(0)
    w = WINDOW
    kcat = jnp.concatenate([kp_ref[...], kc_ref[...]], axis=0)
    vcat = jnp.concatenate([vp_ref[...], vc_ref[...]], axis=0)
    lane_k = lax.broadcasted_iota(jnp.int32, kcat.shape, 1) < HEAD_DIM
    kswap = pltpu.roll(kcat, HEAD_DIM, axis=1)
    vswap = pltpu.roll(vcat, HEAD_DIM, axis=1)
    kdup = [jnp.where(lane_k, kcat, kswap).astype(BF16), jnp.where(lane_k, kswap, kcat).astype(BF16)]
    vdup = [jnp.where(lane_k, vcat, vswap).astype(BF16), jnp.where(lane_k, vswap, vcat).astype(BF16)]
    qi = lax.broadcasted_iota(jnp.int32, (w, 2 * w), 0)
    kj = lax.broadcasted_iota(jnp.int32, (w, 2 * w), 1)
    dist = qi + w - kj
    valid = (dist >= 0) & (dist <= w) & ((n > 0) | (kj >= w))
    distf = dist.astype(F32)
    lane_q = lax.broadcasted_iota(jnp.int32, (w, LANES), 1) < HEAD_DIM
    ones = jnp.ones((2 * w, LANES), BF16)
    sinks = sink_ref[...]
    group = 4
    for j0 in range(0, N_Q_HEADS // 2, group):
        js = range(j0, j0 + group)
        heads = [2 * j + half for j in js for half in range(2)]
        kvh = {h: h // Q_PER_KV for h in heads}
        q2 = {}
        for j in js:
            qp = q_ref[:, LANES * j:LANES * (j + 1)] * (1.0 / math.sqrt(HEAD_DIM))
            q2[j] = jnp.concatenate([jnp.where(lane_q, qp, 0.0), jnp.where(lane_q, 0.0, qp)], axis=0).astype(BF16)
        s2 = {j: _dot_nt(q2[j], kdup[kvh[2 * j]]) for j in js}
        s = {h: jnp.where(valid, s2[h // 2][w * (h % 2):w * (h % 2 + 1)] - _alibi_slope(h) * distf, -jnp.inf)
             for h in heads}
        sink = {h: sinks[0:1, h:h + 1] for h in heads}
        m = {h: jnp.broadcast_to(jnp.maximum(jnp.max(s[h], axis=-1, keepdims=True), sink[h]), (w, LANES))
             for h in heads}
        p = {h: jnp.exp(s[h] - jnp.concatenate([m[h], m[h]], axis=1)).astype(BF16) for h in heads}
        den = {h: _dot(p[h], ones) + jnp.exp(sink[h] - m[h]) for h in heads}
        o = {h: _dot(p[h], vdup[kvh[h]]) / den[h] for h in heads}
        for j in js:
            o_ref[:, LANES * j:LANES * (j + 1)] = jnp.where(lane_q, o[2 * j], o[2 * j + 1])


def _swa_prompt(q, ka, va, sinks):
    t = q.shape[0]
    w = WINDOW
    cur = lambda n: (n, 0)
    prv = lambda n: (jnp.maximum(n - 1, 0), 0)
    kvs = lambda f: pl.BlockSpec((w, KV_WIDTH), f)
    return pl.pallas_call(
        _swa_prompt_kernel,
        grid=(t // w,),
        in_specs=[pl.BlockSpec((w, ATTN_WIDTH), cur), kvs(cur), kvs(prv), kvs(cur), kvs(prv),
                  pl.BlockSpec((1, N_Q_HEADS), lambda n: (0, 0))],
        out_specs=pl.BlockSpec((w, ATTN_WIDTH), cur),
        out_shape=jax.ShapeDtypeStruct((t, ATTN_WIDTH), F32),
        compiler_params=pltpu.CompilerParams(dimension_semantics=("arbitrary",)),
        name="swa_prompt",
    )(q, ka, ka, va, va, sinks)


def _swa_sample_kernel(q_ref, kn_ref, vn_ref, ck_ref, cv_ref, sink_ref, o_ref, ko_ref, vo_ref):
    wb = ck_ref.shape[1]
    q = q_ref[...] * (1.0 / math.sqrt(HEAD_DIM))
    q2 = jnp.concatenate([q, q], axis=2)
    rowh = lax.broadcasted_iota(jnp.int32, q2.shape, 1) // Q_PER_KV
    laneh = lax.broadcasted_iota(jnp.int32, q2.shape, 2) // HEAD_DIM
    qb = jnp.where(rowh == laneh, q2, 0.0)
    kn = kn_ref[...]
    vn = vn_ref[...]
    ck = ck_ref[...]
    cv = cv_ref[...]
    s = jnp.einsum('bqd,bkd->bqk', qb.astype(BF16), ck.astype(BF16), preferred_element_type=F32)
    s_self = jnp.sum(qb * kn, axis=2, keepdims=True)
    head = lax.broadcasted_iota(jnp.int32, (1, N_Q_HEADS, 1), 1).astype(F32)
    slope = jnp.exp2(-8.0 * (head + 1.0) / N_Q_HEADS)
    dist = (wb - lax.broadcasted_iota(jnp.int32, (1, 1, wb), 2)).astype(F32)
    s = s - slope * dist
    sink = sink_ref[...][None]
    m = jnp.maximum(jnp.maximum(jnp.max(s, axis=2, keepdims=True), s_self), sink)
    p = jnp.exp(s - m)
    p_self = jnp.exp(s_self - m)
    den = jnp.sum(p, axis=2, keepdims=True) + p_self + jnp.exp(sink - m)
    o = jnp.einsum('bqk,bkd->bqd', p.astype(BF16), cv.astype(BF16), preferred_element_type=F32)
    o = (o + p_self * vn) / den
    sel = lax.broadcasted_iota(jnp.int32, (1, N_Q_HEADS, HEAD_DIM), 1) < Q_PER_KV
    o_ref[...] = jnp.where(sel, o[:, :, :HEAD_DIM], o[:, :, HEAD_DIM:])
    ko_ref[:, 0:wb - 1, :] = ck_ref[:, 1:wb, :]
    ko_ref[:, wb - 1:wb, :] = kn
    vo_ref[:, 0:wb - 1, :] = cv_ref[:, 1:wb, :]
    vo_ref[:, wb - 1:wb, :] = vn


def _swa_sample(q, ka, va, cache_k, cache_v, sinks_col):
    nb, wb = cache_k.shape[0], cache_k.shape[1]
    bb = 16
    b3 = lambda i: (i, 0, 0)
    nspec = pl.BlockSpec((bb, 1, KV_WIDTH), b3)
    cspec = pl.BlockSpec((bb, wb, KV_WIDTH), b3)
    qspec = pl.BlockSpec((bb, N_Q_HEADS, HEAD_DIM), b3)
    return pl.pallas_call(
        _swa_sample_kernel,
        grid=(nb // bb,),
        in_specs=[qspec, nspec, nspec, cspec, cspec, pl.BlockSpec((N_Q_HEADS, 1), lambda i: (0, 0))],
        out_specs=[qspec, cspec, cspec],
        out_shape=[jax.ShapeDtypeStruct((nb, N_Q_HEADS, HEAD_DIM), F32),
                   jax.ShapeDtypeStruct(cache_k.shape, F32), jax.ShapeDtypeStruct(cache_v.shape, F32)],
        compiler_params=pltpu.CompilerParams(dimension_semantics=("arbitrary",)),
        name="swa_sample",
    )(q.reshape(nb, N_Q_HEADS, HEAD_DIM), ka.reshape(nb, 1, KV_WIDTH), va.reshape(nb, 1, KV_WIDTH),
      cache_k, cache_v, sinks_col)


def _post_mix_kernel(is_prompt, n_main, n_tail, *refs):
    if n_tail:
        tail_ref, refs = refs[16], refs[:16] + refs[17:]
    i = pl.program_id(0)

    @pl.when(i < n_main)
    def _():
        _post_mix_body(is_prompt, *refs)

    if n_tail:
        h2_o = refs[17]

        @pl.when(i == n_main)
        def _():
            h2_o[0:n_tail, :] = tail_ref[...]
            h2_o[n_tail:, :] = jnp.zeros((h2_o.shape[0] - n_tail, h2_o.shape[1]), F32)


def _post_mix_body(is_prompt, yr_ref, g_ref, bonus_ref, ya_ref, x_ref, gt1_ref, sh2_ref, sc2_ref,
                   lnw_ref, lnb_ref, wout_ref, gpost_ref, gpre_ref, rw_ref, rb_ref, cnt_ref,
                   x1_o, h2_o, ti_o, tw_o, rank_o, cnt_o, run_ref):
    @pl.when(pl.program_id(0) == 0)
    def _():
        run_ref[...] = cnt_ref[...]

    gt1 = gt1_ref[...]
    sh2 = sh2_ref[...]
    sc2 = sc2_ref[...]
    if is_prompt:
        gt1, sh2, sc2 = gt1[0:1], sh2[0:1], sc2[0:1]
    y = yr_ref[...]
    mean = _seg_sum64(y) * (1.0 / HEAD_DIM)
    dlt = y - mean
    var = _seg_sum64(dlt * dlt) * (1.0 / HEAD_DIM)
    yn = dlt * lax.rsqrt(var + GN_EPS) * lnw_ref[...] + lnb_ref[...]
    yr = (yn + bonus_ref[...]) * g_ref[...]
    mix = _dot(jnp.concatenate([yr, ya_ref[...]], axis=1).astype(BF16), wout_ref[...])
    x1 = x_ref[...] + gt1 * _rmsnorm(mix, gpost_ref[...])
    x1_o[...] = x1
    h2 = _rmsnorm(x1, gpre_ref[...]) * (1.0 + sc2) + sh2
    h2_o[...] = h2
    logits = _dot3(h2, rw_ref[...]) + rb_ref[...]
    lane = lax.broadcasted_iota(jnp.int32, logits.shape, 1)
    vals, idxs = [], []
    for _ in range(TOP_K):
        m = jnp.max(logits, axis=1, keepdims=True)
        idx = jnp.min(jnp.where(logits == m, lane, N_EXPERTS), axis=1, keepdims=True)
        vals.append(m)
        idxs.append(idx)
        logits = jnp.where(lane == idx, -jnp.inf, logits)
    e = jnp.exp(jnp.concatenate(vals, axis=1) - vals[0])
    tw_o[...] = e / jnp.sum(e, axis=1, keepdims=True)
    ti_o[...] = jnp.concatenate(idxs, axis=1)

    tm = logits.shape[0]
    lane_e = lax.broadcasted_iota(jnp.int32, (tm, LANES), 1)
    tri = jnp.where(lax.broadcasted_iota(jnp.int32, (tm, tm), 0) > lax.broadcasted_iota(jnp.int32, (tm, tm), 1),
                    1.0, 0.0).astype(BF16)
    run = run_ref[...]
    ranks = []
    for idx in idxs:
        onehot = jnp.where(lane_e == idx, 1.0, 0.0)
        before = _dot(tri, onehot.astype(BF16)) + run
        ranks.append(jnp.sum(onehot * before, axis=1, keepdims=True))
        run = run + jnp.sum(onehot, axis=0, keepdims=True)
    run_ref[...] = run
    cnt_o[...] = run
    rank_o[...] = jnp.concatenate(ranks, axis=1).astype(jnp.int32)


def _post_mix(is_prompt, tm, yr, g, bonus, ya, x, mod, lnw, lnb, wout, gpost, gpre, rw, rb, cnt, h2_tail=None):
    m, d = x.shape
    n_main = m // tm
    n_tail = 0 if h2_tail is None else h2_tail.shape[0]
    assert n_tail <= tm
    row = lambda i: (jnp.minimum(i, n_main - 1), 0)
    if is_prompt:
        mod_rows = 8
        mod_map = lambda c: (lambda i: (0, c))
    else:
        mod_rows = tm
        mod_map = lambda c: (lambda i: (jnp.minimum(i, n_main - 1), c))
    wide = pl.BlockSpec((tm, RWKV_WIDTH), row)
    in_specs = [wide, wide, wide, wide, pl.BlockSpec((tm, d), row),
                pl.BlockSpec((mod_rows, d), mod_map(2)), pl.BlockSpec((mod_rows, d), mod_map(3)),
                pl.BlockSpec((mod_rows, d), mod_map(4)),
                _resident(lnw.shape), _resident(lnb.shape), _resident(wout.shape), _resident(gpost.shape),
                _resident(gpre.shape), _resident(rw.shape), _resident(rb.shape), _resident(cnt.shape)]
    args = [yr, g, bonus, ya, x, mod, mod, mod, lnw, lnb, wout, gpost, gpre, rw, rb, cnt]
    if n_tail:
        in_specs.append(_resident(h2_tail.shape))
        args.append(h2_tail)
    out_shape = [jax.ShapeDtypeStruct((m, d), F32), jax.ShapeDtypeStruct((m + n_tail, d), F32),
                 jax.ShapeDtypeStruct((m, TOP_K), jnp.int32), jax.ShapeDtypeStruct((m, TOP_K), F32),
                 jax.ShapeDtypeStruct((m, TOP_K), jnp.int32), jax.ShapeDtypeStruct((1, LANES), F32)]
    out_specs = [pl.BlockSpec((tm, d), row), pl.BlockSpec((tm, d), lambda i: (i, 0)),
                 pl.BlockSpec((tm, TOP_K), row), pl.BlockSpec((tm, TOP_K), row),
                 pl.BlockSpec((tm, TOP_K), row), pl.BlockSpec((1, LANES), lambda i: (0, 0))]
    return pl.pallas_call(
        functools.partial(_post_mix_kernel, is_prompt, n_main, n_tail),
        grid=(n_main + (1 if n_tail else 0),), in_specs=in_specs, out_specs=out_specs, out_shape=out_shape,
        scratch_shapes=[pltpu.VMEM((1, LANES), F32)],
        compiler_params=pltpu.CompilerParams(dimension_semantics=("arbitrary",), vmem_limit_bytes=VMEM_LIMIT),
        name="post_mix_prompt" if is_prompt else "post_mix_sample",
    )(*args)


def _moe_kernel(ex_ref, rb_ref, ns_ref, cb_ref, tok_ref, h2_hbm, w1_ref, b1g_ref, b1l_ref, w2_ref, b2_ref, ys_hbm,
                xbuf, xb, acc, w1p, w2b, sem_in, sem_out):
    s = pl.program_id(0)
    f = pl.program_id(1)
    n_super = pl.num_programs(0)
    ns = ns_ref[s]
    rb = rb_ref[s]
    rows = MOE_ROWS
    grp = 2 * LANES
    share = rows // MOE_NF

    def gather_share(sup, col):
        base = cb_ref[sup] + col * share
        g_col = col * (share // SUBLANES)
        for j in range(MOE_SUB):
            for u in range(share):
                tok = tok_ref[base + (j * rows + u)]
                pltpu.make_async_copy(h2_hbm.at[tok >> 3, pl.ds(tok & (SUBLANES - 1), 1)],
                                      xbuf.at[g_col + (j * (rows // SUBLANES) + u // SUBLANES),
                                              pl.ds(u % SUBLANES, 1)], sem_in).start()

    def gather_wait():
        pltpu.make_async_copy(h2_hbm.at[pl.ds(0, xbuf.shape[0])], xbuf, sem_in).wait()

    def y_copy(r0, n):
        return pltpu.make_async_copy(acc.at[pl.ds(r0, n)], ys_hbm.at[pl.ds(rb * rows + r0, n)], sem_out)

    def for_tiles(fn):
        def body(j, c):
            fn(j)
            return c
        lax.fori_loop(0, ns, body, 0)

    @pl.when(ns > 0)
    def _():
        @pl.when(f == 0)
        def _():
            @pl.when(s == 0)
            def _():
                def first(col, c):
                    gather_share(0, col)
                    return c
                lax.fori_loop(0, MOE_NF, first, 0)
            b2 = jnp.broadcast_to(b2_ref[0], (rows, D_MODEL))

            def init(j):
                acc[pl.ds(pl.multiple_of(j * rows, rows), rows), :] = b2
            for_tiles(init)

            gather_wait()

            def cast(j):
                n8 = rows // SUBLANES
                x = xbuf[pl.ds(pl.multiple_of(j * n8, n8), n8)].reshape(rows, D_MODEL)
                xb[pl.ds(pl.multiple_of(j * rows, rows), rows), :] = x.astype(BF16)
            for_tiles(cast)

        n_grp = 2 * MOE_FT // grp

        def prep_weights():
            pr = lax.broadcasted_iota(jnp.int32, (grp, grp), 0)
            pc = lax.broadcasted_iota(jnp.int32, (grp, grp), 1)
            perm = jnp.where(pr == jnp.where(pc < LANES, 2 * pc, 2 * (pc - LANES) + 1), 1.0, 0.0).astype(BF16)
            for g in range(n_grp):
                gs = slice(grp * g, grp * (g + 1))
                w1p[:, gs] = _dot(w1_ref[0, :, gs].astype(BF16), perm).astype(BF16)
            w2b[...] = w2_ref[0].astype(BF16)

        nxt = jnp.minimum(s + 1, n_super - 1)
        next_live = (s + 1 < n_super) & (ns_ref[nxt] > 0)

        @pl.when(next_live)
        def _():
            gather_share(nxt, f)
            prep_weights()

        @pl.when(jnp.logical_not(next_live))
        def _():
            prep_weights()

        b1g = b1g_ref[0]
        b1l = b1l_ref[0]

        last_col = f == pl.num_programs(1) - 1

        def swiglu(hh):
            glu = jnp.concatenate([hh[:, grp * g:grp * g + LANES] for g in range(n_grp)], axis=1) + b1g
            lin = jnp.concatenate([hh[:, grp * g + LANES:grp * (g + 1)] for g in range(n_grp)], axis=1) + b1l
            glu = jnp.minimum(glu, SWIGLU_LIMIT)
            lin = jnp.clip(lin, -SWIGLU_LIMIT, SWIGLU_LIMIT)
            return (glu * jax.nn.sigmoid(SWIGLU_ALPHA * glu) * (lin + 1.0)).astype(BF16)

        def tiles(starts, n):
            sls = [pl.ds(pl.multiple_of(r0, rows), n) for r0 in starts]
            hh = [_dot(xb[sl, :], w1p[...]) for sl in sls]
            act = [swiglu(h) for h in hh]
            out = [_dot(a, w2b[...]) for a in act]
            for sl, o in zip(sls, out):
                acc[sl, :] += o

            @pl.when(last_col)
            def _():
                for r0 in starts:
                    y_copy(r0, n).start()

        big = 2 * rows

        def tile_pair(q, c):
            tiles([2 * big * q, 2 * big * q + big], big)
            return c
        lax.fori_loop(0, ns // 4, tile_pair, 0)

        @pl.when(ns % 4 >= 2)
        def _():
            tiles([(ns // 4) * 2 * big], big)

        @pl.when(ns % 2 == 1)
        def _():
            tiles([(ns - 1) * rows], rows)

        @pl.when(last_col)
        def _():
            for_tiles(lambda j: y_copy(j * rows, rows).wait())

    @pl.when((ns < 0) & (f == 0))
    def _():
        acc[0:rows, :] = jnp.zeros((rows, D_MODEL), F32)

        def z_copy(j):
            return pltpu.make_async_copy(acc.at[pl.ds(0, rows)], ys_hbm.at[pl.ds((rb + j) * rows, rows)], sem_out)

        def body(j, c, op):
            op(z_copy(j))
            return c
        lax.fori_loop(0, -ns, functools.partial(body, op=lambda cp: cp.start()), 0)
        lax.fori_loop(0, -ns, functools.partial(body, op=lambda cp: cp.wait()), 0)


def _moe_experts(n_rows, ex, rb, ns, cb, tok_sorted, h2, w1, b1g, b1l, w2, b2):
    n_super = ex.shape[0]
    nf = MOE_NF
    last = nf - 1
    sub_rows = MOE_SUB * MOE_ROWS

    def fcol(s, f, ns_):
        return jnp.where(ns_[s] > 0, f, last)

    grid_spec = pltpu.PrefetchScalarGridSpec(
        num_scalar_prefetch=5,
        grid=(n_super, nf),
        in_specs=[pl.BlockSpec(memory_space=pl.ANY),
                  pl.BlockSpec((1, D_MODEL, 2 * MOE_FT), lambda s, f, e_, r_, n_, c_, t_: (e_[s], 0, fcol(s, f, n_))),
                  pl.BlockSpec((1, 1, MOE_FT), lambda s, f, e_, r_, n_, c_, t_: (e_[s], 0, fcol(s, f, n_))),
                  pl.BlockSpec((1, 1, MOE_FT), lambda s, f, e_, r_, n_, c_, t_: (e_[s], 0, fcol(s, f, n_))),
                  pl.BlockSpec((1, MOE_FT, D_MODEL), lambda s, f, e_, r_, n_, c_, t_: (e_[s], fcol(s, f, n_), 0)),
                  pl.BlockSpec((1, 1, D_MODEL), lambda s, f, e_, r_, n_, c_, t_: (e_[s], 0, 0))],
        out_specs=pl.BlockSpec(memory_space=pl.ANY),
        scratch_shapes=[pltpu.VMEM((sub_rows // SUBLANES, SUBLANES, D_MODEL), F32),
                        pltpu.VMEM((sub_rows, D_MODEL), BF16),
                        pltpu.VMEM((sub_rows, D_MODEL), F32),
                        pltpu.VMEM((D_MODEL, 2 * MOE_FT), BF16),
                        pltpu.VMEM((MOE_FT, D_MODEL), BF16),
                        pltpu.SemaphoreType.DMA, pltpu.SemaphoreType.DMA])
    return pl.pallas_call(
        _moe_kernel,
        grid_spec=grid_spec,
        out_shape=jax.ShapeDtypeStruct((n_rows, D_MODEL), F32),
        compiler_params=pltpu.CompilerParams(dimension_semantics=("arbitrary", "arbitrary"),
                                             vmem_limit_bytes=VMEM_LIMIT),
        name="moe_experts",
    )(ex, rb, ns, cb, tok_sorted, h2.reshape(-1, SUBLANES, D_MODEL), w1, b1g, b1l, w2, b2)


def _combine_kernel(is_prompt, idx_ref, nidx_ref, ys_hbm, x1_ref, tw_ref, gt2_ref, gpost_ref, o_ref, buf, sem):
    tk = COMBINE_TOK
    n = TOP_K * tk
    unroll = 16
    i = pl.program_id(0)
    slot = i % 2

    def start_row(ref, buf_slot, j, g, sub):
        p = ref[0, 0, j]
        pltpu.make_async_copy(ys_hbm.at[p >> 3, pl.ds(p & (SUBLANES - 1), 1)],
                              buf.at[buf_slot, g, pl.ds(sub, 1)], sem.at[buf_slot]).start(priority=sub % 2)

    @pl.when(i == 0)
    def _():
        def body(q, c):
            for u in range(unroll):
                start_row(idx_ref, 0, q * unroll + u, q * (unroll // SUBLANES) + u // SUBLANES, u % SUBLANES)
            return c
        lax.fori_loop(0, n // unroll, body, 0)

    pltpu.make_async_copy(ys_hbm.at[pl.ds(0, n // SUBLANES)], buf.at[slot], sem.at[slot]).wait()

    def combine():
        tw = tw_ref[...]
        t8 = tk // SUBLANES
        rows_of = lambda k: buf[slot, k * t8:(k + 1) * t8].reshape(tk, D_MODEL)
        f = tw[:, 0:1] * rows_of(0)
        for k in range(1, TOP_K):
            f = f + tw[:, k:k + 1] * rows_of(k)
        gt2 = gt2_ref[...]
        if is_prompt:
            gt2 = gt2[0:1]
        o_ref[...] = x1_ref[...] + gt2 * _rmsnorm(f, gpost_ref[...])

    @pl.when(i + 1 < pl.num_programs(0))
    def _():
        for j in range(n):
            start_row(nidx_ref, 1 - slot, j, j // SUBLANES, j % SUBLANES)
        combine()

    @pl.when(i + 1 >= pl.num_programs(0))
    def _():
        combine()


def _combine(is_prompt, pos, ys, x1, tw, mod, gpost):
    m, d = x1.shape
    tk = COMBINE_TOK
    nblk = m // tk
    idx = pos.reshape(nblk, tk, TOP_K).transpose(0, 2, 1).reshape(nblk, 1, TOP_K * tk)
    row = lambda i: (i, 0)
    mod_spec = (pl.BlockSpec((8, d), lambda i: (0, 5)) if is_prompt else pl.BlockSpec((tk, d), lambda i: (i, 5)))
    return pl.pallas_call(
        functools.partial(_combine_kernel, is_prompt),
        grid=(nblk,),
        in_specs=[pl.BlockSpec((1, 1, TOP_K * tk), lambda i: (i, 0, 0), memory_space=pltpu.SMEM),
                  pl.BlockSpec((1, 1, TOP_K * tk), lambda i: (jnp.minimum(i + 1, nblk - 1), 0, 0),
                               memory_space=pltpu.SMEM),
                  pl.BlockSpec(memory_space=pl.ANY),
                  pl.BlockSpec((tk, d), row), pl.BlockSpec((tk, TOP_K), row), mod_spec,
                  pl.BlockSpec((1, d), lambda i: (0, 0))],
        out_specs=pl.BlockSpec((tk, d), row),
        out_shape=jax.ShapeDtypeStruct((m, d), F32),
        scratch_shapes=[pltpu.VMEM((2, TOP_K * tk // SUBLANES, SUBLANES, d), F32), pltpu.SemaphoreType.DMA((2,))],
        compiler_params=pltpu.CompilerParams(dimension_semantics=("arbitrary",)),
        name="moe_combine_prompt" if is_prompt else "moe_combine_sample",
    )(idx, idx, ys.reshape(-1, SUBLANES, d), x1, tw, mod, gpost)


def _routing_tables(top_i, rank, counts):
    n_assign = top_i.size
    e_flat = top_i.reshape(-1)
    tiles = (counts + MOE_ROWS - 1) // MOE_ROWS
    tile_start = jnp.cumsum(tiles) - tiles
    group_start = jnp.cumsum(counts) - counts
    pos = tile_start[e_flat] * MOE_ROWS + rank.reshape(-1)
    n_rows = _padded_rows(n_assign)
    assert n_assign <= (1 << 16) and n_rows <= (1 << 16)
    assert n_rows >= n_assign + MOE_SUB * MOE_ROWS
    key = pos.astype(jnp.uint32) * jnp.uint32(1 << 16) + jnp.arange(n_assign, dtype=jnp.uint32)
    tok_sorted = (jnp.sort(key) & jnp.uint32(0xFFFF)).astype(jnp.int32) // TOP_K
    tok_sorted = jnp.pad(tok_sorted, (0, n_rows - n_assign))
    supers = (tiles + MOE_SUB - 1) // MOE_SUB
    super_end = jnp.cumsum(supers)
    s_idx = jnp.arange(_max_supers(n_assign), dtype=jnp.int32)
    ex = jnp.minimum(jnp.sum((s_idx[:, None] >= super_end[None, :]).astype(jnp.int32), axis=1),
                     N_EXPERTS - 1).astype(jnp.int32)
    j = s_idx - (super_end - supers)[ex]
    live = s_idx < super_end[-1]
    tail = jnp.sum(tiles) + MOE_SUB * (s_idx - super_end[-1])
    n_clear = jnp.clip(n_rows // MOE_ROWS - tail, 0, MOE_SUB)
    ns = jnp.where(live, jnp.clip(tiles[ex] - MOE_SUB * j, 0, MOE_SUB), -n_clear).astype(jnp.int32)
    rb = jnp.where(live, tile_start[ex] + MOE_SUB * j, tail).astype(jnp.int32)
    cb = jnp.where(live, group_start[ex] + MOE_SUB * MOE_ROWS * j, 0).astype(jnp.int32)
    last_live = jnp.max(jnp.where(live, ex, 0))
    ex = jnp.where(live, ex, last_live).astype(jnp.int32)
    return pos.astype(jnp.int32), tok_sorted, ex, rb, ns, cb


def _padded_rows(n_assign):
    rows = n_assign + N_EXPERTS * (MOE_ROWS - 1)
    return -(-rows // MOE_ROWS) * MOE_ROWS


def _max_supers(n_assign):
    return N_EXPERTS + -(-_padded_rows(n_assign) // (MOE_ROWS * MOE_SUB))


def kernel(x_prompt, x_sample, cache_k, cache_v, state_wkv, state_shift, c_prompt, c_sample, w_ada, b_ada, g_pre_mix, g_post_mix, g_pre_ffn, g_post_ffn, mu_shift, w_in, rwkv_w0, rwkv_w2, rwkv_a0, rwkv_a2, rwkv_g2, rwkv_k_k, rwkv_k_a, rwkv_r_k, rwkv_ln_w, rwkv_ln_b, attn_sinks, w_out, router_w, router_b, moe_w1, moe_b1, moe_w2, moe_b2):
    assert w_ada.shape[0] == 1, "single-layer step"
    d = D_MODEL
    t = x_prompt.shape[1]
    nb = x_sample.shape[0]
    xp = x_prompt.reshape(t, d)
    xs = x_sample.reshape(nb, d)

    c_all = jnp.concatenate([jnp.broadcast_to(c_prompt, (8, d)), c_sample], axis=0)
    mod = _adaln_mod(c_all, w_ada[0], b_ada[0].reshape(1, 6 * d))
    mod_p, mod_s = mod[:8], mod[8:]

    offs = [0, 1024, 2048, 3072, 3136, 3200, 3360]
    w_in0 = w_in[0]
    pad_to = lambda w, n: jnp.pad(w, ((0, 0), (0, n - w.shape[1])))
    wts = [w_in0[:, offs[0]:offs[1]], w_in0[:, offs[1]:offs[2]], w_in0[:, offs[2]:offs[3]],
           pad_to(w_in0[:, offs[3]:offs[4]], 128), pad_to(w_in0[:, offs[4]:offs[5]], 128),
           pad_to(w_in0[:, offs[5]:offs[6]], 256), w_in0[:, offs[6]:]]
    wts = [w.astype(BF16) for w in wts]
    pad_rows = lambda w, n: jnp.pad(w, ((0, n - w.shape[0]), (0, 0))).astype(BF16)
    row = lambda p: p.reshape(1, -1)
    rw = [row(rwkv_w0[0]), pad_rows(rwkv_w2[0], 128), row(rwkv_a0[0]), pad_rows(rwkv_a2[0], 128),
          pad_rows(rwkv_g2[0], 256), row(rwkv_k_k[0]), row(rwkv_k_a[0]), row(rwkv_r_k[0])]
    gpm = row(g_pre_mix[0])
    mu8 = jnp.pad(mu_shift[0], ((0, 8 - N_SHIFTED), (0, 0)))

    pr = _in_proj(True, 256, xp, xp, mod_p, gpm, mu8, wts, rw)
    sr = _in_proj(False, nb, xs, state_shift[0], mod_s, gpm, mu8, wts, rw)
    r_p, lw_p, kf_p, v_p, na_p, b_p, g_p, bonus_p, q_p, ka_p, va_p, hlast_p = pr
    r_s, lw_s, kf_s, v_s, na_s, b_s, g_s, bonus_s, q_s, ka_s, va_s, h_s = sr

    y_p, st_p = _rwkv_prompt(r_p, lw_p, kf_p, v_p, na_p, b_p)
    y_s, wkv_s = _rwkv_sample(r_s, lw_s, kf_s, v_s, na_s, b_s, state_wkv[0])
    y_s = y_s.reshape(nb, RWKV_WIDTH)
    sinks = attn_sinks[0]
    att_p = _swa_prompt(q_p, ka_p, va_p, sinks.reshape(1, N_Q_HEADS))
    wb = cache_k.shape[2]
    att_s, ck_new, cv_new = _swa_sample(q_s, ka_s, va_s, cache_k[0].reshape(nb, wb, KV_WIDTH),
                                        cache_v[0].reshape(nb, wb, KV_WIDTH), sinks.reshape(N_Q_HEADS, 1))
    att_s = att_s.reshape(nb, ATTN_WIDTH)

    post = [row(rwkv_ln_w[0]), row(rwkv_ln_b[0]), w_out[0].astype(BF16), row(g_post_mix[0]),
            row(g_pre_ffn[0]), router_w[0], row(router_b[0])]
    x1_s, h2_s, ti_s, tw_s, rank_s, cnt_s = _post_mix(False, nb, y_s, g_s, bonus_s, att_s, xs, mod_s, *post,
                                                      jnp.zeros((1, LANES), F32))
    x1_p, h2_all, ti_p, tw_p, rank_p, cnt_all = _post_mix(True, 256, y_p, g_p, bonus_p, att_p, xp, mod_p, *post,
                                                          cnt_s, h2_tail=h2_s)

    pos, tok_sorted, ex, rb, ns, cb = _routing_tables(jnp.concatenate([ti_p, ti_s], axis=0),
                                                      jnp.concatenate([rank_p, rank_s], axis=0),
                                                      cnt_all[0, :N_EXPERTS].astype(jnp.int32))
    b1 = moe_b1[0].reshape(N_EXPERTS, 1, D_FF, 2)
    ys_rows = _moe_experts(tok_sorted.shape[0], ex, rb, ns, cb, tok_sorted, h2_all, moe_w1[0], b1[..., 0], b1[..., 1],
                           moe_w2[0], moe_b2[0].reshape(N_EXPERTS, 1, d))
    pos = pos.reshape(-1, TOP_K)
    gpf = row(g_post_ffn[0])
    out_p = _combine(True, pos[:t], ys_rows, x1_p, tw_p, mod_p, gpf)
    out_s = _combine(False, pos[t:], ys_rows, x1_s, tw_s, mod_s, gpf)

    n_keep = min(WINDOW, t)
    st_heads = jnp.stack([st_p[:, :HEAD_DIM, :HEAD_DIM], st_p[:, HEAD_DIM:, HEAD_DIM:]], axis=1)
    return (out_p.reshape(1, t, d),
            out_s.reshape(nb, 1, d),
            ka_p[t - n_keep:].reshape(1, 1, n_keep, N_KV_HEADS, HEAD_DIM),
            va_p[t - n_keep:].reshape(1, 1, n_keep, N_KV_HEADS, HEAD_DIM),
            st_heads.reshape(1, 1, N_RWKV_HEADS, HEAD_DIM, HEAD_DIM),
            hlast_p[7:8].reshape(1, 1, d),
            ck_new.reshape(1, nb, wb, N_KV_HEADS, HEAD_DIM),
            cv_new.reshape(1, nb, wb, N_KV_HEADS, HEAD_DIM),
            wkv_s.reshape(1, nb, N_RWKV_HEADS, HEAD_DIM, HEAD_DIM),
            h_s.reshape(1, nb, d))
```

```python
import functools
import math

import jax
import jax.numpy as jnp
from jax import lax
from jax.experimental import pallas as pl
from jax.experimental.pallas import tpu as pltpu

F32 = jnp.float32
BF16 = jnp.bfloat16

D_MODEL = 2048
HEAD_DIM = 64
RWKV_WIDTH = 1024
N_RWKV_HEADS = 16
ATTN_WIDTH = 1024
N_Q_HEADS = 16
N_KV_HEADS = 2
Q_PER_KV = 8
KV_WIDTH = 128
WINDOW = 128
N_SHIFTED = 6
N_EXPERTS = 32
TOP_K = 4
D_FF = 2048
SWIGLU_ALPHA = 1.702
SWIGLU_LIMIT = 7.0
NORM_EPS = 1e-6
GN_EPS = 64e-5
L2_EPS = 1e-12

LANES = 128
SUBLANES = 8
VMEM_LIMIT = 56 * 1024 * 1024

CHUNK = 64
RWKV_ROWS = 512
RWKV_PAIRS = 8
MOE_ROWS = 128
MOE_SUB = 12
MOE_FT = 256
MOE_NF = D_FF // MOE_FT
COMBINE_TOK = 128


def _dot(a, b):
    return jnp.dot(a, b, preferred_element_type=F32)


def _dot_nt(a, b):
    return lax.dot_general(a, b, (((1,), (1,)), ((), ())), preferred_element_type=F32)


def _split_bf16(x):
    hi = x.astype(BF16)
    lo = (x - hi.astype(F32)).astype(BF16)
    return hi, lo


def _dot3(a, b):
    ah, al = _split_bf16(a)
    bh, bl = _split_bf16(b)
    return _dot(ah, bh) + _dot(ah, bl) + _dot(al, bh)


def _seg_sum64(x):
    wide = LANES
    r = lax.broadcasted_iota(jnp.int32, (wide, wide), 0) // HEAD_DIM
    c = lax.broadcasted_iota(jnp.int32, (wide, wide), 1) // HEAD_DIM
    bd = jnp.where(r == c, 1.0, 0.0).astype(BF16)
    hi, lo = _split_bf16(x)
    outs = []
    for j in range(x.shape[1] // wide):
        sl = slice(wide * j, wide * (j + 1))
        outs.append(_dot(hi[:, sl], bd) + _dot(lo[:, sl], bd))
    return jnp.concatenate(outs, axis=1)


def _rmsnorm(x, g):
    ms = jnp.mean(x * x, axis=-1, keepdims=True)
    return x * lax.rsqrt(ms + NORM_EPS) * g


def _resident(shape):
    nd = len(shape)
    return pl.BlockSpec(shape, lambda *_: (0,) * nd, pipeline_mode=pl.Buffered(1))


def _mod_kernel(c_ref, w_ref, b_ref, o_ref):
    c = c_ref[...]
    s = c * jax.nn.sigmoid(c)
    o_ref[...] = _dot3(s, w_ref[...]) + b_ref[...]


def _adaln_mod(c, w_ada, b_ada):
    rows, d = c.shape
    n = w_ada.shape[1]
    tn = 512
    return pl.pallas_call(
        _mod_kernel,
        grid=(n // tn,),
        in_specs=[pl.BlockSpec((rows, d), lambda j: (0, 0)),
                  pl.BlockSpec((d, tn), lambda j: (0, j)),
                  pl.BlockSpec((1, tn), lambda j: (0, j))],
        out_specs=pl.BlockSpec((rows, tn), lambda j: (0, j)),
        out_shape=jax.ShapeDtypeStruct((rows, n), F32),
        compiler_params=pltpu.CompilerParams(dimension_semantics=("arbitrary",), vmem_limit_bytes=VMEM_LIMIT),
        name="adaln_mod",
    )(c, w_ada, b_ada)


def _inproj_kernel(is_prompt, tm,
                   x_ref, prev_ref, sh_ref, sc_ref, gpm_ref, mu_ref,
                   wr_ref, wk_ref, wv_ref, wwl_ref, wal_ref, wgl_ref, wqkv_ref,
                   w0_ref, w2_ref, a0_ref, a2_ref, g2_ref, kk_ref, ka_ref, rk_ref,
                   r_o, lw_o, kf_o, v_o, na_o, b_o, g_o, bonus_o, q_o, kat_o, vat_o, h_o):
    i = pl.program_id(0)
    gpm = gpm_ref[...]
    sh = sh_ref[...]
    sc = sc_ref[...]
    if is_prompt:
        sh = sh[0:1]
        sc = sc[0:1]

    def modnorm(x):
        return _rmsnorm(x, gpm) * (1.0 + sc) + sh

    h = modnorm(x_ref[...])
    if is_prompt:
        hp = modnorm(prev_ref[...])[7:8, :]
        hp = jnp.where(i > 0, hp, 0.0)
        row = lax.broadcasted_iota(jnp.int32, h.shape, 0)
        hprev = jnp.where(row == 0, hp, pltpu.roll(h, 1, axis=0))
        h_o[...] = h[tm - 8:tm, :]
    else:
        hprev = prev_ref[...]
        h_o[...] = h
    dx = hprev - h
    mu = mu_ref[...]

    def branch(j, w_ref):
        xi = (h + dx * mu[j:j + 1, :]).astype(BF16)
        return _dot(xi, w_ref[...])

    r = branch(0, wr_ref)
    k = branch(1, wk_ref)
    v = branch(2, wv_ref)
    wl = branch(3, wwl_ref)
    al = branch(4, wal_ref)
    gl = branch(5, wgl_ref)
    qkv = _dot(h.astype(BF16), wqkv_ref[...])
    q_o[...] = qkv[:, :ATTN_WIDTH]
    kat_o[...] = qkv[:, ATTN_WIDTH:ATTN_WIDTH + KV_WIDTH]
    vat_o[...] = qkv[:, ATTN_WIDTH + KV_WIDTH:]

    z = w0_ref[...] + _dot(jnp.tanh(wl).astype(BF16), w2_ref[...])
    w_raw = -jnp.logaddexp(-z, 0.0) - 0.5
    lw_o[...] = -jnp.exp(w_raw)
    a = jax.nn.sigmoid(a0_ref[...] + _dot(al.astype(BF16), a2_ref[...]))
    g_o[...] = _dot(jax.nn.sigmoid(gl).astype(BF16), g2_ref[...])
    kk = k * kk_ref[...]
    kk = kk / jnp.maximum(jnp.sqrt(_seg_sum64(kk * kk)), L2_EPS)
    kf = k * (1.0 + (a - 1.0) * ka_ref[...])
    r_o[...] = r
    kf_o[...] = kf
    v_o[...] = v
    na_o[...] = -kk
    b_o[...] = kk * a
    bonus_o[...] = _seg_sum64(r * kf * rk_ref[...]) * v


def _in_proj(is_prompt, tm, x, prev, mod, gpm, mu8, wts, rw):
    m, d = x.shape
    grid = (m // tm,)
    row = lambda i: (i, 0)
    if is_prompt:
        prev_spec = pl.BlockSpec((8, d), lambda i: (jnp.maximum(i * (tm // 8) - 1, 0), 0))
        mod_rows = 8
        mod_map = lambda c: (lambda i: (0, c))
        h_shape, h_spec = (8, d), pl.BlockSpec((8, d), lambda i: (0, 0))
    else:
        prev_spec = pl.BlockSpec((tm, d), row)
        mod_rows = tm
        mod_map = lambda c: (lambda i: (i, c))
        h_shape, h_spec = (m, d), pl.BlockSpec((tm, d), row)
    in_specs = [pl.BlockSpec((tm, d), row), prev_spec,
                pl.BlockSpec((mod_rows, d), mod_map(0)), pl.BlockSpec((mod_rows, d), mod_map(1)),
                _resident((1, d)), _resident((8, d))]
    in_specs += [_resident(w.shape) for w in wts]
    in_specs += [_resident(p.shape) for p in rw]
    wide = jax.ShapeDtypeStruct((m, RWKV_WIDTH), F32)
    wide_spec = pl.BlockSpec((tm, RWKV_WIDTH), row)
    kv = jax.ShapeDtypeStruct((m, KV_WIDTH), F32)
    kv_spec = pl.BlockSpec((tm, KV_WIDTH), row)
    out_shape = [wide] * 9 + [kv, kv, jax.ShapeDtypeStruct(h_shape, F32)]
    out_specs = [wide_spec] * 9 + [kv_spec, kv_spec, h_spec]
    return pl.pallas_call(
        functools.partial(_inproj_kernel, is_prompt, tm),
        grid=grid, in_specs=in_specs, out_specs=out_specs, out_shape=out_shape,
        compiler_params=pltpu.CompilerParams(dimension_semantics=("arbitrary",), vmem_limit_bytes=VMEM_LIMIT),
        name="in_proj_prompt" if is_prompt else "in_proj_sample",
    )(x, prev, mod, mod, gpm, mu8, *wts, *rw)


def _rwkv_chunk_kernel(r_ref, lw_ref, k_ref, v_ref, a_ref, b_ref, y_ref, s_ref, st_ref):
    t = pl.program_id(1)
    C = CHUNK
    P = 2 * HEAD_DIM

    @pl.when(t == 0)
    def _():
        st_ref[...] = jnp.zeros_like(st_ref)

    ri = lax.broadcasted_iota(jnp.int32, (P, P), 0)
    ci = lax.broadcasted_iota(jnp.int32, (P, P), 1)
    bd = (ri // C) == (ci // C)
    tril_s = bd & ((ri % C) > (ci % C))
    tril_i = bd & ((ri % C) >= (ci % C))
    eye = jnp.where(ri == ci, 1.0, 0.0)
    lane0 = lax.broadcasted_iota(jnp.int32, (C, P), 1) < HEAD_DIM
    trow = lax.broadcasted_iota(jnp.int32, (C, P), 0)

    def stack(x):
        return jnp.concatenate([jnp.where(lane0, x, 0.0), jnp.where(lane0, 0.0, x)], axis=0)

    def dup(x):
        return jnp.concatenate([x, x], axis=0)

    def prep(sl, pp):
        ln = slice(P * pp, P * (pp + 1))
        lw = lw_ref[sl, ln]
        cw = lw
        for s in (1, 2, 4, 8, 16, 32):
            cw = cw + jnp.where(trow >= s, pltpu.roll(cw, s, axis=0), 0.0)
        cw_last = cw[C - 1:C, :]
        e_neg = jnp.exp(-cw)
        e_end = jnp.exp(cw_last - cw)
        k = k_ref[sl, ln]
        v = v_ref[sl, ln]
        b = b_ref[sl, ln]
        a2 = stack(a_ref[sl, ln] * jnp.exp(cw - lw))
        r2 = stack(r_ref[sl, ln] * jnp.exp(cw))
        return dict(
            v2=stack(v),
            lhs=jnp.concatenate([a2, r2], axis=0).astype(BF16),
            rhs=jnp.concatenate([dup(k * e_neg), dup(b * e_neg)], axis=0).astype(BF16),
            kbh=jnp.concatenate([dup(b * e_end), dup(k * e_end)], axis=0).astype(BF16),
            decay=jnp.exp(cw_last))

    def chunk(c, carry):
        sl = pl.ds(pl.multiple_of(c * C, C), C)
        pairs = range(RWKV_PAIRS)
        st = [st_ref[pp] for pp in pairs]
        d = [prep(sl, pp) for pp in pairs]
        gram = [_dot_nt(d[pp]['lhs'], d[pp]['rhs']) for pp in pairs]
        l2 = [jnp.where(tril_s, gram[pp][0:P, P:2 * P], 0.0) for pp in pairs]
        a_s = [_dot_nt(d[pp]['lhs'], st[pp].astype(BF16)) for pp in pairs]
        v2b = [d[pp]['v2'].astype(BF16) for pp in pairs]
        rhs_u = [a_s[pp][0:P] + _dot(jnp.where(tril_s, gram[pp][0:P, 0:P], 0.0).astype(BF16), v2b[pp])
                 for pp in pairs]
        inv = [eye + l2[pp] for pp in pairs]
        lp = l2
        for _ in range(5):
            lpb = [lp[pp].astype(BF16) for pp in pairs]
            lp = [_dot(lpb[pp], lpb[pp]) for pp in pairs]
            inv = [inv[pp] + _dot(lp[pp].astype(BF16), inv[pp].astype(BF16)) for pp in pairs]
        u2 = [_dot(inv[pp].astype(BF16), rhs_u[pp].astype(BF16)) for pp in pairs]
        p_cat = [jnp.concatenate([jnp.where(tril_i, gram[pp][P:2 * P, P:2 * P], 0.0),
                                  jnp.where(tril_i, gram[pp][P:2 * P, 0:P], 0.0)], axis=1).astype(BF16)
                 for pp in pairs]
        y2 = [a_s[pp][P:2 * P] + _dot(p_cat[pp], jnp.concatenate([u2[pp].astype(BF16), v2b[pp]], axis=0))
              for pp in pairs]
        uvt = [jnp.concatenate([u2[pp], d[pp]['v2']], axis=0).T.astype(BF16) for pp in pairs]
        st_new = [jnp.where(bd, st[pp] * d[pp]['decay'] + _dot(uvt[pp], d[pp]['kbh']), 0.0) for pp in pairs]
        for pp in pairs:
            y_ref[sl, P * pp:P * (pp + 1)] = y2[pp][0:C] + y2[pp][C:2 * C]
            st_ref[pp] = st_new[pp]
        return carry

    lax.fori_loop(0, RWKV_ROWS // C, chunk, 0)

    @pl.when(t == pl.num_programs(1) - 1)
    def _():
        s_ref[...] = st_ref[...]


def _rwkv_prompt(r, lw, kf, v, na, b):
    t = r.shape[0]
    n_pairs = RWKV_WIDTH // LANES
    width = RWKV_PAIRS * LANES
    spec = pl.BlockSpec((RWKV_ROWS, width), lambda p, i: (i, p))
    return pl.pallas_call(
        _rwkv_chunk_kernel,
        grid=(n_pairs // RWKV_PAIRS, t // RWKV_ROWS),
        in_specs=[spec] * 6,
        out_specs=[spec, pl.BlockSpec((RWKV_PAIRS, LANES, LANES), lambda p, i: (p, 0, 0))],
        out_shape=[jax.ShapeDtypeStruct((t, RWKV_WIDTH), F32),
                   jax.ShapeDtypeStruct((n_pairs, LANES, LANES), F32)],
        scratch_shapes=[pltpu.VMEM((RWKV_PAIRS, LANES, LANES), F32)],
        compiler_params=pltpu.CompilerParams(dimension_semantics=("arbitrary", "arbitrary"),
                                             vmem_limit_bytes=VMEM_LIMIT),
        name="rwkv_chunked",
    )(r, lw, kf, v, na, b)


def _rwkv_step_kernel(r_ref, lw_ref, k_ref, v_ref, a_ref, b_ref, s_ref, y_ref, so_ref):
    n = HEAD_DIM
    tr = lambda ref: ref[...].T
    rt, kt, vt, at, bt = tr(r_ref), tr(k_ref), tr(v_ref), tr(a_ref), tr(b_ref)
    dt = jnp.exp(tr(lw_ref))
    y_rows = []
    for h in range(2):
        hs = slice(h * n, (h + 1) * n)
        r_h, k_h, a_h, b_h, d_h = rt[hs], kt[hs], at[hs], bt[hs], dt[hs]
        for v in range(n):
            s = s_ref[h, v]
            sa = jnp.sum(s * a_h, axis=0, keepdims=True)
            s2 = s * d_h + sa * b_h + vt[h * n + v:h * n + v + 1] * k_h
            so_ref[h, v] = s2
            y_rows.append(jnp.sum(s2 * r_h, axis=0, keepdims=True))
    y_ref[...] = jnp.concatenate(y_rows, axis=0).T


def _rwkv_sample(r, lw, kf, v, na, b, state):
    nb = r.shape[0]
    assert nb == LANES, "the step kernel puts the whole batch on the lane axis"
    state_t = jnp.transpose(state, (1, 2, 3, 0))
    vspec = pl.BlockSpec((nb, LANES), lambda p: (0, p))
    sspec = pl.BlockSpec((2, HEAD_DIM, HEAD_DIM, nb), lambda p: (p, 0, 0, 0))
    y, state_new = pl.pallas_call(
        _rwkv_step_kernel,
        grid=(N_RWKV_HEADS // 2,),
        in_specs=[vspec] * 6 + [sspec],
        out_specs=[vspec, sspec],
        out_shape=[jax.ShapeDtypeStruct((nb, RWKV_WIDTH), F32), jax.ShapeDtypeStruct(state_t.shape, F32)],
        compiler_params=pltpu.CompilerParams(dimension_semantics=("arbitrary",)),
        name="rwkv_step",
    )(r, lw, kf, v, na, b, state_t)
    return y, jnp.transpose(state_new, (3, 0, 1, 2))


def _alibi_slope(head):
    return 2.0 ** (-8.0 * (head + 1) / N_Q_HEADS)


def _swa_prompt_kernel(q_ref, kc_ref, kp_ref, vc_ref, vp_ref, sink_ref, o_ref):
    n = pl.program_id(0)
    w = WINDOW
    kcat = jnp.concatenate([kp_ref[...], kc_ref[...]], axis=0)
    vcat = jnp.concatenate([vp_ref[...], vc_ref[...]], axis=0)
    lane_k = lax.broadcasted_iota(jnp.int32, kcat.shape, 1) < HEAD_DIM
    kswap = pltpu.roll(kcat, HEAD_DIM, axis=1)
    vswap = pltpu.roll(vcat, HEAD_DIM, axis=1)
    kdup = [jnp.where(lane_k, kcat, kswap).astype(BF16), jnp.where(lane_k, kswap, kcat).astype(BF16)]
    vdup = [jnp.where(lane_k, vcat, vswap).astype(BF16), jnp.where(lane_k, vswap, vcat).astype(BF16)]
    qi = lax.broadcasted_iota(jnp.int32, (w, 2 * w), 0)
    kj = lax.broadcasted_iota(jnp.int32, (w, 2 * w), 1)
    dist = qi + w - kj
    valid = (dist >= 0) & (dist <= w) & ((n > 0) | (kj >= w))
    distf = dist.astype(F32)
    lane_q = lax.broadcasted_iota(jnp.int32, (w, LANES), 1) < HEAD_DIM
    ones = jnp.ones((2 * w, LANES), BF16)
    sinks = sink_ref[...]
    group = 4
    for j0 in range(0, N_Q_HEADS // 2, group):
        js = range(j0, j0 + group)
        heads = [2 * j + half for j in js for half in range(2)]
        kvh = {h: h // Q_PER_KV for h in heads}
        q2 = {}
        for j in js:
            qp = q_ref[:, LANES * j:LANES * (j + 1)] * (1.0 / math.sqrt(HEAD_DIM))
            q2[j] = jnp.concatenate([jnp.where(lane_q, qp, 0.0), jnp.where(lane_q, 0.0, qp)], axis=0).astype(BF16)
        s2 = {j: _dot_nt(q2[j], kdup[kvh[2 * j]]) for j in js}
        s = {h: jnp.where(valid, s2[h // 2][w * (h % 2):w * (h % 2 + 1)] - _alibi_slope(h) * distf, -jnp.inf)
             for h in heads}
        sink = {h: sinks[0:1, h:h + 1] for h in heads}
        m = {h: jnp.broadcast_to(jnp.maximum(jnp.max(s[h], axis=-1, keepdims=True), sink[h]), (w, LANES))
             for h in heads}
        p = {h: jnp.exp(s[h] - jnp.concatenate([m[h], m[h]], axis=1)).astype(BF16) for h in heads}
        den = {h: _dot(p[h], ones) + jnp.exp(sink[h] - m[h]) for h in heads}
        o = {h: _dot(p[h], vdup[kvh[h]]) / den[h] for h in heads}
        for j in js:
            o_ref[:, LANES * j:LANES * (j + 1)] = jnp.where(lane_q, o[2 * j], o[2 * j + 1])


def _swa_prompt(q, ka, va, sinks):
    t = q.shape[0]
    w = WINDOW
    cur = lambda n: (n, 0)
    prv = lambda n: (jnp.maximum(n - 1, 0), 0)
    kvs = lambda f: pl.BlockSpec((w, KV_WIDTH), f)
    return pl.pallas_call(
        _swa_prompt_kernel,
        grid=(t // w,),
        in_specs=[pl.BlockSpec((w, ATTN_WIDTH), cur), kvs(cur), kvs(prv), kvs(cur), kvs(prv),
                  pl.BlockSpec((1, N_Q_HEADS), lambda n: (0, 0))],
        out_specs=pl.BlockSpec((w, ATTN_WIDTH), cur),
        out_shape=jax.ShapeDtypeStruct((t, ATTN_WIDTH), F32),
        compiler_params=pltpu.CompilerParams(dimension_semantics=("arbitrary",)),
        name="swa_prompt",
    )(q, ka, ka, va, va, sinks)


def _swa_sample_kernel(q_ref, kn_ref, vn_ref, ck_ref, cv_ref, sink_ref, o_ref, ko_ref, vo_ref):
    wb = ck_ref.shape[1]
    q = q_ref[...] * (1.0 / math.sqrt(HEAD_DIM))
    q2 = jnp.concatenate([q, q], axis=2)
    rowh = lax.broadcasted_iota(jnp.int32, q2.shape, 1) // Q_PER_KV
    laneh = lax.broadcasted_iota(jnp.int32, q2.shape, 2) // HEAD_DIM
    qb = jnp.where(rowh == laneh, q2, 0.0)
    kn = kn_ref[...]
    vn = vn_ref[...]
    ck = ck_ref[...]
    cv = cv_ref[...]
    s = jnp.einsum('bqd,bkd->bqk', qb.astype(BF16), ck.astype(BF16), preferred_element_type=F32)
    s_self = jnp.sum(qb * kn, axis=2, keepdims=True)
    head = lax.broadcasted_iota(jnp.int32, (1, N_Q_HEADS, 1), 1).astype(F32)
    slope = jnp.exp2(-8.0 * (head + 1.0) / N_Q_HEADS)
    dist = (wb - lax.broadcasted_iota(jnp.int32, (1, 1, wb), 2)).astype(F32)
    s = s - slope * dist
    sink = sink_ref[...][None]
    m = jnp.maximum(jnp.maximum(jnp.max(s, axis=2, keepdims=True), s_self), sink)
    p = jnp.exp(s - m)
    p_self = jnp.exp(s_self - m)
    den = jnp.sum(p, axis=2, keepdims=True) + p_self + jnp.exp(sink - m)
    o = jnp.einsum('bqk,bkd->bqd', p.astype(BF16), cv.astype(BF16), preferred_element_type=F32)
    o = (o + p_self * vn) / den
    sel = lax.broadcasted_iota(jnp.int32, (1, N_Q_HEADS, HEAD_DIM), 1) < Q_PER_KV
    o_ref[...] = jnp.where(sel, o[:, :, :HEAD_DIM], o[:, :, HEAD_DIM:])
    ko_ref[:, 0:wb - 1, :] = ck_ref[:, 1:wb, :]
    ko_ref[:, wb - 1:wb, :] = kn
    vo_ref[:, 0:wb - 1, :] = cv_ref[:, 1:wb, :]
    vo_ref[:, wb - 1:wb, :] = vn


def _swa_sample(q, ka, va, cache_k, cache_v, sinks_col):
    nb, wb = cache_k.shape[0], cache_k.shape[1]
    bb = 16
    b3 = lambda i: (i, 0, 0)
    nspec = pl.BlockSpec((bb, 1, KV_WIDTH), b3)
    cspec = pl.BlockSpec((bb, wb, KV_WIDTH), b3)
    qspec = pl.BlockSpec((bb, N_Q_HEADS, HEAD_DIM), b3)
    return pl.pallas_call(
        _swa_sample_kernel,
        grid=(nb // bb,),
        in_specs=[qspec, nspec, nspec, cspec, cspec, pl.BlockSpec((N_Q_HEADS, 1), lambda i: (0, 0))],
        out_specs=[qspec, cspec, cspec],
        out_shape=[jax.ShapeDtypeStruct((nb, N_Q_HEADS, HEAD_DIM), F32),
                   jax.ShapeDtypeStruct(cache_k.shape, F32), jax.ShapeDtypeStruct(cache_v.shape, F32)],
        compiler_params=pltpu.CompilerParams(dimension_semantics=("arbitrary",)),
        name="swa_sample",
    )(q.reshape(nb, N_Q_HEADS, HEAD_DIM), ka.reshape(nb, 1, KV_WIDTH), va.reshape(nb, 1, KV_WIDTH),
      cache_k, cache_v, sinks_col)


def _post_mix_kernel(is_prompt, n_main, n_tail, *refs):
    if n_tail:
        tail_ref, refs = refs[16], refs[:16] + refs[17:]
    i = pl.program_id(0)

    @pl.when(i < n_main)
    def _():
        _post_mix_body(is_prompt, *refs)

    if n_tail:
        h2_o = refs[17]

        @pl.when(i == n_main)
        def _():
            h2_o[0:n_tail, :] = tail_ref[...]
            h2_o[n_tail:, :] = jnp.zeros((h2_o.shape[0] - n_tail, h2_o.shape[1]), F32)


def _post_mix_body(is_prompt, yr_ref, g_ref, bonus_ref, ya_ref, x_ref, gt1_ref, sh2_ref, sc2_ref,
                   lnw_ref, lnb_ref, wout_ref, gpost_ref, gpre_ref, rw_ref, rb_ref, cnt_ref,
                   x1_o, h2_o, ti_o, tw_o, rank_o, cnt_o, run_ref):
    @pl.when(pl.program_id(0) == 0)
    def _():
        run_ref[...] = cnt_ref[...]

    gt1 = gt1_ref[...]
    sh2 = sh2_ref[...]
    sc2 = sc2_ref[...]
    if is_prompt:
        gt1, sh2, sc2 = gt1[0:1], sh2[0:1], sc2[0:1]
    y = yr_ref[...]
    mean = _seg_sum64(y) * (1.0 / HEAD_DIM)
    dlt = y - mean
    var = _seg_sum64(dlt * dlt) * (1.0 / HEAD_DIM)
    yn = dlt * lax.rsqrt(var + GN_EPS) * lnw_ref[...] + lnb_ref[...]
    yr = (yn + bonus_ref[...]) * g_ref[...]
    mix = _dot(jnp.concatenate([yr, ya_ref[...]], axis=1).astype(BF16), wout_ref[...])
    x1 = x_ref[...] + gt1 * _rmsnorm(mix, gpost_ref[...])
    x1_o[...] = x1
    h2 = _rmsnorm(x1, gpre_ref[...]) * (1.0 + sc2) + sh2
    h2_o[...] = h2
    logits = _dot3(h2, rw_ref[...]) + rb_ref[...]
    lane = lax.broadcasted_iota(jnp.int32, logits.shape, 1)
    vals, idxs = [], []
    for _ in range(TOP_K):
        m = jnp.max(logits, axis=1, keepdims=True)
        idx = jnp.min(jnp.where(logits == m, lane, N_EXPERTS), axis=1, keepdims=True)
        vals.append(m)
        idxs.append(idx)
        logits = jnp.where(lane == idx, -jnp.inf, logits)
    e = jnp.exp(jnp.concatenate(vals, axis=1) - vals[0])
    tw_o[...] = e / jnp.sum(e, axis=1, keepdims=True)
    ti_o[...] = jnp.concatenate(idxs, axis=1)

    tm = logits.shape[0]
    lane_e = lax.broadcasted_iota(jnp.int32, (tm, LANES), 1)
    tri = jnp.where(lax.broadcasted_iota(jnp.int32, (tm, tm), 0) > lax.broadcasted_iota(jnp.int32, (tm, tm), 1),
                    1.0, 0.0).astype(BF16)
    run = run_ref[...]
    ranks = []
    for idx in idxs:
        onehot = jnp.where(lane_e == idx, 1.0, 0.0)
        before = _dot(tri, onehot.astype(BF16)) + run
        ranks.append(jnp.sum(onehot * before, axis=1, keepdims=True))
        run = run + jnp.sum(onehot, axis=0, keepdims=True)
    run_ref[...] = run
    cnt_o[...] = run
    rank_o[...] = jnp.concatenate(ranks, axis=1).astype(jnp.int32)


def _post_mix(is_prompt, tm, yr, g, bonus, ya, x, mod, lnw, lnb, wout, gpost, gpre, rw, rb, cnt, h2_tail=None):
    m, d = x.shape
    n_main = m // tm
    n_tail = 0 if h2_tail is None else h2_tail.shape[0]
    assert n_tail <= tm
    row = lambda i: (jnp.minimum(i, n_main - 1), 0)
    if is_prompt:
        mod_rows = 8
        mod_map = lambda c: (lambda i: (0, c))
    else:
        mod_rows = tm
        mod_map = lambda c: (lambda i: (jnp.minimum(i, n_main - 1), c))
    wide = pl.BlockSpec((tm, RWKV_WIDTH), row)
    in_specs = [wide, wide, wide, wide, pl.BlockSpec((tm, d), row),
                pl.BlockSpec((mod_rows, d), mod_map(2)), pl.BlockSpec((mod_rows, d), mod_map(3)),
                pl.BlockSpec((mod_rows, d), mod_map(4)),
                _resident(lnw.shape), _resident(lnb.shape), _resident(wout.shape), _resident(gpost.shape),
                _resident(gpre.shape), _resident(rw.shape), _resident(rb.shape), _resident(cnt.shape)]
    args = [yr, g, bonus, ya, x, mod, mod, mod, lnw, lnb, wout, gpost, gpre, rw, rb, cnt]
    if n_tail:
        in_specs.append(_resident(h2_tail.shape))
        args.append(h2_tail)
    out_shape = [jax.ShapeDtypeStruct((m, d), F32), jax.ShapeDtypeStruct((m + n_tail, d), F32),
                 jax.ShapeDtypeStruct((m, TOP_K), jnp.int32), jax.ShapeDtypeStruct((m, TOP_K), F32),
                 jax.ShapeDtypeStruct((m, TOP_K), jnp.int32), jax.ShapeDtypeStruct((1, LANES), F32)]
    out_specs = [pl.BlockSpec((tm, d), row), pl.BlockSpec((tm, d), lambda i: (i, 0)),
                 pl.BlockSpec((tm, TOP_K), row), pl.BlockSpec((tm, TOP_K), row),
                 pl.BlockSpec((tm, TOP_K), row), pl.BlockSpec((1, LANES), lambda i: (0, 0))]
    return pl.pallas_call(
        functools.partial(_post_mix_kernel, is_prompt, n_main, n_tail),
        grid=(n_main + (1 if n_tail else 0),), in_specs=in_specs, out_specs=out_specs, out_shape=out_shape,
        scratch_shapes=[pltpu.VMEM((1, LANES), F32)],
        compiler_params=pltpu.CompilerParams(dimension_semantics=("arbitrary",), vmem_limit_bytes=VMEM_LIMIT),
        name="post_mix_prompt" if is_prompt else "post_mix_sample",
    )(*args)


def _moe_kernel(ex_ref, rb_ref, ns_ref, cb_ref, tok_ref, h2_hbm, w1_ref, b1g_ref, b1l_ref, w2_ref, b2_ref, ys_hbm,
                xbuf, xb, acc, w1p, w2b, sem_in, sem_out):
    s = pl.program_id(0)
    f = pl.program_id(1)
    n_super = pl.num_programs(0)
    ns = ns_ref[s]
    rb = rb_ref[s]
    rows = MOE_ROWS
    grp = 2 * LANES
    share = rows // MOE_NF

    def gather_share(sup, col):
        base = cb_ref[sup] + col * share
        g_col = col * (share // SUBLANES)
        for j in range(MOE_SUB):
            for u in range(share):
                tok = tok_ref[base + (j * rows + u)]
                pltpu.make_async_copy(h2_hbm.at[tok >> 3, pl.ds(tok & (SUBLANES - 1), 1)],
                                      xbuf.at[g_col + (j * (rows // SUBLANES) + u // SUBLANES),
                                              pl.ds(u % SUBLANES, 1)], sem_in).start()

    def gather_wait():
        pltpu.make_async_copy(h2_hbm.at[pl.ds(0, xbuf.shape[0])], xbuf, sem_in).wait()

    def y_copy(r0, n):
        return pltpu.make_async_copy(acc.at[pl.ds(r0, n)], ys_hbm.at[pl.ds(rb * rows + r0, n)], sem_out)

    def for_tiles(fn):
        def body(j, c):
            fn(j)
            return c
        lax.fori_loop(0, ns, body, 0)

    @pl.when(ns > 0)
    def _():
        @pl.when(f == 0)
        def _():
            @pl.when(s == 0)
            def _():
                def first(col, c):
                    gather_share(0, col)
                    return c
                lax.fori_loop(0, MOE_NF, first, 0)
            b2 = jnp.broadcast_to(b2_ref[0], (rows, D_MODEL))

            def init(j):
                acc[pl.ds(pl.multiple_of(j * rows, rows), rows), :] = b2
            for_tiles(init)

            gather_wait()

            def cast(j):
                n8 = rows // SUBLANES
                x = xbuf[pl.ds(pl.multiple_of(j * n8, n8), n8)].reshape(rows, D_MODEL)
                xb[pl.ds(pl.multiple_of(j * rows, rows), rows), :] = x.astype(BF16)
            for_tiles(cast)

        n_grp = 2 * MOE_FT // grp

        def prep_weights():
            pr = lax.broadcasted_iota(jnp.int32, (grp, grp), 0)
            pc = lax.broadcasted_iota(jnp.int32, (grp, grp), 1)
            perm = jnp.where(pr == jnp.where(pc < LANES, 2 * pc, 2 * (pc - LANES) + 1), 1.0, 0.0).astype(BF16)
            for g in range(n_grp):
                gs = slice(grp * g, grp * (g + 1))
                w1p[:, gs] = _dot(w1_ref[0, :, gs].astype(BF16), perm).astype(BF16)
            w2b[...] = w2_ref[0].astype(BF16)

        nxt = jnp.minimum(s + 1, n_super - 1)
        next_live = (s + 1 < n_super) & (ns_ref[nxt] > 0)

        @pl.when(next_live)
        def _():
            gather_share(nxt, f)
            prep_weights()

        @pl.when(jnp.logical_not(next_live))
        def _():
            prep_weights()

        b1g = b1g_ref[0]
        b1l = b1l_ref[0]

        last_col = f == pl.num_programs(1) - 1

        def swiglu(hh):
            glu = jnp.concatenate([hh[:, grp * g:grp * g + LANES] for g in range(n_grp)], axis=1) + b1g
            lin = jnp.concatenate([hh[:, grp * g + LANES:grp * (g + 1)] for g in range(n_grp)], axis=1) + b1l
            glu = jnp.minimum(glu, SWIGLU_LIMIT)
            lin = jnp.clip(lin, -SWIGLU_LIMIT, SWIGLU_LIMIT)
            return (glu * jax.nn.sigmoid(SWIGLU_ALPHA * glu) * (lin + 1.0)).astype(BF16)

        def tiles(starts, n):
            sls = [pl.ds(pl.multiple_of(r0, rows), n) for r0 in starts]
            hh = [_dot(xb[sl, :], w1p[...]) for sl in sls]
            act = [swiglu(h) for h in hh]
            out = [_dot(a, w2b[...]) for a in act]
            for sl, o in zip(sls, out):
                acc[sl, :] += o

            @pl.when(last_col)
            def _():
                for r0 in starts:
                    y_copy(r0, n).start()

        big = 2 * rows

        def tile_pair(q, c):
            tiles([2 * big * q, 2 * big * q + big], big)
            return c
        lax.fori_loop(0, ns // 4, tile_pair, 0)

        @pl.when(ns % 4 >= 2)
        def _():
            tiles([(ns // 4) * 2 * big], big)

        @pl.when(ns % 2 == 1)
        def _():
            tiles([(ns - 1) * rows], rows)

        @pl.when(last_col)
        def _():
            for_tiles(lambda j: y_copy(j * rows, rows).wait())

    @pl.when((ns < 0) & (f == 0))
    def _():
        acc[0:rows, :] = jnp.zeros((rows, D_MODEL), F32)

        def z_copy(j):
            return pltpu.make_async_copy(acc.at[pl.ds(0, rows)], ys_hbm.at[pl.ds((rb + j) * rows, rows)], sem_out)

        def body(j, c, op):
            op(z_copy(j))
            return c
        lax.fori_loop(0, -ns, functools.partial(body, op=lambda cp: cp.start()), 0)
        lax.fori_loop(0, -ns, functools.partial(body, op=lambda cp: cp.wait()), 0)


def _moe_experts(n_rows, n_used, ex, rb, ns, cb, tok_sorted, h2, w1, b1g, b1l, w2, b2):
    n_super = n_used[0]
    nf = MOE_NF
    last = nf - 1
    sub_rows = MOE_SUB * MOE_ROWS

    def fcol(s, f, ns_):
        return jnp.where(ns_[s] > 0, f, last)

    grid_spec = pltpu.PrefetchScalarGridSpec(
        num_scalar_prefetch=5,
        grid=(n_super, nf),
        in_specs=[pl.BlockSpec(memory_space=pl.ANY),
                  pl.BlockSpec((1, D_MODEL, 2 * MOE_FT), lambda s, f, e_, r_, n_, c_, t_: (e_[s], 0, fcol(s, f, n_))),
                  pl.BlockSpec((1, 1, MOE_FT), lambda s, f, e_, r_, n_, c_, t_: (e_[s], 0, fcol(s, f, n_))),
                  pl.BlockSpec((1, 1, MOE_FT), lambda s, f, e_, r_, n_, c_, t_: (e_[s], 0, fcol(s, f, n_))),
                  pl.BlockSpec((1, MOE_FT, D_MODEL), lambda s, f, e_, r_, n_, c_, t_: (e_[s], fcol(s, f, n_), 0)),
                  pl.BlockSpec((1, 1, D_MODEL), lambda s, f, e_, r_, n_, c_, t_: (e_[s], 0, 0))],
        out_specs=pl.BlockSpec(memory_space=pl.ANY),
        scratch_shapes=[pltpu.VMEM((sub_rows // SUBLANES, SUBLANES, D_MODEL), F32),
                        pltpu.VMEM((sub_rows, D_MODEL), BF16),
                        pltpu.VMEM((sub_rows, D_MODEL), F32),
                        pltpu.VMEM((D_MODEL, 2 * MOE_FT), BF16),
                        pltpu.VMEM((MOE_FT, D_MODEL), BF16),
                        pltpu.SemaphoreType.DMA, pltpu.SemaphoreType.DMA])
    return pl.pallas_call(
        _moe_kernel,
        grid_spec=grid_spec,
        out_shape=jax.ShapeDtypeStruct((n_rows, D_MODEL), F32),
        compiler_params=pltpu.CompilerParams(dimension_semantics=("arbitrary", "arbitrary"),
                                             vmem_limit_bytes=VMEM_LIMIT),
        name="moe_experts",
    )(ex, rb, ns, cb, tok_sorted, h2.reshape(-1, SUBLANES, D_MODEL), w1, b1g, b1l, w2, b2)


def _combine_kernel(is_prompt, idx_ref, nidx_ref, ys_hbm, x1_ref, tw_ref, gt2_ref, gpost_ref, o_ref, buf, sem):
    tk = COMBINE_TOK
    n = TOP_K * tk
    unroll = 16
    i = pl.program_id(0)
    slot = i % 2

    def start_row(ref, buf_slot, j, g, sub):
        p = ref[0, 0, j]
        pltpu.make_async_copy(ys_hbm.at[p >> 3, pl.ds(p & (SUBLANES - 1), 1)],
                              buf.at[buf_slot, g, pl.ds(sub, 1)], sem.at[buf_slot]).start(priority=sub % 2)

    @pl.when(i == 0)
    def _():
        def body(q, c):
            for u in range(unroll):
                start_row(idx_ref, 0, q * unroll + u, q * (unroll // SUBLANES) + u // SUBLANES, u % SUBLANES)
            return c
        lax.fori_loop(0, n // unroll, body, 0)

    pltpu.make_async_copy(ys_hbm.at[pl.ds(0, n // SUBLANES)], buf.at[slot], sem.at[slot]).wait()

    def combine():
        tw = tw_ref[...]
        t8 = tk // SUBLANES
        rows_of = lambda k: buf[slot, k * t8:(k + 1) * t8].reshape(tk, D_MODEL)
        f = tw[:, 0:1] * rows_of(0)
        for k in range(1, TOP_K):
            f = f + tw[:, k:k + 1] * rows_of(k)
        gt2 = gt2_ref[...]
        if is_prompt:
            gt2 = gt2[0:1]
        o_ref[...] = x1_ref[...] + gt2 * _rmsnorm(f, gpost_ref[...])

    @pl.when(i + 1 < pl.num_programs(0))
    def _():
        for j in range(n):
            start_row(nidx_ref, 1 - slot, j, j // SUBLANES, j % SUBLANES)
        combine()

    @pl.when(i + 1 >= pl.num_programs(0))
    def _():
        combine()


def _combine(is_prompt, pos, ys, x1, tw, mod, gpost):
    m, d = x1.shape
    tk = COMBINE_TOK
    nblk = m // tk
    idx = pos.reshape(nblk, tk, TOP_K).transpose(0, 2, 1).reshape(nblk, 1, TOP_K * tk)
    row = lambda i: (i, 0)
    mod_spec = (pl.BlockSpec((8, d), lambda i: (0, 5)) if is_prompt else pl.BlockSpec((tk, d), lambda i: (i, 5)))
    return pl.pallas_call(
        functools.partial(_combine_kernel, is_prompt),
        grid=(nblk,),
        in_specs=[pl.BlockSpec((1, 1, TOP_K * tk), lambda i: (i, 0, 0), memory_space=pltpu.SMEM),
                  pl.BlockSpec((1, 1, TOP_K * tk), lambda i: (jnp.minimum(i + 1, nblk - 1), 0, 0),
                               memory_space=pltpu.SMEM),
                  pl.BlockSpec(memory_space=pl.ANY),
                  pl.BlockSpec((tk, d), row), pl.BlockSpec((tk, TOP_K), row), mod_spec,
                  pl.BlockSpec((1, d), lambda i: (0, 0))],
        out_specs=pl.BlockSpec((tk, d), row),
        out_shape=jax.ShapeDtypeStruct((m, d), F32),
        scratch_shapes=[pltpu.VMEM((2, TOP_K * tk // SUBLANES, SUBLANES, d), F32), pltpu.SemaphoreType.DMA((2,))],
        compiler_params=pltpu.CompilerParams(dimension_semantics=("arbitrary",)),
        name="moe_combine_prompt" if is_prompt else "moe_combine_sample",
    )(idx, idx, ys.reshape(-1, SUBLANES, d), x1, tw, mod, gpost)


def _routing_tables(top_i, rank, counts):
    n_assign = top_i.size
    e_flat = top_i.reshape(-1)
    tiles = (counts + MOE_ROWS - 1) // MOE_ROWS
    tile_start = jnp.cumsum(tiles) - tiles
    group_start = jnp.cumsum(counts) - counts
    pos = tile_start[e_flat] * MOE_ROWS + rank.reshape(-1)
    n_rows = _padded_rows(n_assign)
    assert n_assign <= (1 << 16) and n_rows <= (1 << 16)
    assert n_rows >= n_assign + MOE_SUB * MOE_ROWS
    key = pos.astype(jnp.uint32) * jnp.uint32(1 << 16) + jnp.arange(n_assign, dtype=jnp.uint32)
    tok_sorted = (jnp.sort(key) & jnp.uint32(0xFFFF)).astype(jnp.int32) // TOP_K
    tok_sorted = jnp.pad(tok_sorted, (0, n_rows - n_assign))
    supers = (tiles + MOE_SUB - 1) // MOE_SUB
    super_end = jnp.cumsum(supers)
    s_idx = jnp.arange(_max_supers(n_assign), dtype=jnp.int32)
    ex = jnp.minimum(jnp.sum((s_idx[:, None] >= super_end[None, :]).astype(jnp.int32), axis=1),
                     N_EXPERTS - 1).astype(jnp.int32)
    j = s_idx - (super_end - supers)[ex]
    live = s_idx < super_end[-1]
    tail = jnp.sum(tiles) + MOE_SUB * (s_idx - super_end[-1])
    n_clear = jnp.clip(n_rows // MOE_ROWS - tail, 0, MOE_SUB)
    ns = jnp.where(live, jnp.clip(tiles[ex] - MOE_SUB * j, 0, MOE_SUB), -n_clear).astype(jnp.int32)
    rb = jnp.where(live, tile_start[ex] + MOE_SUB * j, tail).astype(jnp.int32)
    cb = jnp.where(live, group_start[ex] + MOE_SUB * MOE_ROWS * j, 0).astype(jnp.int32)
    last_live = jnp.max(jnp.where(live, ex, 0))
    ex = jnp.where(live, ex, last_live).astype(jnp.int32)
    n_used = jnp.sum((ns != 0).astype(jnp.int32)).reshape(1)
    return pos.astype(jnp.int32), tok_sorted, ex, rb, ns, cb, n_used


def _padded_rows(n_assign):
    rows = n_assign + N_EXPERTS * (MOE_ROWS - 1)
    return -(-rows // MOE_ROWS) * MOE_ROWS


def _max_supers(n_assign):
    return N_EXPERTS + -(-_padded_rows(n_assign) // (MOE_ROWS * MOE_SUB))


def kernel(x_prompt, x_sample, cache_k, cache_v, state_wkv, state_shift, c_prompt, c_sample, w_ada, b_ada, g_pre_mix, g_post_mix, g_pre_ffn, g_post_ffn, mu_shift, w_in, rwkv_w0, rwkv_w2, rwkv_a0, rwkv_a2, rwkv_g2, rwkv_k_k, rwkv_k_a, rwkv_r_k, rwkv_ln_w, rwkv_ln_b, attn_sinks, w_out, router_w, router_b, moe_w1, moe_b1, moe_w2, moe_b2):
    assert w_ada.shape[0] == 1, "single-layer step"
    d = D_MODEL
    t = x_prompt.shape[1]
    nb = x_sample.shape[0]
    xp = x_prompt.reshape(t, d)
    xs = x_sample.reshape(nb, d)

    c_all = jnp.concatenate([jnp.broadcast_to(c_prompt, (8, d)), c_sample], axis=0)
    mod = _adaln_mod(c_all, w_ada[0], b_ada[0].reshape(1, 6 * d))
    mod_p, mod_s = mod[:8], mod[8:]

    offs = [0, 1024, 2048, 3072, 3136, 3200, 3360]
    w_in0 = w_in[0]
    pad_to = lambda w, n: jnp.pad(w, ((0, 0), (0, n - w.shape[1])))
    wts = [w_in0[:, offs[0]:offs[1]], w_in0[:, offs[1]:offs[2]], w_in0[:, offs[2]:offs[3]],
           pad_to(w_in0[:, offs[3]:offs[4]], 128), pad_to(w_in0[:, offs[4]:offs[5]], 128),
           pad_to(w_in0[:, offs[5]:offs[6]], 256), w_in0[:, offs[6]:]]
    wts = [w.astype(BF16) for w in wts]
    pad_rows = lambda w, n: jnp.pad(w, ((0, n - w.shape[0]), (0, 0))).astype(BF16)
    row = lambda p: p.reshape(1, -1)
    rw = [row(rwkv_w0[0]), pad_rows(rwkv_w2[0], 128), row(rwkv_a0[0]), pad_rows(rwkv_a2[0], 128),
          pad_rows(rwkv_g2[0], 256), row(rwkv_k_k[0]), row(rwkv_k_a[0]), row(rwkv_r_k[0])]
    gpm = row(g_pre_mix[0])
    mu8 = jnp.pad(mu_shift[0], ((0, 8 - N_SHIFTED), (0, 0)))

    pr = _in_proj(True, 256, xp, xp, mod_p, gpm, mu8, wts, rw)
    sr = _in_proj(False, nb, xs, state_shift[0], mod_s, gpm, mu8, wts, rw)
    r_p, lw_p, kf_p, v_p, na_p, b_p, g_p, bonus_p, q_p, ka_p, va_p, hlast_p = pr
    r_s, lw_s, kf_s, v_s, na_s, b_s, g_s, bonus_s, q_s, ka_s, va_s, h_s = sr

    y_p, st_p = _rwkv_prompt(r_p, lw_p, kf_p, v_p, na_p, b_p)
    y_s, wkv_s = _rwkv_sample(r_s, lw_s, kf_s, v_s, na_s, b_s, state_wkv[0])
    y_s = y_s.reshape(nb, RWKV_WIDTH)
    sinks = attn_sinks[0]
    att_p = _swa_prompt(q_p, ka_p, va_p, sinks.reshape(1, N_Q_HEADS))
    wb = cache_k.shape[2]
    att_s, ck_new, cv_new = _swa_sample(q_s, ka_s, va_s, cache_k[0].reshape(nb, wb, KV_WIDTH),
                                        cache_v[0].reshape(nb, wb, KV_WIDTH), sinks.reshape(N_Q_HEADS, 1))
    att_s = att_s.reshape(nb, ATTN_WIDTH)

    post = [row(rwkv_ln_w[0]), row(rwkv_ln_b[0]), w_out[0].astype(BF16), row(g_post_mix[0]),
            row(g_pre_ffn[0]), router_w[0], row(router_b[0])]
    x1_s, h2_s, ti_s, tw_s, rank_s, cnt_s = _post_mix(False, nb, y_s, g_s, bonus_s, att_s, xs, mod_s, *post,
                                                      jnp.zeros((1, LANES), F32))
    x1_p, h2_all, ti_p, tw_p, rank_p, cnt_all = _post_mix(True, 256, y_p, g_p, bonus_p, att_p, xp, mod_p, *post,
                                                          cnt_s, h2_tail=h2_s)

    pos, tok_sorted, ex, rb, ns, cb, n_used = _routing_tables(jnp.concatenate([ti_p, ti_s], axis=0),
                                                      jnp.concatenate([rank_p, rank_s], axis=0),
                                                      cnt_all[0, :N_EXPERTS].astype(jnp.int32))
    b1 = moe_b1[0].reshape(N_EXPERTS, 1, D_FF, 2)
    ys_rows = _moe_experts(tok_sorted.shape[0], n_used, ex, rb, ns, cb, tok_sorted, h2_all, moe_w1[0], b1[..., 0], b1[..., 1],
                           moe_w2[0], moe_b2[0].reshape(N_EXPERTS, 1, d))
    pos = pos.reshape(-1, TOP_K)
    gpf = row(g_post_ffn[0])
    out_p = _combine(True, pos[:t], ys_rows, x1_p, tw_p, mod_p, gpf)
    out_s = _combine(False, pos[t:], ys_rows, x1_s, tw_s, mod_s, gpf)

    n_keep = min(WINDOW, t)
    st_heads = jnp.stack([st_p[:, :HEAD_DIM, :HEAD_DIM], st_p[:, HEAD_DIM:, HEAD_DIM:]], axis=1)
    return (out_p.reshape(1, t, d),
            out_s.reshape(nb, 1, d),
            ka_p[t - n_keep:].reshape(1, 1, n_keep, N_KV_HEADS, HEAD_DIM),
            va_p[t - n_keep:].reshape(1, 1, n_keep, N_KV_HEADS, HEAD_DIM),
            st_heads.reshape(1, 1, N_RWKV_HEADS, HEAD_DIM, HEAD_DIM),
            hlast_p[7:8].reshape(1, 1, d),
            ck_new.reshape(1, nb, wb, N_KV_HEADS, HEAD_DIM),
            cv_new.reshape(1, nb, wb, N_KV_HEADS, HEAD_DIM),
            wkv_s.reshape(1, nb, N_RWKV_HEADS, HEAD_DIM, HEAD_DIM),
            h_s.reshape(1, nb, d))
```
